```python
import jax, jax.numpy as jnp
from jax import lax
import numpy as np

D_MODEL = 1024
BATCH = 8
SEQ = 4096
DEPTH = 4

N_MIXERS = 2
ROPE_THETA = 10000.0
NORM_EPS = 1e-6

A_HEADS = 16
A_HEAD_DIM = 64
A_WIDTH = A_HEADS * A_HEAD_DIM
A_GROUPS = ((128, 1), (512, 4), (2048, 16))
A_N_GROUPS = len(A_GROUPS)
A_QKV_COLS = A_N_GROUPS * 3 * A_WIDTH
A_IN_COLS = A_QKV_COLS + A_WIDTH

B_HEADS = 16
B_KV_HEADS = 4
B_HEAD_DIM = 64
B_WIDTH = B_HEADS * B_HEAD_DIM
B_KV_WIDTH = B_KV_HEADS * B_HEAD_DIM
IDX_HEADS = 8
IDX_DIM = 64
TOPK_MAX = 256
Q_BLOCK = 128
B_SPLITS = (B_WIDTH, B_KV_WIDTH, B_KV_WIDTH, IDX_HEADS * IDX_DIM, IDX_DIM, IDX_HEADS, B_WIDTH)
B_IN_COLS = sum(B_SPLITS)

N_A_LAYERS = (DEPTH + 1) // 2
N_B_LAYERS = DEPTH // 2

kernel_name = 'hybrid_dilated_dsa_adaln_trunk'


def rms_norm(t, g):
    t32 = t.astype(jnp.float32)
    t32 = t32 * lax.rsqrt(jnp.mean(t32 * t32, axis=-1, keepdims=True) + NORM_EPS)
    return t32.astype(t.dtype) * g


def rope(t, pos):
    half = t.shape[-1] // 2
    inv_freq = ROPE_THETA ** (-jnp.arange(half, dtype=jnp.float32) / half)
    ang = pos.astype(jnp.float32)[..., None] * inv_freq
    cos = jnp.cos(ang)[:, :, None, :]
    sin = jnp.sin(ang)[:, :, None, :]
    t32 = t.astype(jnp.float32)
    t1, t2 = t32[..., :half], t32[..., half:]
    return jnp.concatenate([t1 * cos - t2 * sin, t2 * cos + t1 * sin], axis=-1).astype(t.dtype)


def banded_causal_attention(q, k, v, n_back):
    N, n, H, Dh = q.shape
    blk = n_back
    nb = -(-n // blk)
    pad = nb * blk - n
    qb = jnp.pad(q, ((0, 0), (0, pad), (0, 0), (0, 0))).reshape(N, nb, blk, H, Dh)
    kp = jnp.pad(k, ((0, 0), (blk, pad), (0, 0), (0, 0))).reshape(N, nb + 1, blk, H, Dh)
    vp = jnp.pad(v, ((0, 0), (blk, pad), (0, 0), (0, 0))).reshape(N, nb + 1, blk, H, Dh)
    kb = jnp.concatenate([kp[:, :-1], kp[:, 1:]], axis=2)
    vb = jnp.concatenate([vp[:, :-1], vp[:, 1:]], axis=2)
    s = jnp.einsum('nbqhd,nbkhd->nbhqk', qb, kb).astype(jnp.float32) * (Dh ** -0.5)
    qi = jnp.arange(blk)[:, None]
    kj = jnp.arange(2 * blk)[None, :]
    dist = blk + qi - kj
    key_pos = jnp.arange(nb)[:, None, None] * blk - blk + kj[None]
    mask = (dist >= 0) & (dist <= n_back) & (key_pos >= 0)
    s = jnp.where(mask[None, :, None], s, -jnp.inf)
    m = jnp.max(s, axis=-1, keepdims=True)
    e = jnp.exp(s - m)
    den = jnp.sum(e, axis=-1, keepdims=True)
    lse = (m + jnp.log(den))[..., 0]
    o = jnp.einsum('nbhqk,nbkhd->nbqhd', (e / den).astype(v.dtype), vb)
    o = o.reshape(N, nb * blk, H, Dh)[:, :n]
    lse = lse.transpose(0, 1, 3, 2).reshape(N, nb * blk, H)[:, :n]
    return o, lse


def dilated_group_attention(q, k, v, dilation, n_back):
    B, S, H, Dh = q.shape
    n = S // dilation

    def to_res(t):
        return t.reshape(B, n, dilation, H, Dh).transpose(0, 2, 1, 3, 4).reshape(B * dilation, n, H, Dh)

    o, lse = banded_causal_attention(to_res(q), to_res(k), to_res(v), n_back)
    o = o.reshape(B, dilation, n, H, Dh).transpose(0, 2, 1, 3, 4).reshape(B, S, H, Dh)
    lse = lse.reshape(B, dilation, n, H).transpose(0, 2, 1, 3).reshape(B, S, H)
    return o, lse


def dilated_mixer(h, pos, w_in, w_out):
    B, S, _ = h.shape
    proj = h @ w_in
    qkv = proj[..., :A_QKV_COLS].reshape(B, S, A_N_GROUPS, 3, A_HEADS, A_HEAD_DIM)
    gate = proj[..., A_QKV_COLS:]
    outs, lses = [], []
    for g, (window, dilation) in enumerate(A_GROUPS):
        q = rope(qkv[:, :, g, 0], pos)
        k = rope(qkv[:, :, g, 1], pos)
        v = qkv[:, :, g, 2]
        o, lse = dilated_group_attention(q, k, v, dilation, window // dilation)
        outs.append(o)
        lses.append(lse)
    o = jnp.stack(outs, axis=0)
    alpha = jax.nn.softmax(jnp.stack(lses, axis=0), axis=0)
    y = jnp.sum(alpha[..., None].astype(o.dtype) * o, axis=0).reshape(B, S, A_WIDTH)
    return (y * jax.nn.silu(gate)) @ w_out


def dsa_mixer(h, pos, w_in, w_out):
    B, S, _ = h.shape
    proj = h @ w_in
    cuts = list(np.cumsum(B_SPLITS)[:-1])
    q, k, v, qi, ki, wi, gate = jnp.split(proj, cuts, axis=-1)
    group = B_HEADS // B_KV_HEADS
    q = rope(q.reshape(B, S, B_HEADS, B_HEAD_DIM), pos).reshape(B, S, B_KV_HEADS, group, B_HEAD_DIM)
    k = rope(k.reshape(B, S, B_KV_HEADS, B_HEAD_DIM), pos)
    v = v.reshape(B, S, B_KV_HEADS, B_HEAD_DIM)
    qi = rope(qi.reshape(B, S, IDX_HEADS, IDX_DIM), pos)
    ki = rope(ki.reshape(B, S, 1, IDX_DIM), pos)[:, :, 0].astype(jnp.float32)
    wi = wi * (IDX_HEADS ** -0.5)
    k_sel = min(TOPK_MAX, S // 4)
    n_blk = S // Q_BLOCK
    key_idx = jnp.arange(S)

    def to_blocks(t):
        return t.reshape(B, n_blk, Q_BLOCK, *t.shape[2:]).swapaxes(0, 1)

    def block_fn(args):
        qb, qib, wib, start = args
        q_pos = start + jnp.arange(Q_BLOCK)
        sc = jnp.einsum('bqhd,bsd->bqhs', qib.astype(jnp.float32), ki) * (IDX_DIM ** -0.5)
        score = jnp.einsum('bqhs,bqh->bqs', jax.nn.relu(sc), wib.astype(jnp.float32))
        causal = key_idx[None, :] <= q_pos[:, None]
        score = jnp.where(causal[None], score, -jnp.inf)
        _, idx = lax.top_k(score, k_sel)
        valid = idx <= q_pos[None, :, None]
        ks = jax.vmap(lambda kb_, ib_: kb_[ib_])(k, idx)
        vs = jax.vmap(lambda vb_, ib_: vb_[ib_])(v, idx)
        s = jnp.einsum('bqgrd,bqkgd->bqgrk', qb, ks).astype(jnp.float32) * (B_HEAD_DIM ** -0.5)
        s = jnp.where(valid[:, :, None, None, :], s, -jnp.inf)
        p = jax.nn.softmax(s, axis=-1).astype(vs.dtype)
        o = jnp.einsum('bqgrk,bqkgd->bqgrd', p, vs)
        return o.reshape(B, Q_BLOCK, B_WIDTH)

    starts = jnp.arange(n_blk) * Q_BLOCK
    o = lax.map(block_fn, (to_blocks(q), to_blocks(qi), to_blocks(wi), starts))
    o = o.swapaxes(0, 1).reshape(B, S, B_WIDTH)
    return (o * jax.nn.silu(gate)) @ w_out


def setup_inputs(seed: int = 0) -> dict:
    key = jax.random.key(seed)
    ks = jax.random.split(key, 12)
    f32 = jnp.float32
    x = jax.random.normal(ks[0], (BATCH, SEQ, D_MODEL), f32)
    c = jax.random.normal(ks[1], (BATCH, D_MODEL), f32)
    offset = jax.random.randint(ks[2], (BATCH, 1), 0, 4096, dtype=jnp.int32)
    positions = (offset + jnp.arange(SEQ, dtype=jnp.int32)[None, :]).astype(jnp.int32)
    norm_g = 1.0 + 0.02 * jax.random.normal(ks[3], (DEPTH, D_MODEL), f32)
    ada_w = 0.02 * jax.random.normal(ks[4], (DEPTH, D_MODEL, 3 * D_MODEL), f32)
    ada_b = 0.02 * jax.random.normal(ks[5], (DEPTH, 3 * D_MODEL), f32)
    a_w_in = jax.random.normal(ks[6], (N_A_LAYERS, D_MODEL, A_IN_COLS), f32) * (D_MODEL ** -0.5)
    a_w_out = jax.random.normal(ks[7], (N_A_LAYERS, A_WIDTH, D_MODEL), f32) * (A_WIDTH ** -0.5)
    b_w_in = jax.random.normal(ks[8], (N_B_LAYERS, D_MODEL, B_IN_COLS), f32) * (D_MODEL ** -0.5)
    b_w_out = jax.random.normal(ks[9], (N_B_LAYERS, B_WIDTH, D_MODEL), f32) * (B_WIDTH ** -0.5)
    final_g = 1.0 + 0.02 * jax.random.normal(ks[10], (D_MODEL,), f32)
    return {'x': x, 'c': c, 'positions': positions, 'norm_g': norm_g, 'ada_w': ada_w,
            'ada_b': ada_b, 'a_w_in': a_w_in, 'a_w_out': a_w_out, 'b_w_in': b_w_in,
            'b_w_out': b_w_out, 'final_g': final_g}


def reference(x, c, positions, norm_g, ada_w, ada_b, a_w_in, a_w_out, b_w_in, b_w_out, final_g):
    h = x
    c_act = jax.nn.silu(c)
    for i in range(DEPTH):
        mod = c_act @ ada_w[i] + ada_b[i]
        shift, scale, gate = jnp.split(mod, 3, axis=-1)
        u = rms_norm(h, norm_g[i]) * (1.0 + scale[:, None, :]) + shift[:, None, :]
        if i % N_MIXERS == 0:
            y = dilated_mixer(u, positions, a_w_in[i // N_MIXERS], a_w_out[i // N_MIXERS])
        else:
            y = dsa_mixer(u, positions, b_w_in[i // N_MIXERS], b_w_out[i // N_MIXERS])
        h = h + gate[:, None, :] * y
    return rms_norm(h, final_g)
```

```python
import functools

import numpy as np
import jax
import jax.numpy as jnp
from jax import lax
from jax.experimental import pallas as pl
from jax.experimental.pallas import tpu as pltpu

F32 = jnp.float32
BF16 = jnp.bfloat16
I32 = jnp.int32

LANES = 128
HEAD_DIM = 64
HALF = HEAD_DIM // 2
ROPE_THETA = 10000.0
NORM_EPS = 1e-6
VMEM_LIMIT = 56 * 1024 * 1024

A_HEADS = 16
A_WIDTH = A_HEADS * HEAD_DIM
A_GROUPS = ((128, 1), (512, 4), (2048, 16))
A_BLK = 128

B_HEADS = 16
B_KV_HEADS = 4
B_WIDTH = B_HEADS * HEAD_DIM
IDX_HEADS = 8
TOPK_MAX = 256
Q_BLOCK = 128
KEY_TILE = 512
INT_MIN = -(2 ** 31)


def _cparams(sem):
    return pltpu.CompilerParams(dimension_semantics=sem, vmem_limit_bytes=VMEM_LIMIT)


def _adaln_kernel(c_ref, w_ref, b_ref, o_ref):
    c = c_ref[...]
    ca = (c * (1.0 / (1.0 + jnp.exp(-c)))).astype(BF16)
    acc = jnp.dot(ca, w_ref[...].astype(BF16), preferred_element_type=F32)
    o_ref[...] = acc + b_ref[...]


def adaln_mod(c, ada_w, ada_b):
    depth, d, d3 = ada_w.shape
    b = c.shape[0]
    tn = 1024
    return pl.pallas_call(
        _adaln_kernel,
        grid=(depth, d3 // tn),
        in_specs=[
            pl.BlockSpec((b, d), lambda i, j: (0, 0)),
            pl.BlockSpec((None, d, tn), lambda i, j: (i, 0, j)),
            pl.BlockSpec((None, 1, tn), lambda i, j: (i, 0, j)),
        ],
        out_specs=pl.BlockSpec((None, b, tn), lambda i, j: (i, 0, j)),
        out_shape=jax.ShapeDtypeStruct((depth, b, d3), F32),
        compiler_params=_cparams(("arbitrary", "arbitrary")),
        name="adaln_mod",
    )(c, ada_w, ada_b.reshape(depth, 1, d3))


def _rope_table_kernel(pos_ref, inv_ref, cos_ref, sin_ref):
    ang = pos_ref[...].astype(F32) * inv_ref[...]
    lane = lax.broadcasted_iota(I32, ang.shape, 1)
    first_half = (lane % HEAD_DIM) < HALF
    cos_ref[...] = jnp.cos(ang)
    s = jnp.sin(ang)
    sin_ref[...] = jnp.where(first_half, -s, s)


def rope_tables(positions):
    b, s = positions.shape
    inv_freq = ROPE_THETA ** (-jnp.arange(HALF, dtype=F32) / HALF)
    inv_lane = jnp.tile(inv_freq, LANES // HALF).reshape(1, LANES)
    ts = min(s, 1024)
    out = jax.ShapeDtypeStruct((b, s, LANES), F32)
    return pl.pallas_call(
        _rope_table_kernel,
        grid=(b, s // ts),
        in_specs=[
            pl.BlockSpec((None, ts, 1), lambda i, j: (i, j, 0)),
            pl.BlockSpec((1, LANES), lambda i, j: (0, 0)),
        ],
        out_specs=[pl.BlockSpec((None, ts, LANES), lambda i, j: (i, j, 0))] * 2,
        out_shape=[out, out],
        compiler_params=_cparams(("arbitrary", "arbitrary")),
        name="rope_tables",
    )(positions.reshape(b, s, 1), inv_lane)


def _rope_block(t, cos, sin_signed, first_half):
    partner = jnp.where(first_half, pltpu.roll(t, LANES - HALF, 1), pltpu.roll(t, HALF, 1))
    return t * cos + partner * sin_signed


def _in_proj_kernel(flags_ref, x_ref, g_ref, sc_ref, sh_ref, w_ref, cos_ref, sin_ref, o_ref, u_ref):
    j = pl.program_id(2)

    @pl.when(j == 0)
    def _():
        x = x_ref[...]
        ms = jnp.mean(x * x, axis=-1, keepdims=True)
        xn = x * lax.rsqrt(ms + NORM_EPS)
        u = xn * g_ref[...] * (1.0 + sc_ref[...]) + sh_ref[...]
        u_ref[...] = u.astype(BF16)

    acc = jnp.dot(u_ref[...], w_ref[...], preferred_element_type=F32)
    tn = acc.shape[1]

    @pl.when(flags_ref[j] == 1)
    def _():
        cos = cos_ref[...]
        sin = sin_ref[...]
        lane = lax.broadcasted_iota(I32, cos.shape, 1)
        first_half = (lane % HEAD_DIM) < HALF
        for k in range(tn // LANES):
            blk = acc[:, k * LANES:(k + 1) * LANES]
            o_ref[:, k * LANES:(k + 1) * LANES] = _rope_block(blk, cos, sin, first_half).astype(o_ref.dtype)

    @pl.when(flags_ref[j] != 1)
    def _():
        o_ref[...] = acc.astype(o_ref.dtype)


def in_proj(h, g, scale, shift, w, rope_flags, cos, sin, tn):
    b, s, d = h.shape
    n = w.shape[1]
    tm = min(s, 1024)
    grid_spec = pltpu.PrefetchScalarGridSpec(
        num_scalar_prefetch=1,
        grid=(b, s // tm, n // tn),
        in_specs=[
            pl.BlockSpec((None, tm, d), lambda bi, i, j, f: (bi, i, 0)),
            pl.BlockSpec((1, d), lambda bi, i, j, f: (0, 0)),
            pl.BlockSpec((None, 1, d), lambda bi, i, j, f: (bi, 0, 0)),
            pl.BlockSpec((None, 1, d), lambda bi, i, j, f: (bi, 0, 0)),
            pl.BlockSpec((d, tn), lambda bi, i, j, f: (0, j)),
            pl.BlockSpec((None, tm, LANES), lambda bi, i, j, f: (bi, i, 0)),
            pl.BlockSpec((None, tm, LANES), lambda bi, i, j, f: (bi, i, 0)),
        ],
        out_specs=pl.BlockSpec((None, tm, tn), lambda bi, i, j, f: (bi, i, j)),
        scratch_shapes=[pltpu.VMEM((tm, d), BF16)],
    )
    return pl.pallas_call(
        _in_proj_kernel,
        grid_spec=grid_spec,
        out_shape=jax.ShapeDtypeStruct((b, s, n), BF16),
        compiler_params=_cparams(("arbitrary", "arbitrary", "arbitrary")),
        name="in_proj",
    )(rope_flags, h, g.reshape(1, d), scale.reshape(b, 1, d), shift.reshape(b, 1, d), w, cos, sin)


def _out_proj_kernel(y_ref, w_ref, gate_ref, h_ref, o_ref):
    acc = jnp.dot(y_ref[...], w_ref[...], preferred_element_type=F32)
    o_ref[...] = h_ref[...] + gate_ref[...] * acc


def out_proj(y, w, gate, h):
    b, s, d = h.shape
    k = y.shape[-1]
    tm = min(s, 1024)
    return pl.pallas_call(
        _out_proj_kernel,
        grid=(b, s // tm),
        in_specs=[
            pl.BlockSpec((None, tm, k), lambda bi, i: (bi, i, 0)),
            pl.BlockSpec((k, d), lambda bi, i: (0, 0)),
            pl.BlockSpec((None, 1, d), lambda bi, i: (bi, 0, 0)),
            pl.BlockSpec((None, tm, d), lambda bi, i: (bi, i, 0)),
        ],
        out_specs=pl.BlockSpec((None, tm, d), lambda bi, i: (bi, i, 0)),
        out_shape=jax.ShapeDtypeStruct((b, s, d), F32),
        compiler_params=_cparams(("arbitrary", "arbitrary")),
        name="out_proj",
    )(y, w, gate.reshape(b, 1, d), h)


def _final_norm_kernel(x_ref, g_ref, o_ref):
    x = x_ref[...]
    ms = jnp.mean(x * x, axis=-1, keepdims=True)
    o_ref[...] = x * lax.rsqrt(ms + NORM_EPS) * g_ref[...]


def final_norm(h, g):
    b, s, d = h.shape
    tm = min(s, 1024)
    return pl.pallas_call(
        _final_norm_kernel,
        grid=(b, s // tm),
        in_specs=[
            pl.BlockSpec((None, tm, d), lambda bi, i: (bi, i, 0)),
            pl.BlockSpec((1, d), lambda bi, i: (0, 0)),
        ],
        out_specs=pl.BlockSpec((None, tm, d), lambda bi, i: (bi, i, 0)),
        out_shape=jax.ShapeDtypeStruct((b, s, d), F32),
        compiler_params=_cparams(("arbitrary", "arbitrary")),
        name="final_norm",
    )(h, g.reshape(1, d))


def _stack_heads(pair, first_head):
    zero = jnp.zeros_like(pair)
    return jnp.concatenate([jnp.where(first_head, pair, zero), jnp.where(first_head, zero, pair)], axis=0)


def _banded_kernel(q_ref, kp_ref, kc_ref, vp_ref, vc_ref, o_ref, lse_ref, *, tq):
    i = pl.program_id(2)
    nsub = tq // A_BLK
    lane = lax.broadcasted_iota(I32, (A_BLK, LANES), 1)
    first_head = lane < HEAD_DIM
    r2 = lax.broadcasted_iota(I32, (2 * A_BLK, 2 * A_BLK), 0) % A_BLK
    c2 = lax.broadcasted_iota(I32, (2 * A_BLK, 2 * A_BLK), 1)
    own_ok = (c2 >= A_BLK) & (c2 - A_BLK <= r2)
    for a in range(nsub):
        blk = i * nsub + a
        prev_shift = jnp.where(blk > 0, 0, 2 * A_BLK)
        mask = own_ok | ((c2 < A_BLK) & (c2 >= r2 + prev_shift))
        rows = slice(a * A_BLK, (a + 1) * A_BLK)
        lse_tile = jnp.zeros((A_BLK, LANES), F32)
        for p in range(A_HEADS // 2):
            cols = slice(p * LANES, (p + 1) * LANES)
            if a == 0:
                k_prev, v_prev = kp_ref[:, cols], vp_ref[:, cols]
            else:
                prow = slice((a - 1) * A_BLK, a * A_BLK)
                k_prev, v_prev = kc_ref[prow, cols], vc_ref[prow, cols]
            k2 = jnp.concatenate([k_prev, kc_ref[rows, cols]], axis=0)
            v2 = jnp.concatenate([v_prev, vc_ref[rows, cols]], axis=0)
            qs = _stack_heads(q_ref[rows, cols] * jnp.asarray(HEAD_DIM ** -0.5, BF16), first_head)
            s = lax.dot_general(qs, k2, (((1,), (1,)), ((), ())), preferred_element_type=F32)
            s = jnp.where(mask, s, -jnp.inf)
            m = jnp.max(s, axis=-1, keepdims=True)
            e = jnp.exp(s - m)
            den = jnp.sum(e, axis=-1, keepdims=True)
            pv = jnp.dot((e / den).astype(BF16), v2, preferred_element_type=F32)
            o_ref[rows, cols] = jnp.where(first_head, pv[:A_BLK], pv[A_BLK:])
            lse = m + jnp.log(den)
            lse_tile = jnp.where(lane == 2 * p, lse[:A_BLK], lse_tile)
            lse_tile = jnp.where(lane == 2 * p + 1, lse[A_BLK:], lse_tile)
        lse_ref[rows, :] = lse_tile


def banded_group_attention(proj, group, dilation):
    b, s, nc = proj.shape
    n = s // dilation
    tq = min(2 * A_BLK, n)
    view = proj.reshape(b, n, dilation * nc)
    cpb = nc // A_WIDTH
    qb, kb, vb = group * 3, group * 3 + 1, group * 3 + 2
    sub = tq // A_BLK

    def cur(off):
        return pl.BlockSpec((None, tq, A_WIDTH), lambda bi, r, i: (bi, i, r * cpb + off))

    def prev(off):
        return pl.BlockSpec((None, A_BLK, A_WIDTH),
                            lambda bi, r, i: (bi, jnp.maximum(i * sub - 1, 0), r * cpb + off))

    o, lse = pl.pallas_call(
        functools.partial(_banded_kernel, tq=tq),
        grid=(b, dilation, n // tq),
        in_specs=[cur(qb), prev(kb), cur(kb), prev(vb), cur(vb)],
        out_specs=[
            pl.BlockSpec((None, tq, A_WIDTH), lambda bi, r, i: (bi, i, r)),
            pl.BlockSpec((None, tq, LANES), lambda bi, r, i: (bi, i, r)),
        ],
        out_shape=[
            jax.ShapeDtypeStruct((b, n, dilation * A_WIDTH), F32),
            jax.ShapeDtypeStruct((b, n, dilation * LANES), F32),
        ],
        compiler_params=_cparams(("arbitrary", "arbitrary", "arbitrary")),
        name=f"banded_attn_g{group}",
    )(view, view, view, view, view)
    return o.reshape(b, s, A_WIDTH), lse.reshape(b, s, LANES)


def _expand_heads(x, expand):
    hi = x.astype(BF16)
    r1 = x - hi.astype(F32)
    mid = r1.astype(BF16)
    lo = (r1 - mid.astype(F32)).astype(BF16)
    out = jnp.dot(hi, expand, preferred_element_type=F32)
    out += jnp.dot(mid, expand, preferred_element_type=F32)
    out += jnp.dot(lo, expand, preferred_element_type=F32)
    return out


def _silu(x):
    return x * (1.0 / (1.0 + jnp.exp(-x)))


def _merge_kernel(o0_ref, o1_ref, o2_ref, l0_ref, l1_ref, l2_ref, gate_ref, ex_ref, y_ref):
    l0, l1, l2 = l0_ref[...], l1_ref[...], l2_ref[...]
    m = jnp.maximum(jnp.maximum(l0, l1), l2)
    e0, e1, e2 = jnp.exp(l0 - m), jnp.exp(l1 - m), jnp.exp(l2 - m)
    den = e0 + e1 + e2
    ex = ex_ref[...]
    y = _expand_heads(e0 / den, ex) * o0_ref[...]
    y += _expand_heads(e1 / den, ex) * o1_ref[...]
    y += _expand_heads(e2 / den, ex) * o2_ref[...]
    y_ref[...] = (y * _silu(gate_ref[...].astype(F32))).astype(y_ref.dtype)


def merge_groups(outs, lses, proj, gate_block):
    b, s, w = outs[0].shape
    tm = min(s, 512)
    expand = (jnp.arange(LANES)[:, None] == (jnp.arange(w)[None, :] // HEAD_DIM)).astype(BF16)
    ospec = pl.BlockSpec((None, tm, w), lambda bi, i: (bi, i, 0))
    lspec = pl.BlockSpec((None, tm, LANES), lambda bi, i: (bi, i, 0))
    return pl.pallas_call(
        _merge_kernel,
        grid=(b, s // tm),
        in_specs=[ospec, ospec, ospec, lspec, lspec, lspec,
                  pl.BlockSpec((None, tm, w), lambda bi, i: (bi, i, gate_block)),
                  pl.BlockSpec((LANES, w), lambda bi, i: (0, 0))],
        out_specs=ospec,
        out_shape=jax.ShapeDtypeStruct((b, s, w), BF16),
        compiler_params=_cparams(("arbitrary", "arbitrary")),
        name="merge_groups",
    )(*outs, *lses, proj, expand)


def dilated_mixer(h, g, scale, shift, w_in, cos, sin):
    n_groups = len(A_GROUPS)
    tn = 512
    per_block = A_WIDTH // tn
    flags = []
    for _ in range(n_groups):
        flags += [1] * (2 * per_block) + [0] * per_block
    flags += [0] * per_block
    proj = in_proj(h, g, scale, shift, w_in.astype(BF16), jnp.asarray(flags, I32), cos, sin, tn)
    outs, lses = [], []
    for gi, (window, dilation) in enumerate(A_GROUPS):
        assert window // dilation == A_BLK
        o, lse = banded_group_attention(proj, gi, dilation)
        outs.append(o)
        lses.append(lse)
    return merge_groups(outs, lses, proj, 3 * n_groups)


B_COLS = 3584
B_ROPE_FLAGS = (1, 1, 0, 0, 1, 1, 0)


def _dsa_head_perm():
    group = B_HEADS // B_KV_HEADS
    order = []
    for g2 in range(B_KV_HEADS // 2):
        for r in range(group):
            order += [(2 * g2) * group + r, (2 * g2 + 1) * group + r]
    return np.asarray(order)


def _dsa_weights(w_in, w_out):
    d = w_in.shape[0]
    cuts = np.cumsum((B_WIDTH, B_KV_HEADS * HEAD_DIM, B_KV_HEADS * HEAD_DIM, IDX_HEADS * HEAD_DIM,
                      HEAD_DIM, IDX_HEADS, B_WIDTH))[:-1]
    wq, wk, wv, wqi, wki, wwi, wg = jnp.split(w_in, cuts, axis=1)
    cols = (_dsa_head_perm()[:, None] * HEAD_DIM + np.arange(HEAD_DIM)[None, :]).reshape(-1)
    zeros = lambda n: jnp.zeros((d, n), w_in.dtype)
    w = jnp.concatenate([wq[:, cols], wg[:, cols], wqi, wk, wki, wki, zeros(LANES),
                         wwi, zeros(2 * LANES - IDX_HEADS), wv], axis=1)
    assert w.shape[1] == B_COLS
    return w.astype(BF16), w_out[cols, :].astype(BF16)


def _sortable(x):
    bits = pltpu.bitcast(x, I32)
    return bits ^ ((bits >> 31) & jnp.int32(0x7FFFFFFF))


def _dsa_kernel(q_ref, gate_ref, qi_ref, k_ref, ki_ref, wi_ref, v_ref, y_ref,
                keys_ref, bias_ref, s_ref, *, k_sel):
    i = pl.program_id(1)
    n_tiles = (i * Q_BLOCK) // KEY_TILE + 1
    lane = lax.broadcasted_iota(I32, (Q_BLOCK, LANES), 1)
    first_head = lane < HEAD_DIM

    w_t = (wi_ref[...].astype(F32) * (IDX_HEADS ** -0.5 * HEAD_DIM ** -0.5)).T
    qi_stacked = [_stack_heads(qi_ref[:, p * LANES:(p + 1) * LANES], first_head)
                  for p in range(IDX_HEADS // 2)]
    key_row = lax.broadcasted_iota(I32, (KEY_TILE, Q_BLOCK), 0)
    q_pos = i * Q_BLOCK + lax.broadcasted_iota(I32, (KEY_TILE, Q_BLOCK), 1)

    def score_tile(t, carry):
        base = pl.multiple_of(t * KEY_TILE, KEY_TILE)
        kk = ki_ref[pl.ds(base, KEY_TILE), :]
        score = jnp.zeros((KEY_TILE, Q_BLOCK), F32)
        for p in range(IDX_HEADS // 2):
            sc = lax.dot_general(kk, qi_stacked[p], (((1,), (1,)), ((), ())),
                                 preferred_element_type=F32)
            score += jnp.maximum(sc[:, :Q_BLOCK], 0.0) * w_t[2 * p:2 * p + 1, :]
            score += jnp.maximum(sc[:, Q_BLOCK:], 0.0) * w_t[2 * p + 1:2 * p + 2, :]
        causal = (key_row + base) <= q_pos
        score = jnp.where(causal, score + 0.0, -jnp.inf)
        keys_ref[pl.ds(base, KEY_TILE), :] = _sortable(score)
        return carry

    lax.fori_loop(0, n_tiles, score_tile, 0)

    def count_ge(cand):
        def body(t, acc):
            base = pl.multiple_of(t * KEY_TILE, KEY_TILE)
            ge = (keys_ref[pl.ds(base, KEY_TILE), :] >= cand).astype(I32)
            return acc + jnp.sum(ge.reshape(KEY_TILE // 8, 8, Q_BLOCK), axis=0)
        acc = lax.fori_loop(0, n_tiles, body, jnp.zeros((8, Q_BLOCK), I32))
        return jnp.sum(acc, axis=0, keepdims=True)

    zero = jnp.zeros((1, Q_BLOCK), I32)
    thr0 = jnp.where(count_ge(zero) >= k_sel, zero, jnp.full((1, Q_BLOCK), INT_MIN, I32))

    def bit_step(it, thr):
        cand = thr + jnp.left_shift(jnp.int32(1), 30 - it)
        return jnp.where(count_ge(cand) >= k_sel, cand, thr)

    thr = lax.fori_loop(0, 31, bit_step, thr0)

    def bias_tile(t, carry):
        base = pl.multiple_of(t * KEY_TILE, KEY_TILE)
        sel = (keys_ref[pl.ds(base, KEY_TILE), :] >= thr) & ((key_row + base) <= q_pos)
        bias_t = jnp.where(sel, 0.0, -jnp.inf).astype(F32)
        for c in range(KEY_TILE // LANES):
            col = pl.multiple_of(base + c * LANES, LANES)
            bias_ref[:, pl.ds(col, LANES)] = bias_t[c * LANES:(c + 1) * LANES, :].T
        return carry

    lax.fori_loop(0, n_tiles, bias_tile, 0)

    for p in range(B_HEADS // 2):
        cols = slice(p * LANES, (p + 1) * LANES)
        kv_cols = slice((p // (B_HEADS // B_KV_HEADS)) * LANES, (p // (B_HEADS // B_KV_HEADS) + 1) * LANES)
        qs = _stack_heads(q_ref[:, cols] * jnp.asarray(HEAD_DIM ** -0.5, BF16), first_head)

        def qk_tile(t, m_lane):
            base = pl.multiple_of(t * KEY_TILE, KEY_TILE)
            kt = k_ref[pl.ds(base, KEY_TILE), kv_cols]
            s = lax.dot_general(qs, kt, (((1,), (1,)), ((), ())), preferred_element_type=F32)
            bias = bias_ref[:, pl.ds(base, KEY_TILE)]
            s = s + jnp.concatenate([bias, bias], axis=0)
            s_ref[:, pl.ds(base, KEY_TILE)] = s
            for c in range(KEY_TILE // LANES):
                m_lane = jnp.maximum(m_lane, s[:, c * LANES:(c + 1) * LANES])
            return m_lane

        m_lane = lax.fori_loop(0, n_tiles, qk_tile, jnp.full((2 * Q_BLOCK, LANES), -jnp.inf, F32))
        m = jnp.max(m_lane, axis=-1, keepdims=True)

        def pv_tile(t, carry):
            acc, l_lane = carry
            base = pl.multiple_of(t * KEY_TILE, KEY_TILE)
            e = jnp.exp(s_ref[:, pl.ds(base, KEY_TILE)] - m)
            for c in range(KEY_TILE // LANES):
                l_lane = l_lane + e[:, c * LANES:(c + 1) * LANES]
            acc = acc + jnp.dot(e.astype(BF16), v_ref[pl.ds(base, KEY_TILE), kv_cols],
                                preferred_element_type=F32)
            return acc, l_lane

        zeros = jnp.zeros((2 * Q_BLOCK, LANES), F32)
        acc, l_lane = lax.fori_loop(0, n_tiles, pv_tile, (zeros, zeros))
        o = acc / jnp.sum(l_lane, axis=-1, keepdims=True)
        o_pair = jnp.where(first_head, o[:Q_BLOCK], o[Q_BLOCK:])
        y_ref[:, cols] = (o_pair * _silu(gate_ref[:, cols].astype(F32))).astype(y_ref.dtype)


def dsa_attention(proj):
    b, s, _ = proj.shape
    k_sel = min(TOPK_MAX, s // 4)
    assert s % KEY_TILE == 0 and k_sel <= KEY_TILE
    kvw = B_KV_HEADS * HEAD_DIM

    def qblock(width, idx):
        return pl.BlockSpec((None, Q_BLOCK, width), lambda bi, i: (bi, i, idx))

    def full(width, idx):
        return pl.BlockSpec((None, s, width), lambda bi, i: (bi, 0, idx))

    return pl.pallas_call(
        functools.partial(_dsa_kernel, k_sel=k_sel),
        grid=(b, s // Q_BLOCK),
        in_specs=[
            qblock(B_WIDTH, 0),
            qblock(B_WIDTH, 1),
            qblock(IDX_HEADS * HEAD_DIM, 4),
            full(kvw, 10),
            full(LANES, 22),
            qblock(LANES, 24),
            full(kvw, 13),
        ],
        out_specs=qblock(B_WIDTH, 0),
        out_shape=jax.ShapeDtypeStruct((b, s, B_WIDTH), BF16),
        scratch_shapes=[
            pltpu.VMEM((s, Q_BLOCK), I32),
            pltpu.VMEM((Q_BLOCK, s), F32),
            pltpu.VMEM((2 * Q_BLOCK, s), F32),
        ],
        compiler_params=_cparams(("arbitrary", "arbitrary")),
        name="dsa_attention",
    )(proj, proj, proj, proj, proj, proj, proj)


def dsa_mixer(h, g, scale, shift, w_in_packed, cos, sin):
    proj = in_proj(h, g, scale, shift, w_in_packed, jnp.asarray(B_ROPE_FLAGS, I32), cos, sin, 512)
    return dsa_attention(proj)


def kernel(x, c, positions, norm_g, ada_w, ada_b, a_w_in, a_w_out, b_w_in, b_w_out, final_g):
    depth = norm_g.shape[0]
    d = x.shape[-1]
    mod = adaln_mod(c, ada_w, ada_b)
    cos, sin = rope_tables(positions)
    h = x
    for i in range(depth):
        shift, scale, gate = mod[i, :, :d], mod[i, :, d:2 * d], mod[i, :, 2 * d:]
        if i % 2 == 0:
            y = dilated_mixer(h, norm_g[i], scale, shift, a_w_in[i // 2], cos, sin)
            w_out = a_w_out[i // 2].astype(BF16)
        else:
            w_in, w_out = _dsa_weights(b_w_in[i // 2], b_w_out[i // 2])
            y = dsa_mixer(h, norm_g[i], scale, shift, w_in, cos, sin)
        h = out_proj(y, w_out, gate, h)
    return final_norm(h, final_g)
```

```python
import functools

import numpy as np
import jax
import jax.numpy as jnp
from jax import lax
from jax.experimental import pallas as pl
from jax.experimental.pallas import tpu as pltpu

F32 = jnp.float32
BF16 = jnp.bfloat16
I32 = jnp.int32

LANES = 128
HEAD_DIM = 64
HALF = HEAD_DIM // 2
ROPE_THETA = 10000.0
NORM_EPS = 1e-6
VMEM_LIMIT = 56 * 1024 * 1024
ROW_CHUNK = 256

A_HEADS = 16
A_WIDTH = A_HEADS * HEAD_DIM
A_GROUPS = ((128, 1), (512, 4), (2048, 16))
A_BLK = 128

B_HEADS = 16
B_KV_HEADS = 4
B_WIDTH = B_HEADS * HEAD_DIM
IDX_HEADS = 8
TOPK_MAX = 256
Q_BLOCK = 128
KEY_TILE = 512
INT_MIN = -(2 ** 31)
F32_MIN = float(np.finfo(np.float32).min)


def _cparams(sem):
    return pltpu.CompilerParams(dimension_semantics=sem, vmem_limit_bytes=VMEM_LIMIT)


def _adaln_kernel(c_ref, w_ref, b_ref, o_ref):
    c = c_ref[...]
    ca = (c * (1.0 / (1.0 + jnp.exp(-c)))).astype(BF16)
    acc = jnp.dot(ca, w_ref[...].astype(BF16), preferred_element_type=F32)
    o_ref[...] = acc + b_ref[...]


def adaln_mod(c, ada_w, ada_b):
    depth, d, d3 = ada_w.shape
    b = c.shape[0]
    tn = 1024
    return pl.pallas_call(
        _adaln_kernel,
        grid=(depth, d3 // tn),
        in_specs=[
            pl.BlockSpec((b, d), lambda i, j: (0, 0)),
            pl.BlockSpec((None, d, tn), lambda i, j: (i, 0, j)),
            pl.BlockSpec((None, 1, tn), lambda i, j: (i, 0, j)),
        ],
        out_specs=pl.BlockSpec((None, b, tn), lambda i, j: (i, 0, j)),
        out_shape=jax.ShapeDtypeStruct((depth, b, d3), F32),
        compiler_params=_cparams(("arbitrary", "arbitrary")),
        name="adaln_mod",
    )(c, ada_w, ada_b.reshape(depth, 1, d3))


def _rope_table_kernel(pos_ref, inv_ref, cos_ref, sin_ref):
    ang = pos_ref[...].astype(F32) * inv_ref[...]
    lane = lax.broadcasted_iota(I32, ang.shape, 1)
    first_half = (lane % HEAD_DIM) < HALF
    cos_ref[...] = jnp.cos(ang)
    s = jnp.sin(ang)
    sin_ref[...] = jnp.where(first_half, -s, s)


def rope_tables(positions):
    b, s = positions.shape
    inv_freq = ROPE_THETA ** (-jnp.arange(HALF, dtype=F32) / HALF)
    inv_lane = jnp.tile(inv_freq, LANES // HALF).reshape(1, LANES)
    ts = min(s, 1024)
    out = jax.ShapeDtypeStruct((b, s, LANES), F32)
    return pl.pallas_call(
        _rope_table_kernel,
        grid=(b, s // ts),
        in_specs=[
            pl.BlockSpec((None, ts, 1), lambda i, j: (i, j, 0)),
            pl.BlockSpec((1, LANES), lambda i, j: (0, 0)),
        ],
        out_specs=[pl.BlockSpec((None, ts, LANES), lambda i, j: (i, j, 0))] * 2,
        out_shape=[out, out],
        compiler_params=_cparams(("arbitrary", "arbitrary")),
        name="rope_tables",
    )(positions.reshape(b, s, 1), inv_lane)


def _rope_block(t, cos, sin_signed, first_half):
    partner = jnp.where(first_half, pltpu.roll(t, LANES - HALF, 1), pltpu.roll(t, HALF, 1))
    return t * cos + partner * sin_signed


def _in_proj_kernel(flags_ref, x_ref, g_ref, sc_ref, sh_ref, w_ref, cos_ref, sin_ref, o_ref,
                    u_ref, *stage, dilation):
    j = pl.program_id(2)
    tm = u_ref.shape[0]
    tn = w_ref.shape[1]
    nblk = tn // LANES
    per = ROW_CHUNK // dilation

    @pl.when(j == 0)
    def _():
        x = x_ref[...]
        ms = jnp.mean(x * x, axis=-1, keepdims=True)
        xn = x * lax.rsqrt(ms + NORM_EPS)
        u = xn * g_ref[...] * (1.0 + sc_ref[...]) + sh_ref[...]
        u_ref[...] = u.astype(BF16)

    def run(rope):
        lane = lax.broadcasted_iota(I32, (per, LANES), 1)
        first_half = (lane % HEAD_DIM) < HALF
        for c in range(tm // ROW_CHUNK):
            acc = jnp.dot(u_ref[c * ROW_CHUNK:(c + 1) * ROW_CHUNK, :], w_ref[...],
                          preferred_element_type=F32)
            if dilation > 1:
                for k in range(nblk):
                    stage[0][c, k] = acc[:, k * LANES:(k + 1) * LANES]
            for r in range(dilation):
                rows = slice(c * per, (c + 1) * per)
                if rope:
                    cos = cos_ref[rows, :] if dilation == 1 else cos_ref[r, rows, :]
                    sin = sin_ref[rows, :] if dilation == 1 else sin_ref[r, rows, :]
                for k in range(nblk):
                    cols = slice(k * LANES, (k + 1) * LANES)
                    if dilation > 1:
                        blk = stage[0][c, k, pl.ds(r, per, stride=dilation), :]
                    else:
                        blk = acc[:, cols]
                    if rope:
                        blk = _rope_block(blk, cos, sin, first_half)
                    if dilation > 1:
                        o_ref[r, rows, cols] = blk.astype(o_ref.dtype)
                    else:
                        o_ref[rows, cols] = blk.astype(o_ref.dtype)

    @pl.when(flags_ref[j] == 1)
    def _():
        run(True)

    @pl.when(flags_ref[j] != 1)
    def _():
        run(False)


def in_proj(h, g, scale, shift, w, rope_flags, cos, sin, tn, dilation=1):
    b, s, d = h.shape
    n = w.shape[1]
    tm = min(s, 1024)
    assert tm % ROW_CHUNK == 0 and ROW_CHUNK % (16 * dilation) == 0
    if dilation == 1:
        tab_spec = pl.BlockSpec((None, tm, LANES), lambda bi, i, j, f: (bi, i, 0))
        out_spec = pl.BlockSpec((None, tm, tn), lambda bi, i, j, f: (bi, i, j))
        out_shape = jax.ShapeDtypeStruct((b, s, n), BF16)
        scratch = [pltpu.VMEM((tm, d), BF16)]
    else:
        tab_spec = pl.BlockSpec((None, dilation, tm // dilation, LANES), lambda bi, i, j, f: (bi, 0, i, 0))
        out_spec = pl.BlockSpec((None, dilation, tm // dilation, tn), lambda bi, i, j, f: (bi, 0, i, j))
        out_shape = jax.ShapeDtypeStruct((b, dilation, s // dilation, n), BF16)
        scratch = [pltpu.VMEM((tm, d), BF16),
                   pltpu.VMEM((tm // ROW_CHUNK, tn // LANES, ROW_CHUNK, LANES), F32)]
    grid_spec = pltpu.PrefetchScalarGridSpec(
        num_scalar_prefetch=1,
        grid=(b, s // tm, n // tn),
        in_specs=[
            pl.BlockSpec((None, tm, d), lambda bi, i, j, f: (bi, i, 0)),
            pl.BlockSpec((1, d), lambda bi, i, j, f: (0, 0)),
            pl.BlockSpec((None, 1, d), lambda bi, i, j, f: (bi, 0, 0)),
            pl.BlockSpec((None, 1, d), lambda bi, i, j, f: (bi, 0, 0)),
            pl.BlockSpec((d, tn), lambda bi, i, j, f: (0, j)),
            tab_spec, tab_spec,
        ],
        out_specs=out_spec,
        scratch_shapes=scratch,
    )
    return pl.pallas_call(
        functools.partial(_in_proj_kernel, dilation=dilation),
        grid_spec=grid_spec,
        out_shape=out_shape,
        compiler_params=_cparams(("arbitrary", "arbitrary", "arbitrary")),
        name=f"in_proj_d{dilation}",
    )(rope_flags, h, g.reshape(1, d), scale.reshape(b, 1, d), shift.reshape(b, 1, d), w, cos, sin)


def _out_proj_kernel(y_ref, w_ref, gate_ref, h_ref, o_ref):
    acc = jnp.dot(y_ref[...], w_ref[...], preferred_element_type=F32)
    o_ref[...] = h_ref[...] + gate_ref[...] * acc


def out_proj(y, w, gate, h):
    b, s, d = h.shape
    k = y.shape[-1]
    tm = min(s, 1024)
    return pl.pallas_call(
        _out_proj_kernel,
        grid=(b, s // tm),
        in_specs=[
            pl.BlockSpec((None, tm, k), lambda bi, i: (bi, i, 0)),
            pl.BlockSpec((k, d), lambda bi, i: (0, 0)),
            pl.BlockSpec((None, 1, d), lambda bi, i: (bi, 0, 0)),
            pl.BlockSpec((None, tm, d), lambda bi, i: (bi, i, 0)),
        ],
        out_specs=pl.BlockSpec((None, tm, d), lambda bi, i: (bi, i, 0)),
        out_shape=jax.ShapeDtypeStruct((b, s, d), F32),
        compiler_params=_cparams(("arbitrary", "arbitrary")),
        name="out_proj",
    )(y, w, gate.reshape(b, 1, d), h)


def _final_norm_kernel(x_ref, g_ref, o_ref):
    x = x_ref[...]
    ms = jnp.mean(x * x, axis=-1, keepdims=True)
    o_ref[...] = x * lax.rsqrt(ms + NORM_EPS) * g_ref[...]


def final_norm(h, g):
    b, s, d = h.shape
    tm = min(s, 1024)
    return pl.pallas_call(
        _final_norm_kernel,
        grid=(b, s // tm),
        in_specs=[
            pl.BlockSpec((None, tm, d), lambda bi, i: (bi, i, 0)),
            pl.BlockSpec((1, d), lambda bi, i: (0, 0)),
        ],
        out_specs=pl.BlockSpec((None, tm, d), lambda bi, i: (bi, i, 0)),
        out_shape=jax.ShapeDtypeStruct((b, s, d), F32),
        compiler_params=_cparams(("arbitrary", "arbitrary")),
        name="final_norm",
    )(h, g.reshape(1, d))


def _stack_heads(pair, first_head):
    zero = jnp.zeros_like(pair)
    return jnp.concatenate([jnp.where(first_head, pair, zero), jnp.where(first_head, zero, pair)], axis=0)


def _banded_kernel(q_ref, kp_ref, kc_ref, vp_ref, vc_ref, o_ref, lse_ref, *, tq):
    i = pl.program_id(2)
    nsub = tq // A_BLK
    lane = lax.broadcasted_iota(I32, (A_BLK, LANES), 1)
    first_head = lane < HEAD_DIM
    r2 = lax.broadcasted_iota(I32, (2 * A_BLK, 2 * A_BLK), 0) % A_BLK
    c2 = lax.broadcasted_iota(I32, (2 * A_BLK, 2 * A_BLK), 1)
    own_ok = (c2 >= A_BLK) & (c2 - A_BLK <= r2)
    for a in range(nsub):
        blk = i * nsub + a
        prev_shift = jnp.where(blk > 0, 0, 2 * A_BLK)
        mask = own_ok | ((c2 < A_BLK) & (c2 >= r2 + prev_shift))
        rows = slice(a * A_BLK, (a + 1) * A_BLK)
        lse_tile = jnp.zeros((A_BLK, LANES), F32)
        for p in range(A_HEADS // 2):
            cols = slice(p * LANES, (p + 1) * LANES)
            if a == 0:
                k_prev, v_prev = kp_ref[:, cols], vp_ref[:, cols]
            else:
                prow = slice((a - 1) * A_BLK, a * A_BLK)
                k_prev, v_prev = kc_ref[prow, cols], vc_ref[prow, cols]
            k2 = jnp.concatenate([k_prev, kc_ref[rows, cols]], axis=0)
            v2 = jnp.concatenate([v_prev, vc_ref[rows, cols]], axis=0)
            qs = _stack_heads(q_ref[rows, cols] * jnp.asarray(HEAD_DIM ** -0.5, BF16), first_head)
            s = lax.dot_general(qs, k2, (((1,), (1,)), ((), ())), preferred_element_type=F32)
            s = jnp.where(mask, s, -jnp.inf)
            m = jnp.max(s, axis=-1, keepdims=True)
            e = jnp.exp(s - m)
            den = jnp.sum(e, axis=-1, keepdims=True)
            pv = jnp.dot(e.astype(BF16), v2, preferred_element_type=F32) * (1.0 / den)
            o_ref[rows, cols] = jnp.where(first_head, pv[:A_BLK], pv[A_BLK:])
            lse = m + jnp.log(den)
            lse_tile = jnp.where(lane == 2 * p, lse[:A_BLK], lse_tile)
            lse_tile = jnp.where(lane == 2 * p + 1, lse[A_BLK:], lse_tile)
        lse_ref[rows, :] = lse_tile


def banded_group_attention(qkv):
    b, d, n, _ = qkv.shape
    tq = min(2 * A_BLK, n)
    sub = tq // A_BLK

    def cur(blk):
        return pl.BlockSpec((None, None, tq, A_WIDTH), lambda bi, r, i: (bi, r, i, blk))

    def prev(blk):
        return pl.BlockSpec((None, None, A_BLK, A_WIDTH),
                            lambda bi, r, i: (bi, r, jnp.maximum(i * sub - 1, 0), blk))

    return pl.pallas_call(
        functools.partial(_banded_kernel, tq=tq),
        grid=(b, d, n // tq),
        in_specs=[cur(0), prev(1), cur(1), prev(2), cur(2)],
        out_specs=[
            pl.BlockSpec((None, None, tq, A_WIDTH), lambda bi, r, i: (bi, r, i, 0)),
            pl.BlockSpec((None, None, tq, LANES), lambda bi, r, i: (bi, r, i, 0)),
        ],
        out_shape=[
            jax.ShapeDtypeStruct((b, d, n, A_WIDTH), F32),
            jax.ShapeDtypeStruct((b, d, n, LANES), F32),
        ],
        compiler_params=_cparams(("arbitrary", "arbitrary", "arbitrary")),
        name=f"banded_attn_d{d}",
    )(qkv, qkv, qkv, qkv, qkv)


def _expand_heads(x, expand):
    hi = x.astype(BF16)
    r1 = x - hi.astype(F32)
    mid = r1.astype(BF16)
    lo = (r1 - mid.astype(F32)).astype(BF16)
    out = jnp.dot(hi, expand, preferred_element_type=F32)
    out += jnp.dot(mid, expand, preferred_element_type=F32)
    out += jnp.dot(lo, expand, preferred_element_type=F32)
    return out


def _merge_kernel(o0_ref, o1_ref, o2_ref, l0_ref, l1_ref, l2_ref, gate_ref, ex_ref, y_ref,
                  ot_ref, lt_ref, *, dilations):
    o_refs, l_refs = (o0_ref, o1_ref, o2_ref), (l0_ref, l1_ref, l2_ref)
    tm = y_ref.shape[0]
    nblk = y_ref.shape[1] // LANES
    for g, d in enumerate(dilations):
        if d == 1:
            continue
        per = tm // d
        for r in range(d):
            lt_ref[g, pl.ds(r, per, stride=d), :] = l_refs[g][r]
            for k in range(nblk):
                ot_ref[g, k, pl.ds(r, per, stride=d), :] = o_refs[g][r, :, k * LANES:(k + 1) * LANES]
    lses = [l_refs[g][0] if d == 1 else lt_ref[g] for g, d in enumerate(dilations)]
    m = jnp.maximum(jnp.maximum(lses[0], lses[1]), lses[2])
    es = [jnp.exp(l - m) for l in lses]
    inv = 1.0 / (es[0] + es[1] + es[2])
    ex = ex_ref[...]
    alphas = [_expand_heads(e * inv, ex) for e in es]
    for k in range(nblk):
        cols = slice(k * LANES, (k + 1) * LANES)
        y = jnp.zeros((tm, LANES), F32)
        for g, d in enumerate(dilations):
            og = o_refs[g][0, :, cols] if d == 1 else ot_ref[g, k]
            y += alphas[g][:, cols] * og
        gate = gate_ref[:, cols].astype(F32)
        y_ref[:, cols] = (y * gate * (1.0 / (1.0 + jnp.exp(-gate)))).astype(y_ref.dtype)


def merge_groups(outs, lses, gate):
    b, s, w = gate.shape
    tm = min(s, 256)
    dil = tuple(o.shape[1] for o in outs)
    expand = (jnp.arange(LANES)[:, None] == (jnp.arange(w)[None, :] // HEAD_DIM)).astype(BF16)

    def cm(d, width):
        return pl.BlockSpec((None, d, tm // d, width), lambda bi, i: (bi, 0, i, 0))

    tspec = pl.BlockSpec((None, tm, w), lambda bi, i: (bi, i, 0))
    return pl.pallas_call(
        functools.partial(_merge_kernel, dilations=dil),
        grid=(b, s // tm),
        in_specs=[cm(d, w) for d in dil] + [cm(d, LANES) for d in dil]
                 + [tspec, pl.BlockSpec((LANES, w), lambda bi, i: (0, 0))],
        out_specs=tspec,
        out_shape=jax.ShapeDtypeStruct((b, s, w), BF16),
        scratch_shapes=[pltpu.VMEM((len(dil), w // LANES, tm, LANES), F32),
                        pltpu.VMEM((len(dil), tm, LANES), F32)],
        compiler_params=_cparams(("arbitrary", "arbitrary")),
        name="merge_groups",
    )(*outs, *lses, gate, expand)


def _class_major(t, d):
    b, s = t.shape[:2]
    return jnp.swapaxes(t.reshape(b, s // d, d, *t.shape[2:]), 1, 2)


def class_major_rope_tables(positions, dilation):
    pos_cm = _class_major(positions, dilation)
    cos, sin = rope_tables(pos_cm.reshape(positions.shape))
    shape = pos_cm.shape + (LANES,)
    return cos.reshape(shape), sin.reshape(shape)


def dilated_mixer(h, g, scale, shift, w_in, tables):
    tn = 512
    per = A_WIDTH // tn
    qkv_flags = jnp.asarray([1] * (2 * per) + [0] * per, I32)
    w = w_in.astype(BF16)
    outs, lses = [], []
    for gi, (window, dilation) in enumerate(A_GROUPS):
        assert window // dilation == A_BLK
        wg = w[:, gi * 3 * A_WIDTH:(gi + 1) * 3 * A_WIDTH]
        qkv = in_proj(h, g, scale, shift, wg, qkv_flags, *tables[dilation], tn, dilation)
        if dilation == 1:
            qkv = qkv[:, None]
        o, lse = banded_group_attention(qkv)
        outs.append(o)
        lses.append(lse)
    gate = in_proj(h, g, scale, shift, w[:, len(A_GROUPS) * 3 * A_WIDTH:], jnp.zeros((per,), I32),
                   *tables[1], tn)
    return merge_groups(outs, lses, gate)


B_COLS = 3584
B_ROPE_FLAGS = (1, 1, 0, 0, 1, 1, 0)


def _dsa_head_perm():
    group = B_HEADS // B_KV_HEADS
    order = []
    for g2 in range(B_KV_HEADS // 2):
        for r in range(group):
            order += [(2 * g2) * group + r, (2 * g2 + 1) * group + r]
    return np.asarray(order)


def _dsa_weights(w_in, w_out):
    d = w_in.shape[0]
    cuts = np.cumsum((B_WIDTH, B_KV_HEADS * HEAD_DIM, B_KV_HEADS * HEAD_DIM, IDX_HEADS * HEAD_DIM,
                      HEAD_DIM, IDX_HEADS, B_WIDTH))[:-1]
    wq, wk, wv, wqi, wki, wwi, wg = jnp.split(w_in, cuts, axis=1)
    cols = (_dsa_head_perm()[:, None] * HEAD_DIM + np.arange(HEAD_DIM)[None, :]).reshape(-1)
    zeros = lambda n: jnp.zeros((d, n), w_in.dtype)
    wq = wq * (np.log2(np.e) * HEAD_DIM ** -0.5)
    w = jnp.concatenate([wq[:, cols], wg[:, cols], wqi, wk, wki, wki, zeros(LANES),
                         wwi, zeros(2 * LANES - IDX_HEADS), wv], axis=1)
    assert w.shape[1] == B_COLS
    return w.astype(BF16), w_out[cols, :].astype(BF16)


def _sortable(x):
    bits = pltpu.bitcast(x, I32)
    return bits ^ ((bits >> 31) & jnp.int32(0x7FFFFFFF))


def _dsa_kernel(q_ref, gate_ref, qi_ref, k_ref, ki_ref, wi_ref, v_ref, y_ref,
                keys_ref, bias_ref, qs_ref, m_ref, l_ref, acc_ref, *, k_sel):
    i = pl.program_id(1)
    n_tiles = (i * Q_BLOCK) // KEY_TILE + 1
    n_pairs = B_HEADS // 2
    lane = lax.broadcasted_iota(I32, (Q_BLOCK, LANES), 1)
    first_head = lane < HEAD_DIM

    w_t = (wi_ref[...].astype(F32) * (IDX_HEADS ** -0.5 * HEAD_DIM ** -0.5)).T
    qi_stacked = [_stack_heads(qi_ref[:, p * LANES:(p + 1) * LANES], first_head)
                  for p in range(IDX_HEADS // 2)]
    key_row = lax.broadcasted_iota(I32, (KEY_TILE, Q_BLOCK), 0)
    q_pos = i * Q_BLOCK + lax.broadcasted_iota(I32, (KEY_TILE, Q_BLOCK), 1)

    def score_tile(t, carry):
        base = pl.multiple_of(t * KEY_TILE, KEY_TILE)
        kk = ki_ref[pl.ds(base, KEY_TILE), :]
        score = jnp.zeros((KEY_TILE, Q_BLOCK), F32)
        for p in range(IDX_HEADS // 2):
            sc = lax.dot_general(kk, qi_stacked[p], (((1,), (1,)), ((), ())),
                                 preferred_element_type=F32)
            score += jnp.maximum(sc[:, :Q_BLOCK], 0.0) * w_t[2 * p:2 * p + 1, :]
            score += jnp.maximum(sc[:, Q_BLOCK:], 0.0) * w_t[2 * p + 1:2 * p + 2, :]
        causal = (key_row + base) <= q_pos
        score = jnp.where(causal, score + 0.0, -jnp.inf)
        keys_ref[pl.ds(base, KEY_TILE), :] = _sortable(score)
        return carry

    lax.fori_loop(0, n_tiles, score_tile, 0)

    def count_ge(cand):
        def body(t, acc):
            base = pl.multiple_of(t * KEY_TILE, KEY_TILE)
            ge = (keys_ref[pl.ds(base, KEY_TILE), :] >= cand).astype(I32)
            return acc + jnp.sum(ge.reshape(KEY_TILE // 8, 8, Q_BLOCK), axis=0)
        acc = lax.fori_loop(0, n_tiles, body, jnp.zeros((8, Q_BLOCK), I32))
        return jnp.sum(acc, axis=0, keepdims=True)

    zero = jnp.zeros((1, Q_BLOCK), I32)
    thr0 = jnp.where(count_ge(zero) >= k_sel, zero, jnp.full((1, Q_BLOCK), INT_MIN, I32))

    def bit_step(it, thr):
        cand = thr + jnp.left_shift(jnp.int32(1), 30 - it)
        return jnp.where(count_ge(cand) >= k_sel, cand, thr)

    thr = lax.fori_loop(0, 31, bit_step, thr0)

    def bias_tile(t, carry):
        base = pl.multiple_of(t * KEY_TILE, KEY_TILE)
        sel = (keys_ref[pl.ds(base, KEY_TILE), :] >= thr) & ((key_row + base) <= q_pos)
        bias_t = jnp.where(sel, 0.0, -jnp.inf).astype(F32)
        for c in range(KEY_TILE // LANES):
            col = pl.multiple_of(base + c * LANES, LANES)
            bias_ref[:, pl.ds(col, LANES)] = bias_t[c * LANES:(c + 1) * LANES, :].T
        return carry

    lax.fori_loop(0, n_tiles, bias_tile, 0)

    for p in range(n_pairs):
        cols = slice(p * LANES, (p + 1) * LANES)
        qs_ref[p] = _stack_heads(q_ref[:, cols], first_head)
    m_ref[...] = jnp.full(m_ref.shape, F32_MIN, F32)
    l_ref[...] = jnp.zeros(l_ref.shape, F32)
    acc_ref[...] = jnp.zeros(acc_ref.shape, F32)
    heads_per_kv_pair = 2 * (B_HEADS // B_KV_HEADS) // 2

    ones_blk = jnp.ones((KEY_TILE, LANES), BF16)

    def attn_tile(t, carry):
        base = pl.multiple_of(t * KEY_TILE, KEY_TILE)
        bias = bias_ref[:, pl.ds(base, KEY_TILE)]
        bias2 = jnp.concatenate([bias, bias], axis=0)

        def logits(p):
            kv_cols = slice((p // heads_per_kv_pair) * LANES, (p // heads_per_kv_pair + 1) * LANES)
            kt = k_ref[pl.ds(base, KEY_TILE), kv_cols]
            return lax.dot_general(qs_ref[p], kt, (((1,), (1,)), ((), ())),
                                   preferred_element_type=F32)

        s_next = logits(0)
        for p in range(n_pairs):
            kv = p // heads_per_kv_pair
            kv_cols = slice(kv * LANES, (kv + 1) * LANES)
            s = s_next + bias2
            if p + 1 < n_pairs:
                s_next = logits(p + 1)
            m_blk = s[:, :LANES]
            for c in range(1, KEY_TILE // LANES):
                m_blk = jnp.maximum(m_blk, s[:, c * LANES:(c + 1) * LANES])
            m_old = m_ref[p]
            m_new = jnp.maximum(m_old, jnp.max(m_blk, axis=-1, keepdims=True))
            alpha = jnp.exp2(m_old - m_new)
            e = jnp.concatenate(
                [jnp.exp2(s[:, c * LANES:(c + 1) * LANES] - m_new).astype(BF16)
                 for c in range(KEY_TILE // LANES)], axis=1)
            v_ext = jnp.concatenate([v_ref[pl.ds(base, KEY_TILE), kv_cols], ones_blk], axis=1)
            pv = jnp.dot(e, v_ext, preferred_element_type=F32)
            acc_ref[p] = acc_ref[p] * alpha + pv[:, :LANES]
            l_ref[p] = l_ref[p] * alpha + pv[:, LANES:]
            m_ref[p] = m_new
        return carry

    lax.fori_loop(0, n_tiles, attn_tile, 0)

    for p in range(n_pairs):
        cols = slice(p * LANES, (p + 1) * LANES)
        l = l_ref[p]
        acc = acc_ref[p]
        num = jnp.where(first_head, acc[:Q_BLOCK], acc[Q_BLOCK:])
        den = jnp.where(first_head, l[:Q_BLOCK], l[Q_BLOCK:])
        gate = gate_ref[:, cols].astype(F32)
        y_ref[:, cols] = (num * gate / (den * (1.0 + jnp.exp(-gate)))).astype(y_ref.dtype)


def dsa_attention(proj):
    b, s, _ = proj.shape
    k_sel = min(TOPK_MAX, s // 4)
    assert s % KEY_TILE == 0 and k_sel <= KEY_TILE
    kvw = B_KV_HEADS * HEAD_DIM
    n_pairs = B_HEADS // 2

    def qblock(width, idx):
        return pl.BlockSpec((None, Q_BLOCK, width), lambda bi, i: (bi, i, idx))

    def full(width, idx):
        return pl.BlockSpec((None, s, width), lambda bi, i: (bi, 0, idx))

    return pl.pallas_call(
        functools.partial(_dsa_kernel, k_sel=k_sel),
        grid=(b, s // Q_BLOCK),
        in_specs=[
            qblock(B_WIDTH, 0),
            qblock(B_WIDTH, 1),
            qblock(IDX_HEADS * HEAD_DIM, 4),
            full(kvw, 10),
            full(LANES, 22),
            qblock(LANES, 24),
            full(kvw, 13),
        ],
        out_specs=qblock(B_WIDTH, 0),
        out_shape=jax.ShapeDtypeStruct((b, s, B_WIDTH), BF16),
        scratch_shapes=[
            pltpu.VMEM((s, Q_BLOCK), I32),
            pltpu.VMEM((Q_BLOCK, s), F32),
            pltpu.VMEM((n_pairs, 2 * Q_BLOCK, LANES), BF16),
            pltpu.VMEM((n_pairs, 2 * Q_BLOCK, LANES), F32),
            pltpu.VMEM((n_pairs, 2 * Q_BLOCK, LANES), F32),
            pltpu.VMEM((n_pairs, 2 * Q_BLOCK, LANES), F32),
        ],
        compiler_params=_cparams(("arbitrary", "arbitrary")),
        name="dsa_attention",
    )(proj, proj, proj, proj, proj, proj, proj)


def dsa_mixer(h, g, scale, shift, w_in_packed, cos, sin):
    proj = in_proj(h, g, scale, shift, w_in_packed, jnp.asarray(B_ROPE_FLAGS, I32), cos, sin, 512)
    return dsa_attention(proj)


def kernel(x, c, positions, norm_g, ada_w, ada_b, a_w_in, a_w_out, b_w_in, b_w_out, final_g):
    depth = norm_g.shape[0]
    d = x.shape[-1]
    mod = adaln_mod(c, ada_w, ada_b)
    cos, sin = rope_tables(positions)
    tables = {1: (cos, sin)}
    for _, dilation in A_GROUPS:
        if dilation > 1:
            tables[dilation] = class_major_rope_tables(positions, dilation)
    h = x
    for i in range(depth):
        shift, scale, gate = mod[i, :, :d], mod[i, :, d:2 * d], mod[i, :, 2 * d:]
        if i % 2 == 0:
            y = dilated_mixer(h, norm_g[i], scale, shift, a_w_in[i // 2], tables)
            w_out = a_w_out[i // 2].astype(BF16)
        else:
            w_in, w_out = _dsa_weights(b_w_in[i // 2], b_w_out[i // 2])
            y = dsa_mixer(h, norm_g[i], scale, shift, w_in, cos, sin)
        h = out_proj(y, w_out, gate, h)
    return final_norm(h, final_g)
```

```python
import functools

import numpy as np
import jax
import jax.numpy as jnp
from jax import lax
from jax.experimental import pallas as pl
from jax.experimental.pallas import tpu as pltpu

F32 = jnp.float32
BF16 = jnp.bfloat16
I32 = jnp.int32

LANES = 128
HEAD_DIM = 64
HALF = HEAD_DIM // 2
ROPE_THETA = 10000.0
NORM_EPS = 1e-6
VMEM_LIMIT = 56 * 1024 * 1024
ROW_CHUNK = 256

A_HEADS = 16
A_WIDTH = A_HEADS * HEAD_DIM
A_GROUPS = ((128, 1), (512, 4), (2048, 16))
A_BLK = 128

B_HEADS = 16
B_KV_HEADS = 4
B_WIDTH = B_HEADS * HEAD_DIM
IDX_HEADS = 8
TOPK_MAX = 256
Q_BLOCK = 128
KEY_TILE = 512
INT_MIN = -(2 ** 31)
INT_MAX = 2 ** 31 - 1
KEY_NEG_INF = -2139095041
I16 = jnp.int16
F32_MIN = float(np.finfo(np.float32).min)
LOG2_E = float(np.log2(np.e))
Q_PRESCALE = LOG2_E * HEAD_DIM ** -0.5


def _cparams(sem):
    return pltpu.CompilerParams(dimension_semantics=sem, vmem_limit_bytes=VMEM_LIMIT)


def _adaln_kernel(c_ref, w_ref, b_ref, o_ref):
    c = c_ref[...]
    ca = (c * (1.0 / (1.0 + jnp.exp(-c)))).astype(BF16)
    acc = jnp.dot(ca, w_ref[...].astype(BF16), preferred_element_type=F32)
    o_ref[...] = acc + b_ref[...]


def adaln_mod(c, ada_w, ada_b):
    depth, d, d3 = ada_w.shape
    b = c.shape[0]
    tn = 1024
    return pl.pallas_call(
        _adaln_kernel,
        grid=(depth, d3 // tn),
        in_specs=[
            pl.BlockSpec((b, d), lambda i, j: (0, 0)),
            pl.BlockSpec((None, d, tn), lambda i, j: (i, 0, j)),
            pl.BlockSpec((None, 1, tn), lambda i, j: (i, 0, j)),
        ],
        out_specs=pl.BlockSpec((None, b, tn), lambda i, j: (i, 0, j)),
        out_shape=jax.ShapeDtypeStruct((depth, b, d3), F32),
        compiler_params=_cparams(("arbitrary", "arbitrary")),
        name="adaln_mod",
    )(c, ada_w, ada_b.reshape(depth, 1, d3))


def _rope_table_kernel(pos_ref, inv_ref, cos_ref, sin_ref):
    ang = pos_ref[...].astype(F32) * inv_ref[...]
    lane = lax.broadcasted_iota(I32, ang.shape, 1)
    first_half = (lane % HEAD_DIM) < HALF
    cos_ref[...] = jnp.cos(ang)
    s = jnp.sin(ang)
    sin_ref[...] = jnp.where(first_half, -s, s)


def rope_tables(positions):
    b, s = positions.shape
    inv_freq = ROPE_THETA ** (-jnp.arange(HALF, dtype=F32) / HALF)
    inv_lane = jnp.tile(inv_freq, LANES // HALF).reshape(1, LANES)
    ts = min(s, 1024)
    out = jax.ShapeDtypeStruct((b, s, LANES), F32)
    return pl.pallas_call(
        _rope_table_kernel,
        grid=(b, s // ts),
        in_specs=[
            pl.BlockSpec((None, ts, 1), lambda i, j: (i, j, 0)),
            pl.BlockSpec((1, LANES), lambda i, j: (0, 0)),
        ],
        out_specs=[pl.BlockSpec((None, ts, LANES), lambda i, j: (i, j, 0))] * 2,
        out_shape=[out, out],
        compiler_params=_cparams(("arbitrary", "arbitrary")),
        name="rope_tables",
    )(positions.reshape(b, s, 1), inv_lane)


def _rope_block(t, cos, sin_signed, first_half):
    partner = jnp.where(first_half, pltpu.roll(t, LANES - HALF, 1), pltpu.roll(t, HALF, 1))
    return t * cos + partner * sin_signed


def _in_proj_kernel(x_ref, g_ref, sc_ref, sh_ref, w_ref, cos_ref, sin_ref, o_ref, u_ref, *stage,
                    dilation, rope_tiles, tn):
    tm = u_ref.shape[0]
    nblk = tn // LANES
    per = ROW_CHUNK // dilation
    x = x_ref[...]
    ms = jnp.mean(x * x, axis=-1, keepdims=True)
    xn = x * lax.rsqrt(ms + NORM_EPS)
    u_ref[...] = (xn * g_ref[...] * (1.0 + sc_ref[...]) + sh_ref[...]).astype(BF16)
    lane = lax.broadcasted_iota(I32, (per, LANES), 1)
    first_half = (lane % HEAD_DIM) < HALF
    step = 0
    for j, rope in enumerate(rope_tiles):
        for c in range(tm // ROW_CHUNK):
            acc = jnp.dot(u_ref[c * ROW_CHUNK:(c + 1) * ROW_CHUNK, :], w_ref[:, j * tn:(j + 1) * tn],
                          preferred_element_type=F32)
            if dilation > 1:
                slot = step % stage[0].shape[0]
                step += 1
                for k in range(nblk):
                    stage[0][slot, k] = acc[:, k * LANES:(k + 1) * LANES]
            for r in range(dilation):
                rows = slice(c * per, (c + 1) * per)
                if rope:
                    cos = cos_ref[rows, :] if dilation == 1 else cos_ref[r, rows, :]
                    sin = sin_ref[rows, :] if dilation == 1 else sin_ref[r, rows, :]
                for k in range(nblk):
                    cols = slice(j * tn + k * LANES, j * tn + (k + 1) * LANES)
                    if dilation > 1:
                        blk = stage[0][slot, k, pl.ds(r, per, stride=dilation), :]
                    else:
                        blk = acc[:, k * LANES:(k + 1) * LANES]
                    if rope:
                        blk = _rope_block(blk, cos, sin, first_half)
                    if dilation > 1:
                        o_ref[r, rows, cols] = blk.astype(o_ref.dtype)
                    else:
                        o_ref[rows, cols] = blk.astype(o_ref.dtype)


def in_proj(h, g, scale, shift, w, rope_tiles, cos, sin, tn, dilation=1):
    b, s, d = h.shape
    n = w.shape[1]
    tm = min(s, 512)
    assert tm % ROW_CHUNK == 0 and ROW_CHUNK % (16 * dilation) == 0 and len(rope_tiles) * tn == n
    scratch = [pltpu.VMEM((tm, d), BF16)]
    if dilation == 1:
        tab_spec = pl.BlockSpec((None, tm, LANES), lambda bi, i: (bi, i, 0))
        out_spec = pl.BlockSpec((None, tm, n), lambda bi, i: (bi, i, 0))
        out_shape = jax.ShapeDtypeStruct((b, s, n), BF16)
    else:
        tab_spec = pl.BlockSpec((None, dilation, tm // dilation, LANES), lambda bi, i: (bi, 0, i, 0))
        out_spec = pl.BlockSpec((None, dilation, tm // dilation, n), lambda bi, i: (bi, 0, i, 0))
        out_shape = jax.ShapeDtypeStruct((b, dilation, s // dilation, n), BF16)
        scratch.append(pltpu.VMEM((2, tn // LANES, ROW_CHUNK, LANES), F32))
    return pl.pallas_call(
        functools.partial(_in_proj_kernel, dilation=dilation, rope_tiles=tuple(rope_tiles), tn=tn),
        grid=(b, s // tm),
        in_specs=[
            pl.BlockSpec((None, tm, d), lambda bi, i: (bi, i, 0)),
            pl.BlockSpec((1, d), lambda bi, i: (0, 0)),
            pl.BlockSpec((None, 1, d), lambda bi, i: (bi, 0, 0)),
            pl.BlockSpec((None, 1, d), lambda bi, i: (bi, 0, 0)),
            pl.BlockSpec((d, n), lambda bi, i: (0, 0)),
            tab_spec, tab_spec,
        ],
        out_specs=out_spec,
        out_shape=out_shape,
        scratch_shapes=scratch,
        compiler_params=_cparams(("arbitrary", "arbitrary")),
        name=f"in_proj_d{dilation}",
    )(h, g.reshape(1, d), scale.reshape(b, 1, d), shift.reshape(b, 1, d), w, cos, sin)


def _out_proj_kernel(y_ref, w_ref, gate_ref, h_ref, o_ref):
    acc = jnp.dot(y_ref[...], w_ref[...], preferred_element_type=F32)
    o_ref[...] = h_ref[...] + gate_ref[...] * acc


def out_proj(y, w, gate, h):
    b, s, d = h.shape
    k = y.shape[-1]
    tm = min(s, 1024)
    return pl.pallas_call(
        _out_proj_kernel,
        grid=(b, s // tm),
        in_specs=[
            pl.BlockSpec((None, tm, k), lambda bi, i: (bi, i, 0)),
            pl.BlockSpec((k, d), lambda bi, i: (0, 0)),
            pl.BlockSpec((None, 1, d), lambda bi, i: (bi, 0, 0)),
            pl.BlockSpec((None, tm, d), lambda bi, i: (bi, i, 0)),
        ],
        out_specs=pl.BlockSpec((None, tm, d), lambda bi, i: (bi, i, 0)),
        out_shape=jax.ShapeDtypeStruct((b, s, d), F32),
        compiler_params=_cparams(("arbitrary", "arbitrary")),
        name="out_proj",
    )(y, w, gate.reshape(b, 1, d), h)


def _final_norm_kernel(x_ref, g_ref, o_ref):
    x = x_ref[...]
    ms = jnp.mean(x * x, axis=-1, keepdims=True)
    o_ref[...] = x * lax.rsqrt(ms + NORM_EPS) * g_ref[...]


def final_norm(h, g):
    b, s, d = h.shape
    tm = min(s, 1024)
    return pl.pallas_call(
        _final_norm_kernel,
        grid=(b, s // tm),
        in_specs=[
            pl.BlockSpec((None, tm, d), lambda bi, i: (bi, i, 0)),
            pl.BlockSpec((1, d), lambda bi, i: (0, 0)),
        ],
        out_specs=pl.BlockSpec((None, tm, d), lambda bi, i: (bi, i, 0)),
        out_shape=jax.ShapeDtypeStruct((b, s, d), F32),
        compiler_params=_cparams(("arbitrary", "arbitrary")),
        name="final_norm",
    )(h, g.reshape(1, d))


def _stack_heads(pair, first_head):
    zero = jnp.zeros_like(pair)
    return jnp.concatenate([jnp.where(first_head, pair, zero), jnp.where(first_head, zero, pair)], axis=0)


def _banded_kernel(q_ref, kp_ref, kc_ref, vp_ref, vc_ref, o_ref, lse_ref, *, tq):
    i = pl.program_id(2)
    nsub = tq // A_BLK
    lane = lax.broadcasted_iota(I32, (A_BLK, LANES), 1)
    first_head = lane < HEAD_DIM
    r2 = lax.broadcasted_iota(I32, (2 * A_BLK, 2 * A_BLK), 0) % A_BLK
    c2 = lax.broadcasted_iota(I32, (2 * A_BLK, 2 * A_BLK), 1)
    own_ok = (c2 >= A_BLK) & (c2 - A_BLK <= r2)
    ones_blk = jnp.ones((2 * A_BLK, LANES), BF16)
    n_pairs = A_HEADS // 2

    def operands(a, p):
        rows = slice(a * A_BLK, (a + 1) * A_BLK)
        cols = slice(p * LANES, (p + 1) * LANES)
        if a == 0:
            k_prev, v_prev = kp_ref[:, cols], vp_ref[:, cols]
        else:
            prow = slice((a - 1) * A_BLK, a * A_BLK)
            k_prev, v_prev = kc_ref[prow, cols], vc_ref[prow, cols]
        k2 = jnp.concatenate([k_prev, kc_ref[rows, cols]], axis=0)
        v2 = jnp.concatenate([v_prev, vc_ref[rows, cols]], axis=0)
        return _stack_heads(q_ref[rows, cols], first_head), k2, v2

    def logits(a, p):
        qs, k2, _ = operands(a, p)
        return lax.dot_general(qs, k2, (((1,), (1,)), ((), ())), preferred_element_type=F32)

    units = [(a, p) for a in range(nsub) for p in range(n_pairs)]
    s_next = logits(*units[0])
    lse_tile = None
    for u, (a, p) in enumerate(units):
        rows = slice(a * A_BLK, (a + 1) * A_BLK)
        cols = slice(p * LANES, (p + 1) * LANES)
        if p == 0:
            prev_shift = jnp.where(i * nsub + a > 0, 0, 2 * A_BLK)
            mask = own_ok | ((c2 < A_BLK) & (c2 >= r2 + prev_shift))
            lse_tile = jnp.zeros((A_BLK, LANES), F32)
        s = jnp.where(mask, s_next, -jnp.inf)
        if u + 1 < len(units):
            s_next = logits(*units[u + 1])
        m = jnp.max(s, axis=-1, keepdims=True)
        e = jnp.exp2(s - m).astype(BF16)
        v_ext = jnp.concatenate([operands(a, p)[2], ones_blk], axis=1)
        pv = jnp.dot(e, v_ext, preferred_element_type=F32)
        den = pv[:, LANES:]
        o = pv[:, :LANES] * (1.0 / den)
        o_ref[rows, cols] = jnp.where(first_head, o[:A_BLK], o[A_BLK:])
        lse2 = m + jnp.log(den) * LOG2_E
        lse_tile = jnp.where(lane == 2 * p, lse2[:A_BLK], lse_tile)
        lse_tile = jnp.where(lane == 2 * p + 1, lse2[A_BLK:], lse_tile)
        if p == n_pairs - 1:
            lse_ref[rows, :] = lse_tile


def banded_group_attention(qkv):
    b, d, n, _ = qkv.shape
    tq = min(2 * A_BLK, n)
    sub = tq // A_BLK

    def cur(blk):
        return pl.BlockSpec((None, None, tq, A_WIDTH), lambda bi, r, i: (bi, r, i, blk))

    def prev(blk):
        return pl.BlockSpec((None, None, A_BLK, A_WIDTH),
                            lambda bi, r, i: (bi, r, jnp.maximum(i * sub - 1, 0), blk))

    return pl.pallas_call(
        functools.partial(_banded_kernel, tq=tq),
        grid=(b, d, n // tq),
        in_specs=[cur(0), prev(1), cur(1), prev(2), cur(2)],
        out_specs=[
            pl.BlockSpec((None, None, tq, A_WIDTH), lambda bi, r, i: (bi, r, i, 0)),
            pl.BlockSpec((None, None, tq, LANES), lambda bi, r, i: (bi, r, i, 0)),
        ],
        out_shape=[
            jax.ShapeDtypeStruct((b, d, n, A_WIDTH), F32),
            jax.ShapeDtypeStruct((b, d, n, LANES), F32),
        ],
        compiler_params=_cparams(("arbitrary", "arbitrary", "arbitrary")),
        name=f"banded_attn_d{d}",
    )(qkv, qkv, qkv, qkv, qkv)


def _expand_heads(x, expand):
    hi = x.astype(BF16)
    r1 = x - hi.astype(F32)
    mid = r1.astype(BF16)
    lo = (r1 - mid.astype(F32)).astype(BF16)
    out = jnp.dot(hi, expand, preferred_element_type=F32)
    out += jnp.dot(mid, expand, preferred_element_type=F32)
    out += jnp.dot(lo, expand, preferred_element_type=F32)
    return out


def _merge_kernel(o0_ref, o1_ref, o2_ref, l0_ref, l1_ref, l2_ref, gate_ref, ex_ref, y_ref,
                  ot_ref, lt_ref, *, dilations):
    o_refs, l_refs = (o0_ref, o1_ref, o2_ref), (l0_ref, l1_ref, l2_ref)
    tm = y_ref.shape[0]
    nblk = y_ref.shape[1] // LANES
    for g, d in enumerate(dilations):
        if d == 1:
            continue
        per = tm // d
        for r in range(d):
            lt_ref[g, pl.ds(r, per, stride=d), :] = l_refs[g][r]
            for k in range(nblk):
                ot_ref[g, k, pl.ds(r, per, stride=d), :] = o_refs[g][r, :, k * LANES:(k + 1) * LANES]
    lses = [l_refs[g][0] if d == 1 else lt_ref[g] for g, d in enumerate(dilations)]
    m = jnp.maximum(jnp.maximum(lses[0], lses[1]), lses[2])
    es = [jnp.exp2(l - m) for l in lses]
    inv = 1.0 / (es[0] + es[1] + es[2])
    ex = ex_ref[...]
    alphas = [_expand_heads(e * inv, ex) for e in es]
    for k in range(nblk):
        cols = slice(k * LANES, (k + 1) * LANES)
        y = jnp.zeros((tm, LANES), F32)
        for g, d in enumerate(dilations):
            og = o_refs[g][0, :, cols] if d == 1 else ot_ref[g, k]
            y += alphas[g][:, cols] * og
        gate = gate_ref[:, cols].astype(F32)
        y_ref[:, cols] = (y * gate * (1.0 / (1.0 + jnp.exp(-gate)))).astype(y_ref.dtype)


def merge_groups(outs, lses, gate):
    b, s, w = gate.shape
    tm = min(s, 256)
    dil = tuple(o.shape[1] for o in outs)
    expand = (jnp.arange(LANES)[:, None] == (jnp.arange(w)[None, :] // HEAD_DIM)).astype(BF16)

    def cm(d, width):
        return pl.BlockSpec((None, d, tm // d, width), lambda bi, i: (bi, 0, i, 0))

    tspec = pl.BlockSpec((None, tm, w), lambda bi, i: (bi, i, 0))
    return pl.pallas_call(
        functools.partial(_merge_kernel, dilations=dil),
        grid=(b, s // tm),
        in_specs=[cm(d, w) for d in dil] + [cm(d, LANES) for d in dil]
                 + [tspec, pl.BlockSpec((LANES, w), lambda bi, i: (0, 0))],
        out_specs=tspec,
        out_shape=jax.ShapeDtypeStruct((b, s, w), BF16),
        scratch_shapes=[pltpu.VMEM((len(dil), w // LANES, tm, LANES), F32),
                        pltpu.VMEM((len(dil), tm, LANES), F32)],
        compiler_params=_cparams(("arbitrary", "arbitrary")),
        name="merge_groups",
    )(*outs, *lses, gate, expand)


def _class_major(t, d):
    b, s = t.shape[:2]
    return jnp.swapaxes(t.reshape(b, s // d, d, *t.shape[2:]), 1, 2)


def class_major_rope_tables(positions, dilation):
    pos_cm = _class_major(positions, dilation)
    cos, sin = rope_tables(pos_cm.reshape(positions.shape))
    shape = pos_cm.shape + (LANES,)
    return cos.reshape(shape), sin.reshape(shape)


def dilated_mixer(h, g, scale, shift, w_in, tables):
    tn = 512
    per = A_WIDTH // tn
    qkv_flags = (True,) * (2 * per) + (False,) * per
    w = w_in.astype(BF16)
    outs, lses = [], []
    for gi, (window, dilation) in enumerate(A_GROUPS):
        assert window // dilation == A_BLK
        wg = w_in[:, gi * 3 * A_WIDTH:(gi + 1) * 3 * A_WIDTH]
        wg = jnp.concatenate([wg[:, :A_WIDTH] * Q_PRESCALE, wg[:, A_WIDTH:]], axis=1).astype(BF16)
        qkv = in_proj(h, g, scale, shift, wg, qkv_flags, *tables[dilation], tn, dilation)
        if dilation == 1:
            qkv = qkv[:, None]
        o, lse = banded_group_attention(qkv)
        outs.append(o)
        lses.append(lse)
    gate = in_proj(h, g, scale, shift, w[:, len(A_GROUPS) * 3 * A_WIDTH:], (False,) * per,
                   *tables[1], tn)
    return merge_groups(outs, lses, gate)


B_COLS = 3584
B_ROPE_FLAGS = (1, 1, 0, 0, 1, 1, 0)


def _dsa_head_perm():
    group = B_HEADS // B_KV_HEADS
    order = []
    for g2 in range(B_KV_HEADS // 2):
        for r in range(group):
            order += [(2 * g2) * group + r, (2 * g2 + 1) * group + r]
    return np.asarray(order)


def _dsa_weights(w_in, w_out):
    d = w_in.shape[0]
    cuts = np.cumsum((B_WIDTH, B_KV_HEADS * HEAD_DIM, B_KV_HEADS * HEAD_DIM, IDX_HEADS * HEAD_DIM,
                      HEAD_DIM, IDX_HEADS, B_WIDTH))[:-1]
    wq, wk, wv, wqi, wki, wwi, wg = jnp.split(w_in, cuts, axis=1)
    cols = (_dsa_head_perm()[:, None] * HEAD_DIM + np.arange(HEAD_DIM)[None, :]).reshape(-1)
    zeros = lambda n: jnp.zeros((d, n), w_in.dtype)
    wq = wq * Q_PRESCALE
    w = jnp.concatenate([wq[:, cols], wg[:, cols], wqi, wk, wki, wki, zeros(LANES),
                         wwi, zeros(2 * LANES - IDX_HEADS), wv], axis=1)
    assert w.shape[1] == B_COLS
    return w.astype(BF16), w_out[cols, :].astype(BF16)


def _sortable(x):
    bits = pltpu.bitcast(x, I32)
    return bits ^ ((bits >> 31) & jnp.int32(0x7FFFFFFF))


def _dsa_kernel(q_ref, gate_ref, qi_ref, k_ref, ki_ref, wi_ref, v_ref, y_ref,
                keys_ref, hi_ref, lo_ref, bias_ref, kt_ref, qs_ref, s0_ref, m_ref, l_ref, acc_ref,
                *, k_sel):
    i = pl.program_id(1)
    seq = keys_ref.shape[0]
    n_tiles = (i * Q_BLOCK) // KEY_TILE + 1
    n_pairs = B_HEADS // 2
    lane = lax.broadcasted_iota(I32, (Q_BLOCK, LANES), 1)
    first_head = lane < HEAD_DIM

    w_t = (wi_ref[...].astype(F32) * (IDX_HEADS ** -0.5 * HEAD_DIM ** -0.5)).T
    qi_stacked = [_stack_heads(qi_ref[:, p * LANES:(p + 1) * LANES], first_head)
                  for p in range(IDX_HEADS // 2)]
    key_row = lax.broadcasted_iota(I32, (KEY_TILE, Q_BLOCK), 0)
    q_pos = i * Q_BLOCK + lax.broadcasted_iota(I32, (KEY_TILE, Q_BLOCK), 1)

    def score_tile(t, carry):
        base = pl.multiple_of(t * KEY_TILE, KEY_TILE)
        kk = ki_ref[pl.ds(base, KEY_TILE), :]
        score = jnp.zeros((KEY_TILE, Q_BLOCK), F32)
        for p in range(IDX_HEADS // 2):
            sc = lax.dot_general(kk, qi_stacked[p], (((1,), (1,)), ((), ())),
                                 preferred_element_type=F32)
            score += jnp.maximum(sc[:, :Q_BLOCK], 0.0) * w_t[2 * p:2 * p + 1, :]
            score += jnp.maximum(sc[:, Q_BLOCK:], 0.0) * w_t[2 * p + 1:2 * p + 2, :]
        causal = (key_row + base) <= q_pos
        score = jnp.where(causal, score + 0.0, -jnp.inf)
        key = _sortable(score)
        keys_ref[pl.ds(base, KEY_TILE), :] = key
        hi_ref[pl.ds(base, KEY_TILE), :] = (key >> 16).astype(I16)
        return carry

    lax.fori_loop(0, n_tiles, score_tile, 0)

    def count_ge32(cand):
        def body(t, acc):
            base = pl.multiple_of(t * KEY_TILE, KEY_TILE)
            ge = (keys_ref[pl.ds(base, KEY_TILE), :] >= cand).astype(I32)
            return acc + jnp.sum(ge.reshape(KEY_TILE // 8, 8, Q_BLOCK), axis=0)
        acc = lax.fori_loop(0, n_tiles, body, jnp.zeros((8, Q_BLOCK), I32))
        return jnp.sum(acc, axis=0, keepdims=True)

    def count_ge16(ref, cand):
        c16 = cand.astype(I16)
        def body(t, acc):
            base = pl.multiple_of(t * KEY_TILE, KEY_TILE)
            ge = (ref[pl.ds(base, KEY_TILE), :] >= c16).astype(I16)
            parts = [ge[r * 16:(r + 1) * 16, :] for r in range(KEY_TILE // 16)]
            while len(parts) > 1:
                parts = [parts[j] + parts[j + 1] for j in range(0, len(parts), 2)]
            return acc + parts[0]
        acc = lax.fori_loop(0, n_tiles, body, jnp.zeros((16, Q_BLOCK), I16))
        return jnp.sum(acc.astype(I32), axis=0, keepdims=True)

    def bisect16(ref, need):
        zero = jnp.zeros((1, Q_BLOCK), I32)
        x0 = jnp.where(count_ge16(ref, zero) >= need, zero, jnp.full((1, Q_BLOCK), -32768, I32))

        def bit_step(it, x):
            cand = x + jnp.left_shift(jnp.int32(1), 14 - it)
            return jnp.where(count_ge16(ref, cand) >= need, cand, x)

        return lax.fori_loop(0, 15, bit_step, x0)

    def select_threshold():
        k_vec = jnp.full((1, Q_BLOCK), k_sel, I32)
        t_hi = bisect16(hi_ref, k_vec)
        above = jnp.where(t_hi == 32767, 0, count_ge16(hi_ref, jnp.minimum(t_hi + 1, 32767)))

        def lo_tile(t, carry):
            base = pl.multiple_of(t * KEY_TILE, KEY_TILE)
            key = keys_ref[pl.ds(base, KEY_TILE), :]
            lo = (key & 0xFFFF) - 32768
            lo_ref[pl.ds(base, KEY_TILE), :] = jnp.where((key >> 16) == t_hi, lo, -32768).astype(I16)
            return carry

        lax.fori_loop(0, n_tiles, lo_tile, 0)
        t_lo = bisect16(lo_ref, k_vec - above)
        thr = t_hi * 65536 + (t_lo + 32768)

        n_ge = count_ge32(thr)
        tied = (n_ge > k_sel) & (thr > KEY_NEG_INF)
        idx_bits = (seq - 1).bit_length()

        def tie_break():
            need_eq = k_sel - jnp.where(thr == INT_MAX, 0, count_ge32(jnp.minimum(thr, INT_MAX - 1) + 1))

            def count_eq_before(cut):
                def body(t, acc):
                    base = pl.multiple_of(t * KEY_TILE, KEY_TILE)
                    hit = (keys_ref[pl.ds(base, KEY_TILE), :] == thr) & ((key_row + base) < cut)
                    return acc + jnp.sum(hit.astype(I32).reshape(KEY_TILE // 8, 8, Q_BLOCK), axis=0)
                acc = lax.fori_loop(0, n_tiles, body, jnp.zeros((8, Q_BLOCK), I32))
                return jnp.sum(acc, axis=0, keepdims=True)

            def bit_step(it, cut):
                cand = cut + jnp.left_shift(jnp.int32(1), idx_bits - 1 - it)
                return jnp.where(count_eq_before(cand) < need_eq, cand, cut)

            return lax.fori_loop(0, idx_bits, bit_step, jnp.zeros((1, Q_BLOCK), I32))

        any_tied = jnp.max(tied.astype(I32)) > 0
        idx_cut = lax.cond(any_tied, tie_break, lambda: jnp.full((1, Q_BLOCK), seq, I32))
        return thr, idx_cut

    all_selected = (i + 1) * Q_BLOCK <= k_sel
    thr, idx_cut = lax.cond(
        all_selected,
        lambda: (jnp.full((1, Q_BLOCK), INT_MIN, I32), jnp.full((1, Q_BLOCK), seq, I32)),
        select_threshold)

    def bias_tile(t, carry):
        base = pl.multiple_of(t * KEY_TILE, KEY_TILE)
        key = keys_ref[pl.ds(base, KEY_TILE), :]
        k_idx = key_row + base
        sel = ((key > thr) | ((key == thr) & (k_idx <= idx_cut))) & (k_idx <= q_pos)
        bias_t = jnp.where(sel, 0.0, -jnp.inf).astype(F32)
        for c in range(KEY_TILE // LANES):
            col = pl.multiple_of(base + c * LANES, LANES)
            bias_ref[:, pl.ds(col, LANES)] = bias_t[c * LANES:(c + 1) * LANES, :].T
        return carry

    lax.fori_loop(0, n_tiles, bias_tile, 0)

    for p in range(n_pairs):
        cols = slice(p * LANES, (p + 1) * LANES)
        qs_ref[p] = _stack_heads(q_ref[:, cols], first_head)
    m_ref[...] = jnp.full(m_ref.shape, F32_MIN, F32)
    l_ref[...] = jnp.zeros(l_ref.shape, F32)
    acc_ref[...] = jnp.zeros(acc_ref.shape, F32)
    heads_per_kv_pair = 2 * (B_HEADS // B_KV_HEADS) // 2

    ones_blk = jnp.ones((KEY_TILE, LANES), BF16)

    @pl.when(i == 0)
    def _():
        def xpose(r, carry):
            rows = pl.ds(pl.multiple_of(r * LANES, LANES), LANES)
            for c in range(kt_ref.shape[0]):
                blk = k_ref[rows, c * LANES:(c + 1) * LANES].astype(F32)
                kt_ref[c, :, rows] = blk.T.astype(BF16)
            return carry
        lax.fori_loop(0, seq // LANES, xpose, 0)

    def logits(base, p):
        kt = kt_ref[p // heads_per_kv_pair, :, pl.ds(base, KEY_TILE)]
        return jnp.dot(qs_ref[p], kt, preferred_element_type=F32)

    s0_ref[...] = logits(0, 0)

    def attn_tile(t, carry):
        base = pl.multiple_of(t * KEY_TILE, KEY_TILE)
        next_base = pl.multiple_of(jnp.minimum(t + 1, n_tiles - 1) * KEY_TILE, KEY_TILE)
        bias = bias_ref[:, pl.ds(base, KEY_TILE)]
        bias2 = jnp.concatenate([bias, bias], axis=0)

        s_next = s0_ref[...]
        for p in range(n_pairs):
            kv = p // heads_per_kv_pair
            kv_cols = slice(kv * LANES, (kv + 1) * LANES)
            s = s_next + bias2
            s_next = logits(base, p + 1) if p + 1 < n_pairs else logits(next_base, 0)
            m_blk = s[:, :LANES]
            for c in range(1, KEY_TILE // LANES):
                m_blk = jnp.maximum(m_blk, s[:, c * LANES:(c + 1) * LANES])
            m_old = m_ref[p]
            m_new = jnp.maximum(m_old, jnp.max(m_blk, axis=-1, keepdims=True))
            alpha = jnp.exp2(m_old - m_new)
            e = jnp.concatenate(
                [jnp.exp2(s[:, c * LANES:(c + 1) * LANES] - m_new).astype(BF16)
                 for c in range(KEY_TILE // LANES)], axis=1)
            v_ext = jnp.concatenate([v_ref[pl.ds(base, KEY_TILE), kv_cols], ones_blk], axis=1)
            pv = jnp.dot(e, v_ext, preferred_element_type=F32)
            acc_ref[p] = acc_ref[p] * alpha + pv[:, :LANES]
            l_ref[p] = l_ref[p] * alpha + pv[:, LANES:]
            m_ref[p] = m_new
        s0_ref[...] = s_next
        return carry

    lax.fori_loop(0, n_tiles, attn_tile, 0)

    for p in range(n_pairs):
        cols = slice(p * LANES, (p + 1) * LANES)
        l = l_ref[p]
        acc = acc_ref[p]
        num = jnp.where(first_head, acc[:Q_BLOCK], acc[Q_BLOCK:])
        den = jnp.where(first_head, l[:Q_BLOCK], l[Q_BLOCK:])
        gate = gate_ref[:, cols].astype(F32)
        y_ref[:, cols] = (num * gate / (den * (1.0 + jnp.exp(-gate)))).astype(y_ref.dtype)


def dsa_attention(proj):
    b, s, _ = proj.shape
    k_sel = min(TOPK_MAX, s // 4)
    assert s % KEY_TILE == 0 and k_sel <= KEY_TILE
    kvw = B_KV_HEADS * HEAD_DIM
    n_pairs = B_HEADS // 2

    def qblock(width, idx):
        return pl.BlockSpec((None, Q_BLOCK, width), lambda bi, i: (bi, i, idx))

    def full(width, idx):
        return pl.BlockSpec((None, s, width), lambda bi, i: (bi, 0, idx))

    return pl.pallas_call(
        functools.partial(_dsa_kernel, k_sel=k_sel),
        grid=(b, s // Q_BLOCK),
        in_specs=[
            qblock(B_WIDTH, 0),
            qblock(B_WIDTH, 1),
            qblock(IDX_HEADS * HEAD_DIM, 4),
            full(kvw, 10),
            full(LANES, 22),
            qblock(LANES, 24),
            full(kvw, 13),
        ],
        out_specs=qblock(B_WIDTH, 0),
        out_shape=jax.ShapeDtypeStruct((b, s, B_WIDTH), BF16),
        scratch_shapes=[
            pltpu.VMEM((s, Q_BLOCK), I32),
            pltpu.VMEM((s, Q_BLOCK), I16),
            pltpu.VMEM((s, Q_BLOCK), I16),
            pltpu.VMEM((Q_BLOCK, s), F32),
            pltpu.VMEM((kvw // LANES, LANES, s), BF16),
            pltpu.VMEM((n_pairs, 2 * Q_BLOCK, LANES), BF16),
            pltpu.VMEM((2 * Q_BLOCK, KEY_TILE), F32),
            pltpu.VMEM((n_pairs, 2 * Q_BLOCK, LANES), F32),
            pltpu.VMEM((n_pairs, 2 * Q_BLOCK, LANES), F32),
            pltpu.VMEM((n_pairs, 2 * Q_BLOCK, LANES), F32),
        ],
        compiler_params=_cparams(("arbitrary", "arbitrary")),
        name="dsa_attention",
    )(proj, proj, proj, proj, proj, proj, proj)


def dsa_mixer(h, g, scale, shift, w_in_packed, cos, sin):
    proj = in_proj(h, g, scale, shift, w_in_packed, [f == 1 for f in B_ROPE_FLAGS], cos, sin, 512)
    return dsa_attention(proj)


def kernel(x, c, positions, norm_g, ada_w, ada_b, a_w_in, a_w_out, b_w_in, b_w_out, final_g):
    depth = norm_g.shape[0]
    d = x.shape[-1]
    mod = adaln_mod(c, ada_w, ada_b)
    cos, sin = rope_tables(positions)
    tables = {1: (cos, sin)}
    for _, dilation in A_GROUPS:
        if dilation > 1:
            tables[dilation] = class_major_rope_tables(positions, dilation)
    h = x
    for i in range(depth):
        shift, scale, gate = mod[i, :, :d], mod[i, :, d:2 * d], mod[i, :, 2 * d:]
        if i % 2 == 0:
            y = dilated_mixer(h, norm_g[i], scale, shift, a_w_in[i // 2], tables)
            w_out = a_w_out[i // 2].astype(BF16)
        else:
            w_in, w_out = _dsa_weights(b_w_in[i // 2], b_w_out[i // 2])
            y = dsa_mixer(h, norm_g[i], scale, shift, w_in, cos, sin)
        h = out_proj(y, w_out, gate, h)
    return final_norm(h, final_g)
```

```python
import functools

import numpy as np
import jax
import jax.numpy as jnp
from jax import lax
from jax.experimental import pallas as pl
from jax.experimental.pallas import tpu as pltpu

F32 = jnp.float32
BF16 = jnp.bfloat16
I32 = jnp.int32

LANES = 128
HEAD_DIM = 64
HALF = HEAD_DIM // 2
ROPE_THETA = 10000.0
NORM_EPS = 1e-6
VMEM_LIMIT = 56 * 1024 * 1024
ROW_CHUNK = 256

A_HEADS = 16
A_WIDTH = A_HEADS * HEAD_DIM
A_GROUPS = ((128, 1), (512, 4), (2048, 16))
A_BLK = 128

B_HEADS = 16
B_KV_HEADS = 4
B_WIDTH = B_HEADS * HEAD_DIM
IDX_HEADS = 8
TOPK_MAX = 256
Q_BLOCK = 128
KEY_TILE = 512
INT_MIN = -(2 ** 31)
INT_MAX = 2 ** 31 - 1
KEY_NEG_INF = -2139095041
I16 = jnp.int16
F32_MIN = float(np.finfo(np.float32).min)
LOG2_E = float(np.log2(np.e))
Q_PRESCALE = LOG2_E * HEAD_DIM ** -0.5


def _cparams(sem):
    return pltpu.CompilerParams(dimension_semantics=sem, vmem_limit_bytes=VMEM_LIMIT)


def _adaln_kernel(c_ref, w_ref, b_ref, o_ref):
    c = c_ref[...]
    ca = (c * (1.0 / (1.0 + jnp.exp(-c)))).astype(BF16)
    acc = jnp.dot(ca, w_ref[...].astype(BF16), preferred_element_type=F32)
    o_ref[...] = acc + b_ref[...]


def adaln_mod(c, ada_w, ada_b):
    depth, d, d3 = ada_w.shape
    b = c.shape[0]
    tn = 1024
    return pl.pallas_call(
        _adaln_kernel,
        grid=(depth, d3 // tn),
        in_specs=[
            pl.BlockSpec((b, d), lambda i, j: (0, 0)),
            pl.BlockSpec((None, d, tn), lambda i, j: (i, 0, j)),
            pl.BlockSpec((None, 1, tn), lambda i, j: (i, 0, j)),
        ],
        out_specs=pl.BlockSpec((None, b, tn), lambda i, j: (i, 0, j)),
        out_shape=jax.ShapeDtypeStruct((depth, b, d3), F32),
        compiler_params=_cparams(("arbitrary", "arbitrary")),
        name="adaln_mod",
    )(c, ada_w, ada_b.reshape(depth, 1, d3))


def _rope_table_kernel(pos_ref, inv_ref, cos_ref, sin_ref):
    ang = pos_ref[...].astype(F32) * inv_ref[...]
    lane = lax.broadcasted_iota(I32, ang.shape, 1)
    first_half = (lane % HEAD_DIM) < HALF
    cos_ref[...] = jnp.cos(ang)
    s = jnp.sin(ang)
    sin_ref[...] = jnp.where(first_half, -s, s)


def rope_tables(positions):
    b, s = positions.shape
    inv_freq = ROPE_THETA ** (-jnp.arange(HALF, dtype=F32) / HALF)
    inv_lane = jnp.tile(inv_freq, LANES // HALF).reshape(1, LANES)
    ts = min(s, 1024)
    out = jax.ShapeDtypeStruct((b, s, LANES), F32)
    return pl.pallas_call(
        _rope_table_kernel,
        grid=(b, s // ts),
        in_specs=[
            pl.BlockSpec((None, ts, 1), lambda i, j: (i, j, 0)),
            pl.BlockSpec((1, LANES), lambda i, j: (0, 0)),
        ],
        out_specs=[pl.BlockSpec((None, ts, LANES), lambda i, j: (i, j, 0))] * 2,
        out_shape=[out, out],
        compiler_params=_cparams(("arbitrary", "arbitrary")),
        name="rope_tables",
    )(positions.reshape(b, s, 1), inv_lane)


def _rope_block(t, cos, sin_signed, first_half):
    partner = jnp.where(first_half, pltpu.roll(t, LANES - HALF, 1), pltpu.roll(t, HALF, 1))
    return t * cos + partner * sin_signed


def _in_proj_kernel(x_ref, g_ref, sc_ref, sh_ref, w_ref, cos_ref, sin_ref, o_ref, u_ref, *stage,
                    dilation, rope_tiles, tn):
    tm = u_ref.shape[0]
    nblk = tn // LANES
    per = ROW_CHUNK // dilation
    x = x_ref[...]
    ms = jnp.mean(x * x, axis=-1, keepdims=True)
    xn = x * lax.rsqrt(ms + NORM_EPS)
    u_ref[...] = (xn * g_ref[...] * (1.0 + sc_ref[...]) + sh_ref[...]).astype(BF16)
    lane = lax.broadcasted_iota(I32, (per, LANES), 1)
    first_half = (lane % HEAD_DIM) < HALF
    step = 0
    for j, rope in enumerate(rope_tiles):
        for c in range(tm // ROW_CHUNK):
            acc = jnp.dot(u_ref[c * ROW_CHUNK:(c + 1) * ROW_CHUNK, :], w_ref[:, j * tn:(j + 1) * tn],
                          preferred_element_type=F32)
            if dilation > 1:
                slot = step % stage[0].shape[0]
                step += 1
                for k in range(nblk):
                    stage[0][slot, k] = acc[:, k * LANES:(k + 1) * LANES]
            for r in range(dilation):
                rows = slice(c * per, (c + 1) * per)
                if rope:
                    cos = cos_ref[rows, :] if dilation == 1 else cos_ref[r, rows, :]
                    sin = sin_ref[rows, :] if dilation == 1 else sin_ref[r, rows, :]
                for k in range(nblk):
                    cols = slice(j * tn + k * LANES, j * tn + (k + 1) * LANES)
                    if dilation > 1:
                        blk = stage[0][slot, k, pl.ds(r, per, stride=dilation), :]
                    else:
                        blk = acc[:, k * LANES:(k + 1) * LANES]
                    if rope:
                        blk = _rope_block(blk, cos, sin, first_half)
                    if dilation > 1:
                        o_ref[r, rows, cols] = blk.astype(o_ref.dtype)
                    else:
                        o_ref[rows, cols] = blk.astype(o_ref.dtype)


def in_proj(h, g, scale, shift, w, rope_tiles, cos, sin, tn, dilation=1):
    b, s, d = h.shape
    n = w.shape[1]
    tm = min(s, 512)
    assert tm % ROW_CHUNK == 0 and ROW_CHUNK % (16 * dilation) == 0 and len(rope_tiles) * tn == n
    scratch = [pltpu.VMEM((tm, d), BF16)]
    if dilation == 1:
        tab_spec = pl.BlockSpec((None, tm, LANES), lambda bi, i: (bi, i, 0))
        out_spec = pl.BlockSpec((None, tm, n), lambda bi, i: (bi, i, 0))
        out_shape = jax.ShapeDtypeStruct((b, s, n), BF16)
    else:
        tab_spec = pl.BlockSpec((None, dilation, tm // dilation, LANES), lambda bi, i: (bi, 0, i, 0))
        out_spec = pl.BlockSpec((None, dilation, tm // dilation, n), lambda bi, i: (bi, 0, i, 0))
        out_shape = jax.ShapeDtypeStruct((b, dilation, s // dilation, n), BF16)
        scratch.append(pltpu.VMEM((2, tn // LANES, ROW_CHUNK, LANES), F32))
    return pl.pallas_call(
        functools.partial(_in_proj_kernel, dilation=dilation, rope_tiles=tuple(rope_tiles), tn=tn),
        grid=(b, s // tm),
        in_specs=[
            pl.BlockSpec((None, tm, d), lambda bi, i: (bi, i, 0)),
            pl.BlockSpec((1, d), lambda bi, i: (0, 0)),
            pl.BlockSpec((None, 1, d), lambda bi, i: (bi, 0, 0)),
            pl.BlockSpec((None, 1, d), lambda bi, i: (bi, 0, 0)),
            pl.BlockSpec((d, n), lambda bi, i: (0, 0)),
            tab_spec, tab_spec,
        ],
        out_specs=out_spec,
        out_shape=out_shape,
        scratch_shapes=scratch,
        compiler_params=_cparams(("arbitrary", "arbitrary")),
        name=f"in_proj_d{dilation}",
    )(h, g.reshape(1, d), scale.reshape(b, 1, d), shift.reshape(b, 1, d), w, cos, sin)


def _out_proj_kernel(y_ref, w_ref, gate_ref, h_ref, o_ref):
    acc = jnp.dot(y_ref[...], w_ref[...], preferred_element_type=F32)
    o_ref[...] = h_ref[...] + gate_ref[...] * acc


def out_proj(y, w, gate, h):
    b, s, d = h.shape
    k = y.shape[-1]
    tm = min(s, 1024)
    return pl.pallas_call(
        _out_proj_kernel,
        grid=(b, s // tm),
        in_specs=[
            pl.BlockSpec((None, tm, k), lambda bi, i: (bi, i, 0)),
            pl.BlockSpec((k, d), lambda bi, i: (0, 0)),
            pl.BlockSpec((None, 1, d), lambda bi, i: (bi, 0, 0)),
            pl.BlockSpec((None, tm, d), lambda bi, i: (bi, i, 0)),
        ],
        out_specs=pl.BlockSpec((None, tm, d), lambda bi, i: (bi, i, 0)),
        out_shape=jax.ShapeDtypeStruct((b, s, d), F32),
        compiler_params=_cparams(("arbitrary", "arbitrary")),
        name="out_proj",
    )(y, w, gate.reshape(b, 1, d), h)


def _final_norm_kernel(x_ref, g_ref, o_ref):
    x = x_ref[...]
    ms = jnp.mean(x * x, axis=-1, keepdims=True)
    o_ref[...] = x * lax.rsqrt(ms + NORM_EPS) * g_ref[...]


def final_norm(h, g):
    b, s, d = h.shape
    tm = min(s, 1024)
    return pl.pallas_call(
        _final_norm_kernel,
        grid=(b, s // tm),
        in_specs=[
            pl.BlockSpec((None, tm, d), lambda bi, i: (bi, i, 0)),
            pl.BlockSpec((1, d), lambda bi, i: (0, 0)),
        ],
        out_specs=pl.BlockSpec((None, tm, d), lambda bi, i: (bi, i, 0)),
        out_shape=jax.ShapeDtypeStruct((b, s, d), F32),
        compiler_params=_cparams(("arbitrary", "arbitrary")),
        name="final_norm",
    )(h, g.reshape(1, d))


def _stack_heads(pair, first_head):
    zero = jnp.zeros_like(pair)
    return jnp.concatenate([jnp.where(first_head, pair, zero), jnp.where(first_head, zero, pair)], axis=0)


def _banded_kernel(q_ref, kp_ref, kc_ref, vp_ref, vc_ref, o_ref, lse_ref, *, tq):
    i = pl.program_id(2)
    nsub = tq // A_BLK
    lane = lax.broadcasted_iota(I32, (A_BLK, LANES), 1)
    first_head = lane < HEAD_DIM
    r2 = lax.broadcasted_iota(I32, (2 * A_BLK, 2 * A_BLK), 0) % A_BLK
    c2 = lax.broadcasted_iota(I32, (2 * A_BLK, 2 * A_BLK), 1)
    own_ok = (c2 >= A_BLK) & (c2 - A_BLK <= r2)
    ones_blk = jnp.ones((2 * A_BLK, LANES), BF16)
    n_pairs = A_HEADS // 2

    def operands(a, p):
        rows = slice(a * A_BLK, (a + 1) * A_BLK)
        cols = slice(p * LANES, (p + 1) * LANES)
        if a == 0:
            k_prev, v_prev = kp_ref[:, cols], vp_ref[:, cols]
        else:
            prow = slice((a - 1) * A_BLK, a * A_BLK)
            k_prev, v_prev = kc_ref[prow, cols], vc_ref[prow, cols]
        k2 = jnp.concatenate([k_prev, kc_ref[rows, cols]], axis=0)
        v2 = jnp.concatenate([v_prev, vc_ref[rows, cols]], axis=0)
        return _stack_heads(q_ref[rows, cols], first_head), k2, v2

    def logits(a, p):
        qs, k2, _ = operands(a, p)
        return lax.dot_general(qs, k2, (((1,), (1,)), ((), ())), preferred_element_type=F32)

    units = [(a, p) for a in range(nsub) for p in range(n_pairs)]
    s_next = logits(*units[0])
    lse_tile = None
    for u, (a, p) in enumerate(units):
        rows = slice(a * A_BLK, (a + 1) * A_BLK)
        cols = slice(p * LANES, (p + 1) * LANES)
        if p == 0:
            prev_shift = jnp.where(i * nsub + a > 0, 0, 2 * A_BLK)
            mask = own_ok | ((c2 < A_BLK) & (c2 >= r2 + prev_shift))
            lse_tile = jnp.zeros((A_BLK, LANES), F32)
        s = jnp.where(mask, s_next, -jnp.inf)
        if u + 1 < len(units):
            s_next = logits(*units[u + 1])
        m = jnp.max(s, axis=-1, keepdims=True)
        e = jnp.exp2(s - m).astype(BF16)
        v_ext = jnp.concatenate([operands(a, p)[2], ones_blk], axis=1)
        pv = jnp.dot(e, v_ext, preferred_element_type=F32)
        den = pv[:, LANES:]
        o = pv[:, :LANES] * (1.0 / den)
        o_ref[rows, cols] = jnp.where(first_head, o[:A_BLK], o[A_BLK:])
        lse2 = m + jnp.log(den) * LOG2_E
        lse_tile = jnp.where(lane == 2 * p, lse2[:A_BLK], lse_tile)
        lse_tile = jnp.where(lane == 2 * p + 1, lse2[A_BLK:], lse_tile)
        if p == n_pairs - 1:
            lse_ref[rows, :] = lse_tile


def banded_group_attention(qkv):
    b, d, n, _ = qkv.shape
    tq = min(2 * A_BLK, n)
    sub = tq // A_BLK

    def cur(blk):
        return pl.BlockSpec((None, None, tq, A_WIDTH), lambda bi, r, i: (bi, r, i, blk))

    def prev(blk):
        return pl.BlockSpec((None, None, A_BLK, A_WIDTH),
                            lambda bi, r, i: (bi, r, jnp.maximum(i * sub - 1, 0), blk))

    return pl.pallas_call(
        functools.partial(_banded_kernel, tq=tq),
        grid=(b, d, n // tq),
        in_specs=[cur(0), prev(1), cur(1), prev(2), cur(2)],
        out_specs=[
            pl.BlockSpec((None, None, tq, A_WIDTH), lambda bi, r, i: (bi, r, i, 0)),
            pl.BlockSpec((None, None, tq, LANES), lambda bi, r, i: (bi, r, i, 0)),
        ],
        out_shape=[
            jax.ShapeDtypeStruct((b, d, n, A_WIDTH), F32),
            jax.ShapeDtypeStruct((b, d, n, LANES), F32),
        ],
        compiler_params=_cparams(("arbitrary", "arbitrary", "arbitrary")),
        name=f"banded_attn_d{d}",
    )(qkv, qkv, qkv, qkv, qkv)


def _expand_heads(x, expand):
    hi = x.astype(BF16)
    r1 = x - hi.astype(F32)
    mid = r1.astype(BF16)
    lo = (r1 - mid.astype(F32)).astype(BF16)
    out = jnp.dot(hi, expand, preferred_element_type=F32)
    out += jnp.dot(mid, expand, preferred_element_type=F32)
    out += jnp.dot(lo, expand, preferred_element_type=F32)
    return out


def _merge_kernel(o0_ref, o1_ref, o2_ref, l0_ref, l1_ref, l2_ref, gate_ref, ex_ref, y_ref,
                  ot_ref, lt_ref, *, dilations):
    o_refs, l_refs = (o0_ref, o1_ref, o2_ref), (l0_ref, l1_ref, l2_ref)
    tm = y_ref.shape[0]
    nblk = y_ref.shape[1] // LANES
    for g, d in enumerate(dilations):
        if d == 1:
            continue
        per = tm // d
        for r in range(d):
            lt_ref[g, pl.ds(r, per, stride=d), :] = l_refs[g][r]
            for k in range(nblk):
                ot_ref[g, k, pl.ds(r, per, stride=d), :] = o_refs[g][r, :, k * LANES:(k + 1) * LANES]
    lses = [l_refs[g][0] if d == 1 else lt_ref[g] for g, d in enumerate(dilations)]
    m = jnp.maximum(jnp.maximum(lses[0], lses[1]), lses[2])
    es = [jnp.exp2(l - m) for l in lses]
    inv = 1.0 / (es[0] + es[1] + es[2])
    ex = ex_ref[...]
    alphas = [_expand_heads(e * inv, ex) for e in es]
    for k in range(nblk):
        cols = slice(k * LANES, (k + 1) * LANES)
        y = jnp.zeros((tm, LANES), F32)
        for g, d in enumerate(dilations):
            og = o_refs[g][0, :, cols] if d == 1 else ot_ref[g, k]
            y += alphas[g][:, cols] * og
        gate = gate_ref[:, cols].astype(F32)
        y_ref[:, cols] = (y * gate * (1.0 / (1.0 + jnp.exp(-gate)))).astype(y_ref.dtype)


def merge_groups(outs, lses, gate):
    b, s, w = gate.shape
    tm = min(s, 256)
    dil = tuple(o.shape[1] for o in outs)
    expand = (jnp.arange(LANES)[:, None] == (jnp.arange(w)[None, :] // HEAD_DIM)).astype(BF16)

    def cm(d, width):
        return pl.BlockSpec((None, d, tm // d, width), lambda bi, i: (bi, 0, i, 0))

    tspec = pl.BlockSpec((None, tm, w), lambda bi, i: (bi, i, 0))
    return pl.pallas_call(
        functools.partial(_merge_kernel, dilations=dil),
        grid=(b, s // tm),
        in_specs=[cm(d, w) for d in dil] + [cm(d, LANES) for d in dil]
                 + [tspec, pl.BlockSpec((LANES, w), lambda bi, i: (0, 0))],
        out_specs=tspec,
        out_shape=jax.ShapeDtypeStruct((b, s, w), BF16),
        scratch_shapes=[pltpu.VMEM((len(dil), w // LANES, tm, LANES), F32),
                        pltpu.VMEM((len(dil), tm, LANES), F32)],
        compiler_params=_cparams(("arbitrary", "arbitrary")),
        name="merge_groups",
    )(*outs, *lses, gate, expand)


def _class_major(t, d):
    b, s = t.shape[:2]
    return jnp.swapaxes(t.reshape(b, s // d, d, *t.shape[2:]), 1, 2)


def class_major_rope_tables(positions, dilation):
    pos_cm = _class_major(positions, dilation)
    cos, sin = rope_tables(pos_cm.reshape(positions.shape))
    shape = pos_cm.shape + (LANES,)
    return cos.reshape(shape), sin.reshape(shape)


def dilated_mixer(h, g, scale, shift, w_in, tables):
    tn = 512
    per = A_WIDTH // tn
    qkv_flags = (True,) * (2 * per) + (False,) * per
    w = w_in.astype(BF16)
    outs, lses = [], []
    for gi, (window, dilation) in enumerate(A_GROUPS):
        assert window // dilation == A_BLK
        wg = w_in[:, gi * 3 * A_WIDTH:(gi + 1) * 3 * A_WIDTH]
        wg = jnp.concatenate([wg[:, :A_WIDTH] * Q_PRESCALE, wg[:, A_WIDTH:]], axis=1).astype(BF16)
        qkv = in_proj(h, g, scale, shift, wg, qkv_flags, *tables[dilation], tn, dilation)
        if dilation == 1:
            qkv = qkv[:, None]
        o, lse = banded_group_attention(qkv)
        outs.append(o)
        lses.append(lse)
    gate = in_proj(h, g, scale, shift, w[:, len(A_GROUPS) * 3 * A_WIDTH:], (False,) * per,
                   *tables[1], tn)
    return merge_groups(outs, lses, gate)


B_COLS = 3584
B_ROPE_FLAGS = (1, 1, 0, 0, 1, 1, 0)


def _dsa_head_perm():
    group = B_HEADS // B_KV_HEADS
    order = []
    for g2 in range(B_KV_HEADS // 2):
        for r in range(group):
            order += [(2 * g2) * group + r, (2 * g2 + 1) * group + r]
    return np.asarray(order)


def _dsa_weights(w_in, w_out):
    d = w_in.shape[0]
    cuts = np.cumsum((B_WIDTH, B_KV_HEADS * HEAD_DIM, B_KV_HEADS * HEAD_DIM, IDX_HEADS * HEAD_DIM,
                      HEAD_DIM, IDX_HEADS, B_WIDTH))[:-1]
    wq, wk, wv, wqi, wki, wwi, wg = jnp.split(w_in, cuts, axis=1)
    cols = (_dsa_head_perm()[:, None] * HEAD_DIM + np.arange(HEAD_DIM)[None, :]).reshape(-1)
    zeros = lambda n: jnp.zeros((d, n), w_in.dtype)
    wq = wq * Q_PRESCALE
    w = jnp.concatenate([wq[:, cols], wg[:, cols], wqi, wk, wki, wki, zeros(LANES),
                         wwi, zeros(2 * LANES - IDX_HEADS), wv], axis=1)
    assert w.shape[1] == B_COLS
    return w.astype(BF16), w_out[cols, :].astype(BF16)


def _sortable(x):
    bits = pltpu.bitcast(x, I32)
    return bits ^ ((bits >> 31) & jnp.int32(0x7FFFFFFF))


def _dsa_kernel(q_ref, gate_ref, qi_ref, k_ref, ki_ref, wi_ref, v_ref, y_ref,
                keys_ref, hi_ref, lo_ref, bias_ref, kt_ref, qs_ref, s0_ref, m_ref, l_ref, acc_ref,
                *, k_sel):
    i = pl.program_id(1)
    seq = keys_ref.shape[0]
    n_tiles = (i * Q_BLOCK) // KEY_TILE + 1
    n_pairs = B_HEADS // 2
    lane = lax.broadcasted_iota(I32, (Q_BLOCK, LANES), 1)
    first_head = lane < HEAD_DIM

    w_t = (wi_ref[...].astype(F32) * (IDX_HEADS ** -0.5 * HEAD_DIM ** -0.5)).T
    qi_stacked = [_stack_heads(qi_ref[:, p * LANES:(p + 1) * LANES], first_head)
                  for p in range(IDX_HEADS // 2)]
    key_row = lax.broadcasted_iota(I32, (KEY_TILE, Q_BLOCK), 0)
    q_pos = i * Q_BLOCK + lax.broadcasted_iota(I32, (KEY_TILE, Q_BLOCK), 1)

    def score_tile(t, carry):
        base = pl.multiple_of(t * KEY_TILE, KEY_TILE)
        kk = ki_ref[pl.ds(base, KEY_TILE), :]
        score = jnp.zeros((KEY_TILE, Q_BLOCK), F32)
        for p in range(IDX_HEADS // 2):
            sc = lax.dot_general(kk, qi_stacked[p], (((1,), (1,)), ((), ())),
                                 preferred_element_type=F32)
            score += jnp.maximum(sc[:, :Q_BLOCK], 0.0) * w_t[2 * p:2 * p + 1, :]
            score += jnp.maximum(sc[:, Q_BLOCK:], 0.0) * w_t[2 * p + 1:2 * p + 2, :]
        causal = (key_row + base) <= q_pos
        score = jnp.where(causal, score + 0.0, -jnp.inf)
        key = _sortable(score)
        keys_ref[pl.ds(base, KEY_TILE), :] = key
        hi_ref[pl.ds(base, KEY_TILE), :] = (key >> 16).astype(I16)
        return carry

    lax.fori_loop(0, n_tiles, score_tile, 0)

    def count_ge32(cand):
        def body(t, acc):
            base = pl.multiple_of(t * KEY_TILE, KEY_TILE)
            ge = (keys_ref[pl.ds(base, KEY_TILE), :] >= cand).astype(I32)
            return acc + jnp.sum(ge.reshape(KEY_TILE // 8, 8, Q_BLOCK), axis=0)
        acc = lax.fori_loop(0, n_tiles, body, jnp.zeros((8, Q_BLOCK), I32))
        return jnp.sum(acc, axis=0, keepdims=True)

    def count_ge16(ref, cand):
        c16 = cand.astype(I16)
        def body(t, acc):
            base = pl.multiple_of(t * KEY_TILE, KEY_TILE)
            ge = (ref[pl.ds(base, KEY_TILE), :] >= c16).astype(I16)
            parts = [ge[r * 16:(r + 1) * 16, :] for r in range(KEY_TILE // 16)]
            while len(parts) > 1:
                parts = [parts[j] + parts[j + 1] for j in range(0, len(parts), 2)]
            return acc + parts[0]
        acc = lax.fori_loop(0, n_tiles, body, jnp.zeros((16, Q_BLOCK), I16))
        return jnp.sum(acc.astype(I32), axis=0, keepdims=True)

    def bisect16(ref, need):
        zero = jnp.zeros((1, Q_BLOCK), I32)
        x0 = jnp.where(count_ge16(ref, zero) >= need, zero, jnp.full((1, Q_BLOCK), -32768, I32))

        def bit_step(it, x):
            cand = x + jnp.left_shift(jnp.int32(1), 14 - it)
            return jnp.where(count_ge16(ref, cand) >= need, cand, x)

        return lax.fori_loop(0, 15, bit_step, x0)

    def select_threshold():
        zero = jnp.zeros((1, Q_BLOCK), I32)
        thr0 = jnp.where(count_ge32(zero) >= k_sel, zero, jnp.full((1, Q_BLOCK), INT_MIN, I32))

        def bit_step32(it, t):
            cand = t + jnp.left_shift(jnp.int32(1), 30 - it)
            return jnp.where(count_ge32(cand) >= k_sel, cand, t)

        thr = lax.fori_loop(0, 31, bit_step32, thr0)

        n_ge = count_ge32(thr)
        tied = (n_ge > k_sel) & (thr > KEY_NEG_INF)
        idx_bits = (seq - 1).bit_length()

        def tie_break():
            need_eq = k_sel - jnp.where(thr == INT_MAX, 0, count_ge32(jnp.minimum(thr, INT_MAX - 1) + 1))

            def count_eq_before(cut):
                def body(t, acc):
                    base = pl.multiple_of(t * KEY_TILE, KEY_TILE)
                    hit = (keys_ref[pl.ds(base, KEY_TILE), :] == thr) & ((key_row + base) < cut)
                    return acc + jnp.sum(hit.astype(I32).reshape(KEY_TILE // 8, 8, Q_BLOCK), axis=0)
                acc = lax.fori_loop(0, n_tiles, body, jnp.zeros((8, Q_BLOCK), I32))
                return jnp.sum(acc, axis=0, keepdims=True)

            def bit_step(it, cut):
                cand = cut + jnp.left_shift(jnp.int32(1), idx_bits - 1 - it)
                return jnp.where(count_eq_before(cand) < need_eq, cand, cut)

            return lax.fori_loop(0, idx_bits, bit_step, jnp.zeros((1, Q_BLOCK), I32))

        any_tied = jnp.max(tied.astype(I32)) > 0
        idx_cut = lax.cond(any_tied, tie_break, lambda: jnp.full((1, Q_BLOCK), seq, I32))
        return thr, idx_cut

    all_selected = (i + 1) * Q_BLOCK <= k_sel
    thr, idx_cut = lax.cond(
        all_selected,
        lambda: (jnp.full((1, Q_BLOCK), INT_MIN, I32), jnp.full((1, Q_BLOCK), seq, I32)),
        select_threshold)

    def bias_tile(t, carry):
        base = pl.multiple_of(t * KEY_TILE, KEY_TILE)
        key = keys_ref[pl.ds(base, KEY_TILE), :]
        k_idx = key_row + base
        sel = ((key > thr) | ((key == thr) & (k_idx <= idx_cut))) & (k_idx <= q_pos)
        bias_t = jnp.where(sel, 0.0, -jnp.inf).astype(F32)
        for c in range(KEY_TILE // LANES):
            col = pl.multiple_of(base + c * LANES, LANES)
            bias_ref[:, pl.ds(col, LANES)] = bias_t[c * LANES:(c + 1) * LANES, :].T
        return carry

    lax.fori_loop(0, n_tiles, bias_tile, 0)

    for p in range(n_pairs):
        cols = slice(p * LANES, (p + 1) * LANES)
        qs_ref[p] = _stack_heads(q_ref[:, cols], first_head)
    m_ref[...] = jnp.full(m_ref.shape, F32_MIN, F32)
    l_ref[...] = jnp.zeros(l_ref.shape, F32)
    acc_ref[...] = jnp.zeros(acc_ref.shape, F32)
    heads_per_kv_pair = 2 * (B_HEADS // B_KV_HEADS) // 2

    ones_blk = jnp.ones((KEY_TILE, LANES), BF16)

    @pl.when(i == 0)
    def _():
        def xpose(r, carry):
            rows = pl.ds(pl.multiple_of(r * LANES, LANES), LANES)
            for c in range(kt_ref.shape[0]):
                blk = k_ref[rows, c * LANES:(c + 1) * LANES].astype(F32)
                kt_ref[c, :, rows] = blk.T.astype(BF16)
            return carry
        lax.fori_loop(0, seq // LANES, xpose, 0)

    def logits(base, p):
        kt = kt_ref[p // heads_per_kv_pair, :, pl.ds(base, KEY_TILE)]
        return jnp.dot(qs_ref[p], kt, preferred_element_type=F32)

    s0_ref[...] = logits(0, 0)

    def attn_tile(t, carry):
        base = pl.multiple_of(t * KEY_TILE, KEY_TILE)
        next_base = pl.multiple_of(jnp.minimum(t + 1, n_tiles - 1) * KEY_TILE, KEY_TILE)
        bias = bias_ref[:, pl.ds(base, KEY_TILE)]
        bias2 = jnp.concatenate([bias, bias], axis=0)

        s_next = s0_ref[...]
        for p in range(n_pairs):
            kv = p // heads_per_kv_pair
            kv_cols = slice(kv * LANES, (kv + 1) * LANES)
            s = s_next + bias2
            s_next = logits(base, p + 1) if p + 1 < n_pairs else logits(next_base, 0)
            m_blk = s[:, :LANES]
            for c in range(1, KEY_TILE // LANES):
                m_blk = jnp.maximum(m_blk, s[:, c * LANES:(c + 1) * LANES])
            m_old = m_ref[p]
            m_new = jnp.maximum(m_old, jnp.max(m_blk, axis=-1, keepdims=True))
            alpha = jnp.exp2(m_old - m_new)
            e = jnp.concatenate(
                [jnp.exp2(s[:, c * LANES:(c + 1) * LANES] - m_new).astype(BF16)
                 for c in range(KEY_TILE // LANES)], axis=1)
            v_ext = jnp.concatenate([v_ref[pl.ds(base, KEY_TILE), kv_cols], ones_blk], axis=1)
            pv = jnp.dot(e, v_ext, preferred_element_type=F32)
            acc_ref[p] = acc_ref[p] * alpha + pv[:, :LANES]
            l_ref[p] = l_ref[p] * alpha + pv[:, LANES:]
            m_ref[p] = m_new
        s0_ref[...] = s_next
        return carry

    lax.fori_loop(0, n_tiles, attn_tile, 0)

    for p in range(n_pairs):
        cols = slice(p * LANES, (p + 1) * LANES)
        l = l_ref[p]
        acc = acc_ref[p]
        num = jnp.where(first_head, acc[:Q_BLOCK], acc[Q_BLOCK:])
        den = jnp.where(first_head, l[:Q_BLOCK], l[Q_BLOCK:])
        gate = gate_ref[:, cols].astype(F32)
        y_ref[:, cols] = (num * gate / (den * (1.0 + jnp.exp(-gate)))).astype(y_ref.dtype)


def dsa_attention(proj):
    b, s, _ = proj.shape
    k_sel = min(TOPK_MAX, s // 4)
    assert s % KEY_TILE == 0 and k_sel <= KEY_TILE
    kvw = B_KV_HEADS * HEAD_DIM
    n_pairs = B_HEADS // 2

    def qblock(width, idx):
        return pl.BlockSpec((None, Q_BLOCK, width), lambda bi, i: (bi, i, idx))

    def full(width, idx):
        return pl.BlockSpec((None, s, width), lambda bi, i: (bi, 0, idx))

    return pl.pallas_call(
        functools.partial(_dsa_kernel, k_sel=k_sel),
        grid=(b, s // Q_BLOCK),
        in_specs=[
            qblock(B_WIDTH, 0),
            qblock(B_WIDTH, 1),
            qblock(IDX_HEADS * HEAD_DIM, 4),
            full(kvw, 10),
            full(LANES, 22),
            qblock(LANES, 24),
            full(kvw, 13),
        ],
        out_specs=qblock(B_WIDTH, 0),
        out_shape=jax.ShapeDtypeStruct((b, s, B_WIDTH), BF16),
        scratch_shapes=[
            pltpu.VMEM((s, Q_BLOCK), I32),
            pltpu.VMEM((s, Q_BLOCK), I16),
            pltpu.VMEM((s, Q_BLOCK), I16),
            pltpu.VMEM((Q_BLOCK, s), F32),
            pltpu.VMEM((kvw // LANES, LANES, s), BF16),
            pltpu.VMEM((n_pairs, 2 * Q_BLOCK, LANES), BF16),
            pltpu.VMEM((2 * Q_BLOCK, KEY_TILE), F32),
            pltpu.VMEM((n_pairs, 2 * Q_BLOCK, LANES), F32),
            pltpu.VMEM((n_pairs, 2 * Q_BLOCK, LANES), F32),
            pltpu.VMEM((n_pairs, 2 * Q_BLOCK, LANES), F32),
        ],
        compiler_params=_cparams(("arbitrary", "arbitrary")),
        name="dsa_attention",
    )(proj, proj, proj, proj, proj, proj, proj)


def dsa_mixer(h, g, scale, shift, w_in_packed, cos, sin):
    proj = in_proj(h, g, scale, shift, w_in_packed, [f == 1 for f in B_ROPE_FLAGS], cos, sin, 512)
    return dsa_attention(proj)


def kernel(x, c, positions, norm_g, ada_w, ada_b, a_w_in, a_w_out, b_w_in, b_w_out, final_g):
    depth = norm_g.shape[0]
    d = x.shape[-1]
    mod = adaln_mod(c, ada_w, ada_b)
    cos, sin = rope_tables(positions)
    tables = {1: (cos, sin)}
    for _, dilation in A_GROUPS:
        if dilation > 1:
            tables[dilation] = class_major_rope_tables(positions, dilation)
    h = x
    for i in range(depth):
        shift, scale, gate = mod[i, :, :d], mod[i, :, d:2 * d], mod[i, :, 2 * d:]
        if i % 2 == 0:
            y = dilated_mixer(h, norm_g[i], scale, shift, a_w_in[i // 2], tables)
            w_out = a_w_out[i // 2].astype(BF16)
        else:
            w_in, w_out = _dsa_weights(b_w_in[i // 2], b_w_out[i // 2])
            y = dsa_mixer(h, norm_g[i], scale, shift, w_in, cos, sin)
        h = out_proj(y, w_out, gate, h)
    return final_norm(h, final_g)
```

```python
import functools

import numpy as np
import jax
import jax.numpy as jnp
from jax import lax
from jax.experimental import pallas as pl
from jax.experimental.pallas import tpu as pltpu

F32 = jnp.float32
BF16 = jnp.bfloat16
I32 = jnp.int32

LANES = 128
HEAD_DIM = 64
HALF = HEAD_DIM // 2
ROPE_THETA = 10000.0
NORM_EPS = 1e-6
VMEM_LIMIT = 56 * 1024 * 1024
ROW_CHUNK = 256

A_HEADS = 16
A_WIDTH = A_HEADS * HEAD_DIM
A_GROUPS = ((128, 1), (512, 4), (2048, 16))
A_BLK = 128

B_HEADS = 16
B_KV_HEADS = 4
B_WIDTH = B_HEADS * HEAD_DIM
IDX_HEADS = 8
TOPK_MAX = 256
Q_BLOCK = 128
KEY_TILE = 512
INT_MIN = -(2 ** 31)
KEY_NEG_INF = -2139095041
F32_MIN = float(np.finfo(np.float32).min)
LOG2_E = float(np.log2(np.e))
Q_PRESCALE = LOG2_E * HEAD_DIM ** -0.5


def _cparams(sem):
    return pltpu.CompilerParams(dimension_semantics=sem, vmem_limit_bytes=VMEM_LIMIT)


def _adaln_kernel(c_ref, w_ref, b_ref, o_ref):
    c = c_ref[...]
    ca = (c * (1.0 / (1.0 + jnp.exp(-c)))).astype(BF16)
    acc = jnp.dot(ca, w_ref[...].astype(BF16), preferred_element_type=F32)
    o_ref[...] = acc + b_ref[...]


def adaln_mod(c, ada_w, ada_b):
    depth, d, d3 = ada_w.shape
    b = c.shape[0]
    tn = 1024
    return pl.pallas_call(
        _adaln_kernel,
        grid=(depth, d3 // tn),
        in_specs=[
            pl.BlockSpec((b, d), lambda i, j: (0, 0)),
            pl.BlockSpec((None, d, tn), lambda i, j: (i, 0, j)),
            pl.BlockSpec((None, 1, tn), lambda i, j: (i, 0, j)),
        ],
        out_specs=pl.BlockSpec((None, b, tn), lambda i, j: (i, 0, j)),
        out_shape=jax.ShapeDtypeStruct((depth, b, d3), F32),
        compiler_params=_cparams(("arbitrary", "arbitrary")),
        name="adaln_mod",
    )(c, ada_w, ada_b.reshape(depth, 1, d3))


def _rope_table_kernel(pos_ref, inv_ref, cos_ref, sin_ref):
    ang = pos_ref[...].astype(F32) * inv_ref[...]
    lane = lax.broadcasted_iota(I32, ang.shape, 1)
    first_half = (lane % HEAD_DIM) < HALF
    cos_ref[...] = jnp.cos(ang)
    s = jnp.sin(ang)
    sin_ref[...] = jnp.where(first_half, -s, s)


def rope_tables(positions):
    b, s = positions.shape
    inv_freq = ROPE_THETA ** (-jnp.arange(HALF, dtype=F32) / HALF)
    inv_lane = jnp.tile(inv_freq, LANES // HALF).reshape(1, LANES)
    ts = min(s, 1024)
    out = jax.ShapeDtypeStruct((b, s, LANES), F32)
    return pl.pallas_call(
        _rope_table_kernel,
        grid=(b, s // ts),
        in_specs=[
            pl.BlockSpec((None, ts, 1), lambda i, j: (i, j, 0)),
            pl.BlockSpec((1, LANES), lambda i, j: (0, 0)),
        ],
        out_specs=[pl.BlockSpec((None, ts, LANES), lambda i, j: (i, j, 0))] * 2,
        out_shape=[out, out],
        compiler_params=_cparams(("arbitrary", "arbitrary")),
        name="rope_tables",
    )(positions.reshape(b, s, 1), inv_lane)


def _rope_block(t, cos, sin_signed, first_half):
    partner = jnp.where(first_half, pltpu.roll(t, LANES - HALF, 1), pltpu.roll(t, HALF, 1))
    return t * cos + partner * sin_signed


def _in_proj_kernel(x_ref, g_ref, sc_ref, sh_ref, w_ref, cos_ref, sin_ref, o_ref, u_ref, *stage,
                    dilation, rope_tiles, tn):
    tm = u_ref.shape[0]
    nblk = tn // LANES
    per = ROW_CHUNK // dilation
    x = x_ref[...]
    ms = jnp.mean(x * x, axis=-1, keepdims=True)
    xn = x * lax.rsqrt(ms + NORM_EPS)
    u_ref[...] = (xn * g_ref[...] * (1.0 + sc_ref[...]) + sh_ref[...]).astype(BF16)
    lane = lax.broadcasted_iota(I32, (per, LANES), 1)
    first_half = (lane % HEAD_DIM) < HALF
    step = 0
    for j, rope in enumerate(rope_tiles):
        for c in range(tm // ROW_CHUNK):
            acc = jnp.dot(u_ref[c * ROW_CHUNK:(c + 1) * ROW_CHUNK, :], w_ref[:, j * tn:(j + 1) * tn],
                          preferred_element_type=F32)
            if dilation > 1:
                slot = step % stage[0].shape[0]
                step += 1
                for k in range(nblk):
                    stage[0][slot, k] = acc[:, k * LANES:(k + 1) * LANES]
            for r in range(dilation):
                rows = slice(c * per, (c + 1) * per)
                if rope:
                    cos = cos_ref[rows, :] if dilation == 1 else cos_ref[r, rows, :]
                    sin = sin_ref[rows, :] if dilation == 1 else sin_ref[r, rows, :]
                for k in range(nblk):
                    cols = slice(j * tn + k * LANES, j * tn + (k + 1) * LANES)
                    if dilation > 1:
                        blk = stage[0][slot, k, pl.ds(r, per, stride=dilation), :]
                    else:
                        blk = acc[:, k * LANES:(k + 1) * LANES]
                    if rope:
                        blk = _rope_block(blk, cos, sin, first_half)
                    if dilation > 1:
                        o_ref[r, rows, cols] = blk.astype(o_ref.dtype)
                    else:
                        o_ref[rows, cols] = blk.astype(o_ref.dtype)


def in_proj(h, g, scale, shift, w, rope_tiles, cos, sin, tn, dilation=1):
    b, s, d = h.shape
    n = w.shape[1]
    tm = min(s, 512)
    assert tm % ROW_CHUNK == 0 and ROW_CHUNK % (16 * dilation) == 0 and len(rope_tiles) * tn == n
    scratch = [pltpu.VMEM((tm, d), BF16)]
    if dilation == 1:
        tab_spec = pl.BlockSpec((None, tm, LANES), lambda bi, i: (bi, i, 0))
        out_spec = pl.BlockSpec((None, tm, n), lambda bi, i: (bi, i, 0))
        out_shape = jax.ShapeDtypeStruct((b, s, n), BF16)
    else:
        tab_spec = pl.BlockSpec((None, dilation, tm // dilation, LANES), lambda bi, i: (bi, 0, i, 0))
        out_spec = pl.BlockSpec((None, dilation, tm // dilation, n), lambda bi, i: (bi, 0, i, 0))
        out_shape = jax.ShapeDtypeStruct((b, dilation, s // dilation, n), BF16)
        scratch.append(pltpu.VMEM((2, tn // LANES, ROW_CHUNK, LANES), F32))
    return pl.pallas_call(
        functools.partial(_in_proj_kernel, dilation=dilation, rope_tiles=tuple(rope_tiles), tn=tn),
        grid=(b, s // tm),
        in_specs=[
            pl.BlockSpec((None, tm, d), lambda bi, i: (bi, i, 0)),
            pl.BlockSpec((1, d), lambda bi, i: (0, 0)),
            pl.BlockSpec((None, 1, d), lambda bi, i: (bi, 0, 0)),
            pl.BlockSpec((None, 1, d), lambda bi, i: (bi, 0, 0)),
            pl.BlockSpec((d, n), lambda bi, i: (0, 0)),
            tab_spec, tab_spec,
        ],
        out_specs=out_spec,
        out_shape=out_shape,
        scratch_shapes=scratch,
        compiler_params=_cparams(("arbitrary", "arbitrary")),
        name=f"in_proj_d{dilation}",
    )(h, g.reshape(1, d), scale.reshape(b, 1, d), shift.reshape(b, 1, d), w, cos, sin)


def _out_proj_kernel(y_ref, w_ref, gate_ref, h_ref, o_ref):
    acc = jnp.dot(y_ref[...], w_ref[...], preferred_element_type=F32)
    o_ref[...] = h_ref[...] + gate_ref[...] * acc


def out_proj(y, w, gate, h):
    b, s, d = h.shape
    k = y.shape[-1]
    tm = min(s, 1024)
    return pl.pallas_call(
        _out_proj_kernel,
        grid=(b, s // tm),
        in_specs=[
            pl.BlockSpec((None, tm, k), lambda bi, i: (bi, i, 0)),
            pl.BlockSpec((k, d), lambda bi, i: (0, 0)),
            pl.BlockSpec((None, 1, d), lambda bi, i: (bi, 0, 0)),
            pl.BlockSpec((None, tm, d), lambda bi, i: (bi, i, 0)),
        ],
        out_specs=pl.BlockSpec((None, tm, d), lambda bi, i: (bi, i, 0)),
        out_shape=jax.ShapeDtypeStruct((b, s, d), F32),
        compiler_params=_cparams(("arbitrary", "arbitrary")),
        name="out_proj",
    )(y, w, gate.reshape(b, 1, d), h)


def _final_norm_kernel(x_ref, g_ref, o_ref):
    x = x_ref[...]
    ms = jnp.mean(x * x, axis=-1, keepdims=True)
    o_ref[...] = x * lax.rsqrt(ms + NORM_EPS) * g_ref[...]


def final_norm(h, g):
    b, s, d = h.shape
    tm = min(s, 1024)
    return pl.pallas_call(
        _final_norm_kernel,
        grid=(b, s // tm),
        in_specs=[
            pl.BlockSpec((None, tm, d), lambda bi, i: (bi, i, 0)),
            pl.BlockSpec((1, d), lambda bi, i: (0, 0)),
        ],
        out_specs=pl.BlockSpec((None, tm, d), lambda bi, i: (bi, i, 0)),
        out_shape=jax.ShapeDtypeStruct((b, s, d), F32),
        compiler_params=_cparams(("arbitrary", "arbitrary")),
        name="final_norm",
    )(h, g.reshape(1, d))


def _stack_heads(pair, first_head):
    zero = jnp.zeros_like(pair)
    return jnp.concatenate([jnp.where(first_head, pair, zero), jnp.where(first_head, zero, pair)], axis=0)


def _banded_kernel(q_ref, kp_ref, kc_ref, vp_ref, vc_ref, o_ref, lse_ref, *, tq):
    i = pl.program_id(2)
    nsub = tq // A_BLK
    lane = lax.broadcasted_iota(I32, (A_BLK, LANES), 1)
    first_head = lane < HEAD_DIM
    r2 = lax.broadcasted_iota(I32, (2 * A_BLK, 2 * A_BLK), 0) % A_BLK
    c2 = lax.broadcasted_iota(I32, (2 * A_BLK, 2 * A_BLK), 1)
    own_ok = (c2 >= A_BLK) & (c2 - A_BLK <= r2)
    ones_blk = jnp.ones((2 * A_BLK, LANES), BF16)
    n_pairs = A_HEADS // 2

    def operands(a, p):
        rows = slice(a * A_BLK, (a + 1) * A_BLK)
        cols = slice(p * LANES, (p + 1) * LANES)
        if a == 0:
            k_prev, v_prev = kp_ref[:, cols], vp_ref[:, cols]
        else:
            prow = slice((a - 1) * A_BLK, a * A_BLK)
            k_prev, v_prev = kc_ref[prow, cols], vc_ref[prow, cols]
        k2 = jnp.concatenate([k_prev, kc_ref[rows, cols]], axis=0)
        v2 = jnp.concatenate([v_prev, vc_ref[rows, cols]], axis=0)
        return _stack_heads(q_ref[rows, cols], first_head), k2, v2

    def logits(a, p):
        qs, k2, _ = operands(a, p)
        return lax.dot_general(qs, k2, (((1,), (1,)), ((), ())), preferred_element_type=F32)

    units = [(a, p) for a in range(nsub) for p in range(n_pairs)]
    s_next = logits(*units[0])
    lse_tile = None
    for u, (a, p) in enumerate(units):
        rows = slice(a * A_BLK, (a + 1) * A_BLK)
        cols = slice(p * LANES, (p + 1) * LANES)
        if p == 0:
            prev_shift = jnp.where(i * nsub + a > 0, 0, 2 * A_BLK)
            mask = own_ok | ((c2 < A_BLK) & (c2 >= r2 + prev_shift))
            lse_tile = jnp.zeros((A_BLK, LANES), F32)
        s = jnp.where(mask, s_next, -jnp.inf)
        if u + 1 < len(units):
            s_next = logits(*units[u + 1])
        m = jnp.max(s, axis=-1, keepdims=True)
        e = jnp.exp2(s - m).astype(BF16)
        v_ext = jnp.concatenate([operands(a, p)[2], ones_blk], axis=1)
        pv = jnp.dot(e, v_ext, preferred_element_type=F32)
        den = pv[:, LANES:]
        o = pv[:, :LANES] * (1.0 / den)
        o_ref[rows, cols] = jnp.where(first_head, o[:A_BLK], o[A_BLK:])
        lse2 = m + jnp.log(den) * LOG2_E
        lse_tile = jnp.where(lane == 2 * p, lse2[:A_BLK], lse_tile)
        lse_tile = jnp.where(lane == 2 * p + 1, lse2[A_BLK:], lse_tile)
        if p == n_pairs - 1:
            lse_ref[rows, :] = lse_tile


def banded_group_attention(qkv):
    b, d, n, _ = qkv.shape
    tq = min(2 * A_BLK, n)
    sub = tq // A_BLK

    def cur(blk):
        return pl.BlockSpec((None, None, tq, A_WIDTH), lambda bi, r, i: (bi, r, i, blk))

    def prev(blk):
        return pl.BlockSpec((None, None, A_BLK, A_WIDTH),
                            lambda bi, r, i: (bi, r, jnp.maximum(i * sub - 1, 0), blk))

    return pl.pallas_call(
        functools.partial(_banded_kernel, tq=tq),
        grid=(b, d, n // tq),
        in_specs=[cur(0), prev(1), cur(1), prev(2), cur(2)],
        out_specs=[
            pl.BlockSpec((None, None, tq, A_WIDTH), lambda bi, r, i: (bi, r, i, 0)),
            pl.BlockSpec((None, None, tq, LANES), lambda bi, r, i: (bi, r, i, 0)),
        ],
        out_shape=[
            jax.ShapeDtypeStruct((b, d, n, A_WIDTH), F32),
            jax.ShapeDtypeStruct((b, d, n, LANES), F32),
        ],
        compiler_params=_cparams(("arbitrary", "arbitrary", "arbitrary")),
        name=f"banded_attn_d{d}",
    )(qkv, qkv, qkv, qkv, qkv)


def _expand_heads(x, expand):
    hi = x.astype(BF16)
    r1 = x - hi.astype(F32)
    mid = r1.astype(BF16)
    lo = (r1 - mid.astype(F32)).astype(BF16)
    out = jnp.dot(hi, expand, preferred_element_type=F32)
    out += jnp.dot(mid, expand, preferred_element_type=F32)
    out += jnp.dot(lo, expand, preferred_element_type=F32)
    return out


def _merge_kernel(o0_ref, o1_ref, o2_ref, l0_ref, l1_ref, l2_ref, gate_ref, ex_ref, y_ref,
                  ot_ref, lt_ref, *, dilations):
    o_refs, l_refs = (o0_ref, o1_ref, o2_ref), (l0_ref, l1_ref, l2_ref)
    tm = y_ref.shape[0]
    nblk = y_ref.shape[1] // LANES
    for g, d in enumerate(dilations):
        if d == 1:
            continue
        per = tm // d
        for r in range(d):
            lt_ref[g, pl.ds(r, per, stride=d), :] = l_refs[g][r]
            for k in range(nblk):
                ot_ref[g, k, pl.ds(r, per, stride=d), :] = o_refs[g][r, :, k * LANES:(k + 1) * LANES]
    lses = [l_refs[g][0] if d == 1 else lt_ref[g] for g, d in enumerate(dilations)]
    m = jnp.maximum(jnp.maximum(lses[0], lses[1]), lses[2])
    es = [jnp.exp2(l - m) for l in lses]
    inv = 1.0 / (es[0] + es[1] + es[2])
    ex = ex_ref[...]
    alphas = [_expand_heads(e * inv, ex) for e in es]
    for k in range(nblk):
        cols = slice(k * LANES, (k + 1) * LANES)
        y = jnp.zeros((tm, LANES), F32)
        for g, d in enumerate(dilations):
            og = o_refs[g][0, :, cols] if d == 1 else ot_ref[g, k]
            y += alphas[g][:, cols] * og
        gate = gate_ref[:, cols].astype(F32)
        y_ref[:, cols] = (y * gate * (1.0 / (1.0 + jnp.exp(-gate)))).astype(y_ref.dtype)


def merge_groups(outs, lses, gate):
    b, s, w = gate.shape
    tm = min(s, 256)
    dil = tuple(o.shape[1] for o in outs)
    expand = (jnp.arange(LANES)[:, None] == (jnp.arange(w)[None, :] // HEAD_DIM)).astype(BF16)

    def cm(d, width):
        return pl.BlockSpec((None, d, tm // d, width), lambda bi, i: (bi, 0, i, 0))

    tspec = pl.BlockSpec((None, tm, w), lambda bi, i: (bi, i, 0))
    return pl.pallas_call(
        functools.partial(_merge_kernel, dilations=dil),
        grid=(b, s // tm),
        in_specs=[cm(d, w) for d in dil] + [cm(d, LANES) for d in dil]
                 + [tspec, pl.BlockSpec((LANES, w), lambda bi, i: (0, 0))],
        out_specs=tspec,
        out_shape=jax.ShapeDtypeStruct((b, s, w), BF16),
        scratch_shapes=[pltpu.VMEM((len(dil), w // LANES, tm, LANES), F32),
                        pltpu.VMEM((len(dil), tm, LANES), F32)],
        compiler_params=_cparams(("arbitrary", "arbitrary")),
        name="merge_groups",
    )(*outs, *lses, gate, expand)


def _class_major(t, d):
    b, s = t.shape[:2]
    return jnp.swapaxes(t.reshape(b, s // d, d, *t.shape[2:]), 1, 2)


def class_major_rope_tables(positions, dilation):
    pos_cm = _class_major(positions, dilation)
    cos, sin = rope_tables(pos_cm.reshape(positions.shape))
    shape = pos_cm.shape + (LANES,)
    return cos.reshape(shape), sin.reshape(shape)


def dilated_mixer(h, g, scale, shift, w_in, tables):
    tn = 512
    per = A_WIDTH // tn
    qkv_flags = (True,) * (2 * per) + (False,) * per
    w = w_in.astype(BF16)
    outs, lses = [], []
    for gi, (window, dilation) in enumerate(A_GROUPS):
        assert window // dilation == A_BLK
        wg = w_in[:, gi * 3 * A_WIDTH:(gi + 1) * 3 * A_WIDTH]
        wg = jnp.concatenate([wg[:, :A_WIDTH] * Q_PRESCALE, wg[:, A_WIDTH:]], axis=1).astype(BF16)
        qkv = in_proj(h, g, scale, shift, wg, qkv_flags, *tables[dilation], tn, dilation)
        if dilation == 1:
            qkv = qkv[:, None]
        o, lse = banded_group_attention(qkv)
        outs.append(o)
        lses.append(lse)
    gate = in_proj(h, g, scale, shift, w[:, len(A_GROUPS) * 3 * A_WIDTH:], (False,) * per,
                   *tables[1], tn)
    return merge_groups(outs, lses, gate)


B_COLS = 3584
B_ROPE_FLAGS = (1, 1, 0, 0, 1, 1, 0)


def _dsa_head_perm():
    group = B_HEADS // B_KV_HEADS
    order = []
    for g2 in range(B_KV_HEADS // 2):
        for r in range(group):
            order += [(2 * g2) * group + r, (2 * g2 + 1) * group + r]
    return np.asarray(order)


def _dsa_weights(w_in, w_out):
    d = w_in.shape[0]
    cuts = np.cumsum((B_WIDTH, B_KV_HEADS * HEAD_DIM, B_KV_HEADS * HEAD_DIM, IDX_HEADS * HEAD_DIM,
                      HEAD_DIM, IDX_HEADS, B_WIDTH))[:-1]
    wq, wk, wv, wqi, wki, wwi, wg = jnp.split(w_in, cuts, axis=1)
    cols = (_dsa_head_perm()[:, None] * HEAD_DIM + np.arange(HEAD_DIM)[None, :]).reshape(-1)
    zeros = lambda n: jnp.zeros((d, n), w_in.dtype)
    wq = wq * Q_PRESCALE
    w = jnp.concatenate([wq[:, cols], wg[:, cols], wqi, wk, wki, wki, zeros(LANES),
                         wwi, zeros(2 * LANES - IDX_HEADS), wv], axis=1)
    assert w.shape[1] == B_COLS
    return w.astype(BF16), w_out[cols, :].astype(BF16)


def _sortable(x):
    bits = pltpu.bitcast(x, I32)
    return bits ^ ((bits >> 31) & jnp.int32(0x7FFFFFFF))


def _bit_transpose32(words):
    a = list(words)
    j, m = 16, 0x0000FFFF
    while j:
        mask = jnp.int32(m - (1 << 32) if m >= (1 << 31) else m)
        shift = jnp.full(a[0].shape, j, I32)
        k = 0
        while k < 32:
            t = (a[k] ^ lax.shift_right_logical(a[k + j], shift)) & mask
            a[k] = a[k] ^ t
            a[k + j] = a[k + j] ^ lax.shift_left(t, shift)
            k = (k + j + 1) & ~j
        j >>= 1
        m = (m ^ (m << j)) & 0xFFFFFFFF
    return a


def _sublane_allsum(x):
    x = x + pltpu.roll(x, 4, 0)
    x = x + pltpu.roll(x, 2, 0)
    return x + pltpu.roll(x, 1, 0)


def _tree_sum(parts):
    parts = list(parts)
    while len(parts) > 1:
        odd = [parts[-1]] if len(parts) % 2 else []
        parts = [parts[j] + parts[j + 1] for j in range(0, len(parts) - 1, 2)] + odd
    return parts[0]


GROUP_KEYS = 256


def _dsa_kernel(q_ref, gate_ref, qi_ref, k_ref, ki_ref, wi_ref, v_ref, y_ref,
                keys_ref, planes_ref, bias_ref, kt_ref, qs_ref, s0_ref, m_ref, l_ref, acc_ref,
                *, k_sel):
    i = pl.program_id(1)
    seq = keys_ref.shape[0]
    n_tiles = (i * Q_BLOCK) // KEY_TILE + 1
    n_pairs = B_HEADS // 2
    n_groups_max = seq // GROUP_KEYS
    lane = lax.broadcasted_iota(I32, (Q_BLOCK, LANES), 1)
    first_head = lane < HEAD_DIM

    @pl.when(i == 0)
    def _():
        planes_ref[...] = jnp.zeros(planes_ref.shape, I32)

    w_t = (wi_ref[...].astype(F32) * (IDX_HEADS ** -0.5 * HEAD_DIM ** -0.5)).T
    qi_stacked = [_stack_heads(qi_ref[:, p * LANES:(p + 1) * LANES], first_head)
                  for p in range(IDX_HEADS // 2)]
    key_row = lax.broadcasted_iota(I32, (KEY_TILE, Q_BLOCK), 0)
    q_pos = i * Q_BLOCK + lax.broadcasted_iota(I32, (KEY_TILE, Q_BLOCK), 1)

    def score_tile(t, carry):
        base = pl.multiple_of(t * KEY_TILE, KEY_TILE)
        kk = ki_ref[pl.ds(base, KEY_TILE), :]
        score = jnp.zeros((KEY_TILE, Q_BLOCK), F32)
        for p in range(IDX_HEADS // 2):
            sc = lax.dot_general(kk, qi_stacked[p], (((1,), (1,)), ((), ())),
                                 preferred_element_type=F32)
            score += jnp.maximum(sc[:, :Q_BLOCK], 0.0) * w_t[2 * p:2 * p + 1, :]
            score += jnp.maximum(sc[:, Q_BLOCK:], 0.0) * w_t[2 * p + 1:2 * p + 2, :]
        causal = (key_row + base) <= q_pos
        score = jnp.where(causal, score + 0.0, -jnp.inf)
        key = _sortable(score)
        keys_ref[pl.ds(base, KEY_TILE), :] = key
        ukey = key ^ jnp.int32(INT_MIN)
        for g in range(KEY_TILE // GROUP_KEYS):
            words = [ukey[g * GROUP_KEYS + 8 * j:g * GROUP_KEYS + 8 * j + 8, :] for j in range(32)]
            rows = pl.ds(pl.multiple_of(t * (KEY_TILE // 32) + 8 * g, 8), 8)
            for b, plane in enumerate(_bit_transpose32(words)):
                planes_ref[b, rows, :] = plane
        return carry

    lax.fori_loop(0, n_tiles, score_tile, 0)

    def select_threshold():
        n_groups = n_tiles * (KEY_TILE // GROUP_KEYS)
        alive0 = tuple(jnp.full((8, Q_BLOCK), -1, I32) * (g < n_groups).astype(I32)
                       for g in range(n_groups_max))
        need0 = jnp.full((8, Q_BLOCK), k_sel, I32)

        def step(st, carry):
            alive, need, prefix = carry
            planes = [planes_ref[st, 8 * g:8 * g + 8, :] for g in range(n_groups_max)]
            ones = [a & p for a, p in zip(alive, planes)]
            cnt = _sublane_allsum(_tree_sum([lax.population_count(o) for o in ones]))
            take = cnt >= need
            prefix = jnp.where(take, prefix | jnp.left_shift(jnp.int32(1), 31 - st), prefix)
            need = jnp.where(take, need, need - cnt)
            flip = jnp.where(take, 0, -1)
            alive = tuple(a & (p ^ flip) for a, p in zip(alive, planes))
            return alive, need, prefix

        alive, need, prefix = lax.fori_loop(0, 32, step, (alive0, need0, jnp.zeros((8, Q_BLOCK), I32)))
        thr = (prefix ^ jnp.int32(INT_MIN))[:1, :]
        need_eq = need[:1, :]

        n_eq = _sublane_allsum(_tree_sum([lax.population_count(a) for a in alive]))[:1, :]
        tied = (n_eq > need_eq) & (thr > KEY_NEG_INF)
        idx_bits = (seq - 1).bit_length()

        def tie_break():
            def count_eq_before(cut):
                def body(t, acc):
                    base = pl.multiple_of(t * KEY_TILE, KEY_TILE)
                    hit = (keys_ref[pl.ds(base, KEY_TILE), :] == thr) & ((key_row + base) < cut)
                    return acc + jnp.sum(hit.astype(I32).reshape(KEY_TILE // 8, 8, Q_BLOCK), axis=0)
                acc = lax.fori_loop(0, n_tiles, body, jnp.zeros((8, Q_BLOCK), I32))
                return jnp.sum(acc, axis=0, keepdims=True)

            def bit_step(it, cut):
                cand = cut + jnp.left_shift(jnp.int32(1), idx_bits - 1 - it)
                return jnp.where(count_eq_before(cand) < need_eq, cand, cut)

            return lax.fori_loop(0, idx_bits, bit_step, jnp.zeros((1, Q_BLOCK), I32))

        any_tied = jnp.max(tied.astype(I32)) > 0
        idx_cut = lax.cond(any_tied, tie_break, lambda: jnp.full((1, Q_BLOCK), seq, I32))
        return thr, idx_cut

    all_selected = (i + 1) * Q_BLOCK <= k_sel
    thr, idx_cut = lax.cond(
        all_selected,
        lambda: (jnp.full((1, Q_BLOCK), INT_MIN, I32), jnp.full((1, Q_BLOCK), seq, I32)),
        select_threshold)

    def bias_tile(t, carry):
        base = pl.multiple_of(t * KEY_TILE, KEY_TILE)
        key = keys_ref[pl.ds(base, KEY_TILE), :]
        k_idx = key_row + base
        sel = ((key > thr) | ((key == thr) & (k_idx <= idx_cut))) & (k_idx <= q_pos)
        bias_t = jnp.where(sel, 0.0, -jnp.inf).astype(F32)
        for c in range(KEY_TILE // LANES):
            col = pl.multiple_of(base + c * LANES, LANES)
            bias_ref[:, pl.ds(col, LANES)] = bias_t[c * LANES:(c + 1) * LANES, :].T
        return carry

    lax.fori_loop(0, n_tiles, bias_tile, 0)

    for p in range(n_pairs):
        cols = slice(p * LANES, (p + 1) * LANES)
        qs_ref[p] = _stack_heads(q_ref[:, cols], first_head)
    m_ref[...] = jnp.full(m_ref.shape, F32_MIN, F32)
    l_ref[...] = jnp.zeros(l_ref.shape, F32)
    acc_ref[...] = jnp.zeros(acc_ref.shape, F32)
    heads_per_kv_pair = 2 * (B_HEADS // B_KV_HEADS) // 2

    ones_blk = jnp.ones((KEY_TILE, LANES), BF16)

    @pl.when(i == 0)
    def _():
        def xpose(r, carry):
            rows = pl.ds(pl.multiple_of(r * LANES, LANES), LANES)
            for c in range(kt_ref.shape[0]):
                blk = k_ref[rows, c * LANES:(c + 1) * LANES].astype(F32)
                kt_ref[c, :, rows] = blk.T.astype(BF16)
            return carry
        lax.fori_loop(0, seq // LANES, xpose, 0)

    def logits(base, p):
        kt = kt_ref[p // heads_per_kv_pair, :, pl.ds(base, KEY_TILE)]
        return jnp.dot(qs_ref[p], kt, preferred_element_type=F32)

    s0_ref[...] = logits(0, 0)

    def attn_tile(t, carry):
        base = pl.multiple_of(t * KEY_TILE, KEY_TILE)
        next_base = pl.multiple_of(jnp.minimum(t + 1, n_tiles - 1) * KEY_TILE, KEY_TILE)
        bias = bias_ref[:, pl.ds(base, KEY_TILE)]
        bias2 = jnp.concatenate([bias, bias], axis=0)

        s_next = s0_ref[...]
        for p in range(n_pairs):
            kv = p // heads_per_kv_pair
            kv_cols = slice(kv * LANES, (kv + 1) * LANES)
            s = s_next + bias2
            s_next = logits(base, p + 1) if p + 1 < n_pairs else logits(next_base, 0)
            m_blk = s[:, :LANES]
            for c in range(1, KEY_TILE // LANES):
                m_blk = jnp.maximum(m_blk, s[:, c * LANES:(c + 1) * LANES])
            m_old = m_ref[p]
            m_new = jnp.maximum(m_old, jnp.max(m_blk, axis=-1, keepdims=True))
            alpha = jnp.exp2(m_old - m_new)
            e = jnp.concatenate(
                [jnp.exp2(s[:, c * LANES:(c + 1) * LANES] - m_new).astype(BF16)
                 for c in range(KEY_TILE // LANES)], axis=1)
            v_ext = jnp.concatenate([v_ref[pl.ds(base, KEY_TILE), kv_cols], ones_blk], axis=1)
            pv = jnp.dot(e, v_ext, preferred_element_type=F32)
            acc_ref[p] = acc_ref[p] * alpha + pv[:, :LANES]
            l_ref[p] = l_ref[p] * alpha + pv[:, LANES:]
            m_ref[p] = m_new
        s0_ref[...] = s_next
        return carry

    lax.fori_loop(0, n_tiles, attn_tile, 0)

    for p in range(n_pairs):
        cols = slice(p * LANES, (p + 1) * LANES)
        l = l_ref[p]
        acc = acc_ref[p]
        num = jnp.where(first_head, acc[:Q_BLOCK], acc[Q_BLOCK:])
        den = jnp.where(first_head, l[:Q_BLOCK], l[Q_BLOCK:])
        gate = gate_ref[:, cols].astype(F32)
        y_ref[:, cols] = (num * gate / (den * (1.0 + jnp.exp(-gate)))).astype(y_ref.dtype)


def dsa_attention(proj):
    b, s, _ = proj.shape
    k_sel = min(TOPK_MAX, s // 4)
    assert s % KEY_TILE == 0 and k_sel <= KEY_TILE
    kvw = B_KV_HEADS * HEAD_DIM
    n_pairs = B_HEADS // 2

    def qblock(width, idx):
        return pl.BlockSpec((None, Q_BLOCK, width), lambda bi, i: (bi, i, idx))

    def full(width, idx):
        return pl.BlockSpec((None, s, width), lambda bi, i: (bi, 0, idx))

    return pl.pallas_call(
        functools.partial(_dsa_kernel, k_sel=k_sel),
        grid=(b, s // Q_BLOCK),
        in_specs=[
            qblock(B_WIDTH, 0),
            qblock(B_WIDTH, 1),
            qblock(IDX_HEADS * HEAD_DIM, 4),
            full(kvw, 10),
            full(LANES, 22),
            qblock(LANES, 24),
            full(kvw, 13),
        ],
        out_specs=qblock(B_WIDTH, 0),
        out_shape=jax.ShapeDtypeStruct((b, s, B_WIDTH), BF16),
        scratch_shapes=[
            pltpu.VMEM((s, Q_BLOCK), I32),
            pltpu.VMEM((32, s // 32, Q_BLOCK), I32),
            pltpu.VMEM((Q_BLOCK, s), F32),
            pltpu.VMEM((kvw // LANES, LANES, s), BF16),
            pltpu.VMEM((n_pairs, 2 * Q_BLOCK, LANES), BF16),
            pltpu.VMEM((2 * Q_BLOCK, KEY_TILE), F32),
            pltpu.VMEM((n_pairs, 2 * Q_BLOCK, LANES), F32),
            pltpu.VMEM((n_pairs, 2 * Q_BLOCK, LANES), F32),
            pltpu.VMEM((n_pairs, 2 * Q_BLOCK, LANES), F32),
        ],
        compiler_params=_cparams(("arbitrary", "arbitrary")),
        name="dsa_attention",
    )(proj, proj, proj, proj, proj, proj, proj)


def dsa_mixer(h, g, scale, shift, w_in_packed, cos, sin):
    proj = in_proj(h, g, scale, shift, w_in_packed, [f == 1 for f in B_ROPE_FLAGS], cos, sin, 512)
    return dsa_attention(proj)


def kernel(x, c, positions, norm_g, ada_w, ada_b, a_w_in, a_w_out, b_w_in, b_w_out, final_g):
    depth = norm_g.shape[0]
    d = x.shape[-1]
    mod = adaln_mod(c, ada_w, ada_b)
    cos, sin = rope_tables(positions)
    tables = {1: (cos, sin)}
    for _, dilation in A_GROUPS:
        if dilation > 1:
            tables[dilation] = class_major_rope_tables(positions, dilation)
    h = x
    for i in range(depth):
        shift, scale, gate = mod[i, :, :d], mod[i, :, d:2 * d], mod[i, :, 2 * d:]
        if i % 2 == 0:
            y = dilated_mixer(h, norm_g[i], scale, shift, a_w_in[i // 2], tables)
            w_out = a_w_out[i // 2].astype(BF16)
        else:
            w_in, w_out = _dsa_weights(b_w_in[i // 2], b_w_out[i // 2])
            y = dsa_mixer(h, norm_g[i], scale, shift, w_in, cos, sin)
        h = out_proj(y, w_out, gate, h)
    return final_norm(h, final_g)
```

```python
import functools

import numpy as np
import jax
import jax.numpy as jnp
from jax import lax
from jax.experimental import pallas as pl
from jax.experimental.pallas import tpu as pltpu

F32 = jnp.float32
BF16 = jnp.bfloat16
I32 = jnp.int32

LANES = 128
HEAD_DIM = 64
HALF = HEAD_DIM // 2
ROPE_THETA = 10000.0
NORM_EPS = 1e-6
VMEM_LIMIT = 56 * 1024 * 1024
ROW_CHUNK = 256

A_HEADS = 16
A_WIDTH = A_HEADS * HEAD_DIM
A_GROUPS = ((128, 1), (512, 4), (2048, 16))
A_BLK = 128

B_HEADS = 16
B_KV_HEADS = 4
B_WIDTH = B_HEADS * HEAD_DIM
IDX_HEADS = 8
TOPK_MAX = 256
Q_BLOCK = 128
KEY_TILE = 512
INT_MIN = -(2 ** 31)
KEY_NEG_INF = -2139095041
F32_MIN = float(np.finfo(np.float32).min)
LOG2_E = float(np.log2(np.e))
Q_PRESCALE = LOG2_E * HEAD_DIM ** -0.5


def _cparams(sem):
    return pltpu.CompilerParams(dimension_semantics=sem, vmem_limit_bytes=VMEM_LIMIT)


def _adaln_kernel(c_ref, w_ref, b_ref, o_ref):
    c = c_ref[...]
    ca = (c * (1.0 / (1.0 + jnp.exp(-c)))).astype(BF16)
    acc = jnp.dot(ca, w_ref[...].astype(BF16), preferred_element_type=F32)
    o_ref[...] = acc + b_ref[...]


def adaln_mod(c, ada_w, ada_b):
    depth, d, d3 = ada_w.shape
    b = c.shape[0]
    tn = 1024
    return pl.pallas_call(
        _adaln_kernel,
        grid=(depth, d3 // tn),
        in_specs=[
            pl.BlockSpec((b, d), lambda i, j: (0, 0)),
            pl.BlockSpec((None, d, tn), lambda i, j: (i, 0, j)),
            pl.BlockSpec((None, 1, tn), lambda i, j: (i, 0, j)),
        ],
        out_specs=pl.BlockSpec((None, b, tn), lambda i, j: (i, 0, j)),
        out_shape=jax.ShapeDtypeStruct((depth, b, d3), F32),
        compiler_params=_cparams(("arbitrary", "arbitrary")),
        name="adaln_mod",
    )(c, ada_w, ada_b.reshape(depth, 1, d3))


def _rope_table_kernel(pos_ref, inv_ref, cos_ref, sin_ref):
    ang = pos_ref[...].astype(F32) * inv_ref[...]
    lane = lax.broadcasted_iota(I32, ang.shape, 1)
    first_half = (lane % HEAD_DIM) < HALF
    cos_ref[...] = jnp.cos(ang)
    s = jnp.sin(ang)
    sin_ref[...] = jnp.where(first_half, -s, s)


def rope_tables(positions):
    b, s = positions.shape
    inv_freq = ROPE_THETA ** (-jnp.arange(HALF, dtype=F32) / HALF)
    inv_lane = jnp.tile(inv_freq, LANES // HALF).reshape(1, LANES)
    ts = min(s, 1024)
    out = jax.ShapeDtypeStruct((b, s, LANES), F32)
    return pl.pallas_call(
        _rope_table_kernel,
        grid=(b, s // ts),
        in_specs=[
            pl.BlockSpec((None, ts, 1), lambda i, j: (i, j, 0)),
            pl.BlockSpec((1, LANES), lambda i, j: (0, 0)),
        ],
        out_specs=[pl.BlockSpec((None, ts, LANES), lambda i, j: (i, j, 0))] * 2,
        out_shape=[out, out],
        compiler_params=_cparams(("arbitrary", "arbitrary")),
        name="rope_tables",
    )(positions.reshape(b, s, 1), inv_lane)


def _rope_block(t, cos, sin_signed, first_half):
    partner = jnp.where(first_half, pltpu.roll(t, LANES - HALF, 1), pltpu.roll(t, HALF, 1))
    return t * cos + partner * sin_signed


def _in_proj_kernel(x_ref, g_ref, sc_ref, sh_ref, w_ref, cos_ref, sin_ref, o_ref, u_ref, *stage,
                    dilation, rope_tiles, tn):
    tm = u_ref.shape[0]
    nblk = tn // LANES
    per = ROW_CHUNK // dilation
    x = x_ref[...]
    ms = jnp.mean(x * x, axis=-1, keepdims=True)
    xn = x * lax.rsqrt(ms + NORM_EPS)
    u_ref[...] = (xn * g_ref[...] * (1.0 + sc_ref[...]) + sh_ref[...]).astype(BF16)
    lane = lax.broadcasted_iota(I32, (per, LANES), 1)
    first_half = (lane % HEAD_DIM) < HALF
    step = 0
    for j, rope in enumerate(rope_tiles):
        for c in range(tm // ROW_CHUNK):
            acc = jnp.dot(u_ref[c * ROW_CHUNK:(c + 1) * ROW_CHUNK, :], w_ref[:, j * tn:(j + 1) * tn],
                          preferred_element_type=F32)
            if dilation > 1:
                slot = step % stage[0].shape[0]
                step += 1
                for k in range(nblk):
                    stage[0][slot, k] = acc[:, k * LANES:(k + 1) * LANES]
            for r in range(dilation):
                rows = slice(c * per, (c + 1) * per)
                if rope:
                    cos = cos_ref[rows, :] if dilation == 1 else cos_ref[r, rows, :]
                    sin = sin_ref[rows, :] if dilation == 1 else sin_ref[r, rows, :]
                for k in range(nblk):
                    cols = slice(j * tn + k * LANES, j * tn + (k + 1) * LANES)
                    if dilation > 1:
                        blk = stage[0][slot, k, pl.ds(r, per, stride=dilation), :]
                    else:
                        blk = acc[:, k * LANES:(k + 1) * LANES]
                    if rope:
                        blk = _rope_block(blk, cos, sin, first_half)
                    if dilation > 1:
                        o_ref[r, rows, cols] = blk.astype(o_ref.dtype)
                    else:
                        o_ref[rows, cols] = blk.astype(o_ref.dtype)


def in_proj(h, g, scale, shift, w, rope_tiles, cos, sin, tn, dilation=1):
    b, s, d = h.shape
    n = w.shape[1]
    tm = min(s, 512)
    assert tm % ROW_CHUNK == 0 and ROW_CHUNK % (16 * dilation) == 0 and len(rope_tiles) * tn == n
    scratch = [pltpu.VMEM((tm, d), BF16)]
    if dilation == 1:
        tab_spec = pl.BlockSpec((None, tm, LANES), lambda bi, i: (bi, i, 0))
        out_spec = pl.BlockSpec((None, tm, n), lambda bi, i: (bi, i, 0))
        out_shape = jax.ShapeDtypeStruct((b, s, n), BF16)
    else:
        tab_spec = pl.BlockSpec((None, dilation, tm // dilation, LANES), lambda bi, i: (bi, 0, i, 0))
        out_spec = pl.BlockSpec((None, dilation, tm // dilation, n), lambda bi, i: (bi, 0, i, 0))
        out_shape = jax.ShapeDtypeStruct((b, dilation, s // dilation, n), BF16)
        scratch.append(pltpu.VMEM((2, tn // LANES, ROW_CHUNK, LANES), F32))
    return pl.pallas_call(
        functools.partial(_in_proj_kernel, dilation=dilation, rope_tiles=tuple(rope_tiles), tn=tn),
        grid=(b, s // tm),
        in_specs=[
            pl.BlockSpec((None, tm, d), lambda bi, i: (bi, i, 0)),
            pl.BlockSpec((1, d), lambda bi, i: (0, 0)),
            pl.BlockSpec((None, 1, d), lambda bi, i: (bi, 0, 0)),
            pl.BlockSpec((None, 1, d), lambda bi, i: (bi, 0, 0)),
            pl.BlockSpec((d, n), lambda bi, i: (0, 0)),
            tab_spec, tab_spec,
        ],
        out_specs=out_spec,
        out_shape=out_shape,
        scratch_shapes=scratch,
        compiler_params=_cparams(("arbitrary", "arbitrary")),
        name=f"in_proj_d{dilation}",
    )(h, g.reshape(1, d), scale.reshape(b, 1, d), shift.reshape(b, 1, d), w, cos, sin)


def _out_proj_kernel(y_ref, w_ref, gate_ref, h_ref, o_ref):
    acc = jnp.dot(y_ref[...], w_ref[...], preferred_element_type=F32)
    o_ref[...] = h_ref[...] + gate_ref[...] * acc


def out_proj(y, w, gate, h):
    b, s, d = h.shape
    k = y.shape[-1]
    tm = min(s, 1024)
    return pl.pallas_call(
        _out_proj_kernel,
        grid=(b, s // tm),
        in_specs=[
            pl.BlockSpec((None, tm, k), lambda bi, i: (bi, i, 0)),
            pl.BlockSpec((k, d), lambda bi, i: (0, 0)),
            pl.BlockSpec((None, 1, d), lambda bi, i: (bi, 0, 0)),
            pl.BlockSpec((None, tm, d), lambda bi, i: (bi, i, 0)),
        ],
        out_specs=pl.BlockSpec((None, tm, d), lambda bi, i: (bi, i, 0)),
        out_shape=jax.ShapeDtypeStruct((b, s, d), F32),
        compiler_params=_cparams(("arbitrary", "arbitrary")),
        name="out_proj",
    )(y, w, gate.reshape(b, 1, d), h)


def _final_norm_kernel(x_ref, g_ref, o_ref):
    x = x_ref[...]
    ms = jnp.mean(x * x, axis=-1, keepdims=True)
    o_ref[...] = x * lax.rsqrt(ms + NORM_EPS) * g_ref[...]


def final_norm(h, g):
    b, s, d = h.shape
    tm = min(s, 1024)
    return pl.pallas_call(
        _final_norm_kernel,
        grid=(b, s // tm),
        in_specs=[
            pl.BlockSpec((None, tm, d), lambda bi, i: (bi, i, 0)),
            pl.BlockSpec((1, d), lambda bi, i: (0, 0)),
        ],
        out_specs=pl.BlockSpec((None, tm, d), lambda bi, i: (bi, i, 0)),
        out_shape=jax.ShapeDtypeStruct((b, s, d), F32),
        compiler_params=_cparams(("arbitrary", "arbitrary")),
        name="final_norm",
    )(h, g.reshape(1, d))


def _stack_heads(pair, first_head):
    zero = jnp.zeros_like(pair)
    return jnp.concatenate([jnp.where(first_head, pair, zero), jnp.where(first_head, zero, pair)], axis=0)


def _banded_kernel(q_ref, kp_ref, kc_ref, vp_ref, vc_ref, o_ref, lse_ref, kwin_ref, vwin_ref, *, tq):
    i = pl.program_id(2)
    nsub = tq // A_BLK
    lane = lax.broadcasted_iota(I32, (A_BLK, LANES), 1)
    first_head = lane < HEAD_DIM
    odd_lane = (lane % 2) == 1
    r2 = lax.broadcasted_iota(I32, (2 * A_BLK, 2 * A_BLK), 0) % A_BLK
    c2 = lax.broadcasted_iota(I32, (2 * A_BLK, 2 * A_BLK), 1)
    own_ok = (c2 >= A_BLK) & (c2 - A_BLK <= r2)
    ones_blk = jnp.ones((2 * A_BLK, LANES), BF16)
    n_pairs = A_HEADS // 2

    kwin_ref[:A_BLK, :] = kp_ref[...]
    kwin_ref[A_BLK:, :] = kc_ref[...]
    vwin_ref[:A_BLK, :] = vp_ref[...]
    vwin_ref[A_BLK:, :] = vc_ref[...]

    def logits(a, p):
        cols = slice(p * LANES, (p + 1) * LANES)
        qs = _stack_heads(q_ref[a * A_BLK:(a + 1) * A_BLK, cols], first_head)
        k2 = kwin_ref[a * A_BLK:(a + 2) * A_BLK, cols]
        return lax.dot_general(qs, k2, (((1,), (1,)), ((), ())), preferred_element_type=F32)

    units = [(a, p) for a in range(nsub) for p in range(n_pairs)]
    s_next = logits(*units[0])
    lse_tile = bias = None
    for u, (a, p) in enumerate(units):
        rows = slice(a * A_BLK, (a + 1) * A_BLK)
        cols = slice(p * LANES, (p + 1) * LANES)
        if p == 0:
            prev_shift = jnp.where(i * nsub + a > 0, 0, 2 * A_BLK)
            mask = own_ok | ((c2 < A_BLK) & (c2 >= r2 + prev_shift))
            bias = jnp.where(mask, 0.0, -jnp.inf).astype(F32)
            lse_tile = jnp.zeros((A_BLK, LANES), F32)
        s = s_next + bias
        if u + 1 < len(units):
            s_next = logits(*units[u + 1])
        m = jnp.max(s, axis=-1, keepdims=True)
        e = jnp.exp2(s - m).astype(BF16)
        v_ext = jnp.concatenate([vwin_ref[a * A_BLK:(a + 2) * A_BLK, cols], ones_blk], axis=1)
        pv = jnp.dot(e, v_ext, preferred_element_type=F32)
        num = jnp.where(first_head, pv[:A_BLK, :LANES], pv[A_BLK:, :LANES])
        o_ref[rows, cols] = num * (1.0 / jnp.where(first_head, pv[:A_BLK, LANES:], pv[A_BLK:, LANES:]))
        den_pair = jnp.where(odd_lane, pv[A_BLK:, LANES:], pv[:A_BLK, LANES:])
        m_pair = jnp.where(odd_lane, m[A_BLK:], m[:A_BLK])
        lse_tile = jnp.where((lane // 2) == p, m_pair + jnp.log(den_pair) * LOG2_E, lse_tile)
        if p == n_pairs - 1:
            lse_ref[rows, :] = lse_tile


def banded_group_attention(qkv):
    b, d, n, _ = qkv.shape
    tq = min(2 * A_BLK, n)
    sub = tq // A_BLK

    def cur(blk):
        return pl.BlockSpec((None, None, tq, A_WIDTH), lambda bi, r, i: (bi, r, i, blk))

    def prev(blk):
        return pl.BlockSpec((None, None, A_BLK, A_WIDTH),
                            lambda bi, r, i: (bi, r, jnp.maximum(i * sub - 1, 0), blk))

    return pl.pallas_call(
        functools.partial(_banded_kernel, tq=tq),
        grid=(b, d, n // tq),
        in_specs=[cur(0), prev(1), cur(1), prev(2), cur(2)],
        out_specs=[
            pl.BlockSpec((None, None, tq, A_WIDTH), lambda bi, r, i: (bi, r, i, 0)),
            pl.BlockSpec((None, None, tq, LANES), lambda bi, r, i: (bi, r, i, 0)),
        ],
        out_shape=[
            jax.ShapeDtypeStruct((b, d, n, A_WIDTH), F32),
            jax.ShapeDtypeStruct((b, d, n, LANES), F32),
        ],
        scratch_shapes=[pltpu.VMEM((tq + A_BLK, A_WIDTH), BF16)] * 2,
        compiler_params=_cparams(("arbitrary", "arbitrary", "arbitrary")),
        name=f"banded_attn_d{d}",
    )(qkv, qkv, qkv, qkv, qkv)


def _expand_heads(x, expand):
    hi = x.astype(BF16)
    r1 = x - hi.astype(F32)
    mid = r1.astype(BF16)
    lo = (r1 - mid.astype(F32)).astype(BF16)
    out = jnp.dot(hi, expand, preferred_element_type=F32)
    out += jnp.dot(mid, expand, preferred_element_type=F32)
    out += jnp.dot(lo, expand, preferred_element_type=F32)
    return out


def _merge_kernel(o0_ref, o1_ref, o2_ref, l0_ref, l1_ref, l2_ref, gate_ref, ex_ref, y_ref,
                  ot_ref, lt_ref, *, dilations):
    o_refs, l_refs = (o0_ref, o1_ref, o2_ref), (l0_ref, l1_ref, l2_ref)
    tm = y_ref.shape[0]
    nblk = y_ref.shape[1] // LANES
    for g, d in enumerate(dilations):
        if d == 1:
            continue
        per = tm // d
        for r in range(d):
            lt_ref[g, pl.ds(r, per, stride=d), :] = l_refs[g][r]
            for k in range(nblk):
                ot_ref[g, k, pl.ds(r, per, stride=d), :] = o_refs[g][r, :, k * LANES:(k + 1) * LANES]
    lses = [l_refs[g][0] if d == 1 else lt_ref[g] for g, d in enumerate(dilations)]
    m = jnp.maximum(jnp.maximum(lses[0], lses[1]), lses[2])
    es = [jnp.exp2(l - m) for l in lses]
    inv = 1.0 / (es[0] + es[1] + es[2])
    ex = ex_ref[...]
    alphas = [_expand_heads(e * inv, ex) for e in es]
    for k in range(nblk):
        cols = slice(k * LANES, (k + 1) * LANES)
        y = jnp.zeros((tm, LANES), F32)
        for g, d in enumerate(dilations):
            og = o_refs[g][0, :, cols] if d == 1 else ot_ref[g, k]
            y += alphas[g][:, cols] * og
        gate = gate_ref[:, cols].astype(F32)
        y_ref[:, cols] = (y * gate * (1.0 / (1.0 + jnp.exp(-gate)))).astype(y_ref.dtype)


def merge_groups(outs, lses, gate):
    b, s, w = gate.shape
    tm = min(s, 256)
    dil = tuple(o.shape[1] for o in outs)
    expand = (jnp.arange(LANES)[:, None] == (jnp.arange(w)[None, :] // HEAD_DIM)).astype(BF16)

    def cm(d, width):
        return pl.BlockSpec((None, d, tm // d, width), lambda bi, i: (bi, 0, i, 0))

    tspec = pl.BlockSpec((None, tm, w), lambda bi, i: (bi, i, 0))
    return pl.pallas_call(
        functools.partial(_merge_kernel, dilations=dil),
        grid=(b, s // tm),
        in_specs=[cm(d, w) for d in dil] + [cm(d, LANES) for d in dil]
                 + [tspec, pl.BlockSpec((LANES, w), lambda bi, i: (0, 0))],
        out_specs=tspec,
        out_shape=jax.ShapeDtypeStruct((b, s, w), BF16),
        scratch_shapes=[pltpu.VMEM((len(dil), w // LANES, tm, LANES), F32),
                        pltpu.VMEM((len(dil), tm, LANES), F32)],
        compiler_params=_cparams(("arbitrary", "arbitrary")),
        name="merge_groups",
    )(*outs, *lses, gate, expand)


def _class_major(t, d):
    b, s = t.shape[:2]
    return jnp.swapaxes(t.reshape(b, s // d, d, *t.shape[2:]), 1, 2)


def class_major_rope_tables(cos, sin, dilation):
    return _class_major(cos, dilation), _class_major(sin, dilation)


def dilated_mixer(h, g, scale, shift, w_in, tables):
    tn = 512
    per = A_WIDTH // tn
    qkv_flags = (True,) * (2 * per) + (False,) * per
    w = w_in.astype(BF16)
    outs, lses = [], []
    for gi, (window, dilation) in enumerate(A_GROUPS):
        assert window // dilation == A_BLK
        wg = w_in[:, gi * 3 * A_WIDTH:(gi + 1) * 3 * A_WIDTH]
        wg = jnp.concatenate([wg[:, :A_WIDTH] * Q_PRESCALE, wg[:, A_WIDTH:]], axis=1).astype(BF16)
        qkv = in_proj(h, g, scale, shift, wg, qkv_flags, *tables[dilation], tn, dilation)
        if dilation == 1:
            qkv = qkv[:, None]
        o, lse = banded_group_attention(qkv)
        outs.append(o)
        lses.append(lse)
    gate = in_proj(h, g, scale, shift, w[:, len(A_GROUPS) * 3 * A_WIDTH:], (False,) * per,
                   *tables[1], tn)
    return merge_groups(outs, lses, gate)


B_COLS = 3584
B_ROPE_FLAGS = (1, 1, 0, 0, 1, 1, 0)


def _dsa_head_perm():
    group = B_HEADS // B_KV_HEADS
    order = []
    for g2 in range(B_KV_HEADS // 2):
        for r in range(group):
            order += [(2 * g2) * group + r, (2 * g2 + 1) * group + r]
    return np.asarray(order)


def _dsa_weights(w_in, w_out):
    d = w_in.shape[0]
    cuts = np.cumsum((B_WIDTH, B_KV_HEADS * HEAD_DIM, B_KV_HEADS * HEAD_DIM, IDX_HEADS * HEAD_DIM,
                      HEAD_DIM, IDX_HEADS, B_WIDTH))[:-1]
    wq, wk, wv, wqi, wki, wwi, wg = jnp.split(w_in, cuts, axis=1)
    cols = (_dsa_head_perm()[:, None] * HEAD_DIM + np.arange(HEAD_DIM)[None, :]).reshape(-1)
    zeros = lambda n: jnp.zeros((d, n), w_in.dtype)
    wq = wq * Q_PRESCALE
    w = jnp.concatenate([wq[:, cols], wg[:, cols], wqi, wk, wki, wki, zeros(LANES),
                         wwi, zeros(2 * LANES - IDX_HEADS), wv], axis=1)
    assert w.shape[1] == B_COLS
    return w.astype(BF16), w_out[cols, :].astype(BF16)


def _sortable(x):
    bits = pltpu.bitcast(x, I32)
    return bits ^ ((bits >> 31) & jnp.int32(0x7FFFFFFF))


def _bit_transpose32(words):
    a = list(words)
    j, m = 16, 0x0000FFFF
    while j:
        mask = jnp.int32(m - (1 << 32) if m >= (1 << 31) else m)
        shift = jnp.full(a[0].shape, j, I32)
        k = 0
        while k < 32:
            t = (a[k] ^ lax.shift_right_logical(a[k + j], shift)) & mask
            a[k] = a[k] ^ t
            a[k + j] = a[k + j] ^ lax.shift_left(t, shift)
            k = (k + j + 1) & ~j
        j >>= 1
        m = (m ^ (m << j)) & 0xFFFFFFFF
    return a


def _sublane_allsum(x):
    x = x + pltpu.roll(x, 4, 0)
    x = x + pltpu.roll(x, 2, 0)
    return x + pltpu.roll(x, 1, 0)


def _tree_sum(parts):
    parts = list(parts)
    while len(parts) > 1:
        odd = [parts[-1]] if len(parts) % 2 else []
        parts = [parts[j] + parts[j + 1] for j in range(0, len(parts) - 1, 2)] + odd
    return parts[0]


GROUP_KEYS = 256


def _dsa_kernel(q_ref, gate_ref, qi_ref, k_ref, ki_ref, wi_ref, v_ref, y_ref,
                keys_ref, planes_ref, bias_ref, kt_ref, qs_ref, s0_ref, m_ref, l_ref, acc_ref,
                *, k_sel):
    i = pl.program_id(1)
    seq = keys_ref.shape[0]
    n_tiles = (i * Q_BLOCK) // KEY_TILE + 1
    n_pairs = B_HEADS // 2
    n_groups_max = seq // GROUP_KEYS
    lane = lax.broadcasted_iota(I32, (Q_BLOCK, LANES), 1)
    first_head = lane < HEAD_DIM

    @pl.when(i == 0)
    def _():
        planes_ref[...] = jnp.zeros(planes_ref.shape, I32)

    w_t = (wi_ref[...].astype(F32) * (IDX_HEADS ** -0.5 * HEAD_DIM ** -0.5)).T
    qi_stacked = [_stack_heads(qi_ref[:, p * LANES:(p + 1) * LANES], first_head)
                  for p in range(IDX_HEADS // 2)]
    key_row = lax.broadcasted_iota(I32, (KEY_TILE, Q_BLOCK), 0)
    q_pos = i * Q_BLOCK + lax.broadcasted_iota(I32, (KEY_TILE, Q_BLOCK), 1)

    def score_tile(t, carry):
        base = pl.multiple_of(t * KEY_TILE, KEY_TILE)
        kk = ki_ref[pl.ds(base, KEY_TILE), :]
        score = jnp.zeros((KEY_TILE, Q_BLOCK), F32)
        for p in range(IDX_HEADS // 2):
            sc = lax.dot_general(kk, qi_stacked[p], (((1,), (1,)), ((), ())),
                                 preferred_element_type=F32)
            score += jnp.maximum(sc[:, :Q_BLOCK], 0.0) * w_t[2 * p:2 * p + 1, :]
            score += jnp.maximum(sc[:, Q_BLOCK:], 0.0) * w_t[2 * p + 1:2 * p + 2, :]
        causal = (key_row + base) <= q_pos
        score = jnp.where(causal, score + 0.0, -jnp.inf)
        key = _sortable(score)
        keys_ref[pl.ds(base, KEY_TILE), :] = key
        ukey = key ^ jnp.int32(INT_MIN)
        for g in range(KEY_TILE // GROUP_KEYS):
            words = [ukey[g * GROUP_KEYS + 8 * j:g * GROUP_KEYS + 8 * j + 8, :] for j in range(32)]
            rows = pl.ds(pl.multiple_of(t * (KEY_TILE // 32) + 8 * g, 8), 8)
            for b, plane in enumerate(_bit_transpose32(words)):
                planes_ref[b, rows, :] = plane
        return carry

    lax.fori_loop(0, n_tiles, score_tile, 0)

    def select_threshold():
        n_groups = n_tiles * (KEY_TILE // GROUP_KEYS)
        alive0 = tuple(jnp.full((8, Q_BLOCK), -1, I32) * (g < n_groups).astype(I32)
                       for g in range(n_groups_max))
        need0 = jnp.full((8, Q_BLOCK), k_sel, I32)

        def step(st, carry):
            alive, need, prefix = carry
            planes = [planes_ref[st, 8 * g:8 * g + 8, :] for g in range(n_groups_max)]
            ones = [a & p for a, p in zip(alive, planes)]
            cnt = _sublane_allsum(_tree_sum([lax.population_count(o) for o in ones]))
            take = cnt >= need
            prefix = jnp.where(take, prefix | jnp.left_shift(jnp.int32(1), 31 - st), prefix)
            need = jnp.where(take, need, need - cnt)
            flip = jnp.where(take, 0, -1)
            alive = tuple(a & (p ^ flip) for a, p in zip(alive, planes))
            return alive, need, prefix

        alive, need, prefix = lax.fori_loop(0, 32, step, (alive0, need0, jnp.zeros((8, Q_BLOCK), I32)))
        thr = (prefix ^ jnp.int32(INT_MIN))[:1, :]
        need_eq = need[:1, :]

        n_eq = _sublane_allsum(_tree_sum([lax.population_count(a) for a in alive]))[:1, :]
        tied = (n_eq > need_eq) & (thr > KEY_NEG_INF)
        idx_bits = (seq - 1).bit_length()

        def tie_break():
            def count_eq_before(cut):
                def body(t, acc):
                    base = pl.multiple_of(t * KEY_TILE, KEY_TILE)
                    hit = (keys_ref[pl.ds(base, KEY_TILE), :] == thr) & ((key_row + base) < cut)
                    return acc + jnp.sum(hit.astype(I32).reshape(KEY_TILE // 8, 8, Q_BLOCK), axis=0)
                acc = lax.fori_loop(0, n_tiles, body, jnp.zeros((8, Q_BLOCK), I32))
                return jnp.sum(acc, axis=0, keepdims=True)

            def bit_step(it, cut):
                cand = cut + jnp.left_shift(jnp.int32(1), idx_bits - 1 - it)
                return jnp.where(count_eq_before(cand) < need_eq, cand, cut)

            return lax.fori_loop(0, idx_bits, bit_step, jnp.zeros((1, Q_BLOCK), I32))

        any_tied = jnp.max(tied.astype(I32)) > 0
        idx_cut = lax.cond(any_tied, tie_break, lambda: jnp.full((1, Q_BLOCK), seq, I32))
        return thr, idx_cut

    all_selected = (i + 1) * Q_BLOCK <= k_sel
    thr, idx_cut = lax.cond(
        all_selected,
        lambda: (jnp.full((1, Q_BLOCK), INT_MIN, I32), jnp.full((1, Q_BLOCK), seq, I32)),
        select_threshold)

    def bias_tile(t, carry):
        base = pl.multiple_of(t * KEY_TILE, KEY_TILE)
        key = keys_ref[pl.ds(base, KEY_TILE), :]
        k_idx = key_row + base
        sel = ((key > thr) | ((key == thr) & (k_idx <= idx_cut))) & (k_idx <= q_pos)
        bias_t = jnp.where(sel, 0.0, -jnp.inf).astype(F32)
        for c in range(KEY_TILE // LANES):
            col = pl.multiple_of(base + c * LANES, LANES)
            bias_ref[:, pl.ds(col, LANES)] = bias_t[c * LANES:(c + 1) * LANES, :].T
        return carry

    lax.fori_loop(0, n_tiles, bias_tile, 0)

    for p in range(n_pairs):
        cols = slice(p * LANES, (p + 1) * LANES)
        qs_ref[p] = _stack_heads(q_ref[:, cols], first_head)
    m_ref[...] = jnp.full(m_ref.shape, F32_MIN, F32)
    l_ref[...] = jnp.zeros(l_ref.shape, F32)
    acc_ref[...] = jnp.zeros(acc_ref.shape, F32)
    heads_per_kv_pair = 2 * (B_HEADS // B_KV_HEADS) // 2

    ones_blk = jnp.ones((KEY_TILE, LANES), BF16)

    @pl.when(i == 0)
    def _():
        def xpose(r, carry):
            rows = pl.ds(pl.multiple_of(r * LANES, LANES), LANES)
            for c in range(kt_ref.shape[0]):
                blk = k_ref[rows, c * LANES:(c + 1) * LANES].astype(F32)
                kt_ref[c, :, rows] = blk.T.astype(BF16)
            return carry
        lax.fori_loop(0, seq // LANES, xpose, 0)

    def logits(base, p):
        kt = kt_ref[p // heads_per_kv_pair, :, pl.ds(base, KEY_TILE)]
        return jnp.dot(qs_ref[p], kt, preferred_element_type=F32)

    s0_ref[...] = logits(0, 0)

    def attn_tiles(t0, count):
        s_next = s0_ref[...]
        for dt in range(count):
            t = t0 + dt
            base = pl.multiple_of(t * KEY_TILE, KEY_TILE)
            next_base = pl.multiple_of(jnp.minimum(t + 1, n_tiles - 1) * KEY_TILE, KEY_TILE)
            bias = bias_ref[:, pl.ds(base, KEY_TILE)]
            bias2 = jnp.concatenate([bias, bias], axis=0)
            for p in range(n_pairs):
                kv = p // heads_per_kv_pair
                kv_cols = slice(kv * LANES, (kv + 1) * LANES)
                s = s_next + bias2
                s_next = logits(base, p + 1) if p + 1 < n_pairs else logits(next_base, 0)
                m_blk = s[:, :LANES]
                for c in range(1, KEY_TILE // LANES):
                    m_blk = jnp.maximum(m_blk, s[:, c * LANES:(c + 1) * LANES])
                m_old = m_ref[p]
                m_new = jnp.maximum(m_old, jnp.max(m_blk, axis=-1, keepdims=True))
                alpha = jnp.exp2(m_old - m_new)
                e = jnp.concatenate(
                    [jnp.exp2(s[:, c * LANES:(c + 1) * LANES] - m_new).astype(BF16)
                     for c in range(KEY_TILE // LANES)], axis=1)
                v_ext = jnp.concatenate([v_ref[pl.ds(base, KEY_TILE), kv_cols], ones_blk], axis=1)
                pv = jnp.dot(e, v_ext, preferred_element_type=F32)
                acc_ref[p] = acc_ref[p] * alpha + pv[:, :LANES]
                l_ref[p] = l_ref[p] * alpha + pv[:, LANES:]
                m_ref[p] = m_new
        s0_ref[...] = s_next

    def attn_two(t2, carry):
        attn_tiles(2 * t2, 2)
        return carry

    lax.fori_loop(0, n_tiles // 2, attn_two, 0)

    @pl.when(n_tiles % 2 == 1)
    def _():
        attn_tiles(n_tiles - 1, 1)

    for p in range(n_pairs):
        cols = slice(p * LANES, (p + 1) * LANES)
        l = l_ref[p]
        acc = acc_ref[p]
        num = jnp.where(first_head, acc[:Q_BLOCK], acc[Q_BLOCK:])
        den = jnp.where(first_head, l[:Q_BLOCK], l[Q_BLOCK:])
        gate = gate_ref[:, cols].astype(F32)
        y_ref[:, cols] = (num * gate / (den * (1.0 + jnp.exp(-gate)))).astype(y_ref.dtype)


def dsa_attention(proj):
    b, s, _ = proj.shape
    k_sel = min(TOPK_MAX, s // 4)
    assert s % KEY_TILE == 0 and k_sel <= KEY_TILE
    kvw = B_KV_HEADS * HEAD_DIM
    n_pairs = B_HEADS // 2

    def qblock(width, idx):
        return pl.BlockSpec((None, Q_BLOCK, width), lambda bi, i: (bi, i, idx))

    def full(width, idx):
        return pl.BlockSpec((None, s, width), lambda bi, i: (bi, 0, idx))

    return pl.pallas_call(
        functools.partial(_dsa_kernel, k_sel=k_sel),
        grid=(b, s // Q_BLOCK),
        in_specs=[
            qblock(B_WIDTH, 0),
            qblock(B_WIDTH, 1),
            qblock(IDX_HEADS * HEAD_DIM, 4),
            full(kvw, 10),
            full(LANES, 22),
            qblock(LANES, 24),
            full(kvw, 13),
        ],
        out_specs=qblock(B_WIDTH, 0),
        out_shape=jax.ShapeDtypeStruct((b, s, B_WIDTH), BF16),
        scratch_shapes=[
            pltpu.VMEM((s, Q_BLOCK), I32),
            pltpu.VMEM((32, s // 32, Q_BLOCK), I32),
            pltpu.VMEM((Q_BLOCK, s), F32),
            pltpu.VMEM((kvw // LANES, LANES, s), BF16),
            pltpu.VMEM((n_pairs, 2 * Q_BLOCK, LANES), BF16),
            pltpu.VMEM((2 * Q_BLOCK, KEY_TILE), F32),
            pltpu.VMEM((n_pairs, 2 * Q_BLOCK, LANES), F32),
            pltpu.VMEM((n_pairs, 2 * Q_BLOCK, LANES), F32),
            pltpu.VMEM((n_pairs, 2 * Q_BLOCK, LANES), F32),
        ],
        compiler_params=_cparams(("arbitrary", "arbitrary")),
        name="dsa_attention",
    )(proj, proj, proj, proj, proj, proj, proj)


def dsa_mixer(h, g, scale, shift, w_in_packed, cos, sin):
    proj = in_proj(h, g, scale, shift, w_in_packed, [f == 1 for f in B_ROPE_FLAGS], cos, sin, 512)
    return dsa_attention(proj)


def kernel(x, c, positions, norm_g, ada_w, ada_b, a_w_in, a_w_out, b_w_in, b_w_out, final_g):
    depth = norm_g.shape[0]
    d = x.shape[-1]
    mod = adaln_mod(c, ada_w, ada_b)
    cos, sin = rope_tables(positions)
    tables = {1: (cos, sin)}
    for _, dilation in A_GROUPS:
        if dilation > 1:
            tables[dilation] = class_major_rope_tables(cos, sin, dilation)
    h = x
    for i in range(depth):
        shift, scale, gate = mod[i, :, :d], mod[i, :, d:2 * d], mod[i, :, 2 * d:]
        if i % 2 == 0:
            y = dilated_mixer(h, norm_g[i], scale, shift, a_w_in[i // 2], tables)
            w_out = a_w_out[i // 2].astype(BF16)
        else:
            w_in, w_out = _dsa_weights(b_w_in[i // 2], b_w_out[i // 2])
            y = dsa_mixer(h, norm_g[i], scale, shift, w_in, cos, sin)
        h = out_proj(y, w_out, gate, h)
    return final_norm(h, final_g)
```

```python
import functools

import numpy as np
import jax
import jax.numpy as jnp
from jax import lax
from jax.experimental import pallas as pl
from jax.experimental.pallas import tpu as pltpu

F32 = jnp.float32
BF16 = jnp.bfloat16
I32 = jnp.int32

LANES = 128
HEAD_DIM = 64
HALF = HEAD_DIM // 2
ROPE_THETA = 10000.0
NORM_EPS = 1e-6
VMEM_LIMIT = 56 * 1024 * 1024
ROW_CHUNK = 256

A_HEADS = 16
A_WIDTH = A_HEADS * HEAD_DIM
A_GROUPS = ((128, 1), (512, 4), (2048, 16))
A_BLK = 128

B_HEADS = 16
B_KV_HEADS = 4
B_WIDTH = B_HEADS * HEAD_DIM
IDX_HEADS = 8
TOPK_MAX = 256
Q_BLOCK = 128
KEY_TILE = 512
INT_MIN = -(2 ** 31)
KEY_NEG_INF = -2139095041
F32_MIN = float(np.finfo(np.float32).min)
LOG2_E = float(np.log2(np.e))
Q_PRESCALE = LOG2_E * HEAD_DIM ** -0.5


def _cparams(sem):
    return pltpu.CompilerParams(dimension_semantics=sem, vmem_limit_bytes=VMEM_LIMIT)


def _adaln_kernel(c_ref, w_ref, b_ref, o_ref):
    c = c_ref[...]
    ca = (c * (1.0 / (1.0 + jnp.exp(-c)))).astype(BF16)
    acc = jnp.dot(ca, w_ref[...].astype(BF16), preferred_element_type=F32)
    o_ref[...] = acc + b_ref[...]


def adaln_mod(c, ada_w, ada_b):
    depth, d, d3 = ada_w.shape
    b = c.shape[0]
    tn = 1024
    return pl.pallas_call(
        _adaln_kernel,
        grid=(depth, d3 // tn),
        in_specs=[
            pl.BlockSpec((b, d), lambda i, j: (0, 0)),
            pl.BlockSpec((None, d, tn), lambda i, j: (i, 0, j)),
            pl.BlockSpec((None, 1, tn), lambda i, j: (i, 0, j)),
        ],
        out_specs=pl.BlockSpec((None, b, tn), lambda i, j: (i, 0, j)),
        out_shape=jax.ShapeDtypeStruct((depth, b, d3), F32),
        compiler_params=_cparams(("arbitrary", "arbitrary")),
        name="adaln_mod",
    )(c, ada_w, ada_b.reshape(depth, 1, d3))


def _rope_table_kernel(pos_ref, inv_ref, cos_ref, sin_ref):
    ang = pos_ref[...].astype(F32) * inv_ref[...]
    lane = lax.broadcasted_iota(I32, ang.shape, 1)
    first_half = (lane % HEAD_DIM) < HALF
    cos_ref[...] = jnp.cos(ang)
    s = jnp.sin(ang)
    sin_ref[...] = jnp.where(first_half, -s, s)


def rope_tables(positions):
    b, s = positions.shape
    inv_freq = ROPE_THETA ** (-jnp.arange(HALF, dtype=F32) / HALF)
    inv_lane = jnp.tile(inv_freq, LANES // HALF).reshape(1, LANES)
    ts = min(s, 1024)
    out = jax.ShapeDtypeStruct((b, s, LANES), F32)
    return pl.pallas_call(
        _rope_table_kernel,
        grid=(b, s // ts),
        in_specs=[
            pl.BlockSpec((None, ts, 1), lambda i, j: (i, j, 0)),
            pl.BlockSpec((1, LANES), lambda i, j: (0, 0)),
        ],
        out_specs=[pl.BlockSpec((None, ts, LANES), lambda i, j: (i, j, 0))] * 2,
        out_shape=[out, out],
        compiler_params=_cparams(("arbitrary", "arbitrary")),
        name="rope_tables",
    )(positions.reshape(b, s, 1), inv_lane)


def _rope_block(t, cos, sin_signed, first_half):
    partner = jnp.where(first_half, pltpu.roll(t, LANES - HALF, 1), pltpu.roll(t, HALF, 1))
    return t * cos + partner * sin_signed


def _in_proj_kernel(x_ref, g_ref, sc_ref, sh_ref, w_ref, cos_ref, sin_ref, o_ref, u_ref, *stage,
                    dilation, rope_tiles, tn):
    tm = u_ref.shape[0]
    nblk = tn // LANES
    per = ROW_CHUNK // dilation
    x = x_ref[...]
    ms = jnp.mean(x * x, axis=-1, keepdims=True)
    xn = x * lax.rsqrt(ms + NORM_EPS)
    u_ref[...] = (xn * g_ref[...] * (1.0 + sc_ref[...]) + sh_ref[...]).astype(BF16)
    lane = lax.broadcasted_iota(I32, (per, LANES), 1)
    first_half = (lane % HEAD_DIM) < HALF
    step = 0
    for j, rope in enumerate(rope_tiles):
        for c in range(tm // ROW_CHUNK):
            acc = jnp.dot(u_ref[c * ROW_CHUNK:(c + 1) * ROW_CHUNK, :], w_ref[:, j * tn:(j + 1) * tn],
                          preferred_element_type=F32)
            if dilation > 1:
                slot = step % stage[0].shape[0]
                step += 1
                for k in range(nblk):
                    stage[0][slot, k] = acc[:, k * LANES:(k + 1) * LANES]
            for r in range(dilation):
                rows = slice(c * per, (c + 1) * per)
                if rope:
                    cos = cos_ref[rows, :] if dilation == 1 else cos_ref[r, rows, :]
                    sin = sin_ref[rows, :] if dilation == 1 else sin_ref[r, rows, :]
                for k in range(nblk):
                    cols = slice(j * tn + k * LANES, j * tn + (k + 1) * LANES)
                    if dilation > 1:
                        blk = stage[0][slot, k, pl.ds(r, per, stride=dilation), :]
                    else:
                        blk = acc[:, k * LANES:(k + 1) * LANES]
                    if rope:
                        blk = _rope_block(blk, cos, sin, first_half)
                    if dilation > 1:
                        o_ref[r, rows, cols] = blk.astype(o_ref.dtype)
                    else:
                        o_ref[rows, cols] = blk.astype(o_ref.dtype)


def in_proj(h, g, scale, shift, w, rope_tiles, cos, sin, tn, dilation=1):
    b, s, d = h.shape
    n = w.shape[1]
    tm = min(s, 512)
    assert tm % ROW_CHUNK == 0 and ROW_CHUNK % (16 * dilation) == 0 and len(rope_tiles) * tn == n
    scratch = [pltpu.VMEM((tm, d), BF16)]
    if dilation == 1:
        tab_spec = pl.BlockSpec((None, tm, LANES), lambda bi, i: (bi, i, 0))
        out_spec = pl.BlockSpec((None, tm, n), lambda bi, i: (bi, i, 0))
        out_shape = jax.ShapeDtypeStruct((b, s, n), BF16)
    else:
        tab_spec = pl.BlockSpec((None, dilation, tm // dilation, LANES), lambda bi, i: (bi, 0, i, 0))
        out_spec = pl.BlockSpec((None, dilation, tm // dilation, n), lambda bi, i: (bi, 0, i, 0))
        out_shape = jax.ShapeDtypeStruct((b, dilation, s // dilation, n), BF16)
        scratch.append(pltpu.VMEM((2, tn // LANES, ROW_CHUNK, LANES), F32))
    return pl.pallas_call(
        functools.partial(_in_proj_kernel, dilation=dilation, rope_tiles=tuple(rope_tiles), tn=tn),
        grid=(b, s // tm),
        in_specs=[
            pl.BlockSpec((None, tm, d), lambda bi, i: (bi, i, 0)),
            pl.BlockSpec((1, d), lambda bi, i: (0, 0)),
            pl.BlockSpec((None, 1, d), lambda bi, i: (bi, 0, 0)),
            pl.BlockSpec((None, 1, d), lambda bi, i: (bi, 0, 0)),
            pl.BlockSpec((d, n), lambda bi, i: (0, 0)),
            tab_spec, tab_spec,
        ],
        out_specs=out_spec,
        out_shape=out_shape,
        scratch_shapes=scratch,
        compiler_params=_cparams(("arbitrary", "arbitrary")),
        name=f"in_proj_d{dilation}",
    )(h, g.reshape(1, d), scale.reshape(b, 1, d), shift.reshape(b, 1, d), w, cos, sin)


def _out_proj_kernel(y_ref, w_ref, gate_ref, h_ref, *rest):
    acc = jnp.dot(y_ref[...], w_ref[...], preferred_element_type=F32)
    h_new = h_ref[...] + gate_ref[...] * acc
    if len(rest) == 2:
        g_ref, o_ref = rest
        ms = jnp.mean(h_new * h_new, axis=-1, keepdims=True)
        o_ref[...] = h_new * lax.rsqrt(ms + NORM_EPS) * g_ref[...]
    else:
        rest[0][...] = h_new


def out_proj(y, w, gate, h, final_g=None):
    b, s, d = h.shape
    k = y.shape[-1]
    tm = min(s, 1024)
    in_specs = [
        pl.BlockSpec((None, tm, k), lambda bi, i: (bi, i, 0)),
        pl.BlockSpec((k, d), lambda bi, i: (0, 0)),
        pl.BlockSpec((None, 1, d), lambda bi, i: (bi, 0, 0)),
        pl.BlockSpec((None, tm, d), lambda bi, i: (bi, i, 0)),
    ]
    args = [y, w, gate.reshape(b, 1, d), h]
    if final_g is not None:
        in_specs.append(pl.BlockSpec((1, d), lambda bi, i: (0, 0)))
        args.append(final_g.reshape(1, d))
    return pl.pallas_call(
        _out_proj_kernel,
        grid=(b, s // tm),
        in_specs=in_specs,
        out_specs=pl.BlockSpec((None, tm, d), lambda bi, i: (bi, i, 0)),
        out_shape=jax.ShapeDtypeStruct((b, s, d), F32),
        compiler_params=_cparams(("arbitrary", "arbitrary")),
        name="out_proj",
    )(*args)


def _final_norm_kernel(x_ref, g_ref, o_ref):
    x = x_ref[...]
    ms = jnp.mean(x * x, axis=-1, keepdims=True)
    o_ref[...] = x * lax.rsqrt(ms + NORM_EPS) * g_ref[...]


def final_norm(h, g):
    b, s, d = h.shape
    tm = min(s, 1024)
    return pl.pallas_call(
        _final_norm_kernel,
        grid=(b, s // tm),
        in_specs=[
            pl.BlockSpec((None, tm, d), lambda bi, i: (bi, i, 0)),
            pl.BlockSpec((1, d), lambda bi, i: (0, 0)),
        ],
        out_specs=pl.BlockSpec((None, tm, d), lambda bi, i: (bi, i, 0)),
        out_shape=jax.ShapeDtypeStruct((b, s, d), F32),
        compiler_params=_cparams(("arbitrary", "arbitrary")),
        name="final_norm",
    )(h, g.reshape(1, d))


def _stack_heads(pair, first_head):
    zero = jnp.zeros_like(pair)
    return jnp.concatenate([jnp.where(first_head, pair, zero), jnp.where(first_head, zero, pair)], axis=0)


def _banded_kernel(q_ref, kp_ref, kc_ref, vp_ref, vc_ref, o_ref, lse_ref, kwin_ref, vwin_ref, *, tq):
    i = pl.program_id(2)
    nsub = tq // A_BLK
    lane = lax.broadcasted_iota(I32, (A_BLK, LANES), 1)
    first_head = lane < HEAD_DIM
    odd_lane = (lane % 2) == 1
    r2 = lax.broadcasted_iota(I32, (2 * A_BLK, 2 * A_BLK), 0) % A_BLK
    c2 = lax.broadcasted_iota(I32, (2 * A_BLK, 2 * A_BLK), 1)
    own_ok = (c2 >= A_BLK) & (c2 - A_BLK <= r2)
    ones_blk = jnp.ones((2 * A_BLK, LANES), BF16)
    n_pairs = A_HEADS // 2

    kwin_ref[:A_BLK, :] = kp_ref[...]
    kwin_ref[A_BLK:, :] = kc_ref[...]
    vwin_ref[:A_BLK, :] = vp_ref[...]
    vwin_ref[A_BLK:, :] = vc_ref[...]

    def logits(a, p):
        cols = slice(p * LANES, (p + 1) * LANES)
        qs = _stack_heads(q_ref[a * A_BLK:(a + 1) * A_BLK, cols], first_head)
        k2 = kwin_ref[a * A_BLK:(a + 2) * A_BLK, cols]
        return lax.dot_general(qs, k2, (((1,), (1,)), ((), ())), preferred_element_type=F32)

    units = [(a, p) for a in range(nsub) for p in range(n_pairs)]
    s_next = logits(*units[0])
    lse_tile = bias = None
    for u, (a, p) in enumerate(units):
        rows = slice(a * A_BLK, (a + 1) * A_BLK)
        cols = slice(p * LANES, (p + 1) * LANES)
        if p == 0:
            prev_shift = jnp.where(i * nsub + a > 0, 0, 2 * A_BLK)
            mask = own_ok | ((c2 < A_BLK) & (c2 >= r2 + prev_shift))
            bias = jnp.where(mask, 0.0, -jnp.inf).astype(F32)
            lse_tile = jnp.zeros((A_BLK, LANES), F32)
        s = s_next + bias
        if u + 1 < len(units):
            s_next = logits(*units[u + 1])
        m = jnp.max(s, axis=-1, keepdims=True)
        e = jnp.exp2(s - m).astype(BF16)
        v_ext = jnp.concatenate([vwin_ref[a * A_BLK:(a + 2) * A_BLK, cols], ones_blk], axis=1)
        pv = jnp.dot(e, v_ext, preferred_element_type=F32)
        num = jnp.where(first_head, pv[:A_BLK, :LANES], pv[A_BLK:, :LANES])
        o_ref[rows, cols] = num * (1.0 / jnp.where(first_head, pv[:A_BLK, LANES:], pv[A_BLK:, LANES:]))
        den_pair = jnp.where(odd_lane, pv[A_BLK:, LANES:], pv[:A_BLK, LANES:])
        m_pair = jnp.where(odd_lane, m[A_BLK:], m[:A_BLK])
        lse_tile = jnp.where((lane // 2) == p, m_pair + jnp.log(den_pair) * LOG2_E, lse_tile)
        if p == n_pairs - 1:
            lse_ref[rows, :] = lse_tile


def banded_group_attention(qkv):
    b, d, n, _ = qkv.shape
    tq = min(2 * A_BLK, n)
    sub = tq // A_BLK

    def cur(blk):
        return pl.BlockSpec((None, None, tq, A_WIDTH), lambda bi, r, i: (bi, r, i, blk))

    def prev(blk):
        return pl.BlockSpec((None, None, A_BLK, A_WIDTH),
                            lambda bi, r, i: (bi, r, jnp.maximum(i * sub - 1, 0), blk))

    return pl.pallas_call(
        functools.partial(_banded_kernel, tq=tq),
        grid=(b, d, n // tq),
        in_specs=[cur(0), prev(1), cur(1), prev(2), cur(2)],
        out_specs=[
            pl.BlockSpec((None, None, tq, A_WIDTH), lambda bi, r, i: (bi, r, i, 0)),
            pl.BlockSpec((None, None, tq, LANES), lambda bi, r, i: (bi, r, i, 0)),
        ],
        out_shape=[
            jax.ShapeDtypeStruct((b, d, n, A_WIDTH), F32),
            jax.ShapeDtypeStruct((b, d, n, LANES), F32),
        ],
        scratch_shapes=[pltpu.VMEM((tq + A_BLK, A_WIDTH), BF16)] * 2,
        compiler_params=_cparams(("arbitrary", "arbitrary", "arbitrary")),
        name=f"banded_attn_d{d}",
    )(qkv, qkv, qkv, qkv, qkv)


def _expand_heads(x, expand):
    hi = x.astype(BF16)
    r1 = x - hi.astype(F32)
    mid = r1.astype(BF16)
    lo = (r1 - mid.astype(F32)).astype(BF16)
    out = jnp.dot(hi, expand, preferred_element_type=F32)
    out += jnp.dot(mid, expand, preferred_element_type=F32)
    out += jnp.dot(lo, expand, preferred_element_type=F32)
    return out


def _merge_out_kernel(o0_ref, o1_ref, o2_ref, l0_ref, l1_ref, l2_ref, gate_ref, ex_ref, w_ref,
                      mod_ref, h_ref, hout_ref, y_ref, ot_ref, lt_ref, *, dilations):
    o_refs, l_refs = (o0_ref, o1_ref, o2_ref), (l0_ref, l1_ref, l2_ref)
    tm = y_ref.shape[0]
    nblk = y_ref.shape[1] // LANES
    for g, d in enumerate(dilations):
        if d == 1:
            continue
        per = tm // d
        for r in range(d):
            lt_ref[g, pl.ds(r, per, stride=d), :] = l_refs[g][r]
            for k in range(nblk):
                ot_ref[g, k, pl.ds(r, per, stride=d), :] = o_refs[g][r, :, k * LANES:(k + 1) * LANES]
    lses = [l_refs[g][0] if d == 1 else lt_ref[g] for g, d in enumerate(dilations)]
    m = jnp.maximum(jnp.maximum(lses[0], lses[1]), lses[2])
    es = [jnp.exp2(l - m) for l in lses]
    inv = 1.0 / (es[0] + es[1] + es[2])
    ex = ex_ref[...]
    alphas = [_expand_heads(e * inv, ex) for e in es]
    for k in range(nblk):
        cols = slice(k * LANES, (k + 1) * LANES)
        y = jnp.zeros((tm, LANES), F32)
        for g, d in enumerate(dilations):
            og = o_refs[g][0, :, cols] if d == 1 else ot_ref[g, k]
            y += alphas[g][:, cols] * og
        gate = gate_ref[:, cols].astype(F32)
        y_ref[:, cols] = (y * gate * (1.0 / (1.0 + jnp.exp(-gate)))).astype(y_ref.dtype)
    acc = jnp.dot(y_ref[...], w_ref[...], preferred_element_type=F32)
    hout_ref[...] = h_ref[...] + mod_ref[...] * acc


def merge_out_proj(outs, lses, gate, w_out, mod_gate, h):
    b, s, w = gate.shape
    d_model = h.shape[-1]
    tm = min(s, 256)
    dil = tuple(o.shape[1] for o in outs)
    expand = (jnp.arange(LANES)[:, None] == (jnp.arange(w)[None, :] // HEAD_DIM)).astype(BF16)

    def cm(d, width):
        return pl.BlockSpec((None, d, tm // d, width), lambda bi, i: (bi, 0, i, 0))

    def tspec(width):
        return pl.BlockSpec((None, tm, width), lambda bi, i: (bi, i, 0))

    return pl.pallas_call(
        functools.partial(_merge_out_kernel, dilations=dil),
        grid=(b, s // tm),
        in_specs=[cm(d, w) for d in dil] + [cm(d, LANES) for d in dil]
                 + [tspec(w), pl.BlockSpec((LANES, w), lambda bi, i: (0, 0)),
                    pl.BlockSpec((w, d_model), lambda bi, i: (0, 0)),
                    pl.BlockSpec((None, 1, d_model), lambda bi, i: (bi, 0, 0)),
                    tspec(d_model)],
        out_specs=tspec(d_model),
        out_shape=jax.ShapeDtypeStruct((b, s, d_model), F32),
        scratch_shapes=[pltpu.VMEM((tm, w), BF16),
                        pltpu.VMEM((len(dil), w // LANES, tm, LANES), F32),
                        pltpu.VMEM((len(dil), tm, LANES), F32)],
        compiler_params=_cparams(("arbitrary", "arbitrary")),
        name="merge_out_proj",
    )(*outs, *lses, gate, expand, w_out, mod_gate.reshape(b, 1, d_model), h)


def _class_major(t, d):
    b, s = t.shape[:2]
    return jnp.swapaxes(t.reshape(b, s // d, d, *t.shape[2:]), 1, 2)


def class_major_rope_tables(cos, sin, dilation):
    return _class_major(cos, dilation), _class_major(sin, dilation)


def dilated_layer(h, g, scale, shift, mod_gate, w_in, w_out, tables):
    tn = 512
    per = A_WIDTH // tn
    qkv_flags = (True,) * (2 * per) + (False,) * per
    w = w_in.astype(BF16)
    outs, lses = [], []
    for gi, (window, dilation) in enumerate(A_GROUPS):
        assert window // dilation == A_BLK
        wg = w_in[:, gi * 3 * A_WIDTH:(gi + 1) * 3 * A_WIDTH]
        wg = jnp.concatenate([wg[:, :A_WIDTH] * Q_PRESCALE, wg[:, A_WIDTH:]], axis=1).astype(BF16)
        qkv = in_proj(h, g, scale, shift, wg, qkv_flags, *tables[dilation], tn, dilation)
        if dilation == 1:
            qkv = qkv[:, None]
        o, lse = banded_group_attention(qkv)
        outs.append(o)
        lses.append(lse)
    gate = in_proj(h, g, scale, shift, w[:, len(A_GROUPS) * 3 * A_WIDTH:], (False,) * per,
                   *tables[1], tn)
    return merge_out_proj(outs, lses, gate, w_out.astype(BF16), mod_gate, h)


B_COLS = 3584
B_ROPE_FLAGS = (1, 1, 0, 0, 1, 1, 0)


def _dsa_head_perm():
    group = B_HEADS // B_KV_HEADS
    order = []
    for g2 in range(B_KV_HEADS // 2):
        for r in range(group):
            order += [(2 * g2) * group + r, (2 * g2 + 1) * group + r]
    return np.asarray(order)


def _dsa_weights(w_in, w_out):
    d = w_in.shape[0]
    cuts = np.cumsum((B_WIDTH, B_KV_HEADS * HEAD_DIM, B_KV_HEADS * HEAD_DIM, IDX_HEADS * HEAD_DIM,
                      HEAD_DIM, IDX_HEADS, B_WIDTH))[:-1]
    wq, wk, wv, wqi, wki, wwi, wg = jnp.split(w_in, cuts, axis=1)
    cols = (_dsa_head_perm()[:, None] * HEAD_DIM + np.arange(HEAD_DIM)[None, :]).reshape(-1)
    zeros = lambda n: jnp.zeros((d, n), w_in.dtype)
    wq = wq * Q_PRESCALE
    w = jnp.concatenate([wq[:, cols], wg[:, cols], wqi, wk, wki, wki, zeros(LANES),
                         wwi, zeros(2 * LANES - IDX_HEADS), wv], axis=1)
    assert w.shape[1] == B_COLS
    return w.astype(BF16), w_out[cols, :].astype(BF16)


def _sortable(x):
    bits = pltpu.bitcast(x, I32)
    return bits ^ ((bits >> 31) & jnp.int32(0x7FFFFFFF))


def _bit_transpose32(words):
    a = list(words)
    j, m = 16, 0x0000FFFF
    while j:
        mask = jnp.int32(m - (1 << 32) if m >= (1 << 31) else m)
        shift = jnp.full(a[0].shape, j, I32)
        k = 0
        while k < 32:
            t = (a[k] ^ lax.shift_right_logical(a[k + j], shift)) & mask
            a[k] = a[k] ^ t
            a[k + j] = a[k + j] ^ lax.shift_left(t, shift)
            k = (k + j + 1) & ~j
        j >>= 1
        m = (m ^ (m << j)) & 0xFFFFFFFF
    return a


def _sublane_allsum(x):
    x = x + pltpu.roll(x, 4, 0)
    x = x + pltpu.roll(x, 2, 0)
    return x + pltpu.roll(x, 1, 0)


def _tree_sum(parts):
    parts = list(parts)
    while len(parts) > 1:
        odd = [parts[-1]] if len(parts) % 2 else []
        parts = [parts[j] + parts[j + 1] for j in range(0, len(parts) - 1, 2)] + odd
    return parts[0]


GROUP_KEYS = 256


def _dsa_kernel(q_ref, gate_ref, qi_ref, k_ref, ki_ref, wi_ref, v_ref, y_ref,
                keys_ref, planes_ref, bias_ref, kt_ref, qs_ref, s0_ref, m_ref, l_ref, acc_ref,
                *, k_sel):
    i = pl.program_id(1)
    seq = keys_ref.shape[0]
    n_tiles = (i * Q_BLOCK) // KEY_TILE + 1
    n_pairs = B_HEADS // 2
    n_groups_max = seq // GROUP_KEYS
    lane = lax.broadcasted_iota(I32, (Q_BLOCK, LANES), 1)
    first_head = lane < HEAD_DIM

    @pl.when(i == 0)
    def _():
        planes_ref[...] = jnp.zeros(planes_ref.shape, I32)

    w_t = (wi_ref[...].astype(F32) * (IDX_HEADS ** -0.5 * HEAD_DIM ** -0.5)).T
    qi_stacked = [_stack_heads(qi_ref[:, p * LANES:(p + 1) * LANES], first_head)
                  for p in range(IDX_HEADS // 2)]
    key_row = lax.broadcasted_iota(I32, (KEY_TILE, Q_BLOCK), 0)
    q_pos = i * Q_BLOCK + lax.broadcasted_iota(I32, (KEY_TILE, Q_BLOCK), 1)

    def score_tile(t):
        base = pl.multiple_of(t * KEY_TILE, KEY_TILE)
        kk = ki_ref[pl.ds(base, KEY_TILE), :]
        score = jnp.zeros((KEY_TILE, Q_BLOCK), F32)
        for p in range(IDX_HEADS // 2):
            sc = lax.dot_general(kk, qi_stacked[p], (((1,), (1,)), ((), ())),
                                 preferred_element_type=F32)
            score += jnp.maximum(sc[:, :Q_BLOCK], 0.0) * w_t[2 * p:2 * p + 1, :]
            score += jnp.maximum(sc[:, Q_BLOCK:], 0.0) * w_t[2 * p + 1:2 * p + 2, :]
        causal = (key_row + base) <= q_pos
        score = jnp.where(causal, score + 0.0, -jnp.inf)
        key = _sortable(score)
        keys_ref[pl.ds(base, KEY_TILE), :] = key
        ukey = key ^ jnp.int32(INT_MIN)
        for g in range(KEY_TILE // GROUP_KEYS):
            words = [ukey[g * GROUP_KEYS + 8 * j:g * GROUP_KEYS + 8 * j + 8, :] for j in range(32)]
            rows = pl.ds(pl.multiple_of(t * (KEY_TILE // 32) + 8 * g, 8), 8)
            for b, plane in enumerate(_bit_transpose32(words)):
                planes_ref[b, rows, :] = plane

    def score_two(t2, carry):
        score_tile(2 * t2)
        score_tile(2 * t2 + 1)
        return carry

    lax.fori_loop(0, n_tiles // 2, score_two, 0)

    @pl.when(n_tiles % 2 == 1)
    def _():
        score_tile(n_tiles - 1)

    def select_threshold():
        n_groups = n_tiles * (KEY_TILE // GROUP_KEYS)
        alive0 = tuple(jnp.full((8, Q_BLOCK), -1, I32) * (g < n_groups).astype(I32)
                       for g in range(n_groups_max))
        need0 = jnp.full((8, Q_BLOCK), k_sel, I32)

        def step(st, carry):
            alive, need, prefix = carry
            planes = [planes_ref[st, 8 * g:8 * g + 8, :] for g in range(n_groups_max)]
            ones = [a & p for a, p in zip(alive, planes)]
            cnt = _sublane_allsum(_tree_sum([lax.population_count(o) for o in ones]))
            take = cnt >= need
            prefix = jnp.where(take, prefix | jnp.left_shift(jnp.int32(1), 31 - st), prefix)
            need = jnp.where(take, need, need - cnt)
            flip = jnp.where(take, 0, -1)
            alive = tuple(a & (p ^ flip) for a, p in zip(alive, planes))
            return alive, need, prefix

        alive, need, prefix = lax.fori_loop(0, 32, step, (alive0, need0, jnp.zeros((8, Q_BLOCK), I32)))
        thr = (prefix ^ jnp.int32(INT_MIN))[:1, :]
        need_eq = need[:1, :]

        n_eq = _sublane_allsum(_tree_sum([lax.population_count(a) for a in alive]))[:1, :]
        tied = (n_eq > need_eq) & (thr > KEY_NEG_INF)
        idx_bits = (seq - 1).bit_length()

        def tie_break():
            def count_eq_before(cut):
                def body(t, acc):
                    base = pl.multiple_of(t * KEY_TILE, KEY_TILE)
                    hit = (keys_ref[pl.ds(base, KEY_TILE), :] == thr) & ((key_row + base) < cut)
                    return acc + jnp.sum(hit.astype(I32).reshape(KEY_TILE // 8, 8, Q_BLOCK), axis=0)
                acc = lax.fori_loop(0, n_tiles, body, jnp.zeros((8, Q_BLOCK), I32))
                return jnp.sum(acc, axis=0, keepdims=True)

            def bit_step(it, cut):
                cand = cut + jnp.left_shift(jnp.int32(1), idx_bits - 1 - it)
                return jnp.where(count_eq_before(cand) < need_eq, cand, cut)

            return lax.fori_loop(0, idx_bits, bit_step, jnp.zeros((1, Q_BLOCK), I32))

        any_tied = jnp.max(tied.astype(I32)) > 0
        idx_cut = lax.cond(any_tied, tie_break, lambda: jnp.full((1, Q_BLOCK), seq, I32))
        return thr, idx_cut

    all_selected = (i + 1) * Q_BLOCK <= k_sel
    thr, idx_cut = lax.cond(
        all_selected,
        lambda: (jnp.full((1, Q_BLOCK), INT_MIN, I32), jnp.full((1, Q_BLOCK), seq, I32)),
        select_threshold)

    def write_bias(base):
        key = keys_ref[pl.ds(base, KEY_TILE), :]
        k_idx = key_row + base
        sel = ((key > thr) | ((key == thr) & (k_idx <= idx_cut))) & (k_idx <= q_pos)
        bias_t = jnp.where(sel, 0.0, -jnp.inf).astype(F32)
        for c in range(KEY_TILE // LANES):
            col = base + c * LANES
            col = col if isinstance(col, int) else pl.multiple_of(col, LANES)
            bias_ref[:, pl.ds(col, LANES)] = bias_t[c * LANES:(c + 1) * LANES, :].T

    write_bias(0)

    for p in range(n_pairs):
        cols = slice(p * LANES, (p + 1) * LANES)
        qs_ref[p] = _stack_heads(q_ref[:, cols], first_head)
    m_ref[...] = jnp.full(m_ref.shape, F32_MIN, F32)
    l_ref[...] = jnp.zeros(l_ref.shape, F32)
    acc_ref[...] = jnp.zeros(acc_ref.shape, F32)
    heads_per_kv_pair = 2 * (B_HEADS // B_KV_HEADS) // 2

    ones_blk = jnp.ones((KEY_TILE, LANES), BF16)

    @pl.when(i == 0)
    def _():
        def xpose(r, carry):
            rows = pl.ds(pl.multiple_of(r * LANES, LANES), LANES)
            for c in range(kt_ref.shape[0]):
                blk = k_ref[rows, c * LANES:(c + 1) * LANES].astype(F32)
                kt_ref[c, :, rows] = blk.T.astype(BF16)
            return carry
        lax.fori_loop(0, seq // LANES, xpose, 0)

    def logits(base, p):
        kt = kt_ref[p // heads_per_kv_pair, :, pl.ds(base, KEY_TILE)]
        return jnp.dot(qs_ref[p], kt, preferred_element_type=F32)

    s0_ref[...] = logits(0, 0)

    def attn_tiles(t0, count):
        s_next = s0_ref[...]
        for dt in range(count):
            t = t0 + dt
            base = pl.multiple_of(t * KEY_TILE, KEY_TILE)
            next_base = pl.multiple_of(jnp.minimum(t + 1, n_tiles - 1) * KEY_TILE, KEY_TILE)
            bias = bias_ref[:, pl.ds(base, KEY_TILE)]
            write_bias(next_base)
            bias2 = jnp.concatenate([bias, bias], axis=0)
            for p in range(n_pairs):
                kv = p // heads_per_kv_pair
                kv_cols = slice(kv * LANES, (kv + 1) * LANES)
                s = s_next + bias2
                s_next = logits(base, p + 1) if p + 1 < n_pairs else logits(next_base, 0)
                m_blk = s[:, :LANES]
                for c in range(1, KEY_TILE // LANES):
                    m_blk = jnp.maximum(m_blk, s[:, c * LANES:(c + 1) * LANES])
                m_old = m_ref[p]
                m_new = jnp.maximum(m_old, jnp.max(m_blk, axis=-1, keepdims=True))
                alpha = jnp.exp2(m_old - m_new)
                e = jnp.concatenate(
                    [jnp.exp2(s[:, c * LANES:(c + 1) * LANES] - m_new).astype(BF16)
                     for c in range(KEY_TILE // LANES)], axis=1)
                v_ext = jnp.concatenate([v_ref[pl.ds(base, KEY_TILE), kv_cols], ones_blk], axis=1)
                pv = jnp.dot(e, v_ext, preferred_element_type=F32)
                acc_ref[p] = acc_ref[p] * alpha + pv[:, :LANES]
                l_ref[p] = l_ref[p] * alpha + pv[:, LANES:]
                m_ref[p] = m_new
        s0_ref[...] = s_next

    def attn_two(t2, carry):
        attn_tiles(2 * t2, 2)
        return carry

    lax.fori_loop(0, n_tiles // 2, attn_two, 0)

    @pl.when(n_tiles % 2 == 1)
    def _():
        attn_tiles(n_tiles - 1, 1)

    for p in range(n_pairs):
        cols = slice(p * LANES, (p + 1) * LANES)
        l = l_ref[p]
        acc = acc_ref[p]
        num = jnp.where(first_head, acc[:Q_BLOCK], acc[Q_BLOCK:])
        den = jnp.where(first_head, l[:Q_BLOCK], l[Q_BLOCK:])
        gate = gate_ref[:, cols].astype(F32)
        y_ref[:, cols] = (num * gate / (den * (1.0 + jnp.exp(-gate)))).astype(y_ref.dtype)


def dsa_attention(proj):
    b, s, _ = proj.shape
    k_sel = min(TOPK_MAX, s // 4)
    assert s % KEY_TILE == 0 and k_sel <= KEY_TILE
    kvw = B_KV_HEADS * HEAD_DIM
    n_pairs = B_HEADS // 2

    def qblock(width, idx):
        return pl.BlockSpec((None, Q_BLOCK, width), lambda bi, i: (bi, i, idx))

    def full(width, idx):
        return pl.BlockSpec((None, s, width), lambda bi, i: (bi, 0, idx))

    return pl.pallas_call(
        functools.partial(_dsa_kernel, k_sel=k_sel),
        grid=(b, s // Q_BLOCK),
        in_specs=[
            qblock(B_WIDTH, 0),
            qblock(B_WIDTH, 1),
            qblock(IDX_HEADS * HEAD_DIM, 4),
            full(kvw, 10),
            full(LANES, 22),
            qblock(LANES, 24),
            full(kvw, 13),
        ],
        out_specs=qblock(B_WIDTH, 0),
        out_shape=jax.ShapeDtypeStruct((b, s, B_WIDTH), BF16),
        scratch_shapes=[
            pltpu.VMEM((s, Q_BLOCK), I32),
            pltpu.VMEM((32, s // 32, Q_BLOCK), I32),
            pltpu.VMEM((Q_BLOCK, s), F32),
            pltpu.VMEM((kvw // LANES, LANES, s), BF16),
            pltpu.VMEM((n_pairs, 2 * Q_BLOCK, LANES), BF16),
            pltpu.VMEM((2 * Q_BLOCK, KEY_TILE), F32),
            pltpu.VMEM((n_pairs, 2 * Q_BLOCK, LANES), F32),
            pltpu.VMEM((n_pairs, 2 * Q_BLOCK, LANES), F32),
            pltpu.VMEM((n_pairs, 2 * Q_BLOCK, LANES), F32),
        ],
        compiler_params=_cparams(("arbitrary", "arbitrary")),
        name="dsa_attention",
    )(proj, proj, proj, proj, proj, proj, proj)


def dsa_mixer(h, g, scale, shift, w_in_packed, cos, sin):
    proj = in_proj(h, g, scale, shift, w_in_packed, [f == 1 for f in B_ROPE_FLAGS], cos, sin, 512)
    return dsa_attention(proj)


def kernel(x, c, positions, norm_g, ada_w, ada_b, a_w_in, a_w_out, b_w_in, b_w_out, final_g):
    depth = norm_g.shape[0]
    d = x.shape[-1]
    mod = adaln_mod(c, ada_w, ada_b)
    cos, sin = rope_tables(positions)
    tables = {1: (cos, sin)}
    for _, dilation in A_GROUPS:
        if dilation > 1:
            tables[dilation] = class_major_rope_tables(cos, sin, dilation)
    h = x
    for i in range(depth):
        shift, scale, gate = mod[i, :, :d], mod[i, :, d:2 * d], mod[i, :, 2 * d:]
        last = i == depth - 1
        if i % 2 == 0:
            h = dilated_layer(h, norm_g[i], scale, shift, gate, a_w_in[i // 2], a_w_out[i // 2], tables)
            if last:
                h = final_norm(h, final_g)
        else:
            w_in, w_out = _dsa_weights(b_w_in[i // 2], b_w_out[i // 2])
            y = dsa_mixer(h, norm_g[i], scale, shift, w_in, cos, sin)
            h = out_proj(y, w_out, gate, h, final_g if last else None)
    return h
```

```python
import functools

import numpy as np
import jax
import jax.numpy as jnp
from jax import lax
from jax.experimental import pallas as pl
from jax.experimental.pallas import tpu as pltpu

F32 = jnp.float32
BF16 = jnp.bfloat16
I32 = jnp.int32

LANES = 128
HEAD_DIM = 64
HALF = HEAD_DIM // 2
ROPE_THETA = 10000.0
NORM_EPS = 1e-6
VMEM_LIMIT = 56 * 1024 * 1024
ROW_CHUNK = 256

A_HEADS = 16
A_WIDTH = A_HEADS * HEAD_DIM
A_GROUPS = ((128, 1), (512, 4), (2048, 16))
A_BLK = 128

B_HEADS = 16
B_KV_HEADS = 4
B_WIDTH = B_HEADS * HEAD_DIM
IDX_HEADS = 8
TOPK_MAX = 256
Q_BLOCK = 256
SEL_LANES = 128
KEY_TILE = 512
INT_MIN = -(2 ** 31)
KEY_NEG_INF = -2139095041
F32_MIN = float(np.finfo(np.float32).min)
LOG2_E = float(np.log2(np.e))
Q_PRESCALE = LOG2_E * HEAD_DIM ** -0.5


def _cparams(sem):
    return pltpu.CompilerParams(dimension_semantics=sem, vmem_limit_bytes=VMEM_LIMIT)


def _adaln_kernel(c_ref, w_ref, b_ref, o_ref):
    c = c_ref[...]
    ca = (c * (1.0 / (1.0 + jnp.exp(-c)))).astype(BF16)
    acc = jnp.dot(ca, w_ref[...].astype(BF16), preferred_element_type=F32)
    o_ref[...] = acc + b_ref[...]


def adaln_mod(c, ada_w, ada_b):
    depth, d, d3 = ada_w.shape
    b = c.shape[0]
    tn = 1024
    return pl.pallas_call(
        _adaln_kernel,
        grid=(depth, d3 // tn),
        in_specs=[
            pl.BlockSpec((b, d), lambda i, j: (0, 0)),
            pl.BlockSpec((None, d, tn), lambda i, j: (i, 0, j)),
            pl.BlockSpec((None, 1, tn), lambda i, j: (i, 0, j)),
        ],
        out_specs=pl.BlockSpec((None, b, tn), lambda i, j: (i, 0, j)),
        out_shape=jax.ShapeDtypeStruct((depth, b, d3), F32),
        compiler_params=_cparams(("arbitrary", "arbitrary")),
        name="adaln_mod",
    )(c, ada_w, ada_b.reshape(depth, 1, d3))


def _rope_table_kernel(pos_ref, inv_ref, cos_ref, sin_ref):
    ang = pos_ref[...].astype(F32) * inv_ref[...]
    lane = lax.broadcasted_iota(I32, ang.shape, 1)
    first_half = (lane % HEAD_DIM) < HALF
    cos_ref[...] = jnp.cos(ang)
    s = jnp.sin(ang)
    sin_ref[...] = jnp.where(first_half, -s, s)


def rope_tables(positions):
    b, s = positions.shape
    inv_freq = ROPE_THETA ** (-jnp.arange(HALF, dtype=F32) / HALF)
    inv_lane = jnp.tile(inv_freq, LANES // HALF).reshape(1, LANES)
    ts = min(s, 1024)
    out = jax.ShapeDtypeStruct((b, s, LANES), F32)
    return pl.pallas_call(
        _rope_table_kernel,
        grid=(b, s // ts),
        in_specs=[
            pl.BlockSpec((None, ts, 1), lambda i, j: (i, j, 0)),
            pl.BlockSpec((1, LANES), lambda i, j: (0, 0)),
        ],
        out_specs=[pl.BlockSpec((None, ts, LANES), lambda i, j: (i, j, 0))] * 2,
        out_shape=[out, out],
        compiler_params=_cparams(("arbitrary", "arbitrary")),
        name="rope_tables",
    )(positions.reshape(b, s, 1), inv_lane)


def _rope_block(t, cos, sin_signed, first_half):
    partner = jnp.where(first_half, pltpu.roll(t, LANES - HALF, 1), pltpu.roll(t, HALF, 1))
    return t * cos + partner * sin_signed


def _in_proj_kernel(x_ref, g_ref, sc_ref, sh_ref, w_ref, cos_ref, sin_ref, o_ref, u_ref, *stage,
                    dilation, rope_tiles, tn):
    tm = u_ref.shape[0]
    nblk = tn // LANES
    per = ROW_CHUNK // dilation
    x = x_ref[...]
    ms = jnp.mean(x * x, axis=-1, keepdims=True)
    xn = x * lax.rsqrt(ms + NORM_EPS)
    u_ref[...] = (xn * g_ref[...] * (1.0 + sc_ref[...]) + sh_ref[...]).astype(BF16)
    lane = lax.broadcasted_iota(I32, (per, LANES), 1)
    first_half = (lane % HEAD_DIM) < HALF
    step = 0
    for j, rope in enumerate(rope_tiles):
        for c in range(tm // ROW_CHUNK):
            acc = jnp.dot(u_ref[c * ROW_CHUNK:(c + 1) * ROW_CHUNK, :], w_ref[:, j * tn:(j + 1) * tn],
                          preferred_element_type=F32)
            if dilation > 1:
                slot = step % stage[0].shape[0]
                step += 1
                for k in range(nblk):
                    stage[0][slot, k] = acc[:, k * LANES:(k + 1) * LANES]
            for r in range(dilation):
                rows = slice(c * per, (c + 1) * per)
                if rope:
                    cos = cos_ref[rows, :] if dilation == 1 else cos_ref[r, rows, :]
                    sin = sin_ref[rows, :] if dilation == 1 else sin_ref[r, rows, :]
                for k in range(nblk):
                    cols = slice(j * tn + k * LANES, j * tn + (k + 1) * LANES)
                    if dilation > 1:
                        blk = stage[0][slot, k, pl.ds(r, per, stride=dilation), :]
                    else:
                        blk = acc[:, k * LANES:(k + 1) * LANES]
                    if rope:
                        blk = _rope_block(blk, cos, sin, first_half)
                    if dilation > 1:
                        o_ref[r, rows, cols] = blk.astype(o_ref.dtype)
                    else:
                        o_ref[rows, cols] = blk.astype(o_ref.dtype)


def in_proj(h, g, scale, shift, w, rope_tiles, cos, sin, tn, dilation=1):
    b, s, d = h.shape
    n = w.shape[1]
    tm = min(s, 512)
    assert tm % ROW_CHUNK == 0 and ROW_CHUNK % (16 * dilation) == 0 and len(rope_tiles) * tn == n
    scratch = [pltpu.VMEM((tm, d), BF16)]
    if dilation == 1:
        tab_spec = pl.BlockSpec((None, tm, LANES), lambda bi, i: (bi, i, 0))
        out_spec = pl.BlockSpec((None, tm, n), lambda bi, i: (bi, i, 0))
        out_shape = jax.ShapeDtypeStruct((b, s, n), BF16)
    else:
        tab_spec = pl.BlockSpec((None, dilation, tm // dilation, LANES), lambda bi, i: (bi, 0, i, 0))
        out_spec = pl.BlockSpec((None, dilation, tm // dilation, n), lambda bi, i: (bi, 0, i, 0))
        out_shape = jax.ShapeDtypeStruct((b, dilation, s // dilation, n), BF16)
        scratch.append(pltpu.VMEM((2, tn // LANES, ROW_CHUNK, LANES), F32))
    return pl.pallas_call(
        functools.partial(_in_proj_kernel, dilation=dilation, rope_tiles=tuple(rope_tiles), tn=tn),
        grid=(b, s // tm),
        in_specs=[
            pl.BlockSpec((None, tm, d), lambda bi, i: (bi, i, 0)),
            pl.BlockSpec((1, d), lambda bi, i: (0, 0)),
            pl.BlockSpec((None, 1, d), lambda bi, i: (bi, 0, 0)),
            pl.BlockSpec((None, 1, d), lambda bi, i: (bi, 0, 0)),
            pl.BlockSpec((d, n), lambda bi, i: (0, 0)),
            tab_spec, tab_spec,
        ],
        out_specs=out_spec,
        out_shape=out_shape,
        scratch_shapes=scratch,
        compiler_params=_cparams(("arbitrary", "arbitrary")),
        name=f"in_proj_d{dilation}",
    )(h, g.reshape(1, d), scale.reshape(b, 1, d), shift.reshape(b, 1, d), w, cos, sin)


def _out_proj_kernel(y_ref, w_ref, gate_ref, h_ref, *rest):
    acc = jnp.dot(y_ref[...], w_ref[...], preferred_element_type=F32)
    h_new = h_ref[...] + gate_ref[...] * acc
    if len(rest) == 2:
        g_ref, o_ref = rest
        ms = jnp.mean(h_new * h_new, axis=-1, keepdims=True)
        o_ref[...] = h_new * lax.rsqrt(ms + NORM_EPS) * g_ref[...]
    else:
        rest[0][...] = h_new


def out_proj(y, w, gate, h, final_g=None):
    b, s, d = h.shape
    k = y.shape[-1]
    tm = min(s, 1024)
    in_specs = [
        pl.BlockSpec((None, tm, k), lambda bi, i: (bi, i, 0)),
        pl.BlockSpec((k, d), lambda bi, i: (0, 0)),
        pl.BlockSpec((None, 1, d), lambda bi, i: (bi, 0, 0)),
        pl.BlockSpec((None, tm, d), lambda bi, i: (bi, i, 0)),
    ]
    args = [y, w, gate.reshape(b, 1, d), h]
    if final_g is not None:
        in_specs.append(pl.BlockSpec((1, d), lambda bi, i: (0, 0)))
        args.append(final_g.reshape(1, d))
    return pl.pallas_call(
        _out_proj_kernel,
        grid=(b, s // tm),
        in_specs=in_specs,
        out_specs=pl.BlockSpec((None, tm, d), lambda bi, i: (bi, i, 0)),
        out_shape=jax.ShapeDtypeStruct((b, s, d), F32),
        compiler_params=_cparams(("arbitrary", "arbitrary")),
        name="out_proj",
    )(*args)


def _final_norm_kernel(x_ref, g_ref, o_ref):
    x = x_ref[...]
    ms = jnp.mean(x * x, axis=-1, keepdims=True)
    o_ref[...] = x * lax.rsqrt(ms + NORM_EPS) * g_ref[...]


def final_norm(h, g):
    b, s, d = h.shape
    tm = min(s, 1024)
    return pl.pallas_call(
        _final_norm_kernel,
        grid=(b, s // tm),
        in_specs=[
            pl.BlockSpec((None, tm, d), lambda bi, i: (bi, i, 0)),
            pl.BlockSpec((1, d), lambda bi, i: (0, 0)),
        ],
        out_specs=pl.BlockSpec((None, tm, d), lambda bi, i: (bi, i, 0)),
        out_shape=jax.ShapeDtypeStruct((b, s, d), F32),
        compiler_params=_cparams(("arbitrary", "arbitrary")),
        name="final_norm",
    )(h, g.reshape(1, d))


def _stack_heads(pair, first_head):
    zero = jnp.zeros_like(pair)
    return jnp.concatenate([jnp.where(first_head, pair, zero), jnp.where(first_head, zero, pair)], axis=0)


def _banded_kernel(q_ref, kp_ref, kc_ref, vp_ref, vc_ref, o_ref, lse_ref, kwin_ref, vwin_ref, *, tq):
    i = pl.program_id(2)
    nsub = tq // A_BLK
    lane = lax.broadcasted_iota(I32, (A_BLK, LANES), 1)
    first_head = lane < HEAD_DIM
    odd_lane = (lane % 2) == 1
    r2 = lax.broadcasted_iota(I32, (2 * A_BLK, 2 * A_BLK), 0) % A_BLK
    c2 = lax.broadcasted_iota(I32, (2 * A_BLK, 2 * A_BLK), 1)
    own_ok = (c2 >= A_BLK) & (c2 - A_BLK <= r2)
    ones_blk = jnp.ones((2 * A_BLK, LANES), BF16)
    n_pairs = A_HEADS // 2

    kwin_ref[:A_BLK, :] = kp_ref[...]
    kwin_ref[A_BLK:, :] = kc_ref[...]
    vwin_ref[:A_BLK, :] = vp_ref[...]
    vwin_ref[A_BLK:, :] = vc_ref[...]

    def logits(a, p):
        cols = slice(p * LANES, (p + 1) * LANES)
        qs = _stack_heads(q_ref[a * A_BLK:(a + 1) * A_BLK, cols], first_head)
        k2 = kwin_ref[a * A_BLK:(a + 2) * A_BLK, cols]
        return lax.dot_general(qs, k2, (((1,), (1,)), ((), ())), preferred_element_type=F32)

    units = [(a, p) for a in range(nsub) for p in range(n_pairs)]
    s_next = logits(*units[0])
    lse_tile = bias = None
    for u, (a, p) in enumerate(units):
        rows = slice(a * A_BLK, (a + 1) * A_BLK)
        cols = slice(p * LANES, (p + 1) * LANES)
        if p == 0:
            prev_shift = jnp.where(i * nsub + a > 0, 0, 2 * A_BLK)
            mask = own_ok | ((c2 < A_BLK) & (c2 >= r2 + prev_shift))
            bias = jnp.where(mask, 0.0, -jnp.inf).astype(F32)
            lse_tile = jnp.zeros((A_BLK, LANES), F32)
        s = s_next + bias
        if u + 1 < len(units):
            s_next = logits(*units[u + 1])
        m = jnp.max(s, axis=-1, keepdims=True)
        e = jnp.exp2(s - m).astype(BF16)
        v_ext = jnp.concatenate([vwin_ref[a * A_BLK:(a + 2) * A_BLK, cols], ones_blk], axis=1)
        pv = jnp.dot(e, v_ext, preferred_element_type=F32)
        num = jnp.where(first_head, pv[:A_BLK, :LANES], pv[A_BLK:, :LANES])
        o_ref[rows, cols] = num * (1.0 / jnp.where(first_head, pv[:A_BLK, LANES:], pv[A_BLK:, LANES:]))
        den_pair = jnp.where(odd_lane, pv[A_BLK:, LANES:], pv[:A_BLK, LANES:])
        m_pair = jnp.where(odd_lane, m[A_BLK:], m[:A_BLK])
        lse_tile = jnp.where((lane // 2) == p, m_pair + jnp.log(den_pair) * LOG2_E, lse_tile)
        if p == n_pairs - 1:
            lse_ref[rows, :] = lse_tile


def banded_group_attention(qkv):
    b, d, n, _ = qkv.shape
    tq = min(2 * A_BLK, n)
    sub = tq // A_BLK

    def cur(blk):
        return pl.BlockSpec((None, None, tq, A_WIDTH), lambda bi, r, i: (bi, r, i, blk))

    def prev(blk):
        return pl.BlockSpec((None, None, A_BLK, A_WIDTH),
                            lambda bi, r, i: (bi, r, jnp.maximum(i * sub - 1, 0), blk))

    return pl.pallas_call(
        functools.partial(_banded_kernel, tq=tq),
        grid=(b, d, n // tq),
        in_specs=[cur(0), prev(1), cur(1), prev(2), cur(2)],
        out_specs=[
            pl.BlockSpec((None, None, tq, A_WIDTH), lambda bi, r, i: (bi, r, i, 0)),
            pl.BlockSpec((None, None, tq, LANES), lambda bi, r, i: (bi, r, i, 0)),
        ],
        out_shape=[
            jax.ShapeDtypeStruct((b, d, n, A_WIDTH), F32),
            jax.ShapeDtypeStruct((b, d, n, LANES), F32),
        ],
        scratch_shapes=[pltpu.VMEM((tq + A_BLK, A_WIDTH), BF16)] * 2,
        compiler_params=_cparams(("arbitrary", "arbitrary", "arbitrary")),
        name=f"banded_attn_d{d}",
    )(qkv, qkv, qkv, qkv, qkv)


def _expand_heads(x, expand):
    hi = x.astype(BF16)
    r1 = x - hi.astype(F32)
    mid = r1.astype(BF16)
    lo = (r1 - mid.astype(F32)).astype(BF16)
    out = jnp.dot(hi, expand, preferred_element_type=F32)
    out += jnp.dot(mid, expand, preferred_element_type=F32)
    out += jnp.dot(lo, expand, preferred_element_type=F32)
    return out


def _merge_out_kernel(o0_ref, o1_ref, o2_ref, l0_ref, l1_ref, l2_ref, gate_ref, ex_ref, w_ref,
                      mod_ref, h_ref, hout_ref, y_ref, ot_ref, lt_ref, *, dilations):
    o_refs, l_refs = (o0_ref, o1_ref, o2_ref), (l0_ref, l1_ref, l2_ref)
    tm = y_ref.shape[0]
    nblk = y_ref.shape[1] // LANES
    for g, d in enumerate(dilations):
        if d == 1:
            continue
        per = tm // d
        for r in range(d):
            lt_ref[g, pl.ds(r, per, stride=d), :] = l_refs[g][r]
            for k in range(nblk):
                ot_ref[g, k, pl.ds(r, per, stride=d), :] = o_refs[g][r, :, k * LANES:(k + 1) * LANES]
    lses = [l_refs[g][0] if d == 1 else lt_ref[g] for g, d in enumerate(dilations)]
    m = jnp.maximum(jnp.maximum(lses[0], lses[1]), lses[2])
    es = [jnp.exp2(l - m) for l in lses]
    inv = 1.0 / (es[0] + es[1] + es[2])
    ex = ex_ref[...]
    alphas = [_expand_heads(e * inv, ex) for e in es]
    for k in range(nblk):
        cols = slice(k * LANES, (k + 1) * LANES)
        y = jnp.zeros((tm, LANES), F32)
        for g, d in enumerate(dilations):
            og = o_refs[g][0, :, cols] if d == 1 else ot_ref[g, k]
            y += alphas[g][:, cols] * og
        gate = gate_ref[:, cols].astype(F32)
        y_ref[:, cols] = (y * gate * (1.0 / (1.0 + jnp.exp(-gate)))).astype(y_ref.dtype)
    acc = jnp.dot(y_ref[...], w_ref[...], preferred_element_type=F32)
    hout_ref[...] = h_ref[...] + mod_ref[...] * acc


def merge_out_proj(outs, lses, gate, w_out, mod_gate, h):
    b, s, w = gate.shape
    d_model = h.shape[-1]
    tm = min(s, 256)
    dil = tuple(o.shape[1] for o in outs)
    expand = (jnp.arange(LANES)[:, None] == (jnp.arange(w)[None, :] // HEAD_DIM)).astype(BF16)

    def cm(d, width):
        return pl.BlockSpec((None, d, tm // d, width), lambda bi, i: (bi, 0, i, 0))

    def tspec(width):
        return pl.BlockSpec((None, tm, width), lambda bi, i: (bi, i, 0))

    return pl.pallas_call(
        functools.partial(_merge_out_kernel, dilations=dil),
        grid=(b, s // tm),
        in_specs=[cm(d, w) for d in dil] + [cm(d, LANES) for d in dil]
                 + [tspec(w), pl.BlockSpec((LANES, w), lambda bi, i: (0, 0)),
                    pl.BlockSpec((w, d_model), lambda bi, i: (0, 0)),
                    pl.BlockSpec((None, 1, d_model), lambda bi, i: (bi, 0, 0)),
                    tspec(d_model)],
        out_specs=tspec(d_model),
        out_shape=jax.ShapeDtypeStruct((b, s, d_model), F32),
        scratch_shapes=[pltpu.VMEM((tm, w), BF16),
                        pltpu.VMEM((len(dil), w // LANES, tm, LANES), F32),
                        pltpu.VMEM((len(dil), tm, LANES), F32)],
        compiler_params=_cparams(("arbitrary", "arbitrary")),
        name="merge_out_proj",
    )(*outs, *lses, gate, expand, w_out, mod_gate.reshape(b, 1, d_model), h)


def _class_major(t, d):
    b, s = t.shape[:2]
    return jnp.swapaxes(t.reshape(b, s // d, d, *t.shape[2:]), 1, 2)


def class_major_rope_tables(cos, sin, dilation):
    return _class_major(cos, dilation), _class_major(sin, dilation)


def dilated_layer(h, g, scale, shift, mod_gate, w_in, w_out, tables):
    tn = 512
    per = A_WIDTH // tn
    qkv_flags = (True,) * (2 * per) + (False,) * per
    w = w_in.astype(BF16)
    outs, lses = [], []
    for gi, (window, dilation) in enumerate(A_GROUPS):
        assert window // dilation == A_BLK
        wg = w_in[:, gi * 3 * A_WIDTH:(gi + 1) * 3 * A_WIDTH]
        wg = jnp.concatenate([wg[:, :A_WIDTH] * Q_PRESCALE, wg[:, A_WIDTH:]], axis=1).astype(BF16)
        qkv = in_proj(h, g, scale, shift, wg, qkv_flags, *tables[dilation], tn, dilation)
        if dilation == 1:
            qkv = qkv[:, None]
        o, lse = banded_group_attention(qkv)
        outs.append(o)
        lses.append(lse)
    gate = in_proj(h, g, scale, shift, w[:, len(A_GROUPS) * 3 * A_WIDTH:], (False,) * per,
                   *tables[1], tn)
    return merge_out_proj(outs, lses, gate, w_out.astype(BF16), mod_gate, h)


B_COLS = 3584
B_ROPE_FLAGS = (1, 1, 0, 0, 1, 1, 0)


def _dsa_head_perm():
    group = B_HEADS // B_KV_HEADS
    order = []
    for g2 in range(B_KV_HEADS // 2):
        for r in range(group):
            order += [(2 * g2) * group + r, (2 * g2 + 1) * group + r]
    return np.asarray(order)


def _dsa_weights(w_in, w_out):
    d = w_in.shape[0]
    cuts = np.cumsum((B_WIDTH, B_KV_HEADS * HEAD_DIM, B_KV_HEADS * HEAD_DIM, IDX_HEADS * HEAD_DIM,
                      HEAD_DIM, IDX_HEADS, B_WIDTH))[:-1]
    wq, wk, wv, wqi, wki, wwi, wg = jnp.split(w_in, cuts, axis=1)
    cols = (_dsa_head_perm()[:, None] * HEAD_DIM + np.arange(HEAD_DIM)[None, :]).reshape(-1)
    zeros = lambda n: jnp.zeros((d, n), w_in.dtype)
    wq = wq * Q_PRESCALE
    w = jnp.concatenate([wq[:, cols], wg[:, cols], wqi, wk, wki, wki, zeros(LANES),
                         wwi, zeros(2 * LANES - IDX_HEADS), wv], axis=1)
    assert w.shape[1] == B_COLS
    return w.astype(BF16), w_out[cols, :].astype(BF16)


def _sortable(x):
    bits = pltpu.bitcast(x, I32)
    return bits ^ ((bits >> 31) & jnp.int32(0x7FFFFFFF))


def _bit_transpose32(words):
    a = list(words)
    j, m = 16, 0x0000FFFF
    while j:
        mask = jnp.int32(m - (1 << 32) if m >= (1 << 31) else m)
        shift = jnp.full(a[0].shape, j, I32)
        k = 0
        while k < 32:
            t = (a[k] ^ lax.shift_right_logical(a[k + j], shift)) & mask
            a[k] = a[k] ^ t
            a[k + j] = a[k + j] ^ lax.shift_left(t, shift)
            k = (k + j + 1) & ~j
        j >>= 1
        m = (m ^ (m << j)) & 0xFFFFFFFF
    return a


def _sublane_allsum(x):
    x = x + pltpu.roll(x, 4, 0)
    x = x + pltpu.roll(x, 2, 0)
    return x + pltpu.roll(x, 1, 0)


def _tree_sum(parts):
    parts = list(parts)
    while len(parts) > 1:
        odd = [parts[-1]] if len(parts) % 2 else []
        parts = [parts[j] + parts[j + 1] for j in range(0, len(parts) - 1, 2)] + odd
    return parts[0]


GROUP_KEYS = 256
ATTN_LOOKAHEAD = 1


def _dsa_kernel(q_ref, gate_ref, qi_ref, k_ref, ki_ref, wi_ref, v_ref, y_ref,
                keys_ref, planes_ref, bias_ref, kt_ref, qs_ref, s0_ref, m_ref, l_ref, acc_ref,
                *, k_sel):
    i = pl.program_id(1)
    seq = keys_ref.shape[1]
    n_sub = Q_BLOCK // SEL_LANES
    n_pairs = B_HEADS // 2
    n_groups_max = seq // GROUP_KEYS
    lane_q = lax.broadcasted_iota(I32, (Q_BLOCK, LANES), 1)
    first_head_q = lane_q < HEAD_DIM
    first_head = lax.broadcasted_iota(I32, (SEL_LANES, LANES), 1) < HEAD_DIM
    key_row = lax.broadcasted_iota(I32, (KEY_TILE, SEL_LANES), 0)
    lane_idx = lax.broadcasted_iota(I32, (KEY_TILE, SEL_LANES), 1)

    @pl.when(i == 0)
    def _():
        planes_ref[...] = jnp.zeros(planes_ref.shape, I32)
        keys_ref[...] = jnp.zeros(keys_ref.shape, I32)

    def tiles_of(a):
        return ((i * n_sub + a) * SEL_LANES) // KEY_TILE + 1

    def select(a):
        n_tiles = tiles_of(a)
        rows_a = slice(a * SEL_LANES, (a + 1) * SEL_LANES)
        q_pos = (i * n_sub + a) * SEL_LANES + lane_idx
        keys = keys_ref.at[a]

        w_t = (wi_ref[rows_a, :].astype(F32) * (IDX_HEADS ** -0.5 * HEAD_DIM ** -0.5)).T
        qi_stacked = [_stack_heads(qi_ref[rows_a, p * LANES:(p + 1) * LANES], first_head)
                      for p in range(IDX_HEADS // 2)]

        def score_tile(t):
            base = pl.multiple_of(t * KEY_TILE, KEY_TILE)
            kk = ki_ref[pl.ds(base, KEY_TILE), :]
            score = jnp.zeros((KEY_TILE, SEL_LANES), F32)
            for p in range(IDX_HEADS // 2):
                sc = lax.dot_general(kk, qi_stacked[p], (((1,), (1,)), ((), ())),
                                     preferred_element_type=F32)
                score += jnp.maximum(sc[:, :SEL_LANES], 0.0) * w_t[2 * p:2 * p + 1, :]
                score += jnp.maximum(sc[:, SEL_LANES:], 0.0) * w_t[2 * p + 1:2 * p + 2, :]
            causal = (key_row + base) <= q_pos
            score = jnp.where(causal, score + 0.0, -jnp.inf)
            key = _sortable(score)
            keys[pl.ds(base, KEY_TILE), :] = key
            ukey = key ^ jnp.int32(INT_MIN)
            for g in range(KEY_TILE // GROUP_KEYS):
                words = [ukey[g * GROUP_KEYS + 8 * j:g * GROUP_KEYS + 8 * j + 8, :] for j in range(32)]
                rows = pl.ds(pl.multiple_of(t * (KEY_TILE // 32) + 8 * g, 8), 8)
                for b, plane in enumerate(_bit_transpose32(words)):
                    planes_ref[b, rows, :] = plane

        def score_two(t2, carry):
            score_tile(2 * t2)
            score_tile(2 * t2 + 1)
            return carry

        lax.fori_loop(0, n_tiles // 2, score_two, 0)

        @pl.when(n_tiles % 2 == 1)
        def _():
            score_tile(n_tiles - 1)

        def select_threshold():
            n_groups = n_tiles * (KEY_TILE // GROUP_KEYS)
            alive0 = tuple(jnp.full((8, SEL_LANES), -1, I32) * (g < n_groups).astype(I32)
                           for g in range(n_groups_max))
            need0 = jnp.full((8, SEL_LANES), k_sel, I32)

            def step(st, carry):
                alive, need, prefix = carry
                planes = [planes_ref[st, 8 * g:8 * g + 8, :] for g in range(n_groups_max)]
                ones = [al & pln for al, pln in zip(alive, planes)]
                cnt = _sublane_allsum(_tree_sum([lax.population_count(o) for o in ones]))
                take = cnt >= need
                prefix = jnp.where(take, prefix | jnp.left_shift(jnp.int32(1), 31 - st), prefix)
                need = jnp.where(take, need, need - cnt)
                flip = jnp.where(take, 0, -1)
                alive = tuple(al & (pln ^ flip) for al, pln in zip(alive, planes))
                return alive, need, prefix

            alive, need, prefix = lax.fori_loop(
                0, 32, step, (alive0, need0, jnp.zeros((8, SEL_LANES), I32)))
            thr = (prefix ^ jnp.int32(INT_MIN))[:1, :]
            need_eq = need[:1, :]

            n_eq = _sublane_allsum(_tree_sum([lax.population_count(al) for al in alive]))[:1, :]
            tied = (n_eq > need_eq) & (thr > KEY_NEG_INF)
            idx_bits = (seq - 1).bit_length()

            def tie_break():
                def count_eq_before(cut):
                    def body(t, acc):
                        base = pl.multiple_of(t * KEY_TILE, KEY_TILE)
                        hit = (keys[pl.ds(base, KEY_TILE), :] == thr) & ((key_row + base) < cut)
                        return acc + jnp.sum(hit.astype(I32).reshape(KEY_TILE // 8, 8, SEL_LANES), axis=0)
                    acc = lax.fori_loop(0, n_tiles, body, jnp.zeros((8, SEL_LANES), I32))
                    return jnp.sum(acc, axis=0, keepdims=True)

                def bit_step(it, cut):
                    cand = cut + jnp.left_shift(jnp.int32(1), idx_bits - 1 - it)
                    return jnp.where(count_eq_before(cand) < need_eq, cand, cut)

                return lax.fori_loop(0, idx_bits, bit_step, jnp.zeros((1, SEL_LANES), I32))

            any_tied = jnp.max(tied.astype(I32)) > 0
            idx_cut = lax.cond(any_tied, tie_break, lambda: jnp.full((1, SEL_LANES), seq, I32))
            return thr, idx_cut

        all_selected = (i * n_sub + a + 1) * SEL_LANES <= k_sel
        thr, idx_cut = lax.cond(
            all_selected,
            lambda: (jnp.full((1, SEL_LANES), INT_MIN, I32), jnp.full((1, SEL_LANES), seq, I32)),
            select_threshold)
        return thr, idx_cut, q_pos

    selected = [select(a) for a in range(n_sub)]
    n_tiles = tiles_of(n_sub - 1)

    def write_bias(base):
        k_idx = key_row + base
        for a, (thr, idx_cut, q_pos) in enumerate(selected):
            key = keys_ref[a, pl.ds(base, KEY_TILE), :]
            sel = ((key > thr) | ((key == thr) & (k_idx <= idx_cut))) & (k_idx <= q_pos)
            bias_t = jnp.where(sel, 0.0, -jnp.inf).astype(F32)
            for c in range(KEY_TILE // LANES):
                col = base + c * LANES
                col = col if isinstance(col, int) else pl.multiple_of(col, LANES)
                bias_ref[a * SEL_LANES:(a + 1) * SEL_LANES, pl.ds(col, LANES)] = \
                    bias_t[c * LANES:(c + 1) * LANES, :].T

    write_bias(0)

    for p in range(n_pairs):
        cols = slice(p * LANES, (p + 1) * LANES)
        qs_ref[p] = _stack_heads(q_ref[:, cols], first_head_q)
    m_ref[...] = jnp.full(m_ref.shape, F32_MIN, F32)
    l_ref[...] = jnp.zeros(l_ref.shape, F32)
    acc_ref[...] = jnp.zeros(acc_ref.shape, F32)
    heads_per_kv_pair = 2 * (B_HEADS // B_KV_HEADS) // 2

    ones_blk = jnp.ones((KEY_TILE, LANES), BF16)

    @pl.when(i == 0)
    def _():
        def xpose(r, carry):
            rows = pl.ds(pl.multiple_of(r * LANES, LANES), LANES)
            for c in range(kt_ref.shape[0]):
                blk = k_ref[rows, c * LANES:(c + 1) * LANES].astype(F32)
                kt_ref[c, :, rows] = blk.T.astype(BF16)
            return carry
        lax.fori_loop(0, seq // LANES, xpose, 0)

    def logits(base, p):
        kt = kt_ref[p // heads_per_kv_pair, :, pl.ds(base, KEY_TILE)]
        return jnp.dot(qs_ref[p], kt, preferred_element_type=F32)

    lookahead = s0_ref.shape[0]
    for u in range(lookahead):
        s0_ref[u] = logits(0, u)

    def attn_tiles(t0, count):
        bases = [pl.multiple_of(jnp.minimum(t0 + dt, n_tiles - 1) * KEY_TILE, KEY_TILE)
                 for dt in range(count + 1)]
        units = [(dt, p) for dt in range(count + 1) for p in range(n_pairs)]
        queue = [s0_ref[u] for u in range(lookahead)]
        for u, (dt, p) in enumerate(units[:count * n_pairs]):
            base = bases[dt]
            if p == 0:
                bias = bias_ref[:, pl.ds(base, KEY_TILE)]
                write_bias(bases[dt + 1])
                bias2 = jnp.concatenate([bias, bias], axis=0)
            kv = p // heads_per_kv_pair
            kv_cols = slice(kv * LANES, (kv + 1) * LANES)
            s = queue.pop(0) + bias2
            ahead_dt, ahead_p = units[u + lookahead]
            queue.append(logits(bases[ahead_dt], ahead_p))
            m_blk = s[:, :LANES]
            for c in range(1, KEY_TILE // LANES):
                m_blk = jnp.maximum(m_blk, s[:, c * LANES:(c + 1) * LANES])
            m_old = m_ref[p]
            m_new = jnp.maximum(m_old, jnp.max(m_blk, axis=-1, keepdims=True))
            alpha = jnp.exp2(m_old - m_new)
            e = jnp.concatenate(
                [jnp.exp2(s[:, c * LANES:(c + 1) * LANES] - m_new).astype(BF16)
                 for c in range(KEY_TILE // LANES)], axis=1)
            v_ext = jnp.concatenate([v_ref[pl.ds(base, KEY_TILE), kv_cols], ones_blk], axis=1)
            pv = jnp.dot(e, v_ext, preferred_element_type=F32)
            acc_ref[p] = acc_ref[p] * alpha + pv[:, :LANES]
            l_ref[p] = l_ref[p] * alpha + pv[:, LANES:]
            m_ref[p] = m_new
        for u in range(lookahead):
            s0_ref[u] = queue[u]

    def attn_two(t2, carry):
        attn_tiles(2 * t2, 2)
        return carry

    lax.fori_loop(0, n_tiles // 2, attn_two, 0)

    @pl.when(n_tiles % 2 == 1)
    def _():
        attn_tiles(n_tiles - 1, 1)

    for p in range(n_pairs):
        cols = slice(p * LANES, (p + 1) * LANES)
        l = l_ref[p]
        acc = acc_ref[p]
        num = jnp.where(first_head_q, acc[:Q_BLOCK], acc[Q_BLOCK:])
        den = jnp.where(first_head_q, l[:Q_BLOCK], l[Q_BLOCK:])
        gate = gate_ref[:, cols].astype(F32)
        y_ref[:, cols] = (num * gate / (den * (1.0 + jnp.exp(-gate)))).astype(y_ref.dtype)


def dsa_attention(proj):
    b, s, _ = proj.shape
    k_sel = min(TOPK_MAX, s // 4)
    assert s % KEY_TILE == 0 and s % Q_BLOCK == 0 and k_sel <= KEY_TILE
    kvw = B_KV_HEADS * HEAD_DIM
    n_pairs = B_HEADS // 2

    def qblock(width, idx):
        return pl.BlockSpec((None, Q_BLOCK, width), lambda bi, i: (bi, i, idx))

    def full(width, idx):
        return pl.BlockSpec((None, s, width), lambda bi, i: (bi, 0, idx))

    return pl.pallas_call(
        functools.partial(_dsa_kernel, k_sel=k_sel),
        grid=(b, s // Q_BLOCK),
        in_specs=[
            qblock(B_WIDTH, 0),
            qblock(B_WIDTH, 1),
            qblock(IDX_HEADS * HEAD_DIM, 4),
            full(kvw, 10),
            full(LANES, 22),
            qblock(LANES, 24),
            full(kvw, 13),
        ],
        out_specs=qblock(B_WIDTH, 0),
        out_shape=jax.ShapeDtypeStruct((b, s, B_WIDTH), BF16),
        scratch_shapes=[
            pltpu.VMEM((Q_BLOCK // SEL_LANES, s, SEL_LANES), I32),
            pltpu.VMEM((32, s // 32, SEL_LANES), I32),
            pltpu.VMEM((Q_BLOCK, s), F32),
            pltpu.VMEM((kvw // LANES, LANES, s), BF16),
            pltpu.VMEM((n_pairs, 2 * Q_BLOCK, LANES), BF16),
            pltpu.VMEM((ATTN_LOOKAHEAD, 2 * Q_BLOCK, KEY_TILE), F32),
            pltpu.VMEM((n_pairs, 2 * Q_BLOCK, LANES), F32),
            pltpu.VMEM((n_pairs, 2 * Q_BLOCK, LANES), F32),
            pltpu.VMEM((n_pairs, 2 * Q_BLOCK, LANES), F32),
        ],
        compiler_params=_cparams(("arbitrary", "arbitrary")),
        name="dsa_attention",
    )(proj, proj, proj, proj, proj, proj, proj)


def dsa_mixer(h, g, scale, shift, w_in_packed, cos, sin):
    proj = in_proj(h, g, scale, shift, w_in_packed, [f == 1 for f in B_ROPE_FLAGS], cos, sin, 512)
    return dsa_attention(proj)


def kernel(x, c, positions, norm_g, ada_w, ada_b, a_w_in, a_w_out, b_w_in, b_w_out, final_g):
    depth = norm_g.shape[0]
    d = x.shape[-1]
    mod = adaln_mod(c, ada_w, ada_b)
    cos, sin = rope_tables(positions)
    tables = {1: (cos, sin)}
    for _, dilation in A_GROUPS:
        if dilation > 1:
            tables[dilation] = class_major_rope_tables(cos, sin, dilation)
    h = x
    for i in range(depth):
        shift, scale, gate = mod[i, :, :d], mod[i, :, d:2 * d], mod[i, :, 2 * d:]
        last = i == depth - 1
        if i % 2 == 0:
            h = dilated_layer(h, norm_g[i], scale, shift, gate, a_w_in[i // 2], a_w_out[i // 2], tables)
            if last:
                h = final_norm(h, final_g)
        else:
            w_in, w_out = _dsa_weights(b_w_in[i // 2], b_w_out[i // 2])
            y = dsa_mixer(h, norm_g[i], scale, shift, w_in, cos, sin)
            h = out_proj(y, w_out, gate, h, final_g if last else None)
    return h
```

```python
import functools

import numpy as np
import jax
import jax.numpy as jnp
from jax import lax
from jax.experimental import pallas as pl
from jax.experimental.pallas import tpu as pltpu

F32 = jnp.float32
BF16 = jnp.bfloat16
I32 = jnp.int32

LANES = 128
HEAD_DIM = 64
HALF = HEAD_DIM // 2
ROPE_THETA = 10000.0
NORM_EPS = 1e-6
VMEM_LIMIT = 56 * 1024 * 1024
ROW_CHUNK = 256

A_HEADS = 16
A_WIDTH = A_HEADS * HEAD_DIM
A_GROUPS = ((128, 1), (512, 4), (2048, 16))
A_BLK = 128

B_HEADS = 16
B_KV_HEADS = 4
B_WIDTH = B_HEADS * HEAD_DIM
IDX_HEADS = 8
TOPK_MAX = 256
Q_BLOCK = 256
SEL_LANES = 128
KEY_TILE = 512
INT_MIN = -(2 ** 31)
KEY_NEG_INF = -2139095041
F32_MIN = float(np.finfo(np.float32).min)
LOG2_E = float(np.log2(np.e))
Q_PRESCALE = LOG2_E * HEAD_DIM ** -0.5


def _cparams(sem):
    return pltpu.CompilerParams(dimension_semantics=sem, vmem_limit_bytes=VMEM_LIMIT)


def _adaln_kernel(c_ref, w_ref, b_ref, o_ref):
    c = c_ref[...]
    ca = (c * (1.0 / (1.0 + jnp.exp(-c)))).astype(BF16)
    acc = jnp.dot(ca, w_ref[...].astype(BF16), preferred_element_type=F32)
    o_ref[...] = acc + b_ref[...]


def adaln_mod(c, ada_w, ada_b):
    depth, d, d3 = ada_w.shape
    b = c.shape[0]
    tn = 1024
    return pl.pallas_call(
        _adaln_kernel,
        grid=(depth, d3 // tn),
        in_specs=[
            pl.BlockSpec((b, d), lambda i, j: (0, 0)),
            pl.BlockSpec((None, d, tn), lambda i, j: (i, 0, j)),
            pl.BlockSpec((None, 1, tn), lambda i, j: (i, 0, j)),
        ],
        out_specs=pl.BlockSpec((None, b, tn), lambda i, j: (i, 0, j)),
        out_shape=jax.ShapeDtypeStruct((depth, b, d3), F32),
        compiler_params=_cparams(("arbitrary", "arbitrary")),
        name="adaln_mod",
    )(c, ada_w, ada_b.reshape(depth, 1, d3))


def _rope_table_kernel(pos_ref, inv_ref, cos_ref, sin_ref):
    ang = pos_ref[...].astype(F32) * inv_ref[...]
    lane = lax.broadcasted_iota(I32, ang.shape, 1)
    first_half = (lane % HEAD_DIM) < HALF
    cos_ref[...] = jnp.cos(ang)
    s = jnp.sin(ang)
    sin_ref[...] = jnp.where(first_half, -s, s)


def rope_tables(positions):
    b, s = positions.shape
    inv_freq = ROPE_THETA ** (-jnp.arange(HALF, dtype=F32) / HALF)
    inv_lane = jnp.tile(inv_freq, LANES // HALF).reshape(1, LANES)
    ts = min(s, 1024)
    out = jax.ShapeDtypeStruct((b, s, LANES), F32)
    return pl.pallas_call(
        _rope_table_kernel,
        grid=(b, s // ts),
        in_specs=[
            pl.BlockSpec((None, ts, 1), lambda i, j: (i, j, 0)),
            pl.BlockSpec((1, LANES), lambda i, j: (0, 0)),
        ],
        out_specs=[pl.BlockSpec((None, ts, LANES), lambda i, j: (i, j, 0))] * 2,
        out_shape=[out, out],
        compiler_params=_cparams(("arbitrary", "arbitrary")),
        name="rope_tables",
    )(positions.reshape(b, s, 1), inv_lane)


def _rope_block(t, cos, sin_signed, first_half):
    partner = jnp.where(first_half, pltpu.roll(t, LANES - HALF, 1), pltpu.roll(t, HALF, 1))
    return t * cos + partner * sin_signed


def _in_proj_kernel(x_ref, g_ref, sc_ref, sh_ref, w_ref, cos_ref, sin_ref, o_ref, u_ref, *stage,
                    dilation, rope_tiles, tn):
    tm = u_ref.shape[0]
    nblk = tn // LANES
    per = ROW_CHUNK // dilation
    x = x_ref[...]
    ms = jnp.mean(x * x, axis=-1, keepdims=True)
    xn = x * lax.rsqrt(ms + NORM_EPS)
    u_ref[...] = (xn * g_ref[...] * (1.0 + sc_ref[...]) + sh_ref[...]).astype(BF16)
    lane = lax.broadcasted_iota(I32, (per, LANES), 1)
    first_half = (lane % HEAD_DIM) < HALF
    step = 0
    for j, rope in enumerate(rope_tiles):
        for c in range(tm // ROW_CHUNK):
            acc = jnp.dot(u_ref[c * ROW_CHUNK:(c + 1) * ROW_CHUNK, :], w_ref[:, j * tn:(j + 1) * tn],
                          preferred_element_type=F32)
            if dilation > 1:
                slot = step % stage[0].shape[0]
                step += 1
                for k in range(nblk):
                    stage[0][slot, k] = acc[:, k * LANES:(k + 1) * LANES]
            for r in range(dilation):
                rows = slice(c * per, (c + 1) * per)
                if rope:
                    cos = cos_ref[rows, :] if dilation == 1 else cos_ref[r, rows, :]
                    sin = sin_ref[rows, :] if dilation == 1 else sin_ref[r, rows, :]
                for k in range(nblk):
                    cols = slice(j * tn + k * LANES, j * tn + (k + 1) * LANES)
                    if dilation > 1:
                        blk = stage[0][slot, k, pl.ds(r, per, stride=dilation), :]
                    else:
                        blk = acc[:, k * LANES:(k + 1) * LANES]
                    if rope:
                        blk = _rope_block(blk, cos, sin, first_half)
                    if dilation > 1:
                        o_ref[r, rows, cols] = blk.astype(o_ref.dtype)
                    else:
                        o_ref[rows, cols] = blk.astype(o_ref.dtype)


def in_proj(h, g, scale, shift, w, rope_tiles, cos, sin, tn, dilation=1, w_block=0):
    b, s, d = h.shape
    n = len(rope_tiles) * tn
    tm = min(s, 512)
    assert tm % ROW_CHUNK == 0 and ROW_CHUNK % (16 * dilation) == 0 and (w_block + 1) * n <= w.shape[1]
    scratch = [pltpu.VMEM((tm, d), BF16)]
    if dilation == 1:
        tab_spec = pl.BlockSpec((None, tm, LANES), lambda bi, i: (bi, i, 0))
        out_spec = pl.BlockSpec((None, tm, n), lambda bi, i: (bi, i, 0))
        out_shape = jax.ShapeDtypeStruct((b, s, n), BF16)
    else:
        tab_spec = pl.BlockSpec((None, dilation, tm // dilation, LANES), lambda bi, i: (bi, 0, i, 0))
        out_spec = pl.BlockSpec((None, dilation, tm // dilation, n), lambda bi, i: (bi, 0, i, 0))
        out_shape = jax.ShapeDtypeStruct((b, dilation, s // dilation, n), BF16)
        scratch.append(pltpu.VMEM((2, tn // LANES, ROW_CHUNK, LANES), F32))
    return pl.pallas_call(
        functools.partial(_in_proj_kernel, dilation=dilation, rope_tiles=tuple(rope_tiles), tn=tn),
        grid=(b, s // tm),
        in_specs=[
            pl.BlockSpec((None, tm, d), lambda bi, i: (bi, i, 0)),
            pl.BlockSpec((1, d), lambda bi, i: (0, 0)),
            pl.BlockSpec((None, 1, d), lambda bi, i: (bi, 0, 0)),
            pl.BlockSpec((None, 1, d), lambda bi, i: (bi, 0, 0)),
            pl.BlockSpec((d, n), lambda bi, i: (0, w_block)),
            tab_spec, tab_spec,
        ],
        out_specs=out_spec,
        out_shape=out_shape,
        scratch_shapes=scratch,
        compiler_params=_cparams(("arbitrary", "arbitrary")),
        name=f"in_proj_d{dilation}",
    )(h, g.reshape(1, d), scale.reshape(b, 1, d), shift.reshape(b, 1, d), w, cos, sin)


def _out_proj_kernel(y_ref, w_ref, gate_ref, h_ref, *rest):
    acc = jnp.dot(y_ref[...], w_ref[...], preferred_element_type=F32)
    h_new = h_ref[...] + gate_ref[...] * acc
    if len(rest) == 2:
        g_ref, o_ref = rest
        ms = jnp.mean(h_new * h_new, axis=-1, keepdims=True)
        o_ref[...] = h_new * lax.rsqrt(ms + NORM_EPS) * g_ref[...]
    else:
        rest[0][...] = h_new


def out_proj(y, w, gate, h, final_g=None):
    b, s, d = h.shape
    k = y.shape[-1]
    tm = min(s, 1024)
    in_specs = [
        pl.BlockSpec((None, tm, k), lambda bi, i: (bi, i, 0)),
        pl.BlockSpec((k, d), lambda bi, i: (0, 0)),
        pl.BlockSpec((None, 1, d), lambda bi, i: (bi, 0, 0)),
        pl.BlockSpec((None, tm, d), lambda bi, i: (bi, i, 0)),
    ]
    args = [y, w, gate.reshape(b, 1, d), h]
    if final_g is not None:
        in_specs.append(pl.BlockSpec((1, d), lambda bi, i: (0, 0)))
        args.append(final_g.reshape(1, d))
    return pl.pallas_call(
        _out_proj_kernel,
        grid=(b, s // tm),
        in_specs=in_specs,
        out_specs=pl.BlockSpec((None, tm, d), lambda bi, i: (bi, i, 0)),
        out_shape=jax.ShapeDtypeStruct((b, s, d), F32),
        compiler_params=_cparams(("arbitrary", "arbitrary")),
        name="out_proj",
    )(*args)


def _final_norm_kernel(x_ref, g_ref, o_ref):
    x = x_ref[...]
    ms = jnp.mean(x * x, axis=-1, keepdims=True)
    o_ref[...] = x * lax.rsqrt(ms + NORM_EPS) * g_ref[...]


def final_norm(h, g):
    b, s, d = h.shape
    tm = min(s, 1024)
    return pl.pallas_call(
        _final_norm_kernel,
        grid=(b, s // tm),
        in_specs=[
            pl.BlockSpec((None, tm, d), lambda bi, i: (bi, i, 0)),
            pl.BlockSpec((1, d), lambda bi, i: (0, 0)),
        ],
        out_specs=pl.BlockSpec((None, tm, d), lambda bi, i: (bi, i, 0)),
        out_shape=jax.ShapeDtypeStruct((b, s, d), F32),
        compiler_params=_cparams(("arbitrary", "arbitrary")),
        name="final_norm",
    )(h, g.reshape(1, d))


def _stack_heads(pair, first_head):
    zero = jnp.zeros_like(pair)
    return jnp.concatenate([jnp.where(first_head, pair, zero), jnp.where(first_head, zero, pair)], axis=0)


def _banded_kernel(q_ref, kp_ref, kc_ref, vp_ref, vc_ref, o_ref, lse_ref, kwin_ref, vwin_ref, *, tq):
    i = pl.program_id(2)
    nsub = tq // A_BLK
    lane = lax.broadcasted_iota(I32, (A_BLK, LANES), 1)
    first_head = lane < HEAD_DIM
    odd_lane = (lane % 2) == 1
    r2 = lax.broadcasted_iota(I32, (2 * A_BLK, 2 * A_BLK), 0) % A_BLK
    c2 = lax.broadcasted_iota(I32, (2 * A_BLK, 2 * A_BLK), 1)
    own_ok = (c2 >= A_BLK) & (c2 - A_BLK <= r2)
    ones_blk = jnp.ones((2 * A_BLK, LANES), BF16)
    n_pairs = A_HEADS // 2

    kwin_ref[:A_BLK, :] = kp_ref[...]
    kwin_ref[A_BLK:, :] = kc_ref[...]
    vwin_ref[:A_BLK, :] = vp_ref[...]
    vwin_ref[A_BLK:, :] = vc_ref[...]

    def logits(a, p):
        cols = slice(p * LANES, (p + 1) * LANES)
        qs = _stack_heads(q_ref[a * A_BLK:(a + 1) * A_BLK, cols], first_head)
        k2 = kwin_ref[a * A_BLK:(a + 2) * A_BLK, cols]
        return lax.dot_general(qs, k2, (((1,), (1,)), ((), ())), preferred_element_type=F32)

    units = [(a, p) for a in range(nsub) for p in range(n_pairs)]
    s_next = logits(*units[0])
    lse_tile = bias = None
    for u, (a, p) in enumerate(units):
        rows = slice(a * A_BLK, (a + 1) * A_BLK)
        cols = slice(p * LANES, (p + 1) * LANES)
        if p == 0:
            prev_shift = jnp.where(i * nsub + a > 0, 0, 2 * A_BLK)
            mask = own_ok | ((c2 < A_BLK) & (c2 >= r2 + prev_shift))
            bias = jnp.where(mask, 0.0, -jnp.inf).astype(F32)
            lse_tile = jnp.zeros((A_BLK, LANES), F32)
        s = s_next + bias
        if u + 1 < len(units):
            s_next = logits(*units[u + 1])
        m = jnp.max(s, axis=-1, keepdims=True)
        e = jnp.exp2(s - m).astype(BF16)
        v_ext = jnp.concatenate([vwin_ref[a * A_BLK:(a + 2) * A_BLK, cols], ones_blk], axis=1)
        pv = jnp.dot(e, v_ext, preferred_element_type=F32)
        num = jnp.where(first_head, pv[:A_BLK, :LANES], pv[A_BLK:, :LANES])
        o_ref[rows, cols] = num * (1.0 / jnp.where(first_head, pv[:A_BLK, LANES:], pv[A_BLK:, LANES:]))
        den_pair = jnp.where(odd_lane, pv[A_BLK:, LANES:], pv[:A_BLK, LANES:])
        m_pair = jnp.where(odd_lane, m[A_BLK:], m[:A_BLK])
        lse_tile = jnp.where((lane // 2) == p, m_pair + jnp.log(den_pair) * LOG2_E, lse_tile)
        if p == n_pairs - 1:
            lse_ref[rows, :] = lse_tile


def banded_group_attention(qkv):
    b, d, n, _ = qkv.shape
    tq = min(2 * A_BLK, n)
    sub = tq // A_BLK

    def cur(blk):
        return pl.BlockSpec((None, None, tq, A_WIDTH), lambda bi, r, i: (bi, r, i, blk))

    def prev(blk):
        return pl.BlockSpec((None, None, A_BLK, A_WIDTH),
                            lambda bi, r, i: (bi, r, jnp.maximum(i * sub - 1, 0), blk))

    return pl.pallas_call(
        functools.partial(_banded_kernel, tq=tq),
        grid=(b, d, n // tq),
        in_specs=[cur(0), prev(1), cur(1), prev(2), cur(2)],
        out_specs=[
            pl.BlockSpec((None, None, tq, A_WIDTH), lambda bi, r, i: (bi, r, i, 0)),
            pl.BlockSpec((None, None, tq, LANES), lambda bi, r, i: (bi, r, i, 0)),
        ],
        out_shape=[
            jax.ShapeDtypeStruct((b, d, n, A_WIDTH), F32),
            jax.ShapeDtypeStruct((b, d, n, LANES), F32),
        ],
        scratch_shapes=[pltpu.VMEM((tq + A_BLK, A_WIDTH), BF16)] * 2,
        compiler_params=_cparams(("arbitrary", "arbitrary", "arbitrary")),
        name=f"banded_attn_d{d}",
    )(qkv, qkv, qkv, qkv, qkv)


def _expand_heads(x, expand):
    hi = x.astype(BF16)
    lo = (x - hi.astype(F32)).astype(BF16)
    return (jnp.dot(hi, expand, preferred_element_type=F32)
            + jnp.dot(lo, expand, preferred_element_type=F32))


def _merge_out_kernel(o0_ref, o1_ref, o2_ref, l0_ref, l1_ref, l2_ref, gate_ref, ex_ref, w_ref,
                      mod_ref, h_ref, hout_ref, y_ref, ot_ref, lt_ref, *, dilations):
    o_refs, l_refs = (o0_ref, o1_ref, o2_ref), (l0_ref, l1_ref, l2_ref)
    tm = y_ref.shape[0]
    nblk = y_ref.shape[1] // LANES
    for g, d in enumerate(dilations):
        if d == 1:
            continue
        per = tm // d
        for r in range(d):
            lt_ref[g, pl.ds(r, per, stride=d), :] = l_refs[g][r]
            for k in range(nblk):
                ot_ref[g, k, pl.ds(r, per, stride=d), :] = o_refs[g][r, :, k * LANES:(k + 1) * LANES]
    lses = [l_refs[g][0] if d == 1 else lt_ref[g] for g, d in enumerate(dilations)]
    m = jnp.maximum(jnp.maximum(lses[0], lses[1]), lses[2])
    es = [jnp.exp2(l - m) for l in lses]
    inv = 1.0 / (es[0] + es[1] + es[2])
    ex = ex_ref[...]
    alphas = [_expand_heads(e * inv, ex) for e in es[:-1]]
    alphas.append(1.0 - alphas[0] - alphas[1])
    for k in range(nblk):
        cols = slice(k * LANES, (k + 1) * LANES)
        y = jnp.zeros((tm, LANES), F32)
        for g, d in enumerate(dilations):
            og = o_refs[g][0, :, cols] if d == 1 else ot_ref[g, k]
            y += alphas[g][:, cols] * og
        gate = gate_ref[:, cols].astype(F32)
        y_ref[:, cols] = (y * gate * (1.0 / (1.0 + jnp.exp(-gate)))).astype(y_ref.dtype)
    acc = jnp.dot(y_ref[...], w_ref[...], preferred_element_type=F32)
    hout_ref[...] = h_ref[...] + mod_ref[...] * acc


def merge_out_proj(outs, lses, gate, w_out, mod_gate, h):
    b, s, w = gate.shape
    d_model = h.shape[-1]
    tm = min(s, 256)
    dil = tuple(o.shape[1] for o in outs)
    expand = (jnp.arange(LANES)[:, None] == (jnp.arange(w)[None, :] // HEAD_DIM)).astype(BF16)

    def cm(d, width):
        return pl.BlockSpec((None, d, tm // d, width), lambda bi, i: (bi, 0, i, 0))

    def tspec(width):
        return pl.BlockSpec((None, tm, width), lambda bi, i: (bi, i, 0))

    return pl.pallas_call(
        functools.partial(_merge_out_kernel, dilations=dil),
        grid=(b, s // tm),
        in_specs=[cm(d, w) for d in dil] + [cm(d, LANES) for d in dil]
                 + [tspec(w), pl.BlockSpec((LANES, w), lambda bi, i: (0, 0)),
                    pl.BlockSpec((w, d_model), lambda bi, i: (0, 0)),
                    pl.BlockSpec((None, 1, d_model), lambda bi, i: (bi, 0, 0)),
                    tspec(d_model)],
        out_specs=tspec(d_model),
        out_shape=jax.ShapeDtypeStruct((b, s, d_model), F32),
        scratch_shapes=[pltpu.VMEM((tm, w), BF16),
                        pltpu.VMEM((len(dil), w // LANES, tm, LANES), F32),
                        pltpu.VMEM((len(dil), tm, LANES), F32)],
        compiler_params=_cparams(("arbitrary", "arbitrary")),
        name="merge_out_proj",
    )(*outs, *lses, gate, expand, w_out, mod_gate.reshape(b, 1, d_model), h)


def _class_major(t, d):
    b, s = t.shape[:2]
    return jnp.swapaxes(t.reshape(b, s // d, d, *t.shape[2:]), 1, 2)


def class_major_rope_tables(cos, sin, dilation):
    return _class_major(cos, dilation), _class_major(sin, dilation)


def dilated_layer(h, g, scale, shift, mod_gate, w_in, w_out, tables):
    tn = 512
    per = A_WIDTH // tn
    qkv_flags = (True,) * (2 * per) + (False,) * per
    n_groups = len(A_GROUPS)
    col = np.arange(w_in.shape[1])
    is_q = (col < n_groups * 3 * A_WIDTH) & ((col // A_WIDTH) % 3 == 0)
    w = (w_in * jnp.asarray(np.where(is_q, Q_PRESCALE, 1.0), F32)).astype(BF16)
    outs, lses = [], []
    for gi, (window, dilation) in enumerate(A_GROUPS):
        assert window // dilation == A_BLK
        qkv = in_proj(h, g, scale, shift, w, qkv_flags, *tables[dilation], tn, dilation, w_block=gi)
        if dilation == 1:
            qkv = qkv[:, None]
        o, lse = banded_group_attention(qkv)
        outs.append(o)
        lses.append(lse)
    gate = in_proj(h, g, scale, shift, w, (False,) * per, *tables[1], tn, w_block=3 * n_groups)
    return merge_out_proj(outs, lses, gate, w_out.astype(BF16), mod_gate, h)


B_COLS = 3584
B_ROPE_FLAGS = (1, 1, 0, 0, 1, 1, 0)


def _dsa_head_perm():
    group = B_HEADS // B_KV_HEADS
    order = []
    for g2 in range(B_KV_HEADS // 2):
        for r in range(group):
            order += [(2 * g2) * group + r, (2 * g2 + 1) * group + r]
    return np.asarray(order)


def _dsa_weights(w_in, w_out):
    d = w_in.shape[0]
    cuts = np.cumsum((B_WIDTH, B_KV_HEADS * HEAD_DIM, B_KV_HEADS * HEAD_DIM, IDX_HEADS * HEAD_DIM,
                      HEAD_DIM, IDX_HEADS, B_WIDTH))[:-1]
    wq, wk, wv, wqi, wki, wwi, wg = jnp.split(w_in, cuts, axis=1)
    cols = (_dsa_head_perm()[:, None] * HEAD_DIM + np.arange(HEAD_DIM)[None, :]).reshape(-1)
    zeros = lambda n: jnp.zeros((d, n), w_in.dtype)
    wq = wq * Q_PRESCALE
    w = jnp.concatenate([wq[:, cols], wg[:, cols], wqi, wk, wki, wki, zeros(LANES),
                         wwi, zeros(2 * LANES - IDX_HEADS), wv], axis=1)
    assert w.shape[1] == B_COLS
    return w.astype(BF16), w_out[cols, :].astype(BF16)


def _sortable(x):
    bits = pltpu.bitcast(x, I32)
    return bits ^ ((bits >> 31) & jnp.int32(0x7FFFFFFF))


def _bit_transpose32(words):
    a = list(words)
    j, m = 16, 0x0000FFFF
    while j:
        mask = jnp.int32(m - (1 << 32) if m >= (1 << 31) else m)
        shift = jnp.full(a[0].shape, j, I32)
        k = 0
        while k < 32:
            t = (a[k] ^ lax.shift_right_logical(a[k + j], shift)) & mask
            a[k] = a[k] ^ t
            a[k + j] = a[k + j] ^ lax.shift_left(t, shift)
            k = (k + j + 1) & ~j
        j >>= 1
        m = (m ^ (m << j)) & 0xFFFFFFFF
    return a


def _sublane_allsum(x):
    x = x + pltpu.roll(x, 4, 0)
    x = x + pltpu.roll(x, 2, 0)
    return x + pltpu.roll(x, 1, 0)


def _tree_sum(parts):
    parts = list(parts)
    while len(parts) > 1:
        odd = [parts[-1]] if len(parts) % 2 else []
        parts = [parts[j] + parts[j + 1] for j in range(0, len(parts) - 1, 2)] + odd
    return parts[0]


GROUP_KEYS = 256
ATTN_LOOKAHEAD = 1


def _dsa_kernel(q_ref, gate_ref, qi_ref, k_ref, ki_ref, wi_ref, v_ref, y_ref,
                keys_ref, planes_ref, bias_ref, kt_ref, qs_ref, s0_ref, m_ref, l_ref, acc_ref,
                *, k_sel):
    i = pl.program_id(1)
    seq = keys_ref.shape[1]
    n_sub = Q_BLOCK // SEL_LANES
    n_pairs = B_HEADS // 2
    n_groups_max = seq // GROUP_KEYS
    lane_q = lax.broadcasted_iota(I32, (Q_BLOCK, LANES), 1)
    first_head_q = lane_q < HEAD_DIM
    first_head = lax.broadcasted_iota(I32, (SEL_LANES, LANES), 1) < HEAD_DIM
    key_row = lax.broadcasted_iota(I32, (KEY_TILE, SEL_LANES), 0)
    lane_idx = lax.broadcasted_iota(I32, (KEY_TILE, SEL_LANES), 1)

    @pl.when(i == 0)
    def _():
        planes_ref[...] = jnp.zeros(planes_ref.shape, I32)
        keys_ref[...] = jnp.zeros(keys_ref.shape, I32)

    def tiles_of(a):
        return ((i * n_sub + a) * SEL_LANES) // KEY_TILE + 1

    def select(a):
        n_tiles = tiles_of(a)
        rows_a = slice(a * SEL_LANES, (a + 1) * SEL_LANES)
        q_pos = (i * n_sub + a) * SEL_LANES + lane_idx
        keys = keys_ref.at[a]

        w_t = (wi_ref[rows_a, :].astype(F32) * (IDX_HEADS ** -0.5 * HEAD_DIM ** -0.5)).T
        qi_stacked = [_stack_heads(qi_ref[rows_a, p * LANES:(p + 1) * LANES], first_head)
                      for p in range(IDX_HEADS // 2)]

        def score_tile(t):
            base = pl.multiple_of(t * KEY_TILE, KEY_TILE)
            kk = ki_ref[pl.ds(base, KEY_TILE), :]
            score = jnp.zeros((KEY_TILE, SEL_LANES), F32)
            for p in range(IDX_HEADS // 2):
                sc = lax.dot_general(kk, qi_stacked[p], (((1,), (1,)), ((), ())),
                                     preferred_element_type=F32)
                score += jnp.maximum(sc[:, :SEL_LANES], 0.0) * w_t[2 * p:2 * p + 1, :]
                score += jnp.maximum(sc[:, SEL_LANES:], 0.0) * w_t[2 * p + 1:2 * p + 2, :]
            causal = (key_row + base) <= q_pos
            score = jnp.where(causal, score + 0.0, -jnp.inf)
            key = _sortable(score)
            keys[pl.ds(base, KEY_TILE), :] = key
            ukey = key ^ jnp.int32(INT_MIN)
            for g in range(KEY_TILE // GROUP_KEYS):
                words = [ukey[g * GROUP_KEYS + 8 * j:g * GROUP_KEYS + 8 * j + 8, :] for j in range(32)]
                rows = pl.ds(pl.multiple_of(t * (KEY_TILE // 32) + 8 * g, 8), 8)
                for b, plane in enumerate(_bit_transpose32(words)):
                    planes_ref[b, rows, :] = plane

        def score_two(t2, carry):
            score_tile(2 * t2)
            score_tile(2 * t2 + 1)
            return carry

        lax.fori_loop(0, n_tiles // 2, score_two, 0)

        @pl.when(n_tiles % 2 == 1)
        def _():
            score_tile(n_tiles - 1)

        def select_threshold():
            n_groups = n_tiles * (KEY_TILE // GROUP_KEYS)
            alive0 = tuple(jnp.full((8, SEL_LANES), -1, I32) * (g < n_groups).astype(I32)
                           for g in range(n_groups_max))
            need0 = jnp.full((8, SEL_LANES), k_sel, I32)

            def step(st, carry):
                alive, need, prefix = carry
                planes = [planes_ref[st, 8 * g:8 * g + 8, :] for g in range(n_groups_max)]
                ones = [al & pln for al, pln in zip(alive, planes)]
                cnt = _sublane_allsum(_tree_sum([lax.population_count(o) for o in ones]))
                take = cnt >= need
                prefix = jnp.where(take, prefix | jnp.left_shift(jnp.int32(1), 31 - st), prefix)
                need = jnp.where(take, need, need - cnt)
                flip = jnp.where(take, 0, -1)
                alive = tuple(al & (pln ^ flip) for al, pln in zip(alive, planes))
                return alive, need, prefix

            alive, need, prefix = lax.fori_loop(
                0, 32, step, (alive0, need0, jnp.zeros((8, SEL_LANES), I32)))
            thr = (prefix ^ jnp.int32(INT_MIN))[:1, :]
            need_eq = need[:1, :]

            n_eq = _sublane_allsum(_tree_sum([lax.population_count(al) for al in alive]))[:1, :]
            tied = (n_eq > need_eq) & (thr > KEY_NEG_INF)
            idx_bits = (seq - 1).bit_length()

            def tie_break():
                def count_eq_before(cut):
                    def body(t, acc):
                        base = pl.multiple_of(t * KEY_TILE, KEY_TILE)
                        hit = (keys[pl.ds(base, KEY_TILE), :] == thr) & ((key_row + base) < cut)
                        return acc + jnp.sum(hit.astype(I32).reshape(KEY_TILE // 8, 8, SEL_LANES), axis=0)
                    acc = lax.fori_loop(0, n_tiles, body, jnp.zeros((8, SEL_LANES), I32))
                    return jnp.sum(acc, axis=0, keepdims=True)

                def bit_step(it, cut):
                    cand = cut + jnp.left_shift(jnp.int32(1), idx_bits - 1 - it)
                    return jnp.where(count_eq_before(cand) < need_eq, cand, cut)

                return lax.fori_loop(0, idx_bits, bit_step, jnp.zeros((1, SEL_LANES), I32))

            any_tied = jnp.max(tied.astype(I32)) > 0
            idx_cut = lax.cond(any_tied, tie_break, lambda: jnp.full((1, SEL_LANES), seq, I32))
            return thr, idx_cut

        all_selected = (i * n_sub + a + 1) * SEL_LANES <= k_sel
        thr, idx_cut = lax.cond(
            all_selected,
            lambda: (jnp.full((1, SEL_LANES), INT_MIN, I32), jnp.full((1, SEL_LANES), seq, I32)),
            select_threshold)
        return thr, idx_cut, q_pos

    selected = [select(a) for a in range(n_sub)]
    n_tiles = tiles_of(n_sub - 1)

    def write_bias(base):
        k_idx = key_row + base
        for a, (thr, idx_cut, q_pos) in enumerate(selected):
            key = keys_ref[a, pl.ds(base, KEY_TILE), :]
            sel = ((key > thr) | ((key == thr) & (k_idx <= idx_cut))) & (k_idx <= q_pos)
            bias_t = jnp.where(sel, 0.0, -jnp.inf).astype(F32)
            for c in range(KEY_TILE // LANES):
                col = base + c * LANES
                col = col if isinstance(col, int) else pl.multiple_of(col, LANES)
                bias_ref[a * SEL_LANES:(a + 1) * SEL_LANES, pl.ds(col, LANES)] = \
                    bias_t[c * LANES:(c + 1) * LANES, :].T

    write_bias(0)

    for p in range(n_pairs):
        cols = slice(p * LANES, (p + 1) * LANES)
        qs_ref[p] = _stack_heads(q_ref[:, cols], first_head_q)
    m_ref[...] = jnp.full(m_ref.shape, F32_MIN, F32)
    l_ref[...] = jnp.zeros(l_ref.shape, F32)
    acc_ref[...] = jnp.zeros(acc_ref.shape, F32)
    heads_per_kv_pair = 2 * (B_HEADS // B_KV_HEADS) // 2

    ones_blk = jnp.ones((KEY_TILE, LANES), BF16)

    @pl.when(i == 0)
    def _():
        def xpose(r, carry):
            rows = pl.ds(pl.multiple_of(r * LANES, LANES), LANES)
            for c in range(kt_ref.shape[0]):
                blk = k_ref[rows, c * LANES:(c + 1) * LANES].astype(F32)
                kt_ref[c, :, rows] = blk.T.astype(BF16)
            return carry
        lax.fori_loop(0, seq // LANES, xpose, 0)

    def logits(base, p):
        kt = kt_ref[p // heads_per_kv_pair, :, pl.ds(base, KEY_TILE)]
        return jnp.dot(qs_ref[p], kt, preferred_element_type=F32)

    lookahead = s0_ref.shape[0]
    for u in range(lookahead):
        s0_ref[u] = logits(0, u)

    def attn_tiles(t0, count):
        bases = [pl.multiple_of(jnp.minimum(t0 + dt, n_tiles - 1) * KEY_TILE, KEY_TILE)
                 for dt in range(count + 1)]
        units = [(dt, p) for dt in range(count + 1) for p in range(n_pairs)]
        queue = [s0_ref[u] for u in range(lookahead)]
        for u, (dt, p) in enumerate(units[:count * n_pairs]):
            base = bases[dt]
            if p == 0:
                bias = bias_ref[:, pl.ds(base, KEY_TILE)]
                write_bias(bases[dt + 1])
                bias2 = jnp.concatenate([bias, bias], axis=0)
            kv = p // heads_per_kv_pair
            kv_cols = slice(kv * LANES, (kv + 1) * LANES)
            s = queue.pop(0) + bias2
            ahead_dt, ahead_p = units[u + lookahead]
            queue.append(logits(bases[ahead_dt], ahead_p))
            m_blk = s[:, :LANES]
            for c in range(1, KEY_TILE // LANES):
                m_blk = jnp.maximum(m_blk, s[:, c * LANES:(c + 1) * LANES])
            m_old = m_ref[p]
            m_new = jnp.maximum(m_old, jnp.max(m_blk, axis=-1, keepdims=True))
            alpha = jnp.exp2(m_old - m_new)
            e = jnp.concatenate(
                [jnp.exp2(s[:, c * LANES:(c + 1) * LANES] - m_new).astype(BF16)
                 for c in range(KEY_TILE // LANES)], axis=1)
            v_ext = jnp.concatenate([v_ref[pl.ds(base, KEY_TILE), kv_cols], ones_blk], axis=1)
            pv = jnp.dot(e, v_ext, preferred_element_type=F32)
            acc_ref[p] = acc_ref[p] * alpha + pv[:, :LANES]
            l_ref[p] = l_ref[p] * alpha + pv[:, LANES:]
            m_ref[p] = m_new
        for u in range(lookahead):
            s0_ref[u] = queue[u]

    def attn_two(t2, carry):
        attn_tiles(2 * t2, 2)
        return carry

    lax.fori_loop(0, n_tiles // 2, attn_two, 0)

    @pl.when(n_tiles % 2 == 1)
    def _():
        attn_tiles(n_tiles - 1, 1)

    for p in range(n_pairs):
        cols = slice(p * LANES, (p + 1) * LANES)
        l = l_ref[p]
        acc = acc_ref[p]
        num = jnp.where(first_head_q, acc[:Q_BLOCK], acc[Q_BLOCK:])
        den = jnp.where(first_head_q, l[:Q_BLOCK], l[Q_BLOCK:])
        gate = gate_ref[:, cols].astype(F32)
        y_ref[:, cols] = (num * gate / (den * (1.0 + jnp.exp(-gate)))).astype(y_ref.dtype)


def dsa_attention(proj):
    b, s, _ = proj.shape
    k_sel = min(TOPK_MAX, s // 4)
    assert s % KEY_TILE == 0 and s % Q_BLOCK == 0 and k_sel <= KEY_TILE
    kvw = B_KV_HEADS * HEAD_DIM
    n_pairs = B_HEADS // 2

    def qblock(width, idx):
        return pl.BlockSpec((None, Q_BLOCK, width), lambda bi, i: (bi, i, idx))

    def full(width, idx):
        return pl.BlockSpec((None, s, width), lambda bi, i: (bi, 0, idx))

    return pl.pallas_call(
        functools.partial(_dsa_kernel, k_sel=k_sel),
        grid=(b, s // Q_BLOCK),
        in_specs=[
            qblock(B_WIDTH, 0),
            qblock(B_WIDTH, 1),
            qblock(IDX_HEADS * HEAD_DIM, 4),
            full(kvw, 10),
            full(LANES, 22),
            qblock(LANES, 24),
            full(kvw, 13),
        ],
        out_specs=qblock(B_WIDTH, 0),
        out_shape=jax.ShapeDtypeStruct((b, s, B_WIDTH), BF16),
        scratch_shapes=[
            pltpu.VMEM((Q_BLOCK // SEL_LANES, s, SEL_LANES), I32),
            pltpu.VMEM((32, s // 32, SEL_LANES), I32),
            pltpu.VMEM((Q_BLOCK, s), F32),
            pltpu.VMEM((kvw // LANES, LANES, s), BF16),
            pltpu.VMEM((n_pairs, 2 * Q_BLOCK, LANES), BF16),
            pltpu.VMEM((ATTN_LOOKAHEAD, 2 * Q_BLOCK, KEY_TILE), F32),
            pltpu.VMEM((n_pairs, 2 * Q_BLOCK, LANES), F32),
            pltpu.VMEM((n_pairs, 2 * Q_BLOCK, LANES), F32),
            pltpu.VMEM((n_pairs, 2 * Q_BLOCK, LANES), F32),
        ],
        compiler_params=_cparams(("arbitrary", "arbitrary")),
        name="dsa_attention",
    )(proj, proj, proj, proj, proj, proj, proj)


def dsa_mixer(h, g, scale, shift, w_in_packed, cos, sin):
    proj = in_proj(h, g, scale, shift, w_in_packed, [f == 1 for f in B_ROPE_FLAGS], cos, sin, 512)
    return dsa_attention(proj)


def kernel(x, c, positions, norm_g, ada_w, ada_b, a_w_in, a_w_out, b_w_in, b_w_out, final_g):
    depth = norm_g.shape[0]
    d = x.shape[-1]
    mod = adaln_mod(c, ada_w, ada_b)
    cos, sin = rope_tables(positions)
    tables = {1: (cos, sin)}
    for _, dilation in A_GROUPS:
        if dilation > 1:
            tables[dilation] = class_major_rope_tables(cos, sin, dilation)
    h = x
    for i in range(depth):
        shift, scale, gate = mod[i, :, :d], mod[i, :, d:2 * d], mod[i, :, 2 * d:]
        last = i == depth - 1
        if i % 2 == 0:
            h = dilated_layer(h, norm_g[i], scale, shift, gate, a_w_in[i // 2], a_w_out[i // 2], tables)
            if last:
                h = final_norm(h, final_g)
        else:
            w_in, w_out = _dsa_weights(b_w_in[i // 2], b_w_out[i // 2])
            y = dsa_mixer(h, norm_g[i], scale, shift, w_in, cos, sin)
            h = out_proj(y, w_out, gate, h, final_g if last else None)
    return h
```

```python
import functools

import numpy as np
import jax
import jax.numpy as jnp
from jax import lax
from jax.experimental import pallas as pl
from jax.experimental.pallas import tpu as pltpu

F32 = jnp.float32
BF16 = jnp.bfloat16
I32 = jnp.int32

LANES = 128
HEAD_DIM = 64
HALF = HEAD_DIM // 2
ROPE_THETA = 10000.0
NORM_EPS = 1e-6
VMEM_LIMIT = 56 * 1024 * 1024
ROW_CHUNK = 256

A_HEADS = 16
A_WIDTH = A_HEADS * HEAD_DIM
A_GROUPS = ((128, 1), (512, 4), (2048, 16))
A_BLK = 128

B_HEADS = 16
B_KV_HEADS = 4
B_WIDTH = B_HEADS * HEAD_DIM
IDX_HEADS = 8
TOPK_MAX = 256
Q_BLOCK = 256
SEL_LANES = 128
KEY_TILE = 512
INT_MIN = -(2 ** 31)
KEY_NEG_INF = -2139095041
F32_MIN = float(np.finfo(np.float32).min)
LOG2_E = float(np.log2(np.e))
Q_PRESCALE = LOG2_E * HEAD_DIM ** -0.5


def _cparams(sem):
    return pltpu.CompilerParams(dimension_semantics=sem, vmem_limit_bytes=VMEM_LIMIT)


def _adaln_kernel(c_ref, w_ref, b_ref, o_ref):
    c = c_ref[...]
    ca = (c * (1.0 / (1.0 + jnp.exp(-c)))).astype(BF16)
    acc = jnp.dot(ca, w_ref[...].astype(BF16), preferred_element_type=F32)
    o_ref[...] = acc + b_ref[...]


def adaln_mod(c, ada_w, ada_b):
    depth, d, d3 = ada_w.shape
    b = c.shape[0]
    tn = 1024
    return pl.pallas_call(
        _adaln_kernel,
        grid=(depth, d3 // tn),
        in_specs=[
            pl.BlockSpec((b, d), lambda i, j: (0, 0)),
            pl.BlockSpec((None, d, tn), lambda i, j: (i, 0, j)),
            pl.BlockSpec((None, 1, tn), lambda i, j: (i, 0, j)),
        ],
        out_specs=pl.BlockSpec((None, b, tn), lambda i, j: (i, 0, j)),
        out_shape=jax.ShapeDtypeStruct((depth, b, d3), F32),
        compiler_params=_cparams(("arbitrary", "arbitrary")),
        name="adaln_mod",
    )(c, ada_w, ada_b.reshape(depth, 1, d3))


def _rope_table_kernel(pos_ref, inv_ref, cos_ref, sin_ref):
    ang = pos_ref[...].astype(F32) * inv_ref[...]
    lane = lax.broadcasted_iota(I32, ang.shape, 1)
    first_half = (lane % HEAD_DIM) < HALF
    cos_ref[...] = jnp.cos(ang)
    s = jnp.sin(ang)
    sin_ref[...] = jnp.where(first_half, -s, s)


def rope_tables(positions):
    b, s = positions.shape
    inv_freq = ROPE_THETA ** (-jnp.arange(HALF, dtype=F32) / HALF)
    inv_lane = jnp.tile(inv_freq, LANES // HALF).reshape(1, LANES)
    ts = min(s, 1024)
    out = jax.ShapeDtypeStruct((b, s, LANES), F32)
    return pl.pallas_call(
        _rope_table_kernel,
        grid=(b, s // ts),
        in_specs=[
            pl.BlockSpec((None, ts, 1), lambda i, j: (i, j, 0)),
            pl.BlockSpec((1, LANES), lambda i, j: (0, 0)),
        ],
        out_specs=[pl.BlockSpec((None, ts, LANES), lambda i, j: (i, j, 0))] * 2,
        out_shape=[out, out],
        compiler_params=_cparams(("arbitrary", "arbitrary")),
        name="rope_tables",
    )(positions.reshape(b, s, 1), inv_lane)


def _rope_block(t, cos, sin_signed, first_half):
    partner = jnp.where(first_half, pltpu.roll(t, LANES - HALF, 1), pltpu.roll(t, HALF, 1))
    return t * cos + partner * sin_signed


def _in_proj_kernel(x_ref, g_ref, sc_ref, sh_ref, w_ref, cos_ref, sin_ref, *rest,
                    dilation, rope_tiles, tn, plain_cols):
    if plain_cols:
        w2_ref, o_ref, o2_ref, u_ref, *stage = rest
    else:
        o_ref, u_ref, *stage = rest
    tm = u_ref.shape[0]
    nblk = tn // LANES
    per = ROW_CHUNK // dilation
    x = x_ref[...]
    ms = jnp.mean(x * x, axis=-1, keepdims=True)
    xn = x * lax.rsqrt(ms + NORM_EPS)
    u_ref[...] = (xn * g_ref[...] * (1.0 + sc_ref[...]) + sh_ref[...]).astype(BF16)
    lane = lax.broadcasted_iota(I32, (per, LANES), 1)
    first_half = (lane % HEAD_DIM) < HALF
    step = 0
    for j, rope in enumerate(rope_tiles):
        for c in range(tm // ROW_CHUNK):
            acc = jnp.dot(u_ref[c * ROW_CHUNK:(c + 1) * ROW_CHUNK, :], w_ref[:, j * tn:(j + 1) * tn],
                          preferred_element_type=F32)
            if dilation > 1:
                slot = step % stage[0].shape[0]
                step += 1
                for k in range(nblk):
                    stage[0][slot, k] = acc[:, k * LANES:(k + 1) * LANES]
            for r in range(dilation):
                rows = slice(c * per, (c + 1) * per)
                if rope:
                    cos = cos_ref[rows, :] if dilation == 1 else cos_ref[r, rows, :]
                    sin = sin_ref[rows, :] if dilation == 1 else sin_ref[r, rows, :]
                for k in range(nblk):
                    cols = slice(j * tn + k * LANES, j * tn + (k + 1) * LANES)
                    if dilation > 1:
                        blk = stage[0][slot, k, pl.ds(r, per, stride=dilation), :]
                    else:
                        blk = acc[:, k * LANES:(k + 1) * LANES]
                    if rope:
                        blk = _rope_block(blk, cos, sin, first_half)
                    if dilation > 1:
                        o_ref[r, rows, cols] = blk.astype(o_ref.dtype)
                    else:
                        o_ref[rows, cols] = blk.astype(o_ref.dtype)
    for j in range(plain_cols // tn):
        for c in range(tm // ROW_CHUNK):
            rows = slice(c * ROW_CHUNK, (c + 1) * ROW_CHUNK)
            acc = jnp.dot(u_ref[rows, :], w2_ref[:, j * tn:(j + 1) * tn], preferred_element_type=F32)
            o2_ref[rows, j * tn:(j + 1) * tn] = acc.astype(o2_ref.dtype)


def in_proj(h, g, scale, shift, w, rope_tiles, cos, sin, tn, dilation=1, w_block=0, plain=None):
    b, s, d = h.shape
    n = len(rope_tiles) * tn
    tm = min(s, 512)
    assert tm % ROW_CHUNK == 0 and ROW_CHUNK % (16 * dilation) == 0 and (w_block + 1) * n <= w.shape[1]
    scratch = [pltpu.VMEM((tm, d), BF16)]
    if dilation == 1:
        tab_spec = pl.BlockSpec((None, tm, LANES), lambda bi, i: (bi, i, 0))
        out_spec = pl.BlockSpec((None, tm, n), lambda bi, i: (bi, i, 0))
        out_shape = jax.ShapeDtypeStruct((b, s, n), BF16)
    else:
        tab_spec = pl.BlockSpec((None, dilation, tm // dilation, LANES), lambda bi, i: (bi, 0, i, 0))
        out_spec = pl.BlockSpec((None, dilation, tm // dilation, n), lambda bi, i: (bi, 0, i, 0))
        out_shape = jax.ShapeDtypeStruct((b, dilation, s // dilation, n), BF16)
        scratch.append(pltpu.VMEM((2, tn // LANES, ROW_CHUNK, LANES), F32))
    in_specs = [
        pl.BlockSpec((None, tm, d), lambda bi, i: (bi, i, 0)),
        pl.BlockSpec((1, d), lambda bi, i: (0, 0)),
        pl.BlockSpec((None, 1, d), lambda bi, i: (bi, 0, 0)),
        pl.BlockSpec((None, 1, d), lambda bi, i: (bi, 0, 0)),
        pl.BlockSpec((d, n), lambda bi, i: (0, w_block)),
        tab_spec, tab_spec,
    ]
    args = [h, g.reshape(1, d), scale.reshape(b, 1, d), shift.reshape(b, 1, d), w, cos, sin]
    plain_cols = 0
    if plain is not None:
        plain_cols, plain_block = plain
        assert plain_cols % tn == 0 and (plain_block + 1) * plain_cols <= w.shape[1]
        in_specs.append(pl.BlockSpec((d, plain_cols), lambda bi, i: (0, plain_block)))
        args.append(w)
        out_spec = [out_spec, pl.BlockSpec((None, tm, plain_cols), lambda bi, i: (bi, i, 0))]
        out_shape = [out_shape, jax.ShapeDtypeStruct((b, s, plain_cols), BF16)]
    return pl.pallas_call(
        functools.partial(_in_proj_kernel, dilation=dilation, rope_tiles=tuple(rope_tiles), tn=tn,
                          plain_cols=plain_cols),
        grid=(b, s // tm),
        in_specs=in_specs,
        out_specs=out_spec,
        out_shape=out_shape,
        scratch_shapes=scratch,
        compiler_params=_cparams(("arbitrary", "arbitrary")),
        name=f"in_proj_d{dilation}",
    )(*args)


def _out_proj_kernel(y_ref, w_ref, gate_ref, h_ref, *rest):
    acc = jnp.dot(y_ref[...], w_ref[...], preferred_element_type=F32)
    h_new = h_ref[...] + gate_ref[...] * acc
    if len(rest) == 2:
        g_ref, o_ref = rest
        ms = jnp.mean(h_new * h_new, axis=-1, keepdims=True)
        o_ref[...] = h_new * lax.rsqrt(ms + NORM_EPS) * g_ref[...]
    else:
        rest[0][...] = h_new


def out_proj(y, w, gate, h, final_g=None):
    b, s, d = h.shape
    k = y.shape[-1]
    tm = min(s, 1024)
    in_specs = [
        pl.BlockSpec((None, tm, k), lambda bi, i: (bi, i, 0)),
        pl.BlockSpec((k, d), lambda bi, i: (0, 0)),
        pl.BlockSpec((None, 1, d), lambda bi, i: (bi, 0, 0)),
        pl.BlockSpec((None, tm, d), lambda bi, i: (bi, i, 0)),
    ]
    args = [y, w, gate.reshape(b, 1, d), h]
    if final_g is not None:
        in_specs.append(pl.BlockSpec((1, d), lambda bi, i: (0, 0)))
        args.append(final_g.reshape(1, d))
    return pl.pallas_call(
        _out_proj_kernel,
        grid=(b, s // tm),
        in_specs=in_specs,
        out_specs=pl.BlockSpec((None, tm, d), lambda bi, i: (bi, i, 0)),
        out_shape=jax.ShapeDtypeStruct((b, s, d), F32),
        compiler_params=_cparams(("arbitrary", "arbitrary")),
        name="out_proj",
    )(*args)


def _final_norm_kernel(x_ref, g_ref, o_ref):
    x = x_ref[...]
    ms = jnp.mean(x * x, axis=-1, keepdims=True)
    o_ref[...] = x * lax.rsqrt(ms + NORM_EPS) * g_ref[...]


def final_norm(h, g):
    b, s, d = h.shape
    tm = min(s, 1024)
    return pl.pallas_call(
        _final_norm_kernel,
        grid=(b, s // tm),
        in_specs=[
            pl.BlockSpec((None, tm, d), lambda bi, i: (bi, i, 0)),
            pl.BlockSpec((1, d), lambda bi, i: (0, 0)),
        ],
        out_specs=pl.BlockSpec((None, tm, d), lambda bi, i: (bi, i, 0)),
        out_shape=jax.ShapeDtypeStruct((b, s, d), F32),
        compiler_params=_cparams(("arbitrary", "arbitrary")),
        name="final_norm",
    )(h, g.reshape(1, d))


def _stack_heads(pair, first_head):
    zero = jnp.zeros_like(pair)
    return jnp.concatenate([jnp.where(first_head, pair, zero), jnp.where(first_head, zero, pair)], axis=0)


def _banded_kernel(q_ref, kp_ref, kc_ref, vp_ref, vc_ref, o_ref, lse_ref, kwin_ref, vwin_ref, *, tq):
    i = pl.program_id(2)
    nsub = tq // A_BLK
    lane = lax.broadcasted_iota(I32, (A_BLK, LANES), 1)
    first_head = lane < HEAD_DIM
    odd_lane = (lane % 2) == 1
    r2 = lax.broadcasted_iota(I32, (2 * A_BLK, 2 * A_BLK), 0) % A_BLK
    c2 = lax.broadcasted_iota(I32, (2 * A_BLK, 2 * A_BLK), 1)
    own_ok = (c2 >= A_BLK) & (c2 - A_BLK <= r2)
    ones_blk = jnp.ones((2 * A_BLK, LANES), BF16)
    n_pairs = A_HEADS // 2

    kwin_ref[:A_BLK, :] = kp_ref[...]
    kwin_ref[A_BLK:, :] = kc_ref[...]
    vwin_ref[:A_BLK, :] = vp_ref[...]
    vwin_ref[A_BLK:, :] = vc_ref[...]

    def logits(a, p):
        cols = slice(p * LANES, (p + 1) * LANES)
        qs = _stack_heads(q_ref[a * A_BLK:(a + 1) * A_BLK, cols], first_head)
        k2 = kwin_ref[a * A_BLK:(a + 2) * A_BLK, cols]
        return lax.dot_general(qs, k2, (((1,), (1,)), ((), ())), preferred_element_type=F32)

    units = [(a, p) for a in range(nsub) for p in range(n_pairs)]
    s_next = logits(*units[0])
    lse_tile = bias = None
    for u, (a, p) in enumerate(units):
        rows = slice(a * A_BLK, (a + 1) * A_BLK)
        cols = slice(p * LANES, (p + 1) * LANES)
        if p == 0:
            prev_shift = jnp.where(i * nsub + a > 0, 0, 2 * A_BLK)
            mask = own_ok | ((c2 < A_BLK) & (c2 >= r2 + prev_shift))
            bias = jnp.where(mask, 0.0, -jnp.inf).astype(F32)
            lse_tile = jnp.zeros((A_BLK, LANES), F32)
        s = s_next + bias
        if u + 1 < len(units):
            s_next = logits(*units[u + 1])
        m = jnp.max(s, axis=-1, keepdims=True)
        e = jnp.exp2(s - m).astype(BF16)
        v_ext = jnp.concatenate([vwin_ref[a * A_BLK:(a + 2) * A_BLK, cols], ones_blk], axis=1)
        pv = jnp.dot(e, v_ext, preferred_element_type=F32)
        num = jnp.where(first_head, pv[:A_BLK, :LANES], pv[A_BLK:, :LANES])
        o_ref[rows, cols] = num * (1.0 / jnp.where(first_head, pv[:A_BLK, LANES:], pv[A_BLK:, LANES:]))
        den_pair = jnp.where(odd_lane, pv[A_BLK:, LANES:], pv[:A_BLK, LANES:])
        m_pair = jnp.where(odd_lane, m[A_BLK:], m[:A_BLK])
        lse_tile = jnp.where((lane // 2) == p, m_pair + jnp.log(den_pair) * LOG2_E, lse_tile)
        if p == n_pairs - 1:
            lse_ref[rows, :] = lse_tile


def banded_group_attention(qkv):
    b, d, n, _ = qkv.shape
    tq = min(2 * A_BLK, n)
    sub = tq // A_BLK

    def cur(blk):
        return pl.BlockSpec((None, None, tq, A_WIDTH), lambda bi, r, i: (bi, r, i, blk))

    def prev(blk):
        return pl.BlockSpec((None, None, A_BLK, A_WIDTH),
                            lambda bi, r, i: (bi, r, jnp.maximum(i * sub - 1, 0), blk))

    return pl.pallas_call(
        functools.partial(_banded_kernel, tq=tq),
        grid=(b, d, n // tq),
        in_specs=[cur(0), prev(1), cur(1), prev(2), cur(2)],
        out_specs=[
            pl.BlockSpec((None, None, tq, A_WIDTH), lambda bi, r, i: (bi, r, i, 0)),
            pl.BlockSpec((None, None, tq, LANES), lambda bi, r, i: (bi, r, i, 0)),
        ],
        out_shape=[
            jax.ShapeDtypeStruct((b, d, n, A_WIDTH), F32),
            jax.ShapeDtypeStruct((b, d, n, LANES), F32),
        ],
        scratch_shapes=[pltpu.VMEM((tq + A_BLK, A_WIDTH), BF16)] * 2,
        compiler_params=_cparams(("arbitrary", "arbitrary", "arbitrary")),
        name=f"banded_attn_d{d}",
    )(qkv, qkv, qkv, qkv, qkv)


def _expand_heads(x, expand):
    hi = x.astype(BF16)
    lo = (x - hi.astype(F32)).astype(BF16)
    return (jnp.dot(hi, expand, preferred_element_type=F32)
            + jnp.dot(lo, expand, preferred_element_type=F32))


def _merge_out_kernel(o0_ref, o1_ref, o2_ref, l0_ref, l1_ref, l2_ref, gate_ref, ex_ref, w_ref,
                      mod_ref, h_ref, hout_ref, y_ref, ot_ref, lt_ref, *, dilations):
    o_refs, l_refs = (o0_ref, o1_ref, o2_ref), (l0_ref, l1_ref, l2_ref)
    tm = y_ref.shape[0]
    nblk = y_ref.shape[1] // LANES
    for g, d in enumerate(dilations):
        if d == 1:
            continue
        per = tm // d
        for r in range(d):
            lt_ref[g, pl.ds(r, per, stride=d), :] = l_refs[g][r]
            for k in range(nblk):
                ot_ref[g, k, pl.ds(r, per, stride=d), :] = o_refs[g][r, :, k * LANES:(k + 1) * LANES]
    lses = [l_refs[g][0] if d == 1 else lt_ref[g] for g, d in enumerate(dilations)]
    m = jnp.maximum(jnp.maximum(lses[0], lses[1]), lses[2])
    es = [jnp.exp2(l - m) for l in lses]
    inv = 1.0 / (es[0] + es[1] + es[2])
    ex = ex_ref[...]
    alphas = [_expand_heads(e * inv, ex) for e in es[:-1]]
    alphas.append(1.0 - alphas[0] - alphas[1])
    for k in range(nblk):
        cols = slice(k * LANES, (k + 1) * LANES)
        y = jnp.zeros((tm, LANES), F32)
        for g, d in enumerate(dilations):
            og = o_refs[g][0, :, cols] if d == 1 else ot_ref[g, k]
            y += alphas[g][:, cols] * og
        gate = gate_ref[:, cols].astype(F32)
        y_ref[:, cols] = (y * gate * (1.0 / (1.0 + jnp.exp(-gate)))).astype(y_ref.dtype)
    acc = jnp.dot(y_ref[...], w_ref[...], preferred_element_type=F32)
    hout_ref[...] = h_ref[...] + mod_ref[...] * acc


def merge_out_proj(outs, lses, gate, w_out, mod_gate, h):
    b, s, w = gate.shape
    d_model = h.shape[-1]
    tm = min(s, 256)
    dil = tuple(o.shape[1] for o in outs)
    expand = (jnp.arange(LANES)[:, None] == (jnp.arange(w)[None, :] // HEAD_DIM)).astype(BF16)

    def cm(d, width):
        return pl.BlockSpec((None, d, tm // d, width), lambda bi, i: (bi, 0, i, 0))

    def tspec(width):
        return pl.BlockSpec((None, tm, width), lambda bi, i: (bi, i, 0))

    return pl.pallas_call(
        functools.partial(_merge_out_kernel, dilations=dil),
        grid=(b, s // tm),
        in_specs=[cm(d, w) for d in dil] + [cm(d, LANES) for d in dil]
                 + [tspec(w), pl.BlockSpec((LANES, w), lambda bi, i: (0, 0)),
                    pl.BlockSpec((w, d_model), lambda bi, i: (0, 0)),
                    pl.BlockSpec((None, 1, d_model), lambda bi, i: (bi, 0, 0)),
                    tspec(d_model)],
        out_specs=tspec(d_model),
        out_shape=jax.ShapeDtypeStruct((b, s, d_model), F32),
        scratch_shapes=[pltpu.VMEM((tm, w), BF16),
                        pltpu.VMEM((len(dil), w // LANES, tm, LANES), F32),
                        pltpu.VMEM((len(dil), tm, LANES), F32)],
        compiler_params=_cparams(("arbitrary", "arbitrary")),
        name="merge_out_proj",
    )(*outs, *lses, gate, expand, w_out, mod_gate.reshape(b, 1, d_model), h)


def _class_major(t, d):
    b, s = t.shape[:2]
    return jnp.swapaxes(t.reshape(b, s // d, d, *t.shape[2:]), 1, 2)


def class_major_rope_tables(cos, sin, dilation):
    return _class_major(cos, dilation), _class_major(sin, dilation)


def dilated_layer(h, g, scale, shift, mod_gate, w_in, w_out, tables):
    tn = 512
    per = A_WIDTH // tn
    qkv_flags = (True,) * (2 * per) + (False,) * per
    n_groups = len(A_GROUPS)
    col = np.arange(w_in.shape[1])
    is_q = (col < n_groups * 3 * A_WIDTH) & ((col // A_WIDTH) % 3 == 0)
    w = (w_in * jnp.asarray(np.where(is_q, Q_PRESCALE, 1.0), F32)).astype(BF16)
    outs, lses = [], []
    gate = None
    for gi, (window, dilation) in enumerate(A_GROUPS):
        assert window // dilation == A_BLK
        plain = (A_WIDTH, 3 * n_groups) if gi == 0 else None
        qkv = in_proj(h, g, scale, shift, w, qkv_flags, *tables[dilation], tn, dilation,
                      w_block=gi, plain=plain)
        if plain is not None:
            qkv, gate = qkv
        if dilation == 1:
            qkv = qkv[:, None]
        o, lse = banded_group_attention(qkv)
        outs.append(o)
        lses.append(lse)
    return merge_out_proj(outs, lses, gate, w_out.astype(BF16), mod_gate, h)


B_COLS = 3584
B_ROPE_FLAGS = (1, 1, 0, 0, 1, 1, 0)


def _dsa_head_perm():
    group = B_HEADS // B_KV_HEADS
    order = []
    for g2 in range(B_KV_HEADS // 2):
        for r in range(group):
            order += [(2 * g2) * group + r, (2 * g2 + 1) * group + r]
    return np.asarray(order)


def _dsa_weights(w_in, w_out):
    d = w_in.shape[0]
    cuts = np.cumsum((B_WIDTH, B_KV_HEADS * HEAD_DIM, B_KV_HEADS * HEAD_DIM, IDX_HEADS * HEAD_DIM,
                      HEAD_DIM, IDX_HEADS, B_WIDTH))[:-1]
    wq, wk, wv, wqi, wki, wwi, wg = jnp.split(w_in, cuts, axis=1)
    cols = (_dsa_head_perm()[:, None] * HEAD_DIM + np.arange(HEAD_DIM)[None, :]).reshape(-1)
    zeros = lambda n: jnp.zeros((d, n), w_in.dtype)
    wq = wq * Q_PRESCALE
    w = jnp.concatenate([wq[:, cols], wg[:, cols], wqi, wk, wki, wki, zeros(LANES),
                         wwi, zeros(2 * LANES - IDX_HEADS), wv], axis=1)
    assert w.shape[1] == B_COLS
    return w.astype(BF16), w_out[cols, :].astype(BF16)


def _sortable(x):
    bits = pltpu.bitcast(x, I32)
    return bits ^ ((bits >> 31) & jnp.int32(0x7FFFFFFF))


def _bit_transpose32(words):
    a = list(words)
    j, m = 16, 0x0000FFFF
    while j:
        mask = jnp.int32(m - (1 << 32) if m >= (1 << 31) else m)
        shift = jnp.full(a[0].shape, j, I32)
        k = 0
        while k < 32:
            t = (a[k] ^ lax.shift_right_logical(a[k + j], shift)) & mask
            a[k] = a[k] ^ t
            a[k + j] = a[k + j] ^ lax.shift_left(t, shift)
            k = (k + j + 1) & ~j
        j >>= 1
        m = (m ^ (m << j)) & 0xFFFFFFFF
    return a


def _sublane_allsum(x):
    x = x + pltpu.roll(x, 4, 0)
    x = x + pltpu.roll(x, 2, 0)
    return x + pltpu.roll(x, 1, 0)


def _tree_sum(parts):
    parts = list(parts)
    while len(parts) > 1:
        odd = [parts[-1]] if len(parts) % 2 else []
        parts = [parts[j] + parts[j + 1] for j in range(0, len(parts) - 1, 2)] + odd
    return parts[0]


GROUP_KEYS = 256
ATTN_LOOKAHEAD = 1


def _dsa_kernel(q_ref, gate_ref, qi_ref, k_ref, ki_ref, wi_ref, v_ref, y_ref,
                keys_ref, planes_ref, bias_ref, kt_ref, qs_ref, s0_ref, m_ref, l_ref, acc_ref,
                *, k_sel):
    i = pl.program_id(1)
    seq = keys_ref.shape[1]
    n_sub = Q_BLOCK // SEL_LANES
    n_pairs = B_HEADS // 2
    n_groups_max = seq // GROUP_KEYS
    lane_q = lax.broadcasted_iota(I32, (Q_BLOCK, LANES), 1)
    first_head_q = lane_q < HEAD_DIM
    first_head = lax.broadcasted_iota(I32, (SEL_LANES, LANES), 1) < HEAD_DIM
    key_row = lax.broadcasted_iota(I32, (KEY_TILE, SEL_LANES), 0)
    lane_idx = lax.broadcasted_iota(I32, (KEY_TILE, SEL_LANES), 1)

    @pl.when(i == 0)
    def _():
        planes_ref[...] = jnp.zeros(planes_ref.shape, I32)
        keys_ref[...] = jnp.zeros(keys_ref.shape, I32)

    def tiles_of(a):
        return ((i * n_sub + a) * SEL_LANES) // KEY_TILE + 1

    def select(a):
        n_tiles = tiles_of(a)
        rows_a = slice(a * SEL_LANES, (a + 1) * SEL_LANES)
        q_pos = (i * n_sub + a) * SEL_LANES + lane_idx
        keys = keys_ref.at[a]

        w_t = (wi_ref[rows_a, :].astype(F32) * (IDX_HEADS ** -0.5 * HEAD_DIM ** -0.5)).T
        qi_stacked = [_stack_heads(qi_ref[rows_a, p * LANES:(p + 1) * LANES], first_head)
                      for p in range(IDX_HEADS // 2)]

        def score_tile(t):
            base = pl.multiple_of(t * KEY_TILE, KEY_TILE)
            kk = ki_ref[pl.ds(base, KEY_TILE), :]
            score = jnp.zeros((KEY_TILE, SEL_LANES), F32)
            for p in range(IDX_HEADS // 2):
                sc = lax.dot_general(kk, qi_stacked[p], (((1,), (1,)), ((), ())),
                                     preferred_element_type=F32)
                score += jnp.maximum(sc[:, :SEL_LANES], 0.0) * w_t[2 * p:2 * p + 1, :]
                score += jnp.maximum(sc[:, SEL_LANES:], 0.0) * w_t[2 * p + 1:2 * p + 2, :]
            causal = (key_row + base) <= q_pos
            score = jnp.where(causal, score + 0.0, -jnp.inf)
            key = _sortable(score)
            keys[pl.ds(base, KEY_TILE), :] = key
            ukey = key ^ jnp.int32(INT_MIN)
            for g in range(KEY_TILE // GROUP_KEYS):
                words = [ukey[g * GROUP_KEYS + 8 * j:g * GROUP_KEYS + 8 * j + 8, :] for j in range(32)]
                rows = pl.ds(pl.multiple_of(t * (KEY_TILE // 32) + 8 * g, 8), 8)
                for b, plane in enumerate(_bit_transpose32(words)):
                    planes_ref[b, rows, :] = plane

        def score_two(t2, carry):
            score_tile(2 * t2)
            score_tile(2 * t2 + 1)
            return carry

        lax.fori_loop(0, n_tiles // 2, score_two, 0)

        @pl.when(n_tiles % 2 == 1)
        def _():
            score_tile(n_tiles - 1)

        def select_threshold():
            n_groups = n_tiles * (KEY_TILE // GROUP_KEYS)
            alive0 = tuple(jnp.full((8, SEL_LANES), -1, I32) * (g < n_groups).astype(I32)
                           for g in range(n_groups_max))
            need0 = jnp.full((8, SEL_LANES), k_sel, I32)

            def step(st, carry):
                alive, need, prefix = carry
                planes = [planes_ref[st, 8 * g:8 * g + 8, :] for g in range(n_groups_max)]
                ones = [al & pln for al, pln in zip(alive, planes)]
                cnt = _sublane_allsum(_tree_sum([lax.population_count(o) for o in ones]))
                take = cnt >= need
                prefix = jnp.where(take, prefix | jnp.left_shift(jnp.int32(1), 31 - st), prefix)
                need = jnp.where(take, need, need - cnt)
                flip = jnp.where(take, 0, -1)
                alive = tuple(al & (pln ^ flip) for al, pln in zip(alive, planes))
                return alive, need, prefix

            alive, need, prefix = lax.fori_loop(
                0, 32, step, (alive0, need0, jnp.zeros((8, SEL_LANES), I32)))
            thr = (prefix ^ jnp.int32(INT_MIN))[:1, :]
            need_eq = need[:1, :]

            n_eq = _sublane_allsum(_tree_sum([lax.population_count(al) for al in alive]))[:1, :]
            tied = (n_eq > need_eq) & (thr > KEY_NEG_INF)
            idx_bits = (seq - 1).bit_length()

            def tie_break():
                def count_eq_before(cut):
                    def body(t, acc):
                        base = pl.multiple_of(t * KEY_TILE, KEY_TILE)
                        hit = (keys[pl.ds(base, KEY_TILE), :] == thr) & ((key_row + base) < cut)
                        return acc + jnp.sum(hit.astype(I32).reshape(KEY_TILE // 8, 8, SEL_LANES), axis=0)
                    acc = lax.fori_loop(0, n_tiles, body, jnp.zeros((8, SEL_LANES), I32))
                    return jnp.sum(acc, axis=0, keepdims=True)

                def bit_step(it, cut):
                    cand = cut + jnp.left_shift(jnp.int32(1), idx_bits - 1 - it)
                    return jnp.where(count_eq_before(cand) < need_eq, cand, cut)

                return lax.fori_loop(0, idx_bits, bit_step, jnp.zeros((1, SEL_LANES), I32))

            any_tied = jnp.max(tied.astype(I32)) > 0
            idx_cut = lax.cond(any_tied, tie_break, lambda: jnp.full((1, SEL_LANES), seq, I32))
            return thr, idx_cut

        all_selected = (i * n_sub + a + 1) * SEL_LANES <= k_sel
        thr, idx_cut = lax.cond(
            all_selected,
            lambda: (jnp.full((1, SEL_LANES), INT_MIN, I32), jnp.full((1, SEL_LANES), seq, I32)),
            select_threshold)
        return thr, idx_cut, q_pos

    selected = [select(a) for a in range(n_sub)]
    n_tiles = tiles_of(n_sub - 1)

    def write_bias(base):
        k_idx = key_row + base
        for a, (thr, idx_cut, q_pos) in enumerate(selected):
            key = keys_ref[a, pl.ds(base, KEY_TILE), :]
            sel = ((key > thr) | ((key == thr) & (k_idx <= idx_cut))) & (k_idx <= q_pos)
            bias_t = jnp.where(sel, 0.0, -jnp.inf).astype(F32)
            for c in range(KEY_TILE // LANES):
                col = base + c * LANES
                col = col if isinstance(col, int) else pl.multiple_of(col, LANES)
                bias_ref[a * SEL_LANES:(a + 1) * SEL_LANES, pl.ds(col, LANES)] = \
                    bias_t[c * LANES:(c + 1) * LANES, :].T

    write_bias(0)

    for p in range(n_pairs):
        cols = slice(p * LANES, (p + 1) * LANES)
        qs_ref[p] = _stack_heads(q_ref[:, cols], first_head_q)
    m_ref[...] = jnp.full(m_ref.shape, F32_MIN, F32)
    l_ref[...] = jnp.zeros(l_ref.shape, F32)
    acc_ref[...] = jnp.zeros(acc_ref.shape, F32)
    heads_per_kv_pair = 2 * (B_HEADS // B_KV_HEADS) // 2

    ones_blk = jnp.ones((KEY_TILE, LANES), BF16)

    @pl.when(i == 0)
    def _():
        def xpose(r, carry):
            rows = pl.ds(pl.multiple_of(r * LANES, LANES), LANES)
            for c in range(kt_ref.shape[0]):
                blk = k_ref[rows, c * LANES:(c + 1) * LANES].astype(F32)
                kt_ref[c, :, rows] = blk.T.astype(BF16)
            return carry
        lax.fori_loop(0, seq // LANES, xpose, 0)

    def logits(base, p):
        kt = kt_ref[p // heads_per_kv_pair, :, pl.ds(base, KEY_TILE)]
        return jnp.dot(qs_ref[p], kt, preferred_element_type=F32)

    lookahead = s0_ref.shape[0]
    for u in range(lookahead):
        s0_ref[u] = logits(0, u)

    def attn_tiles(t0, count):
        bases = [pl.multiple_of(jnp.minimum(t0 + dt, n_tiles - 1) * KEY_TILE, KEY_TILE)
                 for dt in range(count + 1)]
        units = [(dt, p) for dt in range(count + 1) for p in range(n_pairs)]
        queue = [s0_ref[u] for u in range(lookahead)]
        for u, (dt, p) in enumerate(units[:count * n_pairs]):
            base = bases[dt]
            if p == 0:
                bias = bias_ref[:, pl.ds(base, KEY_TILE)]
                write_bias(bases[dt + 1])
                bias2 = jnp.concatenate([bias, bias], axis=0)
            kv = p // heads_per_kv_pair
            kv_cols = slice(kv * LANES, (kv + 1) * LANES)
            s = queue.pop(0) + bias2
            ahead_dt, ahead_p = units[u + lookahead]
            queue.append(logits(bases[ahead_dt], ahead_p))
            m_blk = s[:, :LANES]
            for c in range(1, KEY_TILE // LANES):
                m_blk = jnp.maximum(m_blk, s[:, c * LANES:(c + 1) * LANES])
            m_old = m_ref[p]
            m_new = jnp.maximum(m_old, jnp.max(m_blk, axis=-1, keepdims=True))
            alpha = jnp.exp2(m_old - m_new)
            e = jnp.concatenate(
                [jnp.exp2(s[:, c * LANES:(c + 1) * LANES] - m_new).astype(BF16)
                 for c in range(KEY_TILE // LANES)], axis=1)
            v_ext = jnp.concatenate([v_ref[pl.ds(base, KEY_TILE), kv_cols], ones_blk], axis=1)
            pv = jnp.dot(e, v_ext, preferred_element_type=F32)
            acc_ref[p] = acc_ref[p] * alpha + pv[:, :LANES]
            l_ref[p] = l_ref[p] * alpha + pv[:, LANES:]
            m_ref[p] = m_new
        for u in range(lookahead):
            s0_ref[u] = queue[u]

    def attn_two(t2, carry):
        attn_tiles(2 * t2, 2)
        return carry

    lax.fori_loop(0, n_tiles // 2, attn_two, 0)

    @pl.when(n_tiles % 2 == 1)
    def _():
        attn_tiles(n_tiles - 1, 1)

    for p in range(n_pairs):
        cols = slice(p * LANES, (p + 1) * LANES)
        l = l_ref[p]
        acc = acc_ref[p]
        num = jnp.where(first_head_q, acc[:Q_BLOCK], acc[Q_BLOCK:])
        den = jnp.where(first_head_q, l[:Q_BLOCK], l[Q_BLOCK:])
        gate = gate_ref[:, cols].astype(F32)
        y_ref[:, cols] = (num * gate / (den * (1.0 + jnp.exp(-gate)))).astype(y_ref.dtype)


def dsa_attention(proj):
    b, s, _ = proj.shape
    k_sel = min(TOPK_MAX, s // 4)
    assert s % KEY_TILE == 0 and s % Q_BLOCK == 0 and k_sel <= KEY_TILE
    kvw = B_KV_HEADS * HEAD_DIM
    n_pairs = B_HEADS // 2

    def qblock(width, idx):
        return pl.BlockSpec((None, Q_BLOCK, width), lambda bi, i: (bi, i, idx))

    def full(width, idx):
        return pl.BlockSpec((None, s, width), lambda bi, i: (bi, 0, idx))

    return pl.pallas_call(
        functools.partial(_dsa_kernel, k_sel=k_sel),
        grid=(b, s // Q_BLOCK),
        in_specs=[
            qblock(B_WIDTH, 0),
            qblock(B_WIDTH, 1),
            qblock(IDX_HEADS * HEAD_DIM, 4),
            full(kvw, 10),
            full(LANES, 22),
            qblock(LANES, 24),
            full(kvw, 13),
        ],
        out_specs=qblock(B_WIDTH, 0),
        out_shape=jax.ShapeDtypeStruct((b, s, B_WIDTH), BF16),
        scratch_shapes=[
            pltpu.VMEM((Q_BLOCK // SEL_LANES, s, SEL_LANES), I32),
            pltpu.VMEM((32, s // 32, SEL_LANES), I32),
            pltpu.VMEM((Q_BLOCK, s), F32),
            pltpu.VMEM((kvw // LANES, LANES, s), BF16),
            pltpu.VMEM((n_pairs, 2 * Q_BLOCK, LANES), BF16),
            pltpu.VMEM((ATTN_LOOKAHEAD, 2 * Q_BLOCK, KEY_TILE), F32),
            pltpu.VMEM((n_pairs, 2 * Q_BLOCK, LANES), F32),
            pltpu.VMEM((n_pairs, 2 * Q_BLOCK, LANES), F32),
            pltpu.VMEM((n_pairs, 2 * Q_BLOCK, LANES), F32),
        ],
        compiler_params=_cparams(("arbitrary", "arbitrary")),
        name="dsa_attention",
    )(proj, proj, proj, proj, proj, proj, proj)


def dsa_mixer(h, g, scale, shift, w_in_packed, cos, sin):
    proj = in_proj(h, g, scale, shift, w_in_packed, [f == 1 for f in B_ROPE_FLAGS], cos, sin, 512)
    return dsa_attention(proj)


def kernel(x, c, positions, norm_g, ada_w, ada_b, a_w_in, a_w_out, b_w_in, b_w_out, final_g):
    depth = norm_g.shape[0]
    d = x.shape[-1]
    mod = adaln_mod(c, ada_w, ada_b)
    cos, sin = rope_tables(positions)
    tables = {1: (cos, sin)}
    for _, dilation in A_GROUPS:
        if dilation > 1:
            tables[dilation] = class_major_rope_tables(cos, sin, dilation)
    h = x
    for i in range(depth):
        shift, scale, gate = mod[i, :, :d], mod[i, :, d:2 * d], mod[i, :, 2 * d:]
        last = i == depth - 1
        if i % 2 == 0:
            h = dilated_layer(h, norm_g[i], scale, shift, gate, a_w_in[i // 2], a_w_out[i // 2], tables)
            if last:
                h = final_norm(h, final_g)
        else:
            w_in, w_out = _dsa_weights(b_w_in[i // 2], b_w_out[i // 2])
            y = dsa_mixer(h, norm_g[i], scale, shift, w_in, cos, sin)
            h = out_proj(y, w_out, gate, h, final_g if last else None)
    return h
```

```python
import functools

import numpy as np
import jax
import jax.numpy as jnp
from jax import lax
from jax.experimental import pallas as pl
from jax.experimental.pallas import tpu as pltpu

F32 = jnp.float32
BF16 = jnp.bfloat16
I32 = jnp.int32

LANES = 128
HEAD_DIM = 64
HALF = HEAD_DIM // 2
ROPE_THETA = 10000.0
NORM_EPS = 1e-6
VMEM_LIMIT = 56 * 1024 * 1024
ROW_CHUNK = 256

A_HEADS = 16
A_WIDTH = A_HEADS * HEAD_DIM
A_GROUPS = ((128, 1), (512, 4), (2048, 16))
A_BLK = 128

B_HEADS = 16
B_KV_HEADS = 4
B_WIDTH = B_HEADS * HEAD_DIM
IDX_HEADS = 8
TOPK_MAX = 256
Q_BLOCK = 256
SEL_LANES = 128
KEY_TILE = 512
INT_MIN = -(2 ** 31)
KEY_NEG_INF = -2139095041
F32_MIN = float(np.finfo(np.float32).min)
LOG2_E = float(np.log2(np.e))
Q_PRESCALE = LOG2_E * HEAD_DIM ** -0.5


def _cparams(sem):
    return pltpu.CompilerParams(dimension_semantics=sem, vmem_limit_bytes=VMEM_LIMIT)


def _adaln_kernel(c_ref, w_ref, b_ref, o_ref):
    c = c_ref[...]
    ca = (c * (1.0 / (1.0 + jnp.exp(-c)))).astype(BF16)
    acc = jnp.dot(ca, w_ref[...].astype(BF16), preferred_element_type=F32)
    o_ref[...] = acc + b_ref[...]


def adaln_mod(c, ada_w, ada_b):
    depth, d, d3 = ada_w.shape
    b = c.shape[0]
    tn = 1024
    return pl.pallas_call(
        _adaln_kernel,
        grid=(depth, d3 // tn),
        in_specs=[
            pl.BlockSpec((b, d), lambda i, j: (0, 0)),
            pl.BlockSpec((None, d, tn), lambda i, j: (i, 0, j)),
            pl.BlockSpec((None, 1, tn), lambda i, j: (i, 0, j)),
        ],
        out_specs=pl.BlockSpec((None, b, tn), lambda i, j: (i, 0, j)),
        out_shape=jax.ShapeDtypeStruct((depth, b, d3), F32),
        compiler_params=_cparams(("arbitrary", "arbitrary")),
        name="adaln_mod",
    )(c, ada_w, ada_b.reshape(depth, 1, d3))


def _rope_table_kernel(pos_ref, inv_ref, cos_ref, sin_ref):
    ang = pos_ref[...].astype(F32) * inv_ref[...]
    lane = lax.broadcasted_iota(I32, ang.shape, 1)
    first_half = (lane % HEAD_DIM) < HALF
    cos_ref[...] = jnp.cos(ang)
    s = jnp.sin(ang)
    sin_ref[...] = jnp.where(first_half, -s, s)


def rope_tables(positions):
    b, s = positions.shape
    inv_freq = ROPE_THETA ** (-jnp.arange(HALF, dtype=F32) / HALF)
    inv_lane = jnp.tile(inv_freq, LANES // HALF).reshape(1, LANES)
    ts = min(s, 1024)
    out = jax.ShapeDtypeStruct((b, s, LANES), F32)
    return pl.pallas_call(
        _rope_table_kernel,
        grid=(b, s // ts),
        in_specs=[
            pl.BlockSpec((None, ts, 1), lambda i, j: (i, j, 0)),
            pl.BlockSpec((1, LANES), lambda i, j: (0, 0)),
        ],
        out_specs=[pl.BlockSpec((None, ts, LANES), lambda i, j: (i, j, 0))] * 2,
        out_shape=[out, out],
        compiler_params=_cparams(("arbitrary", "arbitrary")),
        name="rope_tables",
    )(positions.reshape(b, s, 1), inv_lane)


def _rope_block(t, cos, sin_signed, first_half):
    partner = jnp.where(first_half, pltpu.roll(t, LANES - HALF, 1), pltpu.roll(t, HALF, 1))
    return t * cos + partner * sin_signed


def _in_proj_kernel(x_ref, g_ref, sc_ref, sh_ref, w_ref, cos_ref, sin_ref, *rest,
                    dilation, rope_tiles, tn, plain_cols):
    if plain_cols:
        w2_ref, o_ref, o2_ref, u_ref, *stage = rest
    else:
        o_ref, u_ref, *stage = rest
    tm = u_ref.shape[0]
    nblk = tn // LANES
    per = ROW_CHUNK // dilation
    x = x_ref[...]
    ms = jnp.mean(x * x, axis=-1, keepdims=True)
    xn = x * lax.rsqrt(ms + NORM_EPS)
    u_ref[...] = (xn * g_ref[...] * (1.0 + sc_ref[...]) + sh_ref[...]).astype(BF16)
    lane = lax.broadcasted_iota(I32, (per, LANES), 1)
    first_half = (lane % HEAD_DIM) < HALF
    step = 0
    for j, rope in enumerate(rope_tiles):
        for c in range(tm // ROW_CHUNK):
            acc = jnp.dot(u_ref[c * ROW_CHUNK:(c + 1) * ROW_CHUNK, :], w_ref[:, j * tn:(j + 1) * tn],
                          preferred_element_type=F32)
            if dilation > 1:
                slot = step % stage[0].shape[0]
                step += 1
                for k in range(nblk):
                    stage[0][slot, k] = acc[:, k * LANES:(k + 1) * LANES]
            for r in range(dilation):
                rows = slice(c * per, (c + 1) * per)
                if rope:
                    cos = cos_ref[rows, :] if dilation == 1 else cos_ref[r, rows, :]
                    sin = sin_ref[rows, :] if dilation == 1 else sin_ref[r, rows, :]
                for k in range(nblk):
                    cols = slice(j * tn + k * LANES, j * tn + (k + 1) * LANES)
                    if dilation > 1:
                        blk = stage[0][slot, k, pl.ds(r, per, stride=dilation), :]
                    else:
                        blk = acc[:, k * LANES:(k + 1) * LANES]
                    if rope:
                        blk = _rope_block(blk, cos, sin, first_half)
                    if dilation > 1:
                        o_ref[r, rows, cols] = blk.astype(o_ref.dtype)
                    else:
                        o_ref[rows, cols] = blk.astype(o_ref.dtype)
    for j in range(plain_cols // tn):
        for c in range(tm // ROW_CHUNK):
            rows = slice(c * ROW_CHUNK, (c + 1) * ROW_CHUNK)
            acc = jnp.dot(u_ref[rows, :], w2_ref[:, j * tn:(j + 1) * tn], preferred_element_type=F32)
            o2_ref[rows, j * tn:(j + 1) * tn] = acc.astype(o2_ref.dtype)


def in_proj(h, g, scale, shift, w, rope_tiles, cos, sin, tn, dilation=1, w_block=0, plain=None):
    b, s, d = h.shape
    n = len(rope_tiles) * tn
    tm = min(s, 512)
    assert tm % ROW_CHUNK == 0 and ROW_CHUNK % (16 * dilation) == 0 and (w_block + 1) * n <= w.shape[1]
    scratch = [pltpu.VMEM((tm, d), BF16)]
    if dilation == 1:
        tab_spec = pl.BlockSpec((None, tm, LANES), lambda bi, i: (bi, i, 0))
        out_spec = pl.BlockSpec((None, tm, n), lambda bi, i: (bi, i, 0))
        out_shape = jax.ShapeDtypeStruct((b, s, n), BF16)
    else:
        tab_spec = pl.BlockSpec((None, dilation, tm // dilation, LANES), lambda bi, i: (bi, 0, i, 0))
        out_spec = pl.BlockSpec((None, dilation, tm // dilation, n), lambda bi, i: (bi, 0, i, 0))
        out_shape = jax.ShapeDtypeStruct((b, dilation, s // dilation, n), BF16)
        scratch.append(pltpu.VMEM((2, tn // LANES, ROW_CHUNK, LANES), F32))
    in_specs = [
        pl.BlockSpec((None, tm, d), lambda bi, i: (bi, i, 0)),
        pl.BlockSpec((1, d), lambda bi, i: (0, 0)),
        pl.BlockSpec((None, 1, d), lambda bi, i: (bi, 0, 0)),
        pl.BlockSpec((None, 1, d), lambda bi, i: (bi, 0, 0)),
        pl.BlockSpec((d, n), lambda bi, i: (0, w_block)),
        tab_spec, tab_spec,
    ]
    args = [h, g.reshape(1, d), scale.reshape(b, 1, d), shift.reshape(b, 1, d), w, cos, sin]
    plain_cols = 0
    if plain is not None:
        plain_cols, plain_block = plain
        assert plain_cols % tn == 0 and (plain_block + 1) * plain_cols <= w.shape[1]
        in_specs.append(pl.BlockSpec((d, plain_cols), lambda bi, i: (0, plain_block)))
        args.append(w)
        out_spec = [out_spec, pl.BlockSpec((None, tm, plain_cols), lambda bi, i: (bi, i, 0))]
        out_shape = [out_shape, jax.ShapeDtypeStruct((b, s, plain_cols), BF16)]
    return pl.pallas_call(
        functools.partial(_in_proj_kernel, dilation=dilation, rope_tiles=tuple(rope_tiles), tn=tn,
                          plain_cols=plain_cols),
        grid=(b, s // tm),
        in_specs=in_specs,
        out_specs=out_spec,
        out_shape=out_shape,
        scratch_shapes=scratch,
        compiler_params=_cparams(("arbitrary", "arbitrary")),
        name=f"in_proj_d{dilation}",
    )(*args)


def _out_proj_kernel(y_ref, w_ref, gate_ref, h_ref, *rest):
    acc = jnp.dot(y_ref[...], w_ref[...], preferred_element_type=F32)
    h_new = h_ref[...] + gate_ref[...] * acc
    if len(rest) == 2:
        g_ref, o_ref = rest
        ms = jnp.mean(h_new * h_new, axis=-1, keepdims=True)
        o_ref[...] = h_new * lax.rsqrt(ms + NORM_EPS) * g_ref[...]
    else:
        rest[0][...] = h_new


def out_proj(y, w, gate, h, final_g=None):
    b, s, d = h.shape
    k = y.shape[-1]
    tm = min(s, 1024)
    in_specs = [
        pl.BlockSpec((None, tm, k), lambda bi, i: (bi, i, 0)),
        pl.BlockSpec((k, d), lambda bi, i: (0, 0)),
        pl.BlockSpec((None, 1, d), lambda bi, i: (bi, 0, 0)),
        pl.BlockSpec((None, tm, d), lambda bi, i: (bi, i, 0)),
    ]
    args = [y, w, gate.reshape(b, 1, d), h]
    if final_g is not None:
        in_specs.append(pl.BlockSpec((1, d), lambda bi, i: (0, 0)))
        args.append(final_g.reshape(1, d))
    return pl.pallas_call(
        _out_proj_kernel,
        grid=(b, s // tm),
        in_specs=in_specs,
        out_specs=pl.BlockSpec((None, tm, d), lambda bi, i: (bi, i, 0)),
        out_shape=jax.ShapeDtypeStruct((b, s, d), F32),
        compiler_params=_cparams(("arbitrary", "arbitrary")),
        name="out_proj",
    )(*args)


def _final_norm_kernel(x_ref, g_ref, o_ref):
    x = x_ref[...]
    ms = jnp.mean(x * x, axis=-1, keepdims=True)
    o_ref[...] = x * lax.rsqrt(ms + NORM_EPS) * g_ref[...]


def final_norm(h, g):
    b, s, d = h.shape
    tm = min(s, 1024)
    return pl.pallas_call(
        _final_norm_kernel,
        grid=(b, s // tm),
        in_specs=[
            pl.BlockSpec((None, tm, d), lambda bi, i: (bi, i, 0)),
            pl.BlockSpec((1, d), lambda bi, i: (0, 0)),
        ],
        out_specs=pl.BlockSpec((None, tm, d), lambda bi, i: (bi, i, 0)),
        out_shape=jax.ShapeDtypeStruct((b, s, d), F32),
        compiler_params=_cparams(("arbitrary", "arbitrary")),
        name="final_norm",
    )(h, g.reshape(1, d))


def _stack_heads(pair, first_head):
    zero = jnp.zeros_like(pair)
    return jnp.concatenate([jnp.where(first_head, pair, zero), jnp.where(first_head, zero, pair)], axis=0)


def _banded_kernel(q_ref, kp_ref, kc_ref, vp_ref, vc_ref, o_ref, lse_ref, kwin_ref, vwin_ref, *, tq):
    i = pl.program_id(2)
    nsub = tq // A_BLK
    lane = lax.broadcasted_iota(I32, (A_BLK, LANES), 1)
    first_head = lane < HEAD_DIM
    odd_lane = (lane % 2) == 1
    r2 = lax.broadcasted_iota(I32, (2 * A_BLK, 2 * A_BLK), 0) % A_BLK
    c2 = lax.broadcasted_iota(I32, (2 * A_BLK, 2 * A_BLK), 1)
    own_ok = (c2 >= A_BLK) & (c2 - A_BLK <= r2)
    ones_blk = jnp.ones((2 * A_BLK, LANES), BF16)
    n_pairs = A_HEADS // 2

    kwin_ref[:A_BLK, :] = kp_ref[...]
    kwin_ref[A_BLK:, :] = kc_ref[...]
    vwin_ref[:A_BLK, :] = vp_ref[...]
    vwin_ref[A_BLK:, :] = vc_ref[...]

    def logits(a, p):
        cols = slice(p * LANES, (p + 1) * LANES)
        qs = _stack_heads(q_ref[a * A_BLK:(a + 1) * A_BLK, cols], first_head)
        k2 = kwin_ref[a * A_BLK:(a + 2) * A_BLK, cols]
        return lax.dot_general(qs, k2, (((1,), (1,)), ((), ())), preferred_element_type=F32)

    units = [(a, p) for a in range(nsub) for p in range(n_pairs)]
    s_next = logits(*units[0])
    lse_tile = bias = None
    for u, (a, p) in enumerate(units):
        rows = slice(a * A_BLK, (a + 1) * A_BLK)
        cols = slice(p * LANES, (p + 1) * LANES)
        if p == 0:
            prev_shift = jnp.where(i * nsub + a > 0, 0, 2 * A_BLK)
            mask = own_ok | ((c2 < A_BLK) & (c2 >= r2 + prev_shift))
            bias = jnp.where(mask, 0.0, -jnp.inf).astype(F32)
            lse_tile = jnp.zeros((A_BLK, LANES), F32)
        s = s_next + bias
        if u + 1 < len(units):
            s_next = logits(*units[u + 1])
        m = jnp.max(s, axis=-1, keepdims=True)
        e = jnp.exp2(s - m).astype(BF16)
        v_ext = jnp.concatenate([vwin_ref[a * A_BLK:(a + 2) * A_BLK, cols], ones_blk], axis=1)
        pv = jnp.dot(e, v_ext, preferred_element_type=F32)
        num = jnp.where(first_head, pv[:A_BLK, :LANES], pv[A_BLK:, :LANES])
        o_ref[rows, cols] = num * (1.0 / jnp.where(first_head, pv[:A_BLK, LANES:], pv[A_BLK:, LANES:]))
        den_pair = jnp.where(odd_lane, pv[A_BLK:, LANES:], pv[:A_BLK, LANES:])
        m_pair = jnp.where(odd_lane, m[A_BLK:], m[:A_BLK])
        lse_tile = jnp.where((lane // 2) == p, m_pair + jnp.log(den_pair) * LOG2_E, lse_tile)
        if p == n_pairs - 1:
            lse_ref[rows, :] = lse_tile


def banded_group_attention(qkv):
    b, d, n, _ = qkv.shape
    tq = min(2 * A_BLK, n)
    sub = tq // A_BLK

    def cur(blk):
        return pl.BlockSpec((None, None, tq, A_WIDTH), lambda bi, r, i: (bi, r, i, blk))

    def prev(blk):
        return pl.BlockSpec((None, None, A_BLK, A_WIDTH),
                            lambda bi, r, i: (bi, r, jnp.maximum(i * sub - 1, 0), blk))

    return pl.pallas_call(
        functools.partial(_banded_kernel, tq=tq),
        grid=(b, d, n // tq),
        in_specs=[cur(0), prev(1), cur(1), prev(2), cur(2)],
        out_specs=[
            pl.BlockSpec((None, None, tq, A_WIDTH), lambda bi, r, i: (bi, r, i, 0)),
            pl.BlockSpec((None, None, tq, LANES), lambda bi, r, i: (bi, r, i, 0)),
        ],
        out_shape=[
            jax.ShapeDtypeStruct((b, d, n, A_WIDTH), F32),
            jax.ShapeDtypeStruct((b, d, n, LANES), F32),
        ],
        scratch_shapes=[pltpu.VMEM((tq + A_BLK, A_WIDTH), BF16)] * 2,
        compiler_params=_cparams(("arbitrary", "arbitrary", "arbitrary")),
        name=f"banded_attn_d{d}",
    )(qkv, qkv, qkv, qkv, qkv)


def _expand_heads(x, expand):
    hi = x.astype(BF16)
    lo = (x - hi.astype(F32)).astype(BF16)
    return (jnp.dot(hi, expand, preferred_element_type=F32)
            + jnp.dot(lo, expand, preferred_element_type=F32))


def _merge_out_kernel(o0_ref, o1_ref, o2_ref, l0_ref, l1_ref, l2_ref, gate_ref, ex_ref, w_ref,
                      mod_ref, h_ref, hout_ref, y_ref, ot_ref, lt_ref, *, dilations):
    o_refs, l_refs = (o0_ref, o1_ref, o2_ref), (l0_ref, l1_ref, l2_ref)
    tm = y_ref.shape[0]
    nblk = y_ref.shape[1] // LANES
    for g, d in enumerate(dilations):
        if d == 1:
            continue
        per = tm // d
        for r in range(d):
            lt_ref[g, pl.ds(r, per, stride=d), :] = l_refs[g][r]
            for k in range(nblk):
                ot_ref[g, k, pl.ds(r, per, stride=d), :] = o_refs[g][r, :, k * LANES:(k + 1) * LANES]
    lses = [l_refs[g][0] if d == 1 else lt_ref[g] for g, d in enumerate(dilations)]
    m = jnp.maximum(jnp.maximum(lses[0], lses[1]), lses[2])
    es = [jnp.exp2(l - m) for l in lses]
    inv = 1.0 / (es[0] + es[1] + es[2])
    ex = ex_ref[...]
    alphas = [_expand_heads(e * inv, ex) for e in es[:-1]]
    alphas.append(1.0 - alphas[0] - alphas[1])
    for k in range(nblk):
        cols = slice(k * LANES, (k + 1) * LANES)
        y = jnp.zeros((tm, LANES), F32)
        for g, d in enumerate(dilations):
            og = o_refs[g][0, :, cols] if d == 1 else ot_ref[g, k]
            y += alphas[g][:, cols] * og
        gate = gate_ref[:, cols].astype(F32)
        y_ref[:, cols] = (y * gate * (1.0 / (1.0 + jnp.exp(-gate)))).astype(y_ref.dtype)
    acc = jnp.dot(y_ref[...], w_ref[...], preferred_element_type=F32)
    hout_ref[...] = h_ref[...] + mod_ref[...] * acc


def merge_out_proj(outs, lses, gate, w_out, mod_gate, h):
    b, s, w = gate.shape
    d_model = h.shape[-1]
    tm = min(s, 256)
    dil = tuple(o.shape[1] for o in outs)
    expand = (jnp.arange(LANES)[:, None] == (jnp.arange(w)[None, :] // HEAD_DIM)).astype(BF16)

    def cm(d, width):
        return pl.BlockSpec((None, d, tm // d, width), lambda bi, i: (bi, 0, i, 0))

    def tspec(width):
        return pl.BlockSpec((None, tm, width), lambda bi, i: (bi, i, 0))

    return pl.pallas_call(
        functools.partial(_merge_out_kernel, dilations=dil),
        grid=(b, s // tm),
        in_specs=[cm(d, w) for d in dil] + [cm(d, LANES) for d in dil]
                 + [tspec(w), pl.BlockSpec((LANES, w), lambda bi, i: (0, 0)),
                    pl.BlockSpec((w, d_model), lambda bi, i: (0, 0)),
                    pl.BlockSpec((None, 1, d_model), lambda bi, i: (bi, 0, 0)),
                    tspec(d_model)],
        out_specs=tspec(d_model),
        out_shape=jax.ShapeDtypeStruct((b, s, d_model), F32),
        scratch_shapes=[pltpu.VMEM((tm, w), BF16),
                        pltpu.VMEM((len(dil), w // LANES, tm, LANES), F32),
                        pltpu.VMEM((len(dil), tm, LANES), F32)],
        compiler_params=_cparams(("arbitrary", "arbitrary")),
        name="merge_out_proj",
    )(*outs, *lses, gate, expand, w_out, mod_gate.reshape(b, 1, d_model), h)


def _class_major(t, d):
    b, s = t.shape[:2]
    return jnp.swapaxes(t.reshape(b, s // d, d, *t.shape[2:]), 1, 2)


def class_major_rope_tables(cos, sin, dilation):
    return _class_major(cos, dilation), _class_major(sin, dilation)


def dilated_layer(h, g, scale, shift, mod_gate, w_in, w_out, tables):
    tn = 512
    per = A_WIDTH // tn
    qkv_flags = (True,) * (2 * per) + (False,) * per
    n_groups = len(A_GROUPS)
    col = np.arange(w_in.shape[1])
    is_q = (col < n_groups * 3 * A_WIDTH) & ((col // A_WIDTH) % 3 == 0)
    w = (w_in * jnp.asarray(np.where(is_q, Q_PRESCALE, 1.0), F32)).astype(BF16)
    outs, lses = [], []
    gate = None
    for gi, (window, dilation) in enumerate(A_GROUPS):
        assert window // dilation == A_BLK
        plain = (A_WIDTH, 3 * n_groups) if gi == 0 else None
        qkv = in_proj(h, g, scale, shift, w, qkv_flags, *tables[dilation], tn, dilation,
                      w_block=gi, plain=plain)
        if plain is not None:
            qkv, gate = qkv
        if dilation == 1:
            qkv = qkv[:, None]
        o, lse = banded_group_attention(qkv)
        outs.append(o)
        lses.append(lse)
    return merge_out_proj(outs, lses, gate, w_out.astype(BF16), mod_gate, h)


B_COLS = 3584
B_ROPE_FLAGS = (1, 1, 0, 0, 1, 1, 0)


def _dsa_head_perm():
    group = B_HEADS // B_KV_HEADS
    order = []
    for g2 in range(B_KV_HEADS // 2):
        for r in range(group):
            order += [(2 * g2) * group + r, (2 * g2 + 1) * group + r]
    return np.asarray(order)


def _dsa_weights(w_in, w_out):
    d = w_in.shape[0]
    cuts = np.cumsum((B_WIDTH, B_KV_HEADS * HEAD_DIM, B_KV_HEADS * HEAD_DIM, IDX_HEADS * HEAD_DIM,
                      HEAD_DIM, IDX_HEADS, B_WIDTH))[:-1]
    wq, wk, wv, wqi, wki, wwi, wg = jnp.split(w_in, cuts, axis=1)
    cols = (_dsa_head_perm()[:, None] * HEAD_DIM + np.arange(HEAD_DIM)[None, :]).reshape(-1)
    zeros = lambda n: jnp.zeros((d, n), w_in.dtype)
    wq = wq * Q_PRESCALE
    w = jnp.concatenate([wq[:, cols], wg[:, cols], wqi, wk, wki, wki, zeros(LANES),
                         wwi, zeros(2 * LANES - IDX_HEADS), wv], axis=1)
    assert w.shape[1] == B_COLS
    return w.astype(BF16), w_out[cols, :].astype(BF16)


def _sortable(x):
    bits = pltpu.bitcast(x, I32)
    return bits ^ ((bits >> 31) & jnp.int32(0x7FFFFFFF))


def _bit_transpose32(words):
    a = list(words)
    j, m = 16, 0x0000FFFF
    while j:
        mask = jnp.int32(m - (1 << 32) if m >= (1 << 31) else m)
        shift = jnp.full(a[0].shape, j, I32)
        k = 0
        while k < 32:
            t = (a[k] ^ lax.shift_right_logical(a[k + j], shift)) & mask
            a[k] = a[k] ^ t
            a[k + j] = a[k + j] ^ lax.shift_left(t, shift)
            k = (k + j + 1) & ~j
        j >>= 1
        m = (m ^ (m << j)) & 0xFFFFFFFF
    return a


def _sublane_allsum(x):
    x = x + pltpu.roll(x, 4, 0)
    x = x + pltpu.roll(x, 2, 0)
    return x + pltpu.roll(x, 1, 0)


def _tree_sum(parts):
    parts = list(parts)
    while len(parts) > 1:
        odd = [parts[-1]] if len(parts) % 2 else []
        parts = [parts[j] + parts[j + 1] for j in range(0, len(parts) - 1, 2)] + odd
    return parts[0]


GROUP_KEYS = 256
ATTN_LOOKAHEAD = 1


def _dsa_kernel(q_ref, gate_ref, qi_ref, k_ref, ki_ref, wi_ref, v_ref, y_ref,
                keys_ref, planes_ref, bias_ref, kt_ref, qs_ref, s0_ref, m_ref, l_ref, acc_ref,
                *, k_sel):
    i = pl.program_id(1)
    seq = keys_ref.shape[1]
    n_sub = Q_BLOCK // SEL_LANES
    n_pairs = B_HEADS // 2
    n_groups_max = seq // GROUP_KEYS
    lane_q = lax.broadcasted_iota(I32, (Q_BLOCK, LANES), 1)
    first_head_q = lane_q < HEAD_DIM
    first_head = lax.broadcasted_iota(I32, (SEL_LANES, LANES), 1) < HEAD_DIM
    key_row = lax.broadcasted_iota(I32, (KEY_TILE, SEL_LANES), 0)
    lane_idx = lax.broadcasted_iota(I32, (KEY_TILE, SEL_LANES), 1)

    @pl.when(i == 0)
    def _():
        planes_ref[...] = jnp.zeros(planes_ref.shape, I32)
        keys_ref[...] = jnp.zeros(keys_ref.shape, I32)

    def tiles_of(a):
        return ((i * n_sub + a) * SEL_LANES) // KEY_TILE + 1

    def select(a):
        n_tiles = tiles_of(a)
        rows_a = slice(a * SEL_LANES, (a + 1) * SEL_LANES)
        q_pos = (i * n_sub + a) * SEL_LANES + lane_idx
        keys = keys_ref.at[a]

        w_t = (wi_ref[rows_a, :].astype(F32) * (IDX_HEADS ** -0.5 * HEAD_DIM ** -0.5)).T
        qi_stacked = [_stack_heads(qi_ref[rows_a, p * LANES:(p + 1) * LANES], first_head)
                      for p in range(IDX_HEADS // 2)]

        def score_tile(t):
            base = pl.multiple_of(t * KEY_TILE, KEY_TILE)
            kk = ki_ref[pl.ds(base, KEY_TILE), :]
            score = jnp.zeros((KEY_TILE, SEL_LANES), F32)
            for p in range(IDX_HEADS // 2):
                sc = lax.dot_general(kk, qi_stacked[p], (((1,), (1,)), ((), ())),
                                     preferred_element_type=F32)
                score += jnp.maximum(sc[:, :SEL_LANES], 0.0) * w_t[2 * p:2 * p + 1, :]
                score += jnp.maximum(sc[:, SEL_LANES:], 0.0) * w_t[2 * p + 1:2 * p + 2, :]
            causal = (key_row + base) <= q_pos
            score = jnp.where(causal, score + 0.0, -jnp.inf)
            key = _sortable(score)
            keys[pl.ds(base, KEY_TILE), :] = key
            ukey = key ^ jnp.int32(INT_MIN)
            for g in range(KEY_TILE // GROUP_KEYS):
                words = [ukey[g * GROUP_KEYS + 8 * j:g * GROUP_KEYS + 8 * j + 8, :] for j in range(32)]
                rows = pl.ds(pl.multiple_of(t * (KEY_TILE // 32) + 8 * g, 8), 8)
                for b, plane in enumerate(_bit_transpose32(words)):
                    planes_ref[b, rows, :] = plane

        def score_two(t2, carry):
            score_tile(2 * t2)
            score_tile(2 * t2 + 1)
            return carry

        lax.fori_loop(0, n_tiles // 2, score_two, 0)

        @pl.when(n_tiles % 2 == 1)
        def _():
            score_tile(n_tiles - 1)

        def select_threshold():
            n_groups = n_tiles * (KEY_TILE // GROUP_KEYS)
            alive0 = tuple(jnp.full((8, SEL_LANES), -1, I32) * (g < n_groups).astype(I32)
                           for g in range(n_groups_max))
            need0 = jnp.full((8, SEL_LANES), k_sel, I32)

            def step(st, carry):
                alive, need, prefix = carry
                planes = [planes_ref[st, 8 * g:8 * g + 8, :] for g in range(n_groups_max)]
                ones = [al & pln for al, pln in zip(alive, planes)]
                cnt = _sublane_allsum(_tree_sum([lax.population_count(o) for o in ones]))
                take = cnt >= need
                prefix = jnp.where(take, prefix | jnp.left_shift(jnp.int32(1), 31 - st), prefix)
                need = jnp.where(take, need, need - cnt)
                flip = jnp.where(take, 0, -1)
                alive = tuple(al & (pln ^ flip) for al, pln in zip(alive, planes))
                return alive, need, prefix

            alive, need, prefix = lax.fori_loop(
                0, 32, step, (alive0, need0, jnp.zeros((8, SEL_LANES), I32)))
            thr = (prefix ^ jnp.int32(INT_MIN))[:1, :]
            need_eq = need[:1, :]

            n_eq = _sublane_allsum(_tree_sum([lax.population_count(al) for al in alive]))[:1, :]
            tied = (n_eq > need_eq) & (thr > KEY_NEG_INF)
            idx_bits = (seq - 1).bit_length()

            def tie_break():
                def count_eq_before(cut):
                    def body(t, acc):
                        base = pl.multiple_of(t * KEY_TILE, KEY_TILE)
                        hit = (keys[pl.ds(base, KEY_TILE), :] == thr) & ((key_row + base) < cut)
                        return acc + jnp.sum(hit.astype(I32).reshape(KEY_TILE // 8, 8, SEL_LANES), axis=0)
                    acc = lax.fori_loop(0, n_tiles, body, jnp.zeros((8, SEL_LANES), I32))
                    return jnp.sum(acc, axis=0, keepdims=True)

                def bit_step(it, cut):
                    cand = cut + jnp.left_shift(jnp.int32(1), idx_bits - 1 - it)
                    return jnp.where(count_eq_before(cand) < need_eq, cand, cut)

                return lax.fori_loop(0, idx_bits, bit_step, jnp.zeros((1, SEL_LANES), I32))

            any_tied = jnp.max(tied.astype(I32)) > 0
            idx_cut = lax.cond(any_tied, tie_break, lambda: jnp.full((1, SEL_LANES), seq, I32))
            return thr, idx_cut

        all_selected = (i * n_sub + a + 1) * SEL_LANES <= k_sel
        thr, idx_cut = lax.cond(
            all_selected,
            lambda: (jnp.full((1, SEL_LANES), INT_MIN, I32), jnp.full((1, SEL_LANES), seq, I32)),
            select_threshold)
        return thr, idx_cut, q_pos

    selected = [select(a) for a in range(n_sub)]
    n_tiles = tiles_of(n_sub - 1)

    def write_bias(base):
        k_idx = key_row + base
        for a, (thr, idx_cut, q_pos) in enumerate(selected):
            key = keys_ref[a, pl.ds(base, KEY_TILE), :]
            sel = ((key > thr) | ((key == thr) & (k_idx <= idx_cut))) & (k_idx <= q_pos)
            bias_t = jnp.where(sel, 0.0, -jnp.inf).astype(F32)
            for c in range(KEY_TILE // LANES):
                col = base + c * LANES
                col = col if isinstance(col, int) else pl.multiple_of(col, LANES)
                bias_ref[a * SEL_LANES:(a + 1) * SEL_LANES, pl.ds(col, LANES)] = \
                    bias_t[c * LANES:(c + 1) * LANES, :].T

    write_bias(0)

    for p in range(n_pairs):
        cols = slice(p * LANES, (p + 1) * LANES)
        qs_ref[p] = _stack_heads(q_ref[:, cols], first_head_q)
    heads_per_kv_pair = 2 * (B_HEADS // B_KV_HEADS) // 2

    ones_blk = jnp.ones((KEY_TILE, LANES), BF16)

    @pl.when(i == 0)
    def _():
        def xpose(r, carry):
            rows = pl.ds(pl.multiple_of(r * LANES, LANES), LANES)
            for c in range(kt_ref.shape[0]):
                blk = k_ref[rows, c * LANES:(c + 1) * LANES].astype(F32)
                kt_ref[c, :, rows] = blk.T.astype(BF16)
            return carry
        lax.fori_loop(0, seq // LANES, xpose, 0)

    def logits(base, p):
        kt = kt_ref[p // heads_per_kv_pair, :, pl.ds(base, KEY_TILE)]
        return jnp.dot(qs_ref[p], kt, preferred_element_type=F32)

    lookahead = s0_ref.shape[0]
    for u in range(lookahead):
        s0_ref[u] = logits(0, u)

    def attn_tiles(t0, count, start=False):
        bases = [pl.multiple_of(jnp.minimum(t0 + dt, n_tiles - 1) * KEY_TILE, KEY_TILE)
                 for dt in range(count + 1)]
        units = [(dt, p) for dt in range(count + 1) for p in range(n_pairs)]
        queue = [s0_ref[u] for u in range(lookahead)]
        for u, (dt, p) in enumerate(units[:count * n_pairs]):
            base = bases[dt]
            if p == 0:
                bias = bias_ref[:, pl.ds(base, KEY_TILE)]
                write_bias(bases[dt + 1])
                bias2 = jnp.concatenate([bias, bias], axis=0)
            kv = p // heads_per_kv_pair
            kv_cols = slice(kv * LANES, (kv + 1) * LANES)
            s = queue.pop(0) + bias2
            ahead_dt, ahead_p = units[u + lookahead]
            queue.append(logits(bases[ahead_dt], ahead_p))
            m_blk = s[:, :LANES]
            for c in range(1, KEY_TILE // LANES):
                m_blk = jnp.maximum(m_blk, s[:, c * LANES:(c + 1) * LANES])
            opening = start and dt == 0
            m_cur = jnp.max(m_blk, axis=-1, keepdims=True)
            if opening:
                m_new = jnp.broadcast_to(jnp.maximum(m_cur, F32_MIN), m_blk.shape)
            else:
                m_old = m_ref[p]
                m_new = jnp.maximum(m_old, m_cur)
                alpha = jnp.exp2(m_old - m_new)
            e = jnp.concatenate(
                [jnp.exp2(s[:, c * LANES:(c + 1) * LANES] - m_new).astype(BF16)
                 for c in range(KEY_TILE // LANES)], axis=1)
            v_ext = jnp.concatenate([v_ref[pl.ds(base, KEY_TILE), kv_cols], ones_blk], axis=1)
            pv = jnp.dot(e, v_ext, preferred_element_type=F32)
            if opening:
                acc_ref[p] = pv[:, :LANES]
                l_ref[p] = pv[:, LANES:]
            else:
                acc_ref[p] = acc_ref[p] * alpha + pv[:, :LANES]
                l_ref[p] = l_ref[p] * alpha + pv[:, LANES:]
            m_ref[p] = m_new
        for u in range(lookahead):
            s0_ref[u] = queue[u]

    attn_tiles(0, 1, start=True)

    def attn_two(t2, carry):
        attn_tiles(1 + 2 * t2, 2)
        return carry

    lax.fori_loop(0, (n_tiles - 1) // 2, attn_two, 0)

    @pl.when(n_tiles % 2 == 0)
    def _():
        attn_tiles(n_tiles - 1, 1)

    for p in range(n_pairs):
        cols = slice(p * LANES, (p + 1) * LANES)
        l = l_ref[p]
        acc = acc_ref[p]
        num = jnp.where(first_head_q, acc[:Q_BLOCK], acc[Q_BLOCK:])
        den = jnp.where(first_head_q, l[:Q_BLOCK], l[Q_BLOCK:])
        gate = gate_ref[:, cols].astype(F32)
        y_ref[:, cols] = (num * gate / (den * (1.0 + jnp.exp(-gate)))).astype(y_ref.dtype)


def dsa_attention(proj):
    b, s, _ = proj.shape
    k_sel = min(TOPK_MAX, s // 4)
    assert s % KEY_TILE == 0 and s % Q_BLOCK == 0 and k_sel <= KEY_TILE
    kvw = B_KV_HEADS * HEAD_DIM
    n_pairs = B_HEADS // 2

    def qblock(width, idx):
        return pl.BlockSpec((None, Q_BLOCK, width), lambda bi, i: (bi, i, idx))

    def full(width, idx):
        return pl.BlockSpec((None, s, width), lambda bi, i: (bi, 0, idx))

    return pl.pallas_call(
        functools.partial(_dsa_kernel, k_sel=k_sel),
        grid=(b, s // Q_BLOCK),
        in_specs=[
            qblock(B_WIDTH, 0),
            qblock(B_WIDTH, 1),
            qblock(IDX_HEADS * HEAD_DIM, 4),
            full(kvw, 10),
            full(LANES, 22),
            qblock(LANES, 24),
            full(kvw, 13),
        ],
        out_specs=qblock(B_WIDTH, 0),
        out_shape=jax.ShapeDtypeStruct((b, s, B_WIDTH), BF16),
        scratch_shapes=[
            pltpu.VMEM((Q_BLOCK // SEL_LANES, s, SEL_LANES), I32),
            pltpu.VMEM((32, s // 32, SEL_LANES), I32),
            pltpu.VMEM((Q_BLOCK, s), F32),
            pltpu.VMEM((kvw // LANES, LANES, s), BF16),
            pltpu.VMEM((n_pairs, 2 * Q_BLOCK, LANES), BF16),
            pltpu.VMEM((ATTN_LOOKAHEAD, 2 * Q_BLOCK, KEY_TILE), F32),
            pltpu.VMEM((n_pairs, 2 * Q_BLOCK, LANES), F32),
            pltpu.VMEM((n_pairs, 2 * Q_BLOCK, LANES), F32),
            pltpu.VMEM((n_pairs, 2 * Q_BLOCK, LANES), F32),
        ],
        compiler_params=_cparams(("arbitrary", "arbitrary")),
        name="dsa_attention",
    )(proj, proj, proj, proj, proj, proj, proj)


def dsa_mixer(h, g, scale, shift, w_in_packed, cos, sin):
    proj = in_proj(h, g, scale, shift, w_in_packed, [f == 1 for f in B_ROPE_FLAGS], cos, sin, 512)
    return dsa_attention(proj)


def kernel(x, c, positions, norm_g, ada_w, ada_b, a_w_in, a_w_out, b_w_in, b_w_out, final_g):
    depth = norm_g.shape[0]
    d = x.shape[-1]
    mod = adaln_mod(c, ada_w, ada_b)
    cos, sin = rope_tables(positions)
    tables = {1: (cos, sin)}
    for _, dilation in A_GROUPS:
        if dilation > 1:
            tables[dilation] = class_major_rope_tables(cos, sin, dilation)
    h = x
    for i in range(depth):
        shift, scale, gate = mod[i, :, :d], mod[i, :, d:2 * d], mod[i, :, 2 * d:]
        last = i == depth - 1
        if i % 2 == 0:
            h = dilated_layer(h, norm_g[i], scale, shift, gate, a_w_in[i // 2], a_w_out[i // 2], tables)
            if last:
                h = final_norm(h, final_g)
        else:
            w_in, w_out = _dsa_weights(b_w_in[i // 2], b_w_out[i // 2])
            y = dsa_mixer(h, norm_g[i], scale, shift, w_in, cos, sin)
            h = out_proj(y, w_out, gate, h, final_g if last else None)
    return h
```

```python
import functools

import numpy as np
import jax
import jax.numpy as jnp
from jax import lax
from jax.experimental import pallas as pl
from jax.experimental.pallas import tpu as pltpu

F32 = jnp.float32
BF16 = jnp.bfloat16
I32 = jnp.int32

LANES = 128
HEAD_DIM = 64
HALF = HEAD_DIM // 2
ROPE_THETA = 10000.0
NORM_EPS = 1e-6
VMEM_LIMIT = 56 * 1024 * 1024
ROW_CHUNK = 256

A_HEADS = 16
A_WIDTH = A_HEADS * HEAD_DIM
A_GROUPS = ((128, 1), (512, 4), (2048, 16))
A_BLK = 128
BANDED_LOOKAHEAD = 1

B_HEADS = 16
B_KV_HEADS = 4
B_WIDTH = B_HEADS * HEAD_DIM
IDX_HEADS = 8
TOPK_MAX = 256
Q_BLOCK = 256
SEL_LANES = 128
KEY_TILE = 512
INT_MIN = -(2 ** 31)
KEY_NEG_INF = -2139095041
F32_MIN = float(np.finfo(np.float32).min)
LOG2_E = float(np.log2(np.e))
Q_PRESCALE = LOG2_E * HEAD_DIM ** -0.5


def _cparams(sem):
    return pltpu.CompilerParams(dimension_semantics=sem, vmem_limit_bytes=VMEM_LIMIT)


def _adaln_kernel(c_ref, w_ref, b_ref, o_ref):
    c = c_ref[...]
    ca = (c * (1.0 / (1.0 + jnp.exp(-c)))).astype(BF16)
    acc = jnp.dot(ca, w_ref[...].astype(BF16), preferred_element_type=F32)
    o_ref[...] = acc + b_ref[...]


def adaln_mod(c, ada_w, ada_b):
    depth, d, d3 = ada_w.shape
    b = c.shape[0]
    tn = 1024
    return pl.pallas_call(
        _adaln_kernel,
        grid=(depth, d3 // tn),
        in_specs=[
            pl.BlockSpec((b, d), lambda i, j: (0, 0)),
            pl.BlockSpec((None, d, tn), lambda i, j: (i, 0, j)),
            pl.BlockSpec((None, 1, tn), lambda i, j: (i, 0, j)),
        ],
        out_specs=pl.BlockSpec((None, b, tn), lambda i, j: (i, 0, j)),
        out_shape=jax.ShapeDtypeStruct((depth, b, d3), F32),
        compiler_params=_cparams(("arbitrary", "arbitrary")),
        name="adaln_mod",
    )(c, ada_w, ada_b.reshape(depth, 1, d3))


def _rope_table_kernel(pos_ref, inv_ref, cos_ref, sin_ref):
    ang = pos_ref[...].astype(F32) * inv_ref[...]
    lane = lax.broadcasted_iota(I32, ang.shape, 1)
    first_half = (lane % HEAD_DIM) < HALF
    cos_ref[...] = jnp.cos(ang)
    s = jnp.sin(ang)
    sin_ref[...] = jnp.where(first_half, -s, s)


def rope_tables(positions):
    b, s = positions.shape
    inv_freq = ROPE_THETA ** (-jnp.arange(HALF, dtype=F32) / HALF)
    inv_lane = jnp.tile(inv_freq, LANES // HALF).reshape(1, LANES)
    ts = min(s, 1024)
    out = jax.ShapeDtypeStruct((b, s, LANES), F32)
    return pl.pallas_call(
        _rope_table_kernel,
        grid=(b, s // ts),
        in_specs=[
            pl.BlockSpec((None, ts, 1), lambda i, j: (i, j, 0)),
            pl.BlockSpec((1, LANES), lambda i, j: (0, 0)),
        ],
        out_specs=[pl.BlockSpec((None, ts, LANES), lambda i, j: (i, j, 0))] * 2,
        out_shape=[out, out],
        compiler_params=_cparams(("arbitrary", "arbitrary")),
        name="rope_tables",
    )(positions.reshape(b, s, 1), inv_lane)


def _rope_block(t, cos, sin_signed, first_half):
    partner = jnp.where(first_half, pltpu.roll(t, LANES - HALF, 1), pltpu.roll(t, HALF, 1))
    return t * cos + partner * sin_signed


def _in_proj_kernel(x_ref, g_ref, sc_ref, sh_ref, w_ref, cos_ref, sin_ref, *rest,
                    dilation, rope_tiles, tn, plain_cols):
    if plain_cols:
        w2_ref, o_ref, o2_ref, u_ref, *stage = rest
    else:
        o_ref, u_ref, *stage = rest
    tm = u_ref.shape[0]
    nblk = tn // LANES
    per = ROW_CHUNK // dilation
    x = x_ref[...]
    ms = jnp.mean(x * x, axis=-1, keepdims=True)
    xn = x * lax.rsqrt(ms + NORM_EPS)
    u_ref[...] = (xn * g_ref[...] * (1.0 + sc_ref[...]) + sh_ref[...]).astype(BF16)
    lane = lax.broadcasted_iota(I32, (per, LANES), 1)
    first_half = (lane % HEAD_DIM) < HALF
    step = 0
    for j, rope in enumerate(rope_tiles):
        for c in range(tm // ROW_CHUNK):
            acc = jnp.dot(u_ref[c * ROW_CHUNK:(c + 1) * ROW_CHUNK, :], w_ref[:, j * tn:(j + 1) * tn],
                          preferred_element_type=F32)
            if dilation > 1:
                slot = step % stage[0].shape[0]
                step += 1
                for k in range(nblk):
                    stage[0][slot, k] = acc[:, k * LANES:(k + 1) * LANES]
            for r in range(dilation):
                rows = slice(c * per, (c + 1) * per)
                if rope:
                    trows = rows if dilation == 1 else pl.ds(c * ROW_CHUNK + r, per, stride=dilation)
                    cos, sin = cos_ref[trows, :], sin_ref[trows, :]
                for k in range(nblk):
                    cols = slice(j * tn + k * LANES, j * tn + (k + 1) * LANES)
                    if dilation > 1:
                        blk = stage[0][slot, k, pl.ds(r, per, stride=dilation), :]
                    else:
                        blk = acc[:, k * LANES:(k + 1) * LANES]
                    if rope:
                        blk = _rope_block(blk, cos, sin, first_half)
                    if dilation > 1:
                        o_ref[r, rows, cols] = blk.astype(o_ref.dtype)
                    else:
                        o_ref[rows, cols] = blk.astype(o_ref.dtype)
    for j in range(plain_cols // tn):
        for c in range(tm // ROW_CHUNK):
            rows = slice(c * ROW_CHUNK, (c + 1) * ROW_CHUNK)
            acc = jnp.dot(u_ref[rows, :], w2_ref[:, j * tn:(j + 1) * tn], preferred_element_type=F32)
            o2_ref[rows, j * tn:(j + 1) * tn] = acc.astype(o2_ref.dtype)


def in_proj(h, g, scale, shift, w, rope_tiles, cos, sin, tn, dilation=1, w_block=0, plain=None):
    b, s, d = h.shape
    n = len(rope_tiles) * tn
    tm = min(s, 512)
    assert tm % ROW_CHUNK == 0 and ROW_CHUNK % (16 * dilation) == 0 and (w_block + 1) * n <= w.shape[1]
    scratch = [pltpu.VMEM((tm, d), BF16)]
    tab_spec = pl.BlockSpec((None, tm, LANES), lambda bi, i: (bi, i, 0))
    if dilation == 1:
        out_spec = pl.BlockSpec((None, tm, n), lambda bi, i: (bi, i, 0))
        out_shape = jax.ShapeDtypeStruct((b, s, n), BF16)
    else:
        out_spec = pl.BlockSpec((None, dilation, tm // dilation, n), lambda bi, i: (bi, 0, i, 0))
        out_shape = jax.ShapeDtypeStruct((b, dilation, s // dilation, n), BF16)
        scratch.append(pltpu.VMEM((2, tn // LANES, ROW_CHUNK, LANES), F32))
    in_specs = [
        pl.BlockSpec((None, tm, d), lambda bi, i: (bi, i, 0)),
        pl.BlockSpec((1, d), lambda bi, i: (0, 0)),
        pl.BlockSpec((None, 1, d), lambda bi, i: (bi, 0, 0)),
        pl.BlockSpec((None, 1, d), lambda bi, i: (bi, 0, 0)),
        pl.BlockSpec((d, n), lambda bi, i: (0, w_block)),
        tab_spec, tab_spec,
    ]
    args = [h, g.reshape(1, d), scale.reshape(b, 1, d), shift.reshape(b, 1, d), w, cos, sin]
    plain_cols = 0
    if plain is not None:
        plain_cols, plain_block = plain
        assert plain_cols % tn == 0 and (plain_block + 1) * plain_cols <= w.shape[1]
        in_specs.append(pl.BlockSpec((d, plain_cols), lambda bi, i: (0, plain_block)))
        args.append(w)
        out_spec = [out_spec, pl.BlockSpec((None, tm, plain_cols), lambda bi, i: (bi, i, 0))]
        out_shape = [out_shape, jax.ShapeDtypeStruct((b, s, plain_cols), BF16)]
    return pl.pallas_call(
        functools.partial(_in_proj_kernel, dilation=dilation, rope_tiles=tuple(rope_tiles), tn=tn,
                          plain_cols=plain_cols),
        grid=(b, s // tm),
        in_specs=in_specs,
        out_specs=out_spec,
        out_shape=out_shape,
        scratch_shapes=scratch,
        compiler_params=_cparams(("arbitrary", "arbitrary")),
        name=f"in_proj_d{dilation}",
    )(*args)


def _out_proj_kernel(y_ref, w_ref, gate_ref, h_ref, *rest):
    acc = jnp.dot(y_ref[...], w_ref[...], preferred_element_type=F32)
    h_new = h_ref[...] + gate_ref[...] * acc
    if len(rest) == 2:
        g_ref, o_ref = rest
        ms = jnp.mean(h_new * h_new, axis=-1, keepdims=True)
        o_ref[...] = h_new * lax.rsqrt(ms + NORM_EPS) * g_ref[...]
    else:
        rest[0][...] = h_new


def out_proj(y, w, gate, h, final_g=None):
    b, s, d = h.shape
    k = y.shape[-1]
    tm = min(s, 1024)
    in_specs = [
        pl.BlockSpec((None, tm, k), lambda bi, i: (bi, i, 0)),
        pl.BlockSpec((k, d), lambda bi, i: (0, 0)),
        pl.BlockSpec((None, 1, d), lambda bi, i: (bi, 0, 0)),
        pl.BlockSpec((None, tm, d), lambda bi, i: (bi, i, 0)),
    ]
    args = [y, w, gate.reshape(b, 1, d), h]
    if final_g is not None:
        in_specs.append(pl.BlockSpec((1, d), lambda bi, i: (0, 0)))
        args.append(final_g.reshape(1, d))
    return pl.pallas_call(
        _out_proj_kernel,
        grid=(b, s // tm),
        in_specs=in_specs,
        out_specs=pl.BlockSpec((None, tm, d), lambda bi, i: (bi, i, 0)),
        out_shape=jax.ShapeDtypeStruct((b, s, d), F32),
        compiler_params=_cparams(("arbitrary", "arbitrary")),
        name="out_proj",
    )(*args)


def _final_norm_kernel(x_ref, g_ref, o_ref):
    x = x_ref[...]
    ms = jnp.mean(x * x, axis=-1, keepdims=True)
    o_ref[...] = x * lax.rsqrt(ms + NORM_EPS) * g_ref[...]


def final_norm(h, g):
    b, s, d = h.shape
    tm = min(s, 1024)
    return pl.pallas_call(
        _final_norm_kernel,
        grid=(b, s // tm),
        in_specs=[
            pl.BlockSpec((None, tm, d), lambda bi, i: (bi, i, 0)),
            pl.BlockSpec((1, d), lambda bi, i: (0, 0)),
        ],
        out_specs=pl.BlockSpec((None, tm, d), lambda bi, i: (bi, i, 0)),
        out_shape=jax.ShapeDtypeStruct((b, s, d), F32),
        compiler_params=_cparams(("arbitrary", "arbitrary")),
        name="final_norm",
    )(h, g.reshape(1, d))


def _stack_heads(pair, first_head):
    zero = jnp.zeros_like(pair)
    return jnp.concatenate([jnp.where(first_head, pair, zero), jnp.where(first_head, zero, pair)], axis=0)


def _banded_kernel(q_ref, kp_ref, kc_ref, vp_ref, vc_ref, o_ref, lse_ref, kwin_ref, vwin_ref, *, tq):
    i = pl.program_id(2)
    nsub = tq // A_BLK
    lane = lax.broadcasted_iota(I32, (A_BLK, LANES), 1)
    first_head = lane < HEAD_DIM
    odd_lane = (lane % 2) == 1
    r2 = lax.broadcasted_iota(I32, (2 * A_BLK, 2 * A_BLK), 0) % A_BLK
    c2 = lax.broadcasted_iota(I32, (2 * A_BLK, 2 * A_BLK), 1)
    own_ok = (c2 >= A_BLK) & (c2 - A_BLK <= r2)
    ones_blk = jnp.ones((2 * A_BLK, LANES), BF16)
    n_pairs = A_HEADS // 2

    kwin_ref[:A_BLK, :] = kp_ref[...]
    kwin_ref[A_BLK:, :] = kc_ref[...]
    vwin_ref[:A_BLK, :] = vp_ref[...]
    vwin_ref[A_BLK:, :] = vc_ref[...]

    def logits(a, p):
        cols = slice(p * LANES, (p + 1) * LANES)
        qs = _stack_heads(q_ref[a * A_BLK:(a + 1) * A_BLK, cols], first_head)
        k2 = kwin_ref[a * A_BLK:(a + 2) * A_BLK, cols]
        return lax.dot_general(qs, k2, (((1,), (1,)), ((), ())), preferred_element_type=F32)

    units = [(a, p) for a in range(nsub) for p in range(n_pairs)]
    queue = [logits(*units[u]) for u in range(BANDED_LOOKAHEAD)]
    lse_tile = bias = None
    for u, (a, p) in enumerate(units):
        rows = slice(a * A_BLK, (a + 1) * A_BLK)
        cols = slice(p * LANES, (p + 1) * LANES)
        if p == 0:
            prev_shift = jnp.where(i * nsub + a > 0, 0, 2 * A_BLK)
            mask = own_ok | ((c2 < A_BLK) & (c2 >= r2 + prev_shift))
            bias = jnp.where(mask, 0.0, -jnp.inf).astype(F32)
            lse_tile = jnp.zeros((A_BLK, LANES), F32)
        s = queue.pop(0) + bias
        if u + BANDED_LOOKAHEAD < len(units):
            queue.append(logits(*units[u + BANDED_LOOKAHEAD]))
        m = jnp.max(s, axis=-1, keepdims=True)
        e = jnp.exp2(s - m).astype(BF16)
        v_ext = jnp.concatenate([vwin_ref[a * A_BLK:(a + 2) * A_BLK, cols], ones_blk], axis=1)
        pv = jnp.dot(e, v_ext, preferred_element_type=F32)
        num = jnp.where(first_head, pv[:A_BLK, :LANES], pv[A_BLK:, :LANES])
        o_ref[rows, cols] = num * (1.0 / jnp.where(first_head, pv[:A_BLK, LANES:], pv[A_BLK:, LANES:]))
        den_pair = jnp.where(odd_lane, pv[A_BLK:, LANES:], pv[:A_BLK, LANES:])
        m_pair = jnp.where(odd_lane, m[A_BLK:], m[:A_BLK])
        lse_tile = jnp.where((lane // 2) == p, m_pair + jnp.log(den_pair) * LOG2_E, lse_tile)
        if p == n_pairs - 1:
            lse_ref[rows, :] = lse_tile


def banded_group_attention(qkv):
    b, d, n, _ = qkv.shape
    tq = min(2 * A_BLK, n)
    sub = tq // A_BLK

    def cur(blk):
        return pl.BlockSpec((None, None, tq, A_WIDTH), lambda bi, r, i: (bi, r, i, blk))

    def prev(blk):
        return pl.BlockSpec((None, None, A_BLK, A_WIDTH),
                            lambda bi, r, i: (bi, r, jnp.maximum(i * sub - 1, 0), blk))

    return pl.pallas_call(
        functools.partial(_banded_kernel, tq=tq),
        grid=(b, d, n // tq),
        in_specs=[cur(0), prev(1), cur(1), prev(2), cur(2)],
        out_specs=[
            pl.BlockSpec((None, None, tq, A_WIDTH), lambda bi, r, i: (bi, r, i, 0)),
            pl.BlockSpec((None, None, tq, LANES), lambda bi, r, i: (bi, r, i, 0)),
        ],
        out_shape=[
            jax.ShapeDtypeStruct((b, d, n, A_WIDTH), F32),
            jax.ShapeDtypeStruct((b, d, n, LANES), F32),
        ],
        scratch_shapes=[pltpu.VMEM((tq + A_BLK, A_WIDTH), BF16)] * 2,
        compiler_params=_cparams(("arbitrary", "arbitrary", "arbitrary")),
        name=f"banded_attn_d{d}",
    )(qkv, qkv, qkv, qkv, qkv)


def _expand_heads(x, expand):
    hi = x.astype(BF16)
    lo = (x - hi.astype(F32)).astype(BF16)
    return (jnp.dot(hi, expand, preferred_element_type=F32)
            + jnp.dot(lo, expand, preferred_element_type=F32))


def _merge_out_kernel(o0_ref, o1_ref, o2_ref, l0_ref, l1_ref, l2_ref, gate_ref, ex_ref, w_ref,
                      mod_ref, h_ref, hout_ref, y_ref, ot_ref, lt_ref, *, dilations):
    o_refs, l_refs = (o0_ref, o1_ref, o2_ref), (l0_ref, l1_ref, l2_ref)
    tm = y_ref.shape[0]
    nblk = y_ref.shape[1] // LANES
    for g, d in enumerate(dilations):
        if d == 1:
            continue
        per = tm // d
        for r in range(d):
            lt_ref[g, pl.ds(r, per, stride=d), :] = l_refs[g][r]
            for k in range(nblk):
                ot_ref[g, k, pl.ds(r, per, stride=d), :] = o_refs[g][r, :, k * LANES:(k + 1) * LANES]
    lses = [l_refs[g][0] if d == 1 else lt_ref[g] for g, d in enumerate(dilations)]
    m = jnp.maximum(jnp.maximum(lses[0], lses[1]), lses[2])
    es = [jnp.exp2(l - m) for l in lses]
    inv = 1.0 / (es[0] + es[1] + es[2])
    ex = ex_ref[...]
    alphas = [_expand_heads(e * inv, ex) for e in es[:-1]]
    alphas.append(1.0 - alphas[0] - alphas[1])
    for k in range(nblk):
        cols = slice(k * LANES, (k + 1) * LANES)
        y = jnp.zeros((tm, LANES), F32)
        for g, d in enumerate(dilations):
            og = o_refs[g][0, :, cols] if d == 1 else ot_ref[g, k]
            y += alphas[g][:, cols] * og
        gate = gate_ref[:, cols].astype(F32)
        y_ref[:, cols] = (y * gate * (1.0 / (1.0 + jnp.exp(-gate)))).astype(y_ref.dtype)
    acc = jnp.dot(y_ref[...], w_ref[...], preferred_element_type=F32)
    hout_ref[...] = h_ref[...] + mod_ref[...] * acc


def merge_out_proj(outs, lses, gate, w_out, mod_gate, h):
    b, s, w = gate.shape
    d_model = h.shape[-1]
    tm = min(s, 256)
    dil = tuple(o.shape[1] for o in outs)
    expand = (jnp.arange(LANES)[:, None] == (jnp.arange(w)[None, :] // HEAD_DIM)).astype(BF16)

    def cm(d, width):
        return pl.BlockSpec((None, d, tm // d, width), lambda bi, i: (bi, 0, i, 0))

    def tspec(width):
        return pl.BlockSpec((None, tm, width), lambda bi, i: (bi, i, 0))

    return pl.pallas_call(
        functools.partial(_merge_out_kernel, dilations=dil),
        grid=(b, s // tm),
        in_specs=[cm(d, w) for d in dil] + [cm(d, LANES) for d in dil]
                 + [tspec(w), pl.BlockSpec((LANES, w), lambda bi, i: (0, 0)),
                    pl.BlockSpec((w, d_model), lambda bi, i: (0, 0)),
                    pl.BlockSpec((None, 1, d_model), lambda bi, i: (bi, 0, 0)),
                    tspec(d_model)],
        out_specs=tspec(d_model),
        out_shape=jax.ShapeDtypeStruct((b, s, d_model), F32),
        scratch_shapes=[pltpu.VMEM((tm, w), BF16),
                        pltpu.VMEM((len(dil), w // LANES, tm, LANES), F32),
                        pltpu.VMEM((len(dil), tm, LANES), F32)],
        compiler_params=_cparams(("arbitrary", "arbitrary")),
        name="merge_out_proj",
    )(*outs, *lses, gate, expand, w_out, mod_gate.reshape(b, 1, d_model), h)


def dilated_layer(h, g, scale, shift, mod_gate, w_in, w_out, cos, sin):
    tn = 512
    per = A_WIDTH // tn
    qkv_flags = (True,) * (2 * per) + (False,) * per
    n_groups = len(A_GROUPS)
    col = np.arange(w_in.shape[1])
    is_q = (col < n_groups * 3 * A_WIDTH) & ((col // A_WIDTH) % 3 == 0)
    w = (w_in * jnp.asarray(np.where(is_q, Q_PRESCALE, 1.0), F32)).astype(BF16)
    outs, lses = [], []
    gate = None
    for gi, (window, dilation) in enumerate(A_GROUPS):
        assert window // dilation == A_BLK
        plain = (A_WIDTH, 3 * n_groups) if gi == 0 else None
        qkv = in_proj(h, g, scale, shift, w, qkv_flags, cos, sin, tn, dilation,
                      w_block=gi, plain=plain)
        if plain is not None:
            qkv, gate = qkv
        if dilation == 1:
            qkv = qkv[:, None]
        o, lse = banded_group_attention(qkv)
        outs.append(o)
        lses.append(lse)
    return merge_out_proj(outs, lses, gate, w_out.astype(BF16), mod_gate, h)


B_COLS = 3584
B_ROPE_FLAGS = (1, 1, 0, 0, 1, 1, 0)


def _dsa_head_perm():
    group = B_HEADS // B_KV_HEADS
    order = []
    for g2 in range(B_KV_HEADS // 2):
        for r in range(group):
            order += [(2 * g2) * group + r, (2 * g2 + 1) * group + r]
    return np.asarray(order)


def _dsa_weights(w_in, w_out):
    d = w_in.shape[0]
    cuts = np.cumsum((B_WIDTH, B_KV_HEADS * HEAD_DIM, B_KV_HEADS * HEAD_DIM, IDX_HEADS * HEAD_DIM,
                      HEAD_DIM, IDX_HEADS, B_WIDTH))[:-1]
    wq, wk, wv, wqi, wki, wwi, wg = jnp.split(w_in, cuts, axis=1)
    cols = (_dsa_head_perm()[:, None] * HEAD_DIM + np.arange(HEAD_DIM)[None, :]).reshape(-1)
    zeros = lambda n: jnp.zeros((d, n), w_in.dtype)
    wq = wq * Q_PRESCALE
    w = jnp.concatenate([wq[:, cols], wg[:, cols], wqi, wk, wki, wki, zeros(LANES),
                         wwi, zeros(2 * LANES - IDX_HEADS), wv], axis=1)
    assert w.shape[1] == B_COLS
    return w.astype(BF16), w_out[cols, :].astype(BF16)


def _sortable(x):
    bits = pltpu.bitcast(x, I32)
    return bits ^ ((bits >> 31) & jnp.int32(0x7FFFFFFF))


def _bit_transpose32(words):
    a = list(words)
    j, m = 16, 0x0000FFFF
    while j:
        mask = jnp.int32(m - (1 << 32) if m >= (1 << 31) else m)
        shift = jnp.full(a[0].shape, j, I32)
        k = 0
        while k < 32:
            t = (a[k] ^ lax.shift_right_logical(a[k + j], shift)) & mask
            a[k] = a[k] ^ t
            a[k + j] = a[k + j] ^ lax.shift_left(t, shift)
            k = (k + j + 1) & ~j
        j >>= 1
        m = (m ^ (m << j)) & 0xFFFFFFFF
    return a


def _sublane_allsum(x):
    x = x + pltpu.roll(x, 4, 0)
    x = x + pltpu.roll(x, 2, 0)
    return x + pltpu.roll(x, 1, 0)


def _tree_sum(parts):
    parts = list(parts)
    while len(parts) > 1:
        odd = [parts[-1]] if len(parts) % 2 else []
        parts = [parts[j] + parts[j + 1] for j in range(0, len(parts) - 1, 2)] + odd
    return parts[0]


GROUP_KEYS = 256
ATTN_LOOKAHEAD = 1


def _dsa_kernel(q_ref, gate_ref, qi_ref, k_ref, ki_ref, wi_ref, v_ref, y_ref,
                keys_ref, planes_ref, bias_ref, kt_ref, qs_ref, s0_ref, m_ref, l_ref, acc_ref,
                *, k_sel):
    i = pl.program_id(1)
    seq = keys_ref.shape[1]
    n_sub = Q_BLOCK // SEL_LANES
    n_pairs = B_HEADS // 2
    n_groups_max = seq // GROUP_KEYS
    lane_q = lax.broadcasted_iota(I32, (Q_BLOCK, LANES), 1)
    first_head_q = lane_q < HEAD_DIM
    first_head = lax.broadcasted_iota(I32, (SEL_LANES, LANES), 1) < HEAD_DIM
    key_row = lax.broadcasted_iota(I32, (KEY_TILE, SEL_LANES), 0)
    lane_idx = lax.broadcasted_iota(I32, (KEY_TILE, SEL_LANES), 1)

    @pl.when(i == 0)
    def _():
        planes_ref[...] = jnp.zeros(planes_ref.shape, I32)
        keys_ref[...] = jnp.zeros(keys_ref.shape, I32)

    def tiles_of(a):
        return ((i * n_sub + a) * SEL_LANES) // KEY_TILE + 1

    def select(a):
        n_tiles = tiles_of(a)
        rows_a = slice(a * SEL_LANES, (a + 1) * SEL_LANES)
        q_pos = (i * n_sub + a) * SEL_LANES + lane_idx
        keys = keys_ref.at[a]

        w_t = (wi_ref[rows_a, :].astype(F32) * (IDX_HEADS ** -0.5 * HEAD_DIM ** -0.5)).T
        qi_stacked = [_stack_heads(qi_ref[rows_a, p * LANES:(p + 1) * LANES], first_head)
                      for p in range(IDX_HEADS // 2)]

        def score_tile(t):
            base = pl.multiple_of(t * KEY_TILE, KEY_TILE)
            kk = ki_ref[pl.ds(base, KEY_TILE), :]
            score = jnp.zeros((KEY_TILE, SEL_LANES), F32)
            for p in range(IDX_HEADS // 2):
                sc = lax.dot_general(kk, qi_stacked[p], (((1,), (1,)), ((), ())),
                                     preferred_element_type=F32)
                score += jnp.maximum(sc[:, :SEL_LANES], 0.0) * w_t[2 * p:2 * p + 1, :]
                score += jnp.maximum(sc[:, SEL_LANES:], 0.0) * w_t[2 * p + 1:2 * p + 2, :]
            causal = (key_row + base) <= q_pos
            score = jnp.where(causal, score + 0.0, -jnp.inf)
            key = _sortable(score)
            keys[pl.ds(base, KEY_TILE), :] = key
            ukey = key ^ jnp.int32(INT_MIN)
            for g in range(KEY_TILE // GROUP_KEYS):
                words = [ukey[g * GROUP_KEYS + 8 * j:g * GROUP_KEYS + 8 * j + 8, :] for j in range(32)]
                rows = pl.ds(pl.multiple_of(t * (KEY_TILE // 32) + 8 * g, 8), 8)
                for b, plane in enumerate(_bit_transpose32(words)):
                    planes_ref[b, rows, :] = plane

        def score_two(t2, carry):
            score_tile(2 * t2)
            score_tile(2 * t2 + 1)
            return carry

        lax.fori_loop(0, n_tiles // 2, score_two, 0)

        @pl.when(n_tiles % 2 == 1)
        def _():
            score_tile(n_tiles - 1)

        def select_threshold():
            n_groups = n_tiles * (KEY_TILE // GROUP_KEYS)
            alive0 = tuple(jnp.full((8, SEL_LANES), -1, I32) * (g < n_groups).astype(I32)
                           for g in range(n_groups_max))
            need0 = jnp.full((8, SEL_LANES), k_sel, I32)

            def step(st, carry):
                alive, need, prefix = carry
                planes = [planes_ref[st, 8 * g:8 * g + 8, :] for g in range(n_groups_max)]
                ones = [al & pln for al, pln in zip(alive, planes)]
                cnt = _sublane_allsum(_tree_sum([lax.population_count(o) for o in ones]))
                take = cnt >= need
                prefix = jnp.where(take, prefix | jnp.left_shift(jnp.int32(1), 31 - st), prefix)
                need = jnp.where(take, need, need - cnt)
                flip = jnp.where(take, 0, -1)
                alive = tuple(al & (pln ^ flip) for al, pln in zip(alive, planes))
                return alive, need, prefix

            alive, need, prefix = lax.fori_loop(
                0, 32, step, (alive0, need0, jnp.zeros((8, SEL_LANES), I32)))
            thr = (prefix ^ jnp.int32(INT_MIN))[:1, :]
            need_eq = need[:1, :]

            n_eq = _sublane_allsum(_tree_sum([lax.population_count(al) for al in alive]))[:1, :]
            tied = (n_eq > need_eq) & (thr > KEY_NEG_INF)
            idx_bits = (seq - 1).bit_length()

            def tie_break():
                def count_eq_before(cut):
                    def body(t, acc):
                        base = pl.multiple_of(t * KEY_TILE, KEY_TILE)
                        hit = (keys[pl.ds(base, KEY_TILE), :] == thr) & ((key_row + base) < cut)
                        return acc + jnp.sum(hit.astype(I32).reshape(KEY_TILE // 8, 8, SEL_LANES), axis=0)
                    acc = lax.fori_loop(0, n_tiles, body, jnp.zeros((8, SEL_LANES), I32))
                    return jnp.sum(acc, axis=0, keepdims=True)

                def bit_step(it, cut):
                    cand = cut + jnp.left_shift(jnp.int32(1), idx_bits - 1 - it)
                    return jnp.where(count_eq_before(cand) < need_eq, cand, cut)

                return lax.fori_loop(0, idx_bits, bit_step, jnp.zeros((1, SEL_LANES), I32))

            any_tied = jnp.max(tied.astype(I32)) > 0
            idx_cut = lax.cond(any_tied, tie_break, lambda: jnp.full((1, SEL_LANES), seq, I32))
            return thr, idx_cut

        all_selected = (i * n_sub + a + 1) * SEL_LANES <= k_sel
        thr, idx_cut = lax.cond(
            all_selected,
            lambda: (jnp.full((1, SEL_LANES), INT_MIN, I32), jnp.full((1, SEL_LANES), seq, I32)),
            select_threshold)
        return thr, idx_cut, q_pos

    selected = [select(a) for a in range(n_sub)]
    n_tiles = tiles_of(n_sub - 1)

    def write_bias(base):
        k_idx = key_row + base
        for a, (thr, idx_cut, q_pos) in enumerate(selected):
            key = keys_ref[a, pl.ds(base, KEY_TILE), :]
            sel = ((key > thr) | ((key == thr) & (k_idx <= idx_cut))) & (k_idx <= q_pos)
            bias_t = jnp.where(sel, 0.0, -jnp.inf).astype(F32)
            for c in range(KEY_TILE // LANES):
                col = base + c * LANES
                col = col if isinstance(col, int) else pl.multiple_of(col, LANES)
                bias_ref[a * SEL_LANES:(a + 1) * SEL_LANES, pl.ds(col, LANES)] = \
                    bias_t[c * LANES:(c + 1) * LANES, :].T

    write_bias(0)

    for p in range(n_pairs):
        cols = slice(p * LANES, (p + 1) * LANES)
        qs_ref[p] = _stack_heads(q_ref[:, cols], first_head_q)
    heads_per_kv_pair = 2 * (B_HEADS // B_KV_HEADS) // 2

    ones_blk = jnp.ones((KEY_TILE, LANES), BF16)

    @pl.when(i == 0)
    def _():
        def xpose(r, carry):
            rows = pl.ds(pl.multiple_of(r * LANES, LANES), LANES)
            for c in range(kt_ref.shape[0]):
                blk = k_ref[rows, c * LANES:(c + 1) * LANES].astype(F32)
                kt_ref[c, :, rows] = blk.T.astype(BF16)
            return carry
        lax.fori_loop(0, seq // LANES, xpose, 0)

    def logits(base, p):
        kt = kt_ref[p // heads_per_kv_pair, :, pl.ds(base, KEY_TILE)]
        return jnp.dot(qs_ref[p], kt, preferred_element_type=F32)

    lookahead = s0_ref.shape[0]
    for u in range(lookahead):
        s0_ref[u] = logits(0, u)

    def attn_tiles(t0, count, start=False):
        bases = [pl.multiple_of(jnp.minimum(t0 + dt, n_tiles - 1) * KEY_TILE, KEY_TILE)
                 for dt in range(count + 1)]
        units = [(dt, p) for dt in range(count + 1) for p in range(n_pairs)]
        queue = [s0_ref[u] for u in range(lookahead)]
        for u, (dt, p) in enumerate(units[:count * n_pairs]):
            base = bases[dt]
            if p == 0:
                bias = bias_ref[:, pl.ds(base, KEY_TILE)]
                write_bias(bases[dt + 1])
                bias2 = jnp.concatenate([bias, bias], axis=0)
            kv = p // heads_per_kv_pair
            kv_cols = slice(kv * LANES, (kv + 1) * LANES)
            s = queue.pop(0) + bias2
            ahead_dt, ahead_p = units[u + lookahead]
            queue.append(logits(bases[ahead_dt], ahead_p))
            m_blk = s[:, :LANES]
            for c in range(1, KEY_TILE // LANES):
                m_blk = jnp.maximum(m_blk, s[:, c * LANES:(c + 1) * LANES])
            opening = start and dt == 0
            m_cur = jnp.max(m_blk, axis=-1, keepdims=True)
            if opening:
                m_new = jnp.broadcast_to(jnp.maximum(m_cur, F32_MIN), m_blk.shape)
            else:
                m_old = m_ref[p]
                m_new = jnp.maximum(m_old, m_cur)
                alpha = jnp.exp2(m_old - m_new)
            e = jnp.concatenate(
                [jnp.exp2(s[:, c * LANES:(c + 1) * LANES] - m_new).astype(BF16)
                 for c in range(KEY_TILE // LANES)], axis=1)
            v_ext = jnp.concatenate([v_ref[pl.ds(base, KEY_TILE), kv_cols], ones_blk], axis=1)
            pv = jnp.dot(e, v_ext, preferred_element_type=F32)
            if opening:
                acc_ref[p] = pv[:, :LANES]
                l_ref[p] = pv[:, LANES:]
            else:
                acc_ref[p] = acc_ref[p] * alpha + pv[:, :LANES]
                l_ref[p] = l_ref[p] * alpha + pv[:, LANES:]
            m_ref[p] = m_new
        for u in range(lookahead):
            s0_ref[u] = queue[u]

    attn_tiles(0, 1, start=True)

    def attn_two(t2, carry):
        attn_tiles(1 + 2 * t2, 2)
        return carry

    lax.fori_loop(0, (n_tiles - 1) // 2, attn_two, 0)

    @pl.when(n_tiles % 2 == 0)
    def _():
        attn_tiles(n_tiles - 1, 1)

    for p in range(n_pairs):
        cols = slice(p * LANES, (p + 1) * LANES)
        l = l_ref[p]
        acc = acc_ref[p]
        num = jnp.where(first_head_q, acc[:Q_BLOCK], acc[Q_BLOCK:])
        den = jnp.where(first_head_q, l[:Q_BLOCK], l[Q_BLOCK:])
        gate = gate_ref[:, cols].astype(F32)
        y_ref[:, cols] = (num * gate / (den * (1.0 + jnp.exp(-gate)))).astype(y_ref.dtype)


def dsa_attention(proj):
    b, s, _ = proj.shape
    k_sel = min(TOPK_MAX, s // 4)
    assert s % KEY_TILE == 0 and s % Q_BLOCK == 0 and k_sel <= KEY_TILE
    kvw = B_KV_HEADS * HEAD_DIM
    n_pairs = B_HEADS // 2

    def qblock(width, idx):
        return pl.BlockSpec((None, Q_BLOCK, width), lambda bi, i: (bi, i, idx))

    def full(width, idx):
        return pl.BlockSpec((None, s, width), lambda bi, i: (bi, 0, idx))

    return pl.pallas_call(
        functools.partial(_dsa_kernel, k_sel=k_sel),
        grid=(b, s // Q_BLOCK),
        in_specs=[
            qblock(B_WIDTH, 0),
            qblock(B_WIDTH, 1),
            qblock(IDX_HEADS * HEAD_DIM, 4),
            full(kvw, 10),
            full(LANES, 22),
            qblock(LANES, 24),
            full(kvw, 13),
        ],
        out_specs=qblock(B_WIDTH, 0),
        out_shape=jax.ShapeDtypeStruct((b, s, B_WIDTH), BF16),
        scratch_shapes=[
            pltpu.VMEM((Q_BLOCK // SEL_LANES, s, SEL_LANES), I32),
            pltpu.VMEM((32, s // 32, SEL_LANES), I32),
            pltpu.VMEM((Q_BLOCK, s), F32),
            pltpu.VMEM((kvw // LANES, LANES, s), BF16),
            pltpu.VMEM((n_pairs, 2 * Q_BLOCK, LANES), BF16),
            pltpu.VMEM((ATTN_LOOKAHEAD, 2 * Q_BLOCK, KEY_TILE), F32),
            pltpu.VMEM((n_pairs, 2 * Q_BLOCK, LANES), F32),
            pltpu.VMEM((n_pairs, 2 * Q_BLOCK, LANES), F32),
            pltpu.VMEM((n_pairs, 2 * Q_BLOCK, LANES), F32),
        ],
        compiler_params=_cparams(("arbitrary", "arbitrary")),
        name="dsa_attention",
    )(proj, proj, proj, proj, proj, proj, proj)


def dsa_mixer(h, g, scale, shift, w_in_packed, cos, sin):
    proj = in_proj(h, g, scale, shift, w_in_packed, [f == 1 for f in B_ROPE_FLAGS], cos, sin, 512)
    return dsa_attention(proj)


def kernel(x, c, positions, norm_g, ada_w, ada_b, a_w_in, a_w_out, b_w_in, b_w_out, final_g):
    depth = norm_g.shape[0]
    d = x.shape[-1]
    mod = adaln_mod(c, ada_w, ada_b)
    cos, sin = rope_tables(positions)
    h = x
    for i in range(depth):
        shift, scale, gate = mod[i, :, :d], mod[i, :, d:2 * d], mod[i, :, 2 * d:]
        last = i == depth - 1
        if i % 2 == 0:
            h = dilated_layer(h, norm_g[i], scale, shift, gate, a_w_in[i // 2], a_w_out[i // 2], cos, sin)
            if last:
                h = final_norm(h, final_g)
        else:
            w_in, w_out = _dsa_weights(b_w_in[i // 2], b_w_out[i // 2])
            y = dsa_mixer(h, norm_g[i], scale, shift, w_in, cos, sin)
            h = out_proj(y, w_out, gate, h, final_g if last else None)
    return h
```

```python
import functools

import numpy as np
import jax
import jax.numpy as jnp
from jax import lax
from jax.experimental import pallas as pl
from jax.experimental.pallas import tpu as pltpu

F32 = jnp.float32
BF16 = jnp.bfloat16
I32 = jnp.int32

LANES = 128
HEAD_DIM = 64
HALF = HEAD_DIM // 2
ROPE_THETA = 10000.0
NORM_EPS = 1e-6
VMEM_LIMIT = 56 * 1024 * 1024
ROW_CHUNK = 256

A_HEADS = 16
A_WIDTH = A_HEADS * HEAD_DIM
A_GROUPS = ((128, 1), (512, 4), (2048, 16))
A_BLK = 128
BANDED_LOOKAHEAD = 1

B_HEADS = 16
B_KV_HEADS = 4
B_WIDTH = B_HEADS * HEAD_DIM
IDX_HEADS = 8
TOPK_MAX = 256
Q_BLOCK = 256
SEL_LANES = 128
KEY_TILE = 512
INT_MIN = -(2 ** 31)
KEY_NEG_INF = -2139095041
F32_MIN = float(np.finfo(np.float32).min)
LOG2_E = float(np.log2(np.e))
Q_PRESCALE = LOG2_E * HEAD_DIM ** -0.5


def _cparams(sem):
    return pltpu.CompilerParams(dimension_semantics=sem, vmem_limit_bytes=VMEM_LIMIT)


def _adaln_kernel(c_ref, w_ref, b_ref, o_ref):
    c = c_ref[...]
    ca = (c * (1.0 / (1.0 + jnp.exp(-c)))).astype(BF16)
    acc = jnp.dot(ca, w_ref[...].astype(BF16), preferred_element_type=F32)
    o_ref[...] = acc + b_ref[...]


def adaln_mod(c, ada_w, ada_b):
    depth, d, d3 = ada_w.shape
    b = c.shape[0]
    tn = 1024
    return pl.pallas_call(
        _adaln_kernel,
        grid=(depth, d3 // tn),
        in_specs=[
            pl.BlockSpec((b, d), lambda i, j: (0, 0)),
            pl.BlockSpec((None, d, tn), lambda i, j: (i, 0, j)),
            pl.BlockSpec((None, 1, tn), lambda i, j: (i, 0, j)),
        ],
        out_specs=pl.BlockSpec((None, b, tn), lambda i, j: (i, 0, j)),
        out_shape=jax.ShapeDtypeStruct((depth, b, d3), F32),
        compiler_params=_cparams(("arbitrary", "arbitrary")),
        name="adaln_mod",
    )(c, ada_w, ada_b.reshape(depth, 1, d3))


def _rope_table_kernel(pos_ref, inv_ref, cos_ref, sin_ref):
    ang = pos_ref[...].astype(F32) * inv_ref[...]
    lane = lax.broadcasted_iota(I32, ang.shape, 1)
    first_half = (lane % HEAD_DIM) < HALF
    cos_ref[...] = jnp.cos(ang)
    s = jnp.sin(ang)
    sin_ref[...] = jnp.where(first_half, -s, s)


def rope_tables(positions):
    b, s = positions.shape
    inv_freq = ROPE_THETA ** (-jnp.arange(HALF, dtype=F32) / HALF)
    inv_lane = jnp.tile(inv_freq, LANES // HALF).reshape(1, LANES)
    ts = min(s, 1024)
    out = jax.ShapeDtypeStruct((b, s, LANES), F32)
    return pl.pallas_call(
        _rope_table_kernel,
        grid=(b, s // ts),
        in_specs=[
            pl.BlockSpec((None, ts, 1), lambda i, j: (i, j, 0)),
            pl.BlockSpec((1, LANES), lambda i, j: (0, 0)),
        ],
        out_specs=[pl.BlockSpec((None, ts, LANES), lambda i, j: (i, j, 0))] * 2,
        out_shape=[out, out],
        compiler_params=_cparams(("arbitrary", "arbitrary")),
        name="rope_tables",
    )(positions.reshape(b, s, 1), inv_lane)


def _rope_block(t, cos, sin_signed, first_half):
    partner = jnp.where(first_half, pltpu.roll(t, LANES - HALF, 1), pltpu.roll(t, HALF, 1))
    return t * cos + partner * sin_signed


def _in_proj_kernel(x_ref, g_ref, sc_ref, sh_ref, w_ref, cos_ref, sin_ref, *rest,
                    dilation, rope_tiles, tn, plain_cols):
    if plain_cols:
        w2_ref, o_ref, o2_ref, u_ref, *stage = rest
    else:
        o_ref, u_ref, *stage = rest
    tm = u_ref.shape[0]
    nblk = tn // LANES
    per = ROW_CHUNK // dilation
    x = x_ref[...]
    ms = jnp.mean(x * x, axis=-1, keepdims=True)
    xn = x * lax.rsqrt(ms + NORM_EPS)
    u_ref[...] = (xn * g_ref[...] * (1.0 + sc_ref[...]) + sh_ref[...]).astype(BF16)
    lane = lax.broadcasted_iota(I32, (per, LANES), 1)
    first_half = (lane % HEAD_DIM) < HALF
    step = 0
    for j, rope in enumerate(rope_tiles):
        for c in range(tm // ROW_CHUNK):
            acc = jnp.dot(u_ref[c * ROW_CHUNK:(c + 1) * ROW_CHUNK, :], w_ref[:, j * tn:(j + 1) * tn],
                          preferred_element_type=F32)
            if dilation > 1:
                slot = step % stage[0].shape[0]
                step += 1
                for k in range(nblk):
                    stage[0][slot, k] = acc[:, k * LANES:(k + 1) * LANES]
            for r in range(dilation):
                rows = slice(c * per, (c + 1) * per)
                if rope:
                    trows = rows if dilation == 1 else pl.ds(c * ROW_CHUNK + r, per, stride=dilation)
                    cos, sin = cos_ref[trows, :], sin_ref[trows, :]
                for k in range(nblk):
                    cols = slice(j * tn + k * LANES, j * tn + (k + 1) * LANES)
                    if dilation > 1:
                        blk = stage[0][slot, k, pl.ds(r, per, stride=dilation), :]
                    else:
                        blk = acc[:, k * LANES:(k + 1) * LANES]
                    if rope:
                        blk = _rope_block(blk, cos, sin, first_half)
                    if dilation > 1:
                        o_ref[r, rows, cols] = blk.astype(o_ref.dtype)
                    else:
                        o_ref[rows, cols] = blk.astype(o_ref.dtype)
    for j in range(plain_cols // tn):
        for c in range(tm // ROW_CHUNK):
            rows = slice(c * ROW_CHUNK, (c + 1) * ROW_CHUNK)
            acc = jnp.dot(u_ref[rows, :], w2_ref[:, j * tn:(j + 1) * tn], preferred_element_type=F32)
            o2_ref[rows, j * tn:(j + 1) * tn] = acc.astype(o2_ref.dtype)


def in_proj(h, g, scale, shift, w, rope_tiles, cos, sin, tn, dilation=1, w_block=0, plain=None):
    b, s, d = h.shape
    n = len(rope_tiles) * tn
    tm = min(s, 512)
    assert tm % ROW_CHUNK == 0 and ROW_CHUNK % (16 * dilation) == 0 and (w_block + 1) * n <= w.shape[1]
    scratch = [pltpu.VMEM((tm, d), BF16)]
    tab_spec = pl.BlockSpec((None, tm, LANES), lambda bi, i: (bi, i, 0))
    if dilation == 1:
        out_spec = pl.BlockSpec((None, tm, n), lambda bi, i: (bi, i, 0))
        out_shape = jax.ShapeDtypeStruct((b, s, n), BF16)
    else:
        out_spec = pl.BlockSpec((None, dilation, tm // dilation, n), lambda bi, i: (bi, 0, i, 0))
        out_shape = jax.ShapeDtypeStruct((b, dilation, s // dilation, n), BF16)
        scratch.append(pltpu.VMEM((2, tn // LANES, ROW_CHUNK, LANES), F32))
    in_specs = [
        pl.BlockSpec((None, tm, d), lambda bi, i: (bi, i, 0)),
        pl.BlockSpec((1, d), lambda bi, i: (0, 0)),
        pl.BlockSpec((None, 1, d), lambda bi, i: (bi, 0, 0)),
        pl.BlockSpec((None, 1, d), lambda bi, i: (bi, 0, 0)),
        pl.BlockSpec((d, n), lambda bi, i: (0, w_block)),
        tab_spec, tab_spec,
    ]
    args = [h, g.reshape(1, d), scale.reshape(b, 1, d), shift.reshape(b, 1, d), w, cos, sin]
    plain_cols = 0
    if plain is not None:
        plain_cols, plain_block = plain
        assert plain_cols % tn == 0 and (plain_block + 1) * plain_cols <= w.shape[1]
        in_specs.append(pl.BlockSpec((d, plain_cols), lambda bi, i: (0, plain_block)))
        args.append(w)
        out_spec = [out_spec, pl.BlockSpec((None, tm, plain_cols), lambda bi, i: (bi, i, 0))]
        out_shape = [out_shape, jax.ShapeDtypeStruct((b, s, plain_cols), BF16)]
    return pl.pallas_call(
        functools.partial(_in_proj_kernel, dilation=dilation, rope_tiles=tuple(rope_tiles), tn=tn,
                          plain_cols=plain_cols),
        grid=(b, s // tm),
        in_specs=in_specs,
        out_specs=out_spec,
        out_shape=out_shape,
        scratch_shapes=scratch,
        compiler_params=_cparams(("arbitrary", "arbitrary")),
        name=f"in_proj_d{dilation}",
    )(*args)


def _out_proj_kernel(y_ref, w_ref, gate_ref, h_ref, *rest):
    acc = jnp.dot(y_ref[...], w_ref[...], preferred_element_type=F32)
    h_new = h_ref[...] + gate_ref[...] * acc
    if len(rest) == 2:
        g_ref, o_ref = rest
        ms = jnp.mean(h_new * h_new, axis=-1, keepdims=True)
        o_ref[...] = h_new * lax.rsqrt(ms + NORM_EPS) * g_ref[...]
    else:
        rest[0][...] = h_new


def out_proj(y, w, gate, h, final_g=None):
    b, s, d = h.shape
    k = y.shape[-1]
    tm = min(s, 1024)
    in_specs = [
        pl.BlockSpec((None, tm, k), lambda bi, i: (bi, i, 0)),
        pl.BlockSpec((k, d), lambda bi, i: (0, 0)),
        pl.BlockSpec((None, 1, d), lambda bi, i: (bi, 0, 0)),
        pl.BlockSpec((None, tm, d), lambda bi, i: (bi, i, 0)),
    ]
    args = [y, w, gate.reshape(b, 1, d), h]
    if final_g is not None:
        in_specs.append(pl.BlockSpec((1, d), lambda bi, i: (0, 0)))
        args.append(final_g.reshape(1, d))
    return pl.pallas_call(
        _out_proj_kernel,
        grid=(b, s // tm),
        in_specs=in_specs,
        out_specs=pl.BlockSpec((None, tm, d), lambda bi, i: (bi, i, 0)),
        out_shape=jax.ShapeDtypeStruct((b, s, d), F32),
        compiler_params=_cparams(("arbitrary", "arbitrary")),
        name="out_proj",
    )(*args)


def _final_norm_kernel(x_ref, g_ref, o_ref):
    x = x_ref[...]
    ms = jnp.mean(x * x, axis=-1, keepdims=True)
    o_ref[...] = x * lax.rsqrt(ms + NORM_EPS) * g_ref[...]


def final_norm(h, g):
    b, s, d = h.shape
    tm = min(s, 1024)
    return pl.pallas_call(
        _final_norm_kernel,
        grid=(b, s // tm),
        in_specs=[
            pl.BlockSpec((None, tm, d), lambda bi, i: (bi, i, 0)),
            pl.BlockSpec((1, d), lambda bi, i: (0, 0)),
        ],
        out_specs=pl.BlockSpec((None, tm, d), lambda bi, i: (bi, i, 0)),
        out_shape=jax.ShapeDtypeStruct((b, s, d), F32),
        compiler_params=_cparams(("arbitrary", "arbitrary")),
        name="final_norm",
    )(h, g.reshape(1, d))


def _stack_heads(pair, first_head):
    zero = jnp.zeros_like(pair)
    return jnp.concatenate([jnp.where(first_head, pair, zero), jnp.where(first_head, zero, pair)], axis=0)


def _banded_kernel(q_ref, kp_ref, kc_ref, vp_ref, vc_ref, o_ref, lse_ref, kwin_ref, vwin_ref, *, tq):
    i = pl.program_id(2)
    n_cls = q_ref.shape[0]
    nsub = tq // A_BLK
    lane = lax.broadcasted_iota(I32, (A_BLK, LANES), 1)
    first_head = lane < HEAD_DIM
    odd_lane = (lane % 2) == 1
    r2 = lax.broadcasted_iota(I32, (2 * A_BLK, 2 * A_BLK), 0) % A_BLK
    c2 = lax.broadcasted_iota(I32, (2 * A_BLK, 2 * A_BLK), 1)
    own_ok = (c2 >= A_BLK) & (c2 - A_BLK <= r2)
    ones_blk = jnp.ones((2 * A_BLK, LANES), BF16)
    n_pairs = A_HEADS // 2

    kwin_ref[:, :A_BLK, :] = kp_ref[...]
    kwin_ref[:, A_BLK:, :] = kc_ref[...]
    vwin_ref[:, :A_BLK, :] = vp_ref[...]
    vwin_ref[:, A_BLK:, :] = vc_ref[...]

    def logits(r, a, p):
        cols = slice(p * LANES, (p + 1) * LANES)
        qs = _stack_heads(q_ref[r, a * A_BLK:(a + 1) * A_BLK, cols], first_head)
        k2 = kwin_ref[r, a * A_BLK:(a + 2) * A_BLK, cols]
        return lax.dot_general(qs, k2, (((1,), (1,)), ((), ())), preferred_element_type=F32)

    def band_bias(prev_shift):
        mask = own_ok | ((c2 < A_BLK) & (c2 >= r2 + prev_shift))
        return jnp.where(mask, 0.0, -jnp.inf).astype(F32)

    bias_inner = band_bias(0)
    biases = [band_bias(jnp.where(i > 0, 0, 2 * A_BLK))] + [bias_inner] * (nsub - 1)

    units = [(r, a, p) for r in range(n_cls) for a in range(nsub) for p in range(n_pairs)]
    queue = [logits(*units[u]) for u in range(BANDED_LOOKAHEAD)]
    lse_tile = None
    for u, (r, a, p) in enumerate(units):
        rows = slice(a * A_BLK, (a + 1) * A_BLK)
        cols = slice(p * LANES, (p + 1) * LANES)
        bias = biases[a]
        if p == 0:
            lse_tile = jnp.zeros((A_BLK, LANES), F32)
        s = queue.pop(0) + bias
        if u + BANDED_LOOKAHEAD < len(units):
            queue.append(logits(*units[u + BANDED_LOOKAHEAD]))
        m = jnp.max(s, axis=-1, keepdims=True)
        e = jnp.exp2(s - m).astype(BF16)
        v_ext = jnp.concatenate([vwin_ref[r, a * A_BLK:(a + 2) * A_BLK, cols], ones_blk], axis=1)
        pv = jnp.dot(e, v_ext, preferred_element_type=F32)
        num = jnp.where(first_head, pv[:A_BLK, :LANES], pv[A_BLK:, :LANES])
        o_ref[r, rows, cols] = num * (1.0 / jnp.where(first_head, pv[:A_BLK, LANES:], pv[A_BLK:, LANES:]))
        den_pair = jnp.where(odd_lane, pv[A_BLK:, LANES:], pv[:A_BLK, LANES:])
        m_pair = jnp.where(odd_lane, m[A_BLK:], m[:A_BLK])
        lse_tile = jnp.where((lane // 2) == p, m_pair + jnp.log(den_pair) * LOG2_E, lse_tile)
        if p == n_pairs - 1:
            lse_ref[r, rows, :] = lse_tile


BANDED_ROWS = 1024


def banded_group_attention(qkv):
    b, d, n, _ = qkv.shape
    tq = min(BANDED_ROWS, n)
    rc = min(BANDED_ROWS // tq, d)
    sub = tq // A_BLK
    assert n % tq == 0 and d % rc == 0

    def cur(blk, width=A_WIDTH):
        return pl.BlockSpec((None, rc, tq, width), lambda bi, r, i: (bi, r, i, blk))

    def prev(blk):
        return pl.BlockSpec((None, rc, A_BLK, A_WIDTH),
                            lambda bi, r, i: (bi, r, jnp.maximum(i * sub - 1, 0), blk))

    return pl.pallas_call(
        functools.partial(_banded_kernel, tq=tq),
        grid=(b, d // rc, n // tq),
        in_specs=[cur(0), prev(1), cur(1), prev(2), cur(2)],
        out_specs=[cur(0), cur(0, LANES)],
        out_shape=[
            jax.ShapeDtypeStruct((b, d, n, A_WIDTH), F32),
            jax.ShapeDtypeStruct((b, d, n, LANES), F32),
        ],
        scratch_shapes=[pltpu.VMEM((rc, tq + A_BLK, A_WIDTH), BF16)] * 2,
        compiler_params=_cparams(("arbitrary", "arbitrary", "arbitrary")),
        name=f"banded_attn_d{d}",
    )(qkv, qkv, qkv, qkv, qkv)


def _expand_heads(x, expand):
    hi = x.astype(BF16)
    lo = (x - hi.astype(F32)).astype(BF16)
    return (jnp.dot(hi, expand, preferred_element_type=F32)
            + jnp.dot(lo, expand, preferred_element_type=F32))


def _merge_out_kernel(o0_ref, o1_ref, o2_ref, l0_ref, l1_ref, l2_ref, gate_ref, ex_ref, w_ref,
                      mod_ref, h_ref, hout_ref, y_ref, ot_ref, lt_ref, *, dilations):
    o_refs, l_refs = (o0_ref, o1_ref, o2_ref), (l0_ref, l1_ref, l2_ref)
    tm = y_ref.shape[0]
    nblk = y_ref.shape[1] // LANES
    for g, d in enumerate(dilations):
        if d == 1:
            continue
        per = tm // d
        for r in range(d):
            lt_ref[g, pl.ds(r, per, stride=d), :] = l_refs[g][r]
            for k in range(nblk):
                ot_ref[g, k, pl.ds(r, per, stride=d), :] = o_refs[g][r, :, k * LANES:(k + 1) * LANES]
    lses = [l_refs[g][0] if d == 1 else lt_ref[g] for g, d in enumerate(dilations)]
    m = jnp.maximum(jnp.maximum(lses[0], lses[1]), lses[2])
    es = [jnp.exp2(l - m) for l in lses]
    inv = 1.0 / (es[0] + es[1] + es[2])
    ex = ex_ref[...]
    alphas = [_expand_heads(e * inv, ex) for e in es[:-1]]
    alphas.append(1.0 - alphas[0] - alphas[1])
    for k in range(nblk):
        cols = slice(k * LANES, (k + 1) * LANES)
        y = jnp.zeros((tm, LANES), F32)
        for g, d in enumerate(dilations):
            og = o_refs[g][0, :, cols] if d == 1 else ot_ref[g, k]
            y += alphas[g][:, cols] * og
        gate = gate_ref[:, cols].astype(F32)
        y_ref[:, cols] = (y * gate * (1.0 / (1.0 + jnp.exp(-gate)))).astype(y_ref.dtype)
    acc = jnp.dot(y_ref[...], w_ref[...], preferred_element_type=F32)
    hout_ref[...] = h_ref[...] + mod_ref[...] * acc


def merge_out_proj(outs, lses, gate, w_out, mod_gate, h):
    b, s, w = gate.shape
    d_model = h.shape[-1]
    tm = min(s, 256)
    dil = tuple(o.shape[1] for o in outs)
    expand = (jnp.arange(LANES)[:, None] == (jnp.arange(w)[None, :] // HEAD_DIM)).astype(BF16)

    def cm(d, width):
        return pl.BlockSpec((None, d, tm // d, width), lambda bi, i: (bi, 0, i, 0))

    def tspec(width):
        return pl.BlockSpec((None, tm, width), lambda bi, i: (bi, i, 0))

    return pl.pallas_call(
        functools.partial(_merge_out_kernel, dilations=dil),
        grid=(b, s // tm),
        in_specs=[cm(d, w) for d in dil] + [cm(d, LANES) for d in dil]
                 + [tspec(w), pl.BlockSpec((LANES, w), lambda bi, i: (0, 0)),
                    pl.BlockSpec((w, d_model), lambda bi, i: (0, 0)),
                    pl.BlockSpec((None, 1, d_model), lambda bi, i: (bi, 0, 0)),
                    tspec(d_model)],
        out_specs=tspec(d_model),
        out_shape=jax.ShapeDtypeStruct((b, s, d_model), F32),
        scratch_shapes=[pltpu.VMEM((tm, w), BF16),
                        pltpu.VMEM((len(dil), w // LANES, tm, LANES), F32),
                        pltpu.VMEM((len(dil), tm, LANES), F32)],
        compiler_params=_cparams(("arbitrary", "arbitrary")),
        name="merge_out_proj",
    )(*outs, *lses, gate, expand, w_out, mod_gate.reshape(b, 1, d_model), h)


def dilated_layer(h, g, scale, shift, mod_gate, w_in, w_out, cos, sin):
    tn = 512
    per = A_WIDTH // tn
    qkv_flags = (True,) * (2 * per) + (False,) * per
    n_groups = len(A_GROUPS)
    col = np.arange(w_in.shape[1])
    is_q = (col < n_groups * 3 * A_WIDTH) & ((col // A_WIDTH) % 3 == 0)
    w = (w_in * jnp.asarray(np.where(is_q, Q_PRESCALE, 1.0), F32)).astype(BF16)
    outs, lses = [], []
    gate = None
    for gi, (window, dilation) in enumerate(A_GROUPS):
        assert window // dilation == A_BLK
        plain = (A_WIDTH, 3 * n_groups) if gi == 0 else None
        qkv = in_proj(h, g, scale, shift, w, qkv_flags, cos, sin, tn, dilation,
                      w_block=gi, plain=plain)
        if plain is not None:
            qkv, gate = qkv
        if dilation == 1:
            qkv = qkv[:, None]
        o, lse = banded_group_attention(qkv)
        outs.append(o)
        lses.append(lse)
    return merge_out_proj(outs, lses, gate, w_out.astype(BF16), mod_gate, h)


B_COLS = 3584
B_ROPE_FLAGS = (1, 1, 0, 0, 1, 1, 0)


def _dsa_head_perm():
    group = B_HEADS // B_KV_HEADS
    order = []
    for g2 in range(B_KV_HEADS // 2):
        for r in range(group):
            order += [(2 * g2) * group + r, (2 * g2 + 1) * group + r]
    return np.asarray(order)


def _dsa_weights(w_in, w_out):
    d = w_in.shape[0]
    cuts = np.cumsum((B_WIDTH, B_KV_HEADS * HEAD_DIM, B_KV_HEADS * HEAD_DIM, IDX_HEADS * HEAD_DIM,
                      HEAD_DIM, IDX_HEADS, B_WIDTH))[:-1]
    wq, wk, wv, wqi, wki, wwi, wg = jnp.split(w_in, cuts, axis=1)
    cols = (_dsa_head_perm()[:, None] * HEAD_DIM + np.arange(HEAD_DIM)[None, :]).reshape(-1)
    zeros = lambda n: jnp.zeros((d, n), w_in.dtype)
    wq = wq * Q_PRESCALE
    w = jnp.concatenate([wq[:, cols], wg[:, cols], wqi, wk, wki, wki, zeros(LANES),
                         wwi, zeros(2 * LANES - IDX_HEADS), wv], axis=1)
    assert w.shape[1] == B_COLS
    return w.astype(BF16), w_out[cols, :].astype(BF16)


def _sortable(x):
    bits = pltpu.bitcast(x, I32)
    return bits ^ ((bits >> 31) & jnp.int32(0x7FFFFFFF))


def _bit_transpose32(words):
    a = list(words)
    j, m = 16, 0x0000FFFF
    while j:
        mask = jnp.int32(m - (1 << 32) if m >= (1 << 31) else m)
        shift = jnp.full(a[0].shape, j, I32)
        k = 0
        while k < 32:
            t = (a[k] ^ lax.shift_right_logical(a[k + j], shift)) & mask
            a[k] = a[k] ^ t
            a[k + j] = a[k + j] ^ lax.shift_left(t, shift)
            k = (k + j + 1) & ~j
        j >>= 1
        m = (m ^ (m << j)) & 0xFFFFFFFF
    return a


def _sublane_allsum(x):
    x = x + pltpu.roll(x, 4, 0)
    x = x + pltpu.roll(x, 2, 0)
    return x + pltpu.roll(x, 1, 0)


def _tree_sum(parts):
    parts = list(parts)
    while len(parts) > 1:
        odd = [parts[-1]] if len(parts) % 2 else []
        parts = [parts[j] + parts[j + 1] for j in range(0, len(parts) - 1, 2)] + odd
    return parts[0]


GROUP_KEYS = 256
ATTN_LOOKAHEAD = 1


def _dsa_kernel(q_ref, gate_ref, qi_ref, k_ref, ki_ref, wi_ref, v_ref, y_ref,
                keys_ref, planes_ref, bias_ref, kt_ref, qs_ref, s0_ref, m_ref, l_ref, acc_ref,
                *, k_sel):
    i = pl.program_id(1)
    seq = keys_ref.shape[1]
    n_sub = Q_BLOCK // SEL_LANES
    n_pairs = B_HEADS // 2
    n_groups_max = seq // GROUP_KEYS
    lane_q = lax.broadcasted_iota(I32, (Q_BLOCK, LANES), 1)
    first_head_q = lane_q < HEAD_DIM
    first_head = lax.broadcasted_iota(I32, (SEL_LANES, LANES), 1) < HEAD_DIM
    key_row = lax.broadcasted_iota(I32, (KEY_TILE, SEL_LANES), 0)
    lane_idx = lax.broadcasted_iota(I32, (KEY_TILE, SEL_LANES), 1)

    @pl.when(i == 0)
    def _():
        planes_ref[...] = jnp.zeros(planes_ref.shape, I32)
        keys_ref[...] = jnp.zeros(keys_ref.shape, I32)

    def tiles_of(a):
        return ((i * n_sub + a) * SEL_LANES) // KEY_TILE + 1

    def select(a):
        n_tiles = tiles_of(a)
        rows_a = slice(a * SEL_LANES, (a + 1) * SEL_LANES)
        q_pos = (i * n_sub + a) * SEL_LANES + lane_idx
        keys = keys_ref.at[a]

        w_t = (wi_ref[rows_a, :].astype(F32) * (IDX_HEADS ** -0.5 * HEAD_DIM ** -0.5)).T
        qi_stacked = [_stack_heads(qi_ref[rows_a, p * LANES:(p + 1) * LANES], first_head)
                      for p in range(IDX_HEADS // 2)]

        def score_tile(t):
            base = pl.multiple_of(t * KEY_TILE, KEY_TILE)
            kk = ki_ref[pl.ds(base, KEY_TILE), :]
            score = jnp.zeros((KEY_TILE, SEL_LANES), F32)
            for p in range(IDX_HEADS // 2):
                sc = lax.dot_general(kk, qi_stacked[p], (((1,), (1,)), ((), ())),
                                     preferred_element_type=F32)
                score += jnp.maximum(sc[:, :SEL_LANES], 0.0) * w_t[2 * p:2 * p + 1, :]
                score += jnp.maximum(sc[:, SEL_LANES:], 0.0) * w_t[2 * p + 1:2 * p + 2, :]
            causal = (key_row + base) <= q_pos
            score = jnp.where(causal, score + 0.0, -jnp.inf)
            key = _sortable(score)
            keys[pl.ds(base, KEY_TILE), :] = key
            ukey = key ^ jnp.int32(INT_MIN)
            for g in range(KEY_TILE // GROUP_KEYS):
                words = [ukey[g * GROUP_KEYS + 8 * j:g * GROUP_KEYS + 8 * j + 8, :] for j in range(32)]
                rows = pl.ds(pl.multiple_of(t * (KEY_TILE // 32) + 8 * g, 8), 8)
                for b, plane in enumerate(_bit_transpose32(words)):
                    planes_ref[b, rows, :] = plane

        def score_two(t2, carry):
            score_tile(2 * t2)
            score_tile(2 * t2 + 1)
            return carry

        lax.fori_loop(0, n_tiles // 2, score_two, 0)

        @pl.when(n_tiles % 2 == 1)
        def _():
            score_tile(n_tiles - 1)

        def select_threshold():
            n_groups = n_tiles * (KEY_TILE // GROUP_KEYS)
            alive0 = tuple(jnp.full((8, SEL_LANES), -1, I32) * (g < n_groups).astype(I32)
                           for g in range(n_groups_max))
            need0 = jnp.full((8, SEL_LANES), k_sel, I32)

            def step(st, carry):
                alive, need, prefix = carry
                planes = [planes_ref[st, 8 * g:8 * g + 8, :] for g in range(n_groups_max)]
                ones = [al & pln for al, pln in zip(alive, planes)]
                cnt = _sublane_allsum(_tree_sum([lax.population_count(o) for o in ones]))
                take = cnt >= need
                prefix = jnp.where(take, prefix | jnp.left_shift(jnp.int32(1), 31 - st), prefix)
                need = jnp.where(take, need, need - cnt)
                flip = jnp.where(take, 0, -1)
                alive = tuple(al & (pln ^ flip) for al, pln in zip(alive, planes))
                return alive, need, prefix

            alive, need, prefix = lax.fori_loop(
                0, 32, step, (alive0, need0, jnp.zeros((8, SEL_LANES), I32)))
            thr = (prefix ^ jnp.int32(INT_MIN))[:1, :]
            need_eq = need[:1, :]

            n_eq = _sublane_allsum(_tree_sum([lax.population_count(al) for al in alive]))[:1, :]
            tied = (n_eq > need_eq) & (thr > KEY_NEG_INF)
            idx_bits = (seq - 1).bit_length()

            def tie_break():
                def count_eq_before(cut):
                    def body(t, acc):
                        base = pl.multiple_of(t * KEY_TILE, KEY_TILE)
                        hit = (keys[pl.ds(base, KEY_TILE), :] == thr) & ((key_row + base) < cut)
                        return acc + jnp.sum(hit.astype(I32).reshape(KEY_TILE // 8, 8, SEL_LANES), axis=0)
                    acc = lax.fori_loop(0, n_tiles, body, jnp.zeros((8, SEL_LANES), I32))
                    return jnp.sum(acc, axis=0, keepdims=True)

                def bit_step(it, cut):
                    cand = cut + jnp.left_shift(jnp.int32(1), idx_bits - 1 - it)
                    return jnp.where(count_eq_before(cand) < need_eq, cand, cut)

                return lax.fori_loop(0, idx_bits, bit_step, jnp.zeros((1, SEL_LANES), I32))

            any_tied = jnp.max(tied.astype(I32)) > 0
            idx_cut = lax.cond(any_tied, tie_break, lambda: jnp.full((1, SEL_LANES), seq, I32))
            return thr, idx_cut

        all_selected = (i * n_sub + a + 1) * SEL_LANES <= k_sel
        thr, idx_cut = lax.cond(
            all_selected,
            lambda: (jnp.full((1, SEL_LANES), INT_MIN, I32), jnp.full((1, SEL_LANES), seq, I32)),
            select_threshold)
        return thr, idx_cut, q_pos

    selected = [select(a) for a in range(n_sub)]
    n_tiles = tiles_of(n_sub - 1)

    def write_bias(base):
        k_idx = key_row + base
        for a, (thr, idx_cut, q_pos) in enumerate(selected):
            key = keys_ref[a, pl.ds(base, KEY_TILE), :]
            sel = ((key > thr) | ((key == thr) & (k_idx <= idx_cut))) & (k_idx <= q_pos)
            bias_t = jnp.where(sel, 0.0, -jnp.inf).astype(F32)
            for c in range(KEY_TILE // LANES):
                col = base + c * LANES
                col = col if isinstance(col, int) else pl.multiple_of(col, LANES)
                bias_ref[a * SEL_LANES:(a + 1) * SEL_LANES, pl.ds(col, LANES)] = \
                    bias_t[c * LANES:(c + 1) * LANES, :].T

    write_bias(0)

    for p in range(n_pairs):
        cols = slice(p * LANES, (p + 1) * LANES)
        qs_ref[p] = _stack_heads(q_ref[:, cols], first_head_q)
    heads_per_kv_pair = 2 * (B_HEADS // B_KV_HEADS) // 2

    ones_blk = jnp.ones((KEY_TILE, LANES), BF16)

    @pl.when(i == 0)
    def _():
        def xpose(r, carry):
            rows = pl.ds(pl.multiple_of(r * LANES, LANES), LANES)
            for c in range(kt_ref.shape[0]):
                blk = k_ref[rows, c * LANES:(c + 1) * LANES].astype(F32)
                kt_ref[c, :, rows] = blk.T.astype(BF16)
            return carry
        lax.fori_loop(0, seq // LANES, xpose, 0)

    def logits(base, p):
        kt = kt_ref[p // heads_per_kv_pair, :, pl.ds(base, KEY_TILE)]
        return jnp.dot(qs_ref[p], kt, preferred_element_type=F32)

    lookahead = s0_ref.shape[0]
    for u in range(lookahead):
        s0_ref[u] = logits(0, u)

    def attn_tiles(t0, count, start=False):
        bases = [pl.multiple_of(jnp.minimum(t0 + dt, n_tiles - 1) * KEY_TILE, KEY_TILE)
                 for dt in range(count + 1)]
        units = [(dt, p) for dt in range(count + 1) for p in range(n_pairs)]
        queue = [s0_ref[u] for u in range(lookahead)]
        for u, (dt, p) in enumerate(units[:count * n_pairs]):
            base = bases[dt]
            if p == 0:
                bias = bias_ref[:, pl.ds(base, KEY_TILE)]
                write_bias(bases[dt + 1])
                bias2 = jnp.concatenate([bias, bias], axis=0)
            kv = p // heads_per_kv_pair
            kv_cols = slice(kv * LANES, (kv + 1) * LANES)
            s = queue.pop(0) + bias2
            ahead_dt, ahead_p = units[u + lookahead]
            queue.append(logits(bases[ahead_dt], ahead_p))
            m_blk = s[:, :LANES]
            for c in range(1, KEY_TILE // LANES):
                m_blk = jnp.maximum(m_blk, s[:, c * LANES:(c + 1) * LANES])
            opening = start and dt == 0
            m_cur = jnp.max(m_blk, axis=-1, keepdims=True)
            if opening:
                m_new = jnp.broadcast_to(jnp.maximum(m_cur, F32_MIN), m_blk.shape)
            else:
                m_old = m_ref[p]
                m_new = jnp.maximum(m_old, m_cur)
                alpha = jnp.exp2(m_old - m_new)
            e = jnp.concatenate(
                [jnp.exp2(s[:, c * LANES:(c + 1) * LANES] - m_new).astype(BF16)
                 for c in range(KEY_TILE // LANES)], axis=1)
            v_ext = jnp.concatenate([v_ref[pl.ds(base, KEY_TILE), kv_cols], ones_blk], axis=1)
            pv = jnp.dot(e, v_ext, preferred_element_type=F32)
            if opening:
                acc_ref[p] = pv[:, :LANES]
                l_ref[p] = pv[:, LANES:]
            else:
                acc_ref[p] = acc_ref[p] * alpha + pv[:, :LANES]
                l_ref[p] = l_ref[p] * alpha + pv[:, LANES:]
            m_ref[p] = m_new
        for u in range(lookahead):
            s0_ref[u] = queue[u]

    attn_tiles(0, 1, start=True)

    def attn_two(t2, carry):
        attn_tiles(1 + 2 * t2, 2)
        return carry

    lax.fori_loop(0, (n_tiles - 1) // 2, attn_two, 0)

    @pl.when(n_tiles % 2 == 0)
    def _():
        attn_tiles(n_tiles - 1, 1)

    for p in range(n_pairs):
        cols = slice(p * LANES, (p + 1) * LANES)
        l = l_ref[p]
        acc = acc_ref[p]
        num = jnp.where(first_head_q, acc[:Q_BLOCK], acc[Q_BLOCK:])
        den = jnp.where(first_head_q, l[:Q_BLOCK], l[Q_BLOCK:])
        gate = gate_ref[:, cols].astype(F32)
        y_ref[:, cols] = (num * gate / (den * (1.0 + jnp.exp(-gate)))).astype(y_ref.dtype)


def dsa_attention(proj):
    b, s, _ = proj.shape
    k_sel = min(TOPK_MAX, s // 4)
    assert s % KEY_TILE == 0 and s % Q_BLOCK == 0 and k_sel <= KEY_TILE
    kvw = B_KV_HEADS * HEAD_DIM
    n_pairs = B_HEADS // 2

    def qblock(width, idx):
        return pl.BlockSpec((None, Q_BLOCK, width), lambda bi, i: (bi, i, idx))

    def full(width, idx):
        return pl.BlockSpec((None, s, width), lambda bi, i: (bi, 0, idx))

    return pl.pallas_call(
        functools.partial(_dsa_kernel, k_sel=k_sel),
        grid=(b, s // Q_BLOCK),
        in_specs=[
            qblock(B_WIDTH, 0),
            qblock(B_WIDTH, 1),
            qblock(IDX_HEADS * HEAD_DIM, 4),
            full(kvw, 10),
            full(LANES, 22),
            qblock(LANES, 24),
            full(kvw, 13),
        ],
        out_specs=qblock(B_WIDTH, 0),
        out_shape=jax.ShapeDtypeStruct((b, s, B_WIDTH), BF16),
        scratch_shapes=[
            pltpu.VMEM((Q_BLOCK // SEL_LANES, s, SEL_LANES), I32),
            pltpu.VMEM((32, s // 32, SEL_LANES), I32),
            pltpu.VMEM((Q_BLOCK, s), F32),
            pltpu.VMEM((kvw // LANES, LANES, s), BF16),
            pltpu.VMEM((n_pairs, 2 * Q_BLOCK, LANES), BF16),
            pltpu.VMEM((ATTN_LOOKAHEAD, 2 * Q_BLOCK, KEY_TILE), F32),
            pltpu.VMEM((n_pairs, 2 * Q_BLOCK, LANES), F32),
            pltpu.VMEM((n_pairs, 2 * Q_BLOCK, LANES), F32),
            pltpu.VMEM((n_pairs, 2 * Q_BLOCK, LANES), F32),
        ],
        compiler_params=_cparams(("arbitrary", "arbitrary")),
        name="dsa_attention",
    )(proj, proj, proj, proj, proj, proj, proj)


def dsa_mixer(h, g, scale, shift, w_in_packed, cos, sin):
    proj = in_proj(h, g, scale, shift, w_in_packed, [f == 1 for f in B_ROPE_FLAGS], cos, sin, 512)
    return dsa_attention(proj)


def kernel(x, c, positions, norm_g, ada_w, ada_b, a_w_in, a_w_out, b_w_in, b_w_out, final_g):
    depth = norm_g.shape[0]
    d = x.shape[-1]
    mod = adaln_mod(c, ada_w, ada_b)
    cos, sin = rope_tables(positions)
    h = x
    for i in range(depth):
        shift, scale, gate = mod[i, :, :d], mod[i, :, d:2 * d], mod[i, :, 2 * d:]
        last = i == depth - 1
        if i % 2 == 0:
            h = dilated_layer(h, norm_g[i], scale, shift, gate, a_w_in[i // 2], a_w_out[i // 2], cos, sin)
            if last:
                h = final_norm(h, final_g)
        else:
            w_in, w_out = _dsa_weights(b_w_in[i // 2], b_w_out[i // 2])
            y = dsa_mixer(h, norm_g[i], scale, shift, w_in, cos, sin)
            h = out_proj(y, w_out, gate, h, final_g if last else None)
    return h
```

```python
import functools

import numpy as np
import jax
import jax.numpy as jnp
from jax import lax
from jax.experimental import pallas as pl
from jax.experimental.pallas import tpu as pltpu

F32 = jnp.float32
BF16 = jnp.bfloat16
I32 = jnp.int32

LANES = 128
HEAD_DIM = 64
HALF = HEAD_DIM // 2
ROPE_THETA = 10000.0
NORM_EPS = 1e-6
VMEM_LIMIT = 56 * 1024 * 1024
ROW_CHUNK = 256

A_HEADS = 16
A_WIDTH = A_HEADS * HEAD_DIM
A_GROUPS = ((128, 1), (512, 4), (2048, 16))
A_BLK = 128
BANDED_LOOKAHEAD = 1

B_HEADS = 16
B_KV_HEADS = 4
B_WIDTH = B_HEADS * HEAD_DIM
IDX_HEADS = 8
TOPK_MAX = 256
Q_BLOCK = 256
SEL_LANES = 128
KEY_TILE = 512
INT_MIN = -(2 ** 31)
KEY_NEG_INF = -2139095041
F32_MIN = float(np.finfo(np.float32).min)
LOG2_E = float(np.log2(np.e))
Q_PRESCALE = LOG2_E * HEAD_DIM ** -0.5


def _cparams(sem):
    return pltpu.CompilerParams(dimension_semantics=sem, vmem_limit_bytes=VMEM_LIMIT)


def _adaln_kernel(c_ref, w_ref, b_ref, o_ref):
    c = c_ref[...]
    ca = (c * (1.0 / (1.0 + jnp.exp(-c)))).astype(BF16)
    acc = jnp.dot(ca, w_ref[...].astype(BF16), preferred_element_type=F32)
    o_ref[...] = acc + b_ref[...]


def adaln_mod(c, ada_w, ada_b):
    depth, d, d3 = ada_w.shape
    b = c.shape[0]
    tn = 1024
    return pl.pallas_call(
        _adaln_kernel,
        grid=(depth, d3 // tn),
        in_specs=[
            pl.BlockSpec((b, d), lambda i, j: (0, 0)),
            pl.BlockSpec((None, d, tn), lambda i, j: (i, 0, j)),
            pl.BlockSpec((None, 1, tn), lambda i, j: (i, 0, j)),
        ],
        out_specs=pl.BlockSpec((None, b, tn), lambda i, j: (i, 0, j)),
        out_shape=jax.ShapeDtypeStruct((depth, b, d3), F32),
        compiler_params=_cparams(("arbitrary", "arbitrary")),
        name="adaln_mod",
    )(c, ada_w, ada_b.reshape(depth, 1, d3))


def _rope_table_kernel(pos_ref, inv_ref, cos_ref, sin_ref):
    ang = pos_ref[...].astype(F32) * inv_ref[...]
    lane = lax.broadcasted_iota(I32, ang.shape, 1)
    first_half = (lane % HEAD_DIM) < HALF
    cos_ref[...] = jnp.cos(ang)
    s = jnp.sin(ang)
    sin_ref[...] = jnp.where(first_half, -s, s)


def rope_tables(positions):
    b, s = positions.shape
    inv_freq = ROPE_THETA ** (-jnp.arange(HALF, dtype=F32) / HALF)
    inv_lane = jnp.tile(inv_freq, LANES // HALF).reshape(1, LANES)
    ts = min(s, 1024)
    out = jax.ShapeDtypeStruct((b, s, LANES), F32)
    return pl.pallas_call(
        _rope_table_kernel,
        grid=(b, s // ts),
        in_specs=[
            pl.BlockSpec((None, ts, 1), lambda i, j: (i, j, 0)),
            pl.BlockSpec((1, LANES), lambda i, j: (0, 0)),
        ],
        out_specs=[pl.BlockSpec((None, ts, LANES), lambda i, j: (i, j, 0))] * 2,
        out_shape=[out, out],
        compiler_params=_cparams(("arbitrary", "arbitrary")),
        name="rope_tables",
    )(positions.reshape(b, s, 1), inv_lane)


def _rope_block(t, cos, sin_signed, first_half):
    partner = jnp.where(first_half, pltpu.roll(t, LANES - HALF, 1), pltpu.roll(t, HALF, 1))
    return t * cos + partner * sin_signed


def _in_proj_kernel(x_ref, g_ref, sc_ref, sh_ref, w_ref, cos_ref, sin_ref, *rest,
                    dilation, rope_tiles, tn, plain_cols):
    if plain_cols:
        w2_ref, o_ref, o2_ref, u_ref, *stage = rest
    else:
        o_ref, u_ref, *stage = rest
    tm = u_ref.shape[0]
    nblk = tn // LANES
    per = ROW_CHUNK // dilation
    x = x_ref[...]
    ms = jnp.mean(x * x, axis=-1, keepdims=True)
    xn = x * lax.rsqrt(ms + NORM_EPS)
    u_ref[...] = (xn * g_ref[...] * (1.0 + sc_ref[...]) + sh_ref[...]).astype(BF16)
    lane = lax.broadcasted_iota(I32, (per, LANES), 1)
    first_half = (lane % HEAD_DIM) < HALF
    step = 0
    for j, rope in enumerate(rope_tiles):
        for c in range(tm // ROW_CHUNK):
            acc = jnp.dot(u_ref[c * ROW_CHUNK:(c + 1) * ROW_CHUNK, :], w_ref[:, j * tn:(j + 1) * tn],
                          preferred_element_type=F32)
            if dilation > 1:
                slot = step % stage[0].shape[0]
                step += 1
                for k in range(nblk):
                    stage[0][slot, k] = acc[:, k * LANES:(k + 1) * LANES]
            for r in range(dilation):
                rows = slice(c * per, (c + 1) * per)
                if rope:
                    trows = rows if dilation == 1 else pl.ds(c * ROW_CHUNK + r, per, stride=dilation)
                    cos, sin = cos_ref[trows, :], sin_ref[trows, :]
                for k in range(nblk):
                    cols = slice(j * tn + k * LANES, j * tn + (k + 1) * LANES)
                    if dilation > 1:
                        blk = stage[0][slot, k, pl.ds(r, per, stride=dilation), :]
                    else:
                        blk = acc[:, k * LANES:(k + 1) * LANES]
                    if rope:
                        blk = _rope_block(blk, cos, sin, first_half)
                    if dilation > 1:
                        o_ref[r, rows, cols] = blk.astype(o_ref.dtype)
                    else:
                        o_ref[rows, cols] = blk.astype(o_ref.dtype)
    for j in range(plain_cols // tn):
        for c in range(tm // ROW_CHUNK):
            rows = slice(c * ROW_CHUNK, (c + 1) * ROW_CHUNK)
            acc = jnp.dot(u_ref[rows, :], w2_ref[:, j * tn:(j + 1) * tn], preferred_element_type=F32)
            o2_ref[rows, j * tn:(j + 1) * tn] = acc.astype(o2_ref.dtype)


def in_proj(h, g, scale, shift, w, rope_tiles, cos, sin, tn, dilation=1, w_block=0, plain=None):
    b, s, d = h.shape
    n = len(rope_tiles) * tn
    tm = min(s, 1024)
    assert tm % ROW_CHUNK == 0 and ROW_CHUNK % (16 * dilation) == 0 and (w_block + 1) * n <= w.shape[1]
    scratch = [pltpu.VMEM((tm, d), BF16)]
    tab_spec = pl.BlockSpec((None, tm, LANES), lambda bi, i: (bi, i, 0))
    if dilation == 1:
        out_spec = pl.BlockSpec((None, tm, n), lambda bi, i: (bi, i, 0))
        out_shape = jax.ShapeDtypeStruct((b, s, n), BF16)
    else:
        out_spec = pl.BlockSpec((None, dilation, tm // dilation, n), lambda bi, i: (bi, 0, i, 0))
        out_shape = jax.ShapeDtypeStruct((b, dilation, s // dilation, n), BF16)
        scratch.append(pltpu.VMEM((2, tn // LANES, ROW_CHUNK, LANES), F32))
    in_specs = [
        pl.BlockSpec((None, tm, d), lambda bi, i: (bi, i, 0)),
        pl.BlockSpec((1, d), lambda bi, i: (0, 0)),
        pl.BlockSpec((None, 1, d), lambda bi, i: (bi, 0, 0)),
        pl.BlockSpec((None, 1, d), lambda bi, i: (bi, 0, 0)),
        pl.BlockSpec((d, n), lambda bi, i: (0, w_block)),
        tab_spec, tab_spec,
    ]
    args = [h, g.reshape(1, d), scale.reshape(b, 1, d), shift.reshape(b, 1, d), w, cos, sin]
    plain_cols = 0
    if plain is not None:
        plain_cols, plain_block = plain
        assert plain_cols % tn == 0 and (plain_block + 1) * plain_cols <= w.shape[1]
        in_specs.append(pl.BlockSpec((d, plain_cols), lambda bi, i: (0, plain_block)))
        args.append(w)
        out_spec = [out_spec, pl.BlockSpec((None, tm, plain_cols), lambda bi, i: (bi, i, 0))]
        out_shape = [out_shape, jax.ShapeDtypeStruct((b, s, plain_cols), BF16)]
    return pl.pallas_call(
        functools.partial(_in_proj_kernel, dilation=dilation, rope_tiles=tuple(rope_tiles), tn=tn,
                          plain_cols=plain_cols),
        grid=(b, s // tm),
        in_specs=in_specs,
        out_specs=out_spec,
        out_shape=out_shape,
        scratch_shapes=scratch,
        compiler_params=_cparams(("arbitrary", "arbitrary")),
        name=f"in_proj_d{dilation}",
    )(*args)


def _out_proj_kernel(y_ref, w_ref, gate_ref, h_ref, *rest):
    acc = jnp.dot(y_ref[...], w_ref[...], preferred_element_type=F32)
    h_new = h_ref[...] + gate_ref[...] * acc
    if len(rest) == 2:
        g_ref, o_ref = rest
        ms = jnp.mean(h_new * h_new, axis=-1, keepdims=True)
        o_ref[...] = h_new * lax.rsqrt(ms + NORM_EPS) * g_ref[...]
    else:
        rest[0][...] = h_new


def out_proj(y, w, gate, h, final_g=None):
    b, s, d = h.shape
    k = y.shape[-1]
    tm = min(s, 1024)
    in_specs = [
        pl.BlockSpec((None, tm, k), lambda bi, i: (bi, i, 0)),
        pl.BlockSpec((k, d), lambda bi, i: (0, 0)),
        pl.BlockSpec((None, 1, d), lambda bi, i: (bi, 0, 0)),
        pl.BlockSpec((None, tm, d), lambda bi, i: (bi, i, 0)),
    ]
    args = [y, w, gate.reshape(b, 1, d), h]
    if final_g is not None:
        in_specs.append(pl.BlockSpec((1, d), lambda bi, i: (0, 0)))
        args.append(final_g.reshape(1, d))
    return pl.pallas_call(
        _out_proj_kernel,
        grid=(b, s // tm),
        in_specs=in_specs,
        out_specs=pl.BlockSpec((None, tm, d), lambda bi, i: (bi, i, 0)),
        out_shape=jax.ShapeDtypeStruct((b, s, d), F32),
        compiler_params=_cparams(("arbitrary", "arbitrary")),
        name="out_proj",
    )(*args)


def _final_norm_kernel(x_ref, g_ref, o_ref):
    x = x_ref[...]
    ms = jnp.mean(x * x, axis=-1, keepdims=True)
    o_ref[...] = x * lax.rsqrt(ms + NORM_EPS) * g_ref[...]


def final_norm(h, g):
    b, s, d = h.shape
    tm = min(s, 1024)
    return pl.pallas_call(
        _final_norm_kernel,
        grid=(b, s // tm),
        in_specs=[
            pl.BlockSpec((None, tm, d), lambda bi, i: (bi, i, 0)),
            pl.BlockSpec((1, d), lambda bi, i: (0, 0)),
        ],
        out_specs=pl.BlockSpec((None, tm, d), lambda bi, i: (bi, i, 0)),
        out_shape=jax.ShapeDtypeStruct((b, s, d), F32),
        compiler_params=_cparams(("arbitrary", "arbitrary")),
        name="final_norm",
    )(h, g.reshape(1, d))


def _stack_heads(pair, first_head):
    zero = jnp.zeros_like(pair)
    return jnp.concatenate([jnp.where(first_head, pair, zero), jnp.where(first_head, zero, pair)], axis=0)


def _banded_kernel(q_ref, kp_ref, kc_ref, vp_ref, vc_ref, o_ref, lse_ref, kwin_ref, vwin_ref, *, tq):
    i = pl.program_id(2)
    n_cls = q_ref.shape[0]
    nsub = tq // A_BLK
    lane = lax.broadcasted_iota(I32, (A_BLK, LANES), 1)
    first_head = lane < HEAD_DIM
    odd_lane = (lane % 2) == 1
    r2 = lax.broadcasted_iota(I32, (2 * A_BLK, 2 * A_BLK), 0) % A_BLK
    c2 = lax.broadcasted_iota(I32, (2 * A_BLK, 2 * A_BLK), 1)
    own_ok = (c2 >= A_BLK) & (c2 - A_BLK <= r2)
    ones_blk = jnp.ones((2 * A_BLK, LANES), BF16)
    n_pairs = A_HEADS // 2

    kwin_ref[:, :A_BLK, :] = kp_ref[...]
    kwin_ref[:, A_BLK:, :] = kc_ref[...]
    vwin_ref[:, :A_BLK, :] = vp_ref[...]
    vwin_ref[:, A_BLK:, :] = vc_ref[...]

    def logits(r, a, p):
        cols = slice(p * LANES, (p + 1) * LANES)
        qs = _stack_heads(q_ref[r, a * A_BLK:(a + 1) * A_BLK, cols], first_head)
        k2 = kwin_ref[r, a * A_BLK:(a + 2) * A_BLK, cols]
        return lax.dot_general(qs, k2, (((1,), (1,)), ((), ())), preferred_element_type=F32)

    def band_bias(prev_shift):
        mask = own_ok | ((c2 < A_BLK) & (c2 >= r2 + prev_shift))
        return jnp.where(mask, 0.0, -jnp.inf).astype(F32)

    bias_inner = band_bias(0)
    biases = [band_bias(jnp.where(i > 0, 0, 2 * A_BLK))] + [bias_inner] * (nsub - 1)

    units = [(r, a, p) for r in range(n_cls) for a in range(nsub) for p in range(n_pairs)]
    queue = [logits(*units[u]) for u in range(BANDED_LOOKAHEAD)]
    lse_tile = None
    for u, (r, a, p) in enumerate(units):
        rows = slice(a * A_BLK, (a + 1) * A_BLK)
        cols = slice(p * LANES, (p + 1) * LANES)
        bias = biases[a]
        if p == 0:
            lse_tile = jnp.zeros((A_BLK, LANES), F32)
        s = queue.pop(0) + bias
        if u + BANDED_LOOKAHEAD < len(units):
            queue.append(logits(*units[u + BANDED_LOOKAHEAD]))
        m = jnp.max(s, axis=-1, keepdims=True)
        e = jnp.exp2(s - m).astype(BF16)
        v_ext = jnp.concatenate([vwin_ref[r, a * A_BLK:(a + 2) * A_BLK, cols], ones_blk], axis=1)
        pv = jnp.dot(e, v_ext, preferred_element_type=F32)
        num = jnp.where(first_head, pv[:A_BLK, :LANES], pv[A_BLK:, :LANES])
        o_ref[r, rows, cols] = num * (1.0 / jnp.where(first_head, pv[:A_BLK, LANES:], pv[A_BLK:, LANES:]))
        den_pair = jnp.where(odd_lane, pv[A_BLK:, LANES:], pv[:A_BLK, LANES:])
        m_pair = jnp.where(odd_lane, m[A_BLK:], m[:A_BLK])
        lse_tile = jnp.where((lane // 2) == p, m_pair + jnp.log(den_pair) * LOG2_E, lse_tile)
        if p == n_pairs - 1:
            lse_ref[r, rows, :] = lse_tile


BANDED_ROWS = 1024


def banded_group_attention(qkv):
    b, d, n, _ = qkv.shape
    tq = min(BANDED_ROWS, n)
    rc = min(BANDED_ROWS // tq, d)
    sub = tq // A_BLK
    assert n % tq == 0 and d % rc == 0

    def cur(blk, width=A_WIDTH):
        return pl.BlockSpec((None, rc, tq, width), lambda bi, r, i: (bi, r, i, blk))

    def prev(blk):
        return pl.BlockSpec((None, rc, A_BLK, A_WIDTH),
                            lambda bi, r, i: (bi, r, jnp.maximum(i * sub - 1, 0), blk))

    return pl.pallas_call(
        functools.partial(_banded_kernel, tq=tq),
        grid=(b, d // rc, n // tq),
        in_specs=[cur(0), prev(1), cur(1), prev(2), cur(2)],
        out_specs=[cur(0), cur(0, LANES)],
        out_shape=[
            jax.ShapeDtypeStruct((b, d, n, A_WIDTH), F32),
            jax.ShapeDtypeStruct((b, d, n, LANES), F32),
        ],
        scratch_shapes=[pltpu.VMEM((rc, tq + A_BLK, A_WIDTH), BF16)] * 2,
        compiler_params=_cparams(("arbitrary", "arbitrary", "arbitrary")),
        name=f"banded_attn_d{d}",
    )(qkv, qkv, qkv, qkv, qkv)


def _expand_heads(x, expand):
    hi = x.astype(BF16)
    lo = (x - hi.astype(F32)).astype(BF16)
    return (jnp.dot(hi, expand, preferred_element_type=F32)
            + jnp.dot(lo, expand, preferred_element_type=F32))


def _merge_out_kernel(o0_ref, o1_ref, o2_ref, l0_ref, l1_ref, l2_ref, gate_ref, ex_ref, w_ref,
                      mod_ref, h_ref, hout_ref, y_ref, ot_ref, lt_ref, *, dilations):
    o_refs, l_refs = (o0_ref, o1_ref, o2_ref), (l0_ref, l1_ref, l2_ref)
    tm = y_ref.shape[0]
    nblk = y_ref.shape[1] // LANES
    for g, d in enumerate(dilations):
        if d == 1:
            continue
        per = tm // d
        for r in range(d):
            lt_ref[g, pl.ds(r, per, stride=d), :] = l_refs[g][r]
            for k in range(nblk):
                ot_ref[g, k, pl.ds(r, per, stride=d), :] = o_refs[g][r, :, k * LANES:(k + 1) * LANES]
    lses = [l_refs[g][0] if d == 1 else lt_ref[g] for g, d in enumerate(dilations)]
    m = jnp.maximum(jnp.maximum(lses[0], lses[1]), lses[2])
    es = [jnp.exp2(l - m) for l in lses]
    inv = 1.0 / (es[0] + es[1] + es[2])
    ex = ex_ref[...]
    alphas = [_expand_heads(e * inv, ex) for e in es[:-1]]
    alphas.append(1.0 - alphas[0] - alphas[1])
    for k in range(nblk):
        cols = slice(k * LANES, (k + 1) * LANES)
        y = jnp.zeros((tm, LANES), F32)
        for g, d in enumerate(dilations):
            og = o_refs[g][0, :, cols] if d == 1 else ot_ref[g, k]
            y += alphas[g][:, cols] * og
        gate = gate_ref[:, cols].astype(F32)
        y_ref[:, cols] = (y * gate * (1.0 / (1.0 + jnp.exp(-gate)))).astype(y_ref.dtype)
    acc = jnp.dot(y_ref[...], w_ref[...], preferred_element_type=F32)
    hout_ref[...] = h_ref[...] + mod_ref[...] * acc


def merge_out_proj(outs, lses, gate, w_out, mod_gate, h):
    b, s, w = gate.shape
    d_model = h.shape[-1]
    tm = min(s, 512)
    dil = tuple(o.shape[1] for o in outs)
    expand = (jnp.arange(LANES)[:, None] == (jnp.arange(w)[None, :] // HEAD_DIM)).astype(BF16)

    def cm(d, width):
        return pl.BlockSpec((None, d, tm // d, width), lambda bi, i: (bi, 0, i, 0))

    def tspec(width):
        return pl.BlockSpec((None, tm, width), lambda bi, i: (bi, i, 0))

    return pl.pallas_call(
        functools.partial(_merge_out_kernel, dilations=dil),
        grid=(b, s // tm),
        in_specs=[cm(d, w) for d in dil] + [cm(d, LANES) for d in dil]
                 + [tspec(w), pl.BlockSpec((LANES, w), lambda bi, i: (0, 0)),
                    pl.BlockSpec((w, d_model), lambda bi, i: (0, 0)),
                    pl.BlockSpec((None, 1, d_model), lambda bi, i: (bi, 0, 0)),
                    tspec(d_model)],
        out_specs=tspec(d_model),
        out_shape=jax.ShapeDtypeStruct((b, s, d_model), F32),
        scratch_shapes=[pltpu.VMEM((tm, w), BF16),
                        pltpu.VMEM((len(dil), w // LANES, tm, LANES), F32),
                        pltpu.VMEM((len(dil), tm, LANES), F32)],
        compiler_params=_cparams(("arbitrary", "arbitrary")),
        name="merge_out_proj",
    )(*outs, *lses, gate, expand, w_out, mod_gate.reshape(b, 1, d_model), h)


def dilated_layer(h, g, scale, shift, mod_gate, w_in, w_out, cos, sin):
    tn = 512
    per = A_WIDTH // tn
    qkv_flags = (True,) * (2 * per) + (False,) * per
    n_groups = len(A_GROUPS)
    col = np.arange(w_in.shape[1])
    is_q = (col < n_groups * 3 * A_WIDTH) & ((col // A_WIDTH) % 3 == 0)
    w = (w_in * jnp.asarray(np.where(is_q, Q_PRESCALE, 1.0), F32)).astype(BF16)
    outs, lses = [], []
    gate = None
    for gi, (window, dilation) in enumerate(A_GROUPS):
        assert window // dilation == A_BLK
        plain = (A_WIDTH, 3 * n_groups) if gi == 0 else None
        qkv = in_proj(h, g, scale, shift, w, qkv_flags, cos, sin, tn, dilation,
                      w_block=gi, plain=plain)
        if plain is not None:
            qkv, gate = qkv
        if dilation == 1:
            qkv = qkv[:, None]
        o, lse = banded_group_attention(qkv)
        outs.append(o)
        lses.append(lse)
    return merge_out_proj(outs, lses, gate, w_out.astype(BF16), mod_gate, h)


B_COLS = 3584
B_ROPE_FLAGS = (1, 1, 0, 0, 1, 1, 0)


def _dsa_head_perm():
    group = B_HEADS // B_KV_HEADS
    order = []
    for g2 in range(B_KV_HEADS // 2):
        for r in range(group):
            order += [(2 * g2) * group + r, (2 * g2 + 1) * group + r]
    return np.asarray(order)


def _dsa_weights(w_in, w_out):
    d = w_in.shape[0]
    cuts = np.cumsum((B_WIDTH, B_KV_HEADS * HEAD_DIM, B_KV_HEADS * HEAD_DIM, IDX_HEADS * HEAD_DIM,
                      HEAD_DIM, IDX_HEADS, B_WIDTH))[:-1]
    wq, wk, wv, wqi, wki, wwi, wg = jnp.split(w_in, cuts, axis=1)
    cols = (_dsa_head_perm()[:, None] * HEAD_DIM + np.arange(HEAD_DIM)[None, :]).reshape(-1)
    zeros = lambda n: jnp.zeros((d, n), w_in.dtype)
    wq = wq * Q_PRESCALE
    w = jnp.concatenate([wq[:, cols], wg[:, cols], wqi, wk, wki, wki, zeros(LANES),
                         wwi, zeros(2 * LANES - IDX_HEADS), wv], axis=1)
    assert w.shape[1] == B_COLS
    return w.astype(BF16), w_out[cols, :].astype(BF16)


def _sortable(x):
    bits = pltpu.bitcast(x, I32)
    return bits ^ ((bits >> 31) & jnp.int32(0x7FFFFFFF))


def _bit_transpose32(words):
    a = list(words)
    j, m = 16, 0x0000FFFF
    while j:
        mask = jnp.int32(m - (1 << 32) if m >= (1 << 31) else m)
        shift = jnp.full(a[0].shape, j, I32)
        k = 0
        while k < 32:
            t = (a[k] ^ lax.shift_right_logical(a[k + j], shift)) & mask
            a[k] = a[k] ^ t
            a[k + j] = a[k + j] ^ lax.shift_left(t, shift)
            k = (k + j + 1) & ~j
        j >>= 1
        m = (m ^ (m << j)) & 0xFFFFFFFF
    return a


def _sublane_allsum(x):
    x = x + pltpu.roll(x, 4, 0)
    x = x + pltpu.roll(x, 2, 0)
    return x + pltpu.roll(x, 1, 0)


def _tree_sum(parts):
    parts = list(parts)
    while len(parts) > 1:
        odd = [parts[-1]] if len(parts) % 2 else []
        parts = [parts[j] + parts[j + 1] for j in range(0, len(parts) - 1, 2)] + odd
    return parts[0]


GROUP_KEYS = 256
ATTN_LOOKAHEAD = 1


def _dsa_kernel(q_ref, gate_ref, qi_ref, k_ref, ki_ref, wi_ref, v_ref, y_ref,
                keys_ref, planes_ref, bias_ref, kt_ref, qs_ref, s0_ref, m_ref, l_ref, acc_ref,
                *, k_sel):
    i = pl.program_id(1)
    seq = keys_ref.shape[1]
    n_sub = Q_BLOCK // SEL_LANES
    n_pairs = B_HEADS // 2
    n_groups_max = seq // GROUP_KEYS
    lane_q = lax.broadcasted_iota(I32, (Q_BLOCK, LANES), 1)
    first_head_q = lane_q < HEAD_DIM
    first_head = lax.broadcasted_iota(I32, (SEL_LANES, LANES), 1) < HEAD_DIM
    key_row = lax.broadcasted_iota(I32, (KEY_TILE, SEL_LANES), 0)
    lane_idx = lax.broadcasted_iota(I32, (KEY_TILE, SEL_LANES), 1)

    @pl.when(i == 0)
    def _():
        planes_ref[...] = jnp.zeros(planes_ref.shape, I32)
        keys_ref[...] = jnp.zeros(keys_ref.shape, I32)

    def tiles_of(a):
        return ((i * n_sub + a) * SEL_LANES) // KEY_TILE + 1

    def select(a):
        n_tiles = tiles_of(a)
        rows_a = slice(a * SEL_LANES, (a + 1) * SEL_LANES)
        q_pos = (i * n_sub + a) * SEL_LANES + lane_idx
        keys = keys_ref.at[a]

        w_t = (wi_ref[rows_a, :].astype(F32) * (IDX_HEADS ** -0.5 * HEAD_DIM ** -0.5)).T
        qi_stacked = [_stack_heads(qi_ref[rows_a, p * LANES:(p + 1) * LANES], first_head)
                      for p in range(IDX_HEADS // 2)]

        def score_tile(t):
            base = pl.multiple_of(t * KEY_TILE, KEY_TILE)
            kk = ki_ref[pl.ds(base, KEY_TILE), :]
            score = jnp.zeros((KEY_TILE, SEL_LANES), F32)
            for p in range(IDX_HEADS // 2):
                sc = lax.dot_general(kk, qi_stacked[p], (((1,), (1,)), ((), ())),
                                     preferred_element_type=F32)
                score += jnp.maximum(sc[:, :SEL_LANES], 0.0) * w_t[2 * p:2 * p + 1, :]
                score += jnp.maximum(sc[:, SEL_LANES:], 0.0) * w_t[2 * p + 1:2 * p + 2, :]
            causal = (key_row + base) <= q_pos
            score = jnp.where(causal, score + 0.0, -jnp.inf)
            key = _sortable(score)
            keys[pl.ds(base, KEY_TILE), :] = key
            ukey = key ^ jnp.int32(INT_MIN)
            for g in range(KEY_TILE // GROUP_KEYS):
                words = [ukey[g * GROUP_KEYS + 8 * j:g * GROUP_KEYS + 8 * j + 8, :] for j in range(32)]
                rows = pl.ds(pl.multiple_of(t * (KEY_TILE // 32) + 8 * g, 8), 8)
                for b, plane in enumerate(_bit_transpose32(words)):
                    planes_ref[b, rows, :] = plane

        def score_two(t2, carry):
            score_tile(2 * t2)
            score_tile(2 * t2 + 1)
            return carry

        lax.fori_loop(0, n_tiles // 2, score_two, 0)

        @pl.when(n_tiles % 2 == 1)
        def _():
            score_tile(n_tiles - 1)

        def select_threshold():
            n_groups = n_tiles * (KEY_TILE // GROUP_KEYS)
            alive0 = tuple(jnp.full((8, SEL_LANES), -1, I32) * (g < n_groups).astype(I32)
                           for g in range(n_groups_max))
            need0 = jnp.full((8, SEL_LANES), k_sel, I32)

            def step(st, carry):
                alive, need, prefix = carry
                planes = [planes_ref[st, 8 * g:8 * g + 8, :] for g in range(n_groups_max)]
                ones = [al & pln for al, pln in zip(alive, planes)]
                cnt = _sublane_allsum(_tree_sum([lax.population_count(o) for o in ones]))
                take = cnt >= need
                prefix = jnp.where(take, prefix | jnp.left_shift(jnp.int32(1), 31 - st), prefix)
                need = jnp.where(take, need, need - cnt)
                flip = jnp.where(take, 0, -1)
                alive = tuple(al & (pln ^ flip) for al, pln in zip(alive, planes))
                return alive, need, prefix

            alive, need, prefix = lax.fori_loop(
                0, 32, step, (alive0, need0, jnp.zeros((8, SEL_LANES), I32)))
            thr = (prefix ^ jnp.int32(INT_MIN))[:1, :]
            need_eq = need[:1, :]

            n_eq = _sublane_allsum(_tree_sum([lax.population_count(al) for al in alive]))[:1, :]
            tied = (n_eq > need_eq) & (thr > KEY_NEG_INF)
            idx_bits = (seq - 1).bit_length()

            def tie_break():
                def count_eq_before(cut):
                    def body(t, acc):
                        base = pl.multiple_of(t * KEY_TILE, KEY_TILE)
                        hit = (keys[pl.ds(base, KEY_TILE), :] == thr) & ((key_row + base) < cut)
                        return acc + jnp.sum(hit.astype(I32).reshape(KEY_TILE // 8, 8, SEL_LANES), axis=0)
                    acc = lax.fori_loop(0, n_tiles, body, jnp.zeros((8, SEL_LANES), I32))
                    return jnp.sum(acc, axis=0, keepdims=True)

                def bit_step(it, cut):
                    cand = cut + jnp.left_shift(jnp.int32(1), idx_bits - 1 - it)
                    return jnp.where(count_eq_before(cand) < need_eq, cand, cut)

                return lax.fori_loop(0, idx_bits, bit_step, jnp.zeros((1, SEL_LANES), I32))

            any_tied = jnp.max(tied.astype(I32)) > 0
            idx_cut = lax.cond(any_tied, tie_break, lambda: jnp.full((1, SEL_LANES), seq, I32))
            return thr, idx_cut

        all_selected = (i * n_sub + a + 1) * SEL_LANES <= k_sel
        thr, idx_cut = lax.cond(
            all_selected,
            lambda: (jnp.full((1, SEL_LANES), INT_MIN, I32), jnp.full((1, SEL_LANES), seq, I32)),
            select_threshold)
        return thr, idx_cut, q_pos

    selected = [select(a) for a in range(n_sub)]
    n_tiles = tiles_of(n_sub - 1)

    def write_bias(base):
        k_idx = key_row + base
        for a, (thr, idx_cut, q_pos) in enumerate(selected):
            key = keys_ref[a, pl.ds(base, KEY_TILE), :]
            sel = ((key > thr) | ((key == thr) & (k_idx <= idx_cut))) & (k_idx <= q_pos)
            bias_t = jnp.where(sel, 0.0, -jnp.inf).astype(F32)
            for c in range(KEY_TILE // LANES):
                col = base + c * LANES
                col = col if isinstance(col, int) else pl.multiple_of(col, LANES)
                bias_ref[a * SEL_LANES:(a + 1) * SEL_LANES, pl.ds(col, LANES)] = \
                    bias_t[c * LANES:(c + 1) * LANES, :].T

    write_bias(0)

    for p in range(n_pairs):
        cols = slice(p * LANES, (p + 1) * LANES)
        qs_ref[p] = _stack_heads(q_ref[:, cols], first_head_q)
    heads_per_kv_pair = 2 * (B_HEADS // B_KV_HEADS) // 2

    ones_blk = jnp.ones((KEY_TILE, LANES), BF16)

    @pl.when(i == 0)
    def _():
        def xpose(r, carry):
            rows = pl.ds(pl.multiple_of(r * LANES, LANES), LANES)
            for c in range(kt_ref.shape[0]):
                blk = k_ref[rows, c * LANES:(c + 1) * LANES].astype(F32)
                kt_ref[c, :, rows] = blk.T.astype(BF16)
            return carry
        lax.fori_loop(0, seq // LANES, xpose, 0)

    def logits(base, p):
        kt = kt_ref[p // heads_per_kv_pair, :, pl.ds(base, KEY_TILE)]
        return jnp.dot(qs_ref[p], kt, preferred_element_type=F32)

    lookahead = s0_ref.shape[0]
    for u in range(lookahead):
        s0_ref[u] = logits(0, u)

    def attn_tiles(t0, count, start=False):
        bases = [pl.multiple_of(jnp.minimum(t0 + dt, n_tiles - 1) * KEY_TILE, KEY_TILE)
                 for dt in range(count + 1)]
        units = [(dt, p) for dt in range(count + 1) for p in range(n_pairs)]
        queue = [s0_ref[u] for u in range(lookahead)]
        for u, (dt, p) in enumerate(units[:count * n_pairs]):
            base = bases[dt]
            if p == 0:
                bias = bias_ref[:, pl.ds(base, KEY_TILE)]
                write_bias(bases[dt + 1])
                bias2 = jnp.concatenate([bias, bias], axis=0)
            kv = p // heads_per_kv_pair
            kv_cols = slice(kv * LANES, (kv + 1) * LANES)
            s = queue.pop(0) + bias2
            ahead_dt, ahead_p = units[u + lookahead]
            queue.append(logits(bases[ahead_dt], ahead_p))
            m_blk = s[:, :LANES]
            for c in range(1, KEY_TILE // LANES):
                m_blk = jnp.maximum(m_blk, s[:, c * LANES:(c + 1) * LANES])
            opening = start and dt == 0
            m_cur = jnp.max(m_blk, axis=-1, keepdims=True)
            if opening:
                m_new = jnp.broadcast_to(jnp.maximum(m_cur, F32_MIN), m_blk.shape)
            else:
                m_old = m_ref[p]
                m_new = jnp.maximum(m_old, m_cur)
                alpha = jnp.exp2(m_old - m_new)
            e = jnp.concatenate(
                [jnp.exp2(s[:, c * LANES:(c + 1) * LANES] - m_new).astype(BF16)
                 for c in range(KEY_TILE // LANES)], axis=1)
            v_ext = jnp.concatenate([v_ref[pl.ds(base, KEY_TILE), kv_cols], ones_blk], axis=1)
            pv = jnp.dot(e, v_ext, preferred_element_type=F32)
            if opening:
                acc_ref[p] = pv[:, :LANES]
                l_ref[p] = pv[:, LANES:]
            else:
                acc_ref[p] = acc_ref[p] * alpha + pv[:, :LANES]
                l_ref[p] = l_ref[p] * alpha + pv[:, LANES:]
            m_ref[p] = m_new
        for u in range(lookahead):
            s0_ref[u] = queue[u]

    attn_tiles(0, 1, start=True)

    def attn_two(t2, carry):
        attn_tiles(1 + 2 * t2, 2)
        return carry

    lax.fori_loop(0, (n_tiles - 1) // 2, attn_two, 0)

    @pl.when(n_tiles % 2 == 0)
    def _():
        attn_tiles(n_tiles - 1, 1)

    for p in range(n_pairs):
        cols = slice(p * LANES, (p + 1) * LANES)
        l = l_ref[p]
        acc = acc_ref[p]
        num = jnp.where(first_head_q, acc[:Q_BLOCK], acc[Q_BLOCK:])
        den = jnp.where(first_head_q, l[:Q_BLOCK], l[Q_BLOCK:])
        gate = gate_ref[:, cols].astype(F32)
        y_ref[:, cols] = (num * gate / (den * (1.0 + jnp.exp(-gate)))).astype(y_ref.dtype)


def dsa_attention(proj):
    b, s, _ = proj.shape
    k_sel = min(TOPK_MAX, s // 4)
    assert s % KEY_TILE == 0 and s % Q_BLOCK == 0 and k_sel <= KEY_TILE
    kvw = B_KV_HEADS * HEAD_DIM
    n_pairs = B_HEADS // 2

    def qblock(width, idx):
        return pl.BlockSpec((None, Q_BLOCK, width), lambda bi, i: (bi, i, idx))

    def full(width, idx):
        return pl.BlockSpec((None, s, width), lambda bi, i: (bi, 0, idx))

    return pl.pallas_call(
        functools.partial(_dsa_kernel, k_sel=k_sel),
        grid=(b, s // Q_BLOCK),
        in_specs=[
            qblock(B_WIDTH, 0),
            qblock(B_WIDTH, 1),
            qblock(IDX_HEADS * HEAD_DIM, 4),
            full(kvw, 10),
            full(LANES, 22),
            qblock(LANES, 24),
            full(kvw, 13),
        ],
        out_specs=qblock(B_WIDTH, 0),
        out_shape=jax.ShapeDtypeStruct((b, s, B_WIDTH), BF16),
        scratch_shapes=[
            pltpu.VMEM((Q_BLOCK // SEL_LANES, s, SEL_LANES), I32),
            pltpu.VMEM((32, s // 32, SEL_LANES), I32),
            pltpu.VMEM((Q_BLOCK, s), F32),
            pltpu.VMEM((kvw // LANES, LANES, s), BF16),
            pltpu.VMEM((n_pairs, 2 * Q_BLOCK, LANES), BF16),
            pltpu.VMEM((ATTN_LOOKAHEAD, 2 * Q_BLOCK, KEY_TILE), F32),
            pltpu.VMEM((n_pairs, 2 * Q_BLOCK, LANES), F32),
            pltpu.VMEM((n_pairs, 2 * Q_BLOCK, LANES), F32),
            pltpu.VMEM((n_pairs, 2 * Q_BLOCK, LANES), F32),
        ],
        compiler_params=_cparams(("arbitrary", "arbitrary")),
        name="dsa_attention",
    )(proj, proj, proj, proj, proj, proj, proj)


def dsa_mixer(h, g, scale, shift, w_in_packed, cos, sin):
    proj = in_proj(h, g, scale, shift, w_in_packed, [f == 1 for f in B_ROPE_FLAGS], cos, sin, 512)
    return dsa_attention(proj)


def kernel(x, c, positions, norm_g, ada_w, ada_b, a_w_in, a_w_out, b_w_in, b_w_out, final_g):
    depth = norm_g.shape[0]
    d = x.shape[-1]
    mod = adaln_mod(c, ada_w, ada_b)
    cos, sin = rope_tables(positions)
    h = x
    for i in range(depth):
        shift, scale, gate = mod[i, :, :d], mod[i, :, d:2 * d], mod[i, :, 2 * d:]
        last = i == depth - 1
        if i % 2 == 0:
            h = dilated_layer(h, norm_g[i], scale, shift, gate, a_w_in[i // 2], a_w_out[i // 2], cos, sin)
            if last:
                h = final_norm(h, final_g)
        else:
            w_in, w_out = _dsa_weights(b_w_in[i // 2], b_w_out[i // 2])
            y = dsa_mixer(h, norm_g[i], scale, shift, w_in, cos, sin)
            h = out_proj(y, w_out, gate, h, final_g if last else None)
    return h
```

```python
import functools

import numpy as np
import jax
import jax.numpy as jnp
from jax import lax
from jax.experimental import pallas as pl
from jax.experimental.pallas import tpu as pltpu

F32 = jnp.float32
BF16 = jnp.bfloat16
I32 = jnp.int32

LANES = 128
HEAD_DIM = 64
HALF = HEAD_DIM // 2
ROPE_THETA = 10000.0
NORM_EPS = 1e-6
VMEM_LIMIT = 56 * 1024 * 1024
ROW_CHUNK = 256

A_HEADS = 16
A_WIDTH = A_HEADS * HEAD_DIM
A_GROUPS = ((128, 1), (512, 4), (2048, 16))
A_BLK = 128
BANDED_LOOKAHEAD = 1

B_HEADS = 16
B_KV_HEADS = 4
B_WIDTH = B_HEADS * HEAD_DIM
IDX_HEADS = 8
TOPK_MAX = 256
Q_BLOCK = 256
SEL_LANES = 128
KEY_TILE = 512
INT_MIN = -(2 ** 31)
KEY_NEG_INF = -2139095041
F32_MIN = float(np.finfo(np.float32).min)
LOG2_E = float(np.log2(np.e))
Q_PRESCALE = LOG2_E * HEAD_DIM ** -0.5


def _cparams(sem):
    return pltpu.CompilerParams(dimension_semantics=sem, vmem_limit_bytes=VMEM_LIMIT)


def _adaln_kernel(c_ref, w_ref, b_ref, o_ref):
    c = c_ref[...]
    ca = (c * (1.0 / (1.0 + jnp.exp(-c)))).astype(BF16)
    acc = jnp.dot(ca, w_ref[...].astype(BF16), preferred_element_type=F32)
    o_ref[...] = acc + b_ref[...]


def adaln_mod(c, ada_w, ada_b):
    depth, d, d3 = ada_w.shape
    b = c.shape[0]
    tn = 1024
    return pl.pallas_call(
        _adaln_kernel,
        grid=(depth, d3 // tn),
        in_specs=[
            pl.BlockSpec((b, d), lambda i, j: (0, 0)),
            pl.BlockSpec((None, d, tn), lambda i, j: (i, 0, j)),
            pl.BlockSpec((None, 1, tn), lambda i, j: (i, 0, j)),
        ],
        out_specs=pl.BlockSpec((None, b, tn), lambda i, j: (i, 0, j)),
        out_shape=jax.ShapeDtypeStruct((depth, b, d3), F32),
        compiler_params=_cparams(("arbitrary", "arbitrary")),
        name="adaln_mod",
    )(c, ada_w, ada_b.reshape(depth, 1, d3))


def _rope_table_kernel(pos_ref, inv_ref, cos_ref, sin_ref):
    ang = pos_ref[...].astype(F32) * inv_ref[...]
    lane = lax.broadcasted_iota(I32, ang.shape, 1)
    first_half = (lane % HEAD_DIM) < HALF
    cos_ref[...] = jnp.cos(ang)
    s = jnp.sin(ang)
    sin_ref[...] = jnp.where(first_half, -s, s)


def rope_tables(positions):
    b, s = positions.shape
    inv_freq = ROPE_THETA ** (-jnp.arange(HALF, dtype=F32) / HALF)
    inv_lane = jnp.tile(inv_freq, LANES // HALF).reshape(1, LANES)
    ts = min(s, 1024)
    out = jax.ShapeDtypeStruct((b, s, LANES), F32)
    return pl.pallas_call(
        _rope_table_kernel,
        grid=(b, s // ts),
        in_specs=[
            pl.BlockSpec((None, ts, 1), lambda i, j: (i, j, 0)),
            pl.BlockSpec((1, LANES), lambda i, j: (0, 0)),
        ],
        out_specs=[pl.BlockSpec((None, ts, LANES), lambda i, j: (i, j, 0))] * 2,
        out_shape=[out, out],
        compiler_params=_cparams(("arbitrary", "arbitrary")),
        name="rope_tables",
    )(positions.reshape(b, s, 1), inv_lane)


def _rope_block(t, cos, sin_signed, first_half):
    partner = jnp.where(first_half, pltpu.roll(t, LANES - HALF, 1), pltpu.roll(t, HALF, 1))
    return t * cos + partner * sin_signed


def _in_proj_kernel(x_ref, g_ref, sc_ref, sh_ref, w_ref, cos_ref, sin_ref, *rest,
                    dilation, rope_tiles, tn, plain_cols):
    if plain_cols:
        w2_ref, o_ref, o2_ref, u_ref, *ucm = rest
    else:
        o_ref, u_ref, *ucm = rest
    tm = u_ref.shape[0]
    nblk = tn // LANES
    per = ROW_CHUNK // dilation
    x = x_ref[...]
    ms = jnp.mean(x * x, axis=-1, keepdims=True)
    xn = x * lax.rsqrt(ms + NORM_EPS)
    u_ref[...] = (xn * g_ref[...] * (1.0 + sc_ref[...]) + sh_ref[...]).astype(BF16)
    lhs_ref = u_ref
    if dilation > 1:
        lhs_ref = ucm[0]
        dst = lax.broadcasted_iota(I32, (ROW_CHUNK, ROW_CHUNK), 0)
        src = lax.broadcasted_iota(I32, (ROW_CHUNK, ROW_CHUNK), 1)
        perm = (src == (dst % per) * dilation + dst // per).astype(BF16)
        for c in range(tm // ROW_CHUNK):
            rows = slice(c * ROW_CHUNK, (c + 1) * ROW_CHUNK)
            lhs_ref[rows, :] = jnp.dot(perm, u_ref[rows, :], preferred_element_type=F32).astype(BF16)
    lane = lax.broadcasted_iota(I32, (per, LANES), 1)
    first_half = (lane % HEAD_DIM) < HALF
    for j, rope in enumerate(rope_tiles):
        for c in range(tm // ROW_CHUNK):
            acc = jnp.dot(lhs_ref[c * ROW_CHUNK:(c + 1) * ROW_CHUNK, :], w_ref[:, j * tn:(j + 1) * tn],
                          preferred_element_type=F32)
            for r in range(dilation):
                rows = slice(c * per, (c + 1) * per)
                if rope:
                    trows = rows if dilation == 1 else pl.ds(c * ROW_CHUNK + r, per, stride=dilation)
                    cos, sin = cos_ref[trows, :], sin_ref[trows, :]
                for k in range(nblk):
                    cols = slice(j * tn + k * LANES, j * tn + (k + 1) * LANES)
                    blk = acc[r * per:(r + 1) * per, k * LANES:(k + 1) * LANES]
                    if rope:
                        blk = _rope_block(blk, cos, sin, first_half)
                    if dilation > 1:
                        o_ref[r, rows, cols] = blk.astype(o_ref.dtype)
                    else:
                        o_ref[rows, cols] = blk.astype(o_ref.dtype)
    for j in range(plain_cols // tn):
        for c in range(tm // ROW_CHUNK):
            rows = slice(c * ROW_CHUNK, (c + 1) * ROW_CHUNK)
            acc = jnp.dot(u_ref[rows, :], w2_ref[:, j * tn:(j + 1) * tn], preferred_element_type=F32)
            o2_ref[rows, j * tn:(j + 1) * tn] = acc.astype(o2_ref.dtype)


def in_proj(h, g, scale, shift, w, rope_tiles, cos, sin, tn, dilation=1, w_block=0, plain=None):
    b, s, d = h.shape
    n = len(rope_tiles) * tn
    tm = min(s, 1024)
    assert tm % ROW_CHUNK == 0 and ROW_CHUNK % (16 * dilation) == 0 and (w_block + 1) * n <= w.shape[1]
    scratch = [pltpu.VMEM((tm, d), BF16)]
    tab_spec = pl.BlockSpec((None, tm, LANES), lambda bi, i: (bi, i, 0))
    if dilation == 1:
        out_spec = pl.BlockSpec((None, tm, n), lambda bi, i: (bi, i, 0))
        out_shape = jax.ShapeDtypeStruct((b, s, n), BF16)
    else:
        out_spec = pl.BlockSpec((None, dilation, tm // dilation, n), lambda bi, i: (bi, 0, i, 0))
        out_shape = jax.ShapeDtypeStruct((b, dilation, s // dilation, n), BF16)
        scratch.append(pltpu.VMEM((tm, d), BF16))
    in_specs = [
        pl.BlockSpec((None, tm, d), lambda bi, i: (bi, i, 0)),
        pl.BlockSpec((1, d), lambda bi, i: (0, 0)),
        pl.BlockSpec((None, 1, d), lambda bi, i: (bi, 0, 0)),
        pl.BlockSpec((None, 1, d), lambda bi, i: (bi, 0, 0)),
        pl.BlockSpec((d, n), lambda bi, i: (0, w_block)),
        tab_spec, tab_spec,
    ]
    args = [h, g.reshape(1, d), scale.reshape(b, 1, d), shift.reshape(b, 1, d), w, cos, sin]
    plain_cols = 0
    if plain is not None:
        plain_cols, plain_block = plain
        assert plain_cols % tn == 0 and (plain_block + 1) * plain_cols <= w.shape[1]
        in_specs.append(pl.BlockSpec((d, plain_cols), lambda bi, i: (0, plain_block)))
        args.append(w)
        out_spec = [out_spec, pl.BlockSpec((None, tm, plain_cols), lambda bi, i: (bi, i, 0))]
        out_shape = [out_shape, jax.ShapeDtypeStruct((b, s, plain_cols), BF16)]
    return pl.pallas_call(
        functools.partial(_in_proj_kernel, dilation=dilation, rope_tiles=tuple(rope_tiles), tn=tn,
                          plain_cols=plain_cols),
        grid=(b, s // tm),
        in_specs=in_specs,
        out_specs=out_spec,
        out_shape=out_shape,
        scratch_shapes=scratch,
        compiler_params=_cparams(("arbitrary", "arbitrary")),
        name=f"in_proj_d{dilation}",
    )(*args)


def _out_proj_kernel(y_ref, w_ref, gate_ref, h_ref, *rest):
    acc = jnp.dot(y_ref[...], w_ref[...], preferred_element_type=F32)
    h_new = h_ref[...] + gate_ref[...] * acc
    if len(rest) == 2:
        g_ref, o_ref = rest
        ms = jnp.mean(h_new * h_new, axis=-1, keepdims=True)
        o_ref[...] = h_new * lax.rsqrt(ms + NORM_EPS) * g_ref[...]
    else:
        rest[0][...] = h_new


def out_proj(y, w, gate, h, final_g=None):
    b, s, d = h.shape
    k = y.shape[-1]
    tm = min(s, 1024)
    in_specs = [
        pl.BlockSpec((None, tm, k), lambda bi, i: (bi, i, 0)),
        pl.BlockSpec((k, d), lambda bi, i: (0, 0)),
        pl.BlockSpec((None, 1, d), lambda bi, i: (bi, 0, 0)),
        pl.BlockSpec((None, tm, d), lambda bi, i: (bi, i, 0)),
    ]
    args = [y, w, gate.reshape(b, 1, d), h]
    if final_g is not None:
        in_specs.append(pl.BlockSpec((1, d), lambda bi, i: (0, 0)))
        args.append(final_g.reshape(1, d))
    return pl.pallas_call(
        _out_proj_kernel,
        grid=(b, s // tm),
        in_specs=in_specs,
        out_specs=pl.BlockSpec((None, tm, d), lambda bi, i: (bi, i, 0)),
        out_shape=jax.ShapeDtypeStruct((b, s, d), F32),
        compiler_params=_cparams(("arbitrary", "arbitrary")),
        name="out_proj",
    )(*args)


def _final_norm_kernel(x_ref, g_ref, o_ref):
    x = x_ref[...]
    ms = jnp.mean(x * x, axis=-1, keepdims=True)
    o_ref[...] = x * lax.rsqrt(ms + NORM_EPS) * g_ref[...]


def final_norm(h, g):
    b, s, d = h.shape
    tm = min(s, 1024)
    return pl.pallas_call(
        _final_norm_kernel,
        grid=(b, s // tm),
        in_specs=[
            pl.BlockSpec((None, tm, d), lambda bi, i: (bi, i, 0)),
            pl.BlockSpec((1, d), lambda bi, i: (0, 0)),
        ],
        out_specs=pl.BlockSpec((None, tm, d), lambda bi, i: (bi, i, 0)),
        out_shape=jax.ShapeDtypeStruct((b, s, d), F32),
        compiler_params=_cparams(("arbitrary", "arbitrary")),
        name="final_norm",
    )(h, g.reshape(1, d))


def _stack_heads(pair, first_head):
    zero = jnp.zeros_like(pair)
    return jnp.concatenate([jnp.where(first_head, pair, zero), jnp.where(first_head, zero, pair)], axis=0)


def _banded_kernel(q_ref, kp_ref, kc_ref, vp_ref, vc_ref, o_ref, lse_ref, kwin_ref, vwin_ref, *, tq):
    i = pl.program_id(2)
    n_cls = q_ref.shape[0]
    nsub = tq // A_BLK
    lane = lax.broadcasted_iota(I32, (A_BLK, LANES), 1)
    first_head = lane < HEAD_DIM
    odd_lane = (lane % 2) == 1
    r2 = lax.broadcasted_iota(I32, (2 * A_BLK, 2 * A_BLK), 0) % A_BLK
    c2 = lax.broadcasted_iota(I32, (2 * A_BLK, 2 * A_BLK), 1)
    own_ok = (c2 >= A_BLK) & (c2 - A_BLK <= r2)
    ones_blk = jnp.ones((2 * A_BLK, LANES), BF16)
    n_pairs = A_HEADS // 2

    kwin_ref[:, :A_BLK, :] = kp_ref[...]
    kwin_ref[:, A_BLK:, :] = kc_ref[...]
    vwin_ref[:, :A_BLK, :] = vp_ref[...]
    vwin_ref[:, A_BLK:, :] = vc_ref[...]

    def logits(r, a, p):
        cols = slice(p * LANES, (p + 1) * LANES)
        qs = _stack_heads(q_ref[r, a * A_BLK:(a + 1) * A_BLK, cols], first_head)
        k2 = kwin_ref[r, a * A_BLK:(a + 2) * A_BLK, cols]
        return lax.dot_general(qs, k2, (((1,), (1,)), ((), ())), preferred_element_type=F32)

    def band_bias(prev_shift):
        mask = own_ok | ((c2 < A_BLK) & (c2 >= r2 + prev_shift))
        return jnp.where(mask, 0.0, -jnp.inf).astype(F32)

    bias_inner = band_bias(0)
    biases = [band_bias(jnp.where(i > 0, 0, 2 * A_BLK))] + [bias_inner] * (nsub - 1)

    units = [(r, a, p) for r in range(n_cls) for a in range(nsub) for p in range(n_pairs)]
    queue = [logits(*units[u]) for u in range(BANDED_LOOKAHEAD)]
    lse_tile = None
    for u, (r, a, p) in enumerate(units):
        rows = slice(a * A_BLK, (a + 1) * A_BLK)
        cols = slice(p * LANES, (p + 1) * LANES)
        bias = biases[a]
        if p == 0:
            lse_tile = jnp.zeros((A_BLK, LANES), F32)
        s = queue.pop(0) + bias
        if u + BANDED_LOOKAHEAD < len(units):
            queue.append(logits(*units[u + BANDED_LOOKAHEAD]))
        m = jnp.max(s, axis=-1, keepdims=True)
        e = jnp.exp2(s - m).astype(BF16)
        v_ext = jnp.concatenate([vwin_ref[r, a * A_BLK:(a + 2) * A_BLK, cols], ones_blk], axis=1)
        pv = jnp.dot(e, v_ext, preferred_element_type=F32)
        num = jnp.where(first_head, pv[:A_BLK, :LANES], pv[A_BLK:, :LANES])
        o_ref[r, rows, cols] = num * (1.0 / jnp.where(first_head, pv[:A_BLK, LANES:], pv[A_BLK:, LANES:]))
        den_pair = jnp.where(odd_lane, pv[A_BLK:, LANES:], pv[:A_BLK, LANES:])
        m_pair = jnp.where(odd_lane, m[A_BLK:], m[:A_BLK])
        lse_tile = jnp.where((lane // 2) == p, m_pair + jnp.log(den_pair) * LOG2_E, lse_tile)
        if p == n_pairs - 1:
            lse_ref[r, rows, :] = lse_tile


BANDED_ROWS = 1024


def banded_group_attention(qkv):
    b, d, n, _ = qkv.shape
    tq = min(BANDED_ROWS, n)
    rc = min(BANDED_ROWS // tq, d)
    sub = tq // A_BLK
    assert n % tq == 0 and d % rc == 0

    def cur(blk, width=A_WIDTH):
        return pl.BlockSpec((None, rc, tq, width), lambda bi, r, i: (bi, r, i, blk))

    def prev(blk):
        return pl.BlockSpec((None, rc, A_BLK, A_WIDTH),
                            lambda bi, r, i: (bi, r, jnp.maximum(i * sub - 1, 0), blk))

    return pl.pallas_call(
        functools.partial(_banded_kernel, tq=tq),
        grid=(b, d // rc, n // tq),
        in_specs=[cur(0), prev(1), cur(1), prev(2), cur(2)],
        out_specs=[cur(0), cur(0, LANES)],
        out_shape=[
            jax.ShapeDtypeStruct((b, d, n, A_WIDTH), F32),
            jax.ShapeDtypeStruct((b, d, n, LANES), F32),
        ],
        scratch_shapes=[pltpu.VMEM((rc, tq + A_BLK, A_WIDTH), BF16)] * 2,
        compiler_params=_cparams(("arbitrary", "arbitrary", "arbitrary")),
        name=f"banded_attn_d{d}",
    )(qkv, qkv, qkv, qkv, qkv)


def _expand_heads(x, expand):
    hi = x.astype(BF16)
    lo = (x - hi.astype(F32)).astype(BF16)
    return (jnp.dot(hi, expand, preferred_element_type=F32)
            + jnp.dot(lo, expand, preferred_element_type=F32))


def _merge_out_kernel(o0_ref, o1_ref, o2_ref, l0_ref, l1_ref, l2_ref, gate_ref, ex_ref, w_ref,
                      mod_ref, h_ref, hout_ref, y_ref, ot_ref, lt_ref, *, dilations):
    o_refs, l_refs = (o0_ref, o1_ref, o2_ref), (l0_ref, l1_ref, l2_ref)
    tm = y_ref.shape[0]
    nblk = y_ref.shape[1] // LANES
    for g, d in enumerate(dilations):
        if d == 1:
            continue
        per = tm // d
        for r in range(d):
            lt_ref[g, pl.ds(r, per, stride=d), :] = l_refs[g][r]
            for k in range(nblk):
                ot_ref[g, k, pl.ds(r, per, stride=d), :] = o_refs[g][r, :, k * LANES:(k + 1) * LANES]
    lses = [l_refs[g][0] if d == 1 else lt_ref[g] for g, d in enumerate(dilations)]
    m = jnp.maximum(jnp.maximum(lses[0], lses[1]), lses[2])
    es = [jnp.exp2(l - m) for l in lses]
    inv = 1.0 / (es[0] + es[1] + es[2])
    ex = ex_ref[...]
    alphas = [_expand_heads(e * inv, ex) for e in es[:-1]]
    alphas.append(1.0 - alphas[0] - alphas[1])
    for k in range(nblk):
        cols = slice(k * LANES, (k + 1) * LANES)
        y = jnp.zeros((tm, LANES), F32)
        for g, d in enumerate(dilations):
            og = o_refs[g][0, :, cols] if d == 1 else ot_ref[g, k]
            y += alphas[g][:, cols] * og
        gate = gate_ref[:, cols].astype(F32)
        y_ref[:, cols] = (y * gate * (1.0 / (1.0 + jnp.exp(-gate)))).astype(y_ref.dtype)
    acc = jnp.dot(y_ref[...], w_ref[...], preferred_element_type=F32)
    hout_ref[...] = h_ref[...] + mod_ref[...] * acc


def merge_out_proj(outs, lses, gate, w_out, mod_gate, h):
    b, s, w = gate.shape
    d_model = h.shape[-1]
    tm = min(s, 512)
    dil = tuple(o.shape[1] for o in outs)
    expand = (jnp.arange(LANES)[:, None] == (jnp.arange(w)[None, :] // HEAD_DIM)).astype(BF16)

    def cm(d, width):
        return pl.BlockSpec((None, d, tm // d, width), lambda bi, i: (bi, 0, i, 0))

    def tspec(width):
        return pl.BlockSpec((None, tm, width), lambda bi, i: (bi, i, 0))

    return pl.pallas_call(
        functools.partial(_merge_out_kernel, dilations=dil),
        grid=(b, s // tm),
        in_specs=[cm(d, w) for d in dil] + [cm(d, LANES) for d in dil]
                 + [tspec(w), pl.BlockSpec((LANES, w), lambda bi, i: (0, 0)),
                    pl.BlockSpec((w, d_model), lambda bi, i: (0, 0)),
                    pl.BlockSpec((None, 1, d_model), lambda bi, i: (bi, 0, 0)),
                    tspec(d_model)],
        out_specs=tspec(d_model),
        out_shape=jax.ShapeDtypeStruct((b, s, d_model), F32),
        scratch_shapes=[pltpu.VMEM((tm, w), BF16),
                        pltpu.VMEM((len(dil), w // LANES, tm, LANES), F32),
                        pltpu.VMEM((len(dil), tm, LANES), F32)],
        compiler_params=_cparams(("arbitrary", "arbitrary")),
        name="merge_out_proj",
    )(*outs, *lses, gate, expand, w_out, mod_gate.reshape(b, 1, d_model), h)


def dilated_layer(h, g, scale, shift, mod_gate, w_in, w_out, cos, sin):
    tn = 512
    per = A_WIDTH // tn
    qkv_flags = (True,) * (2 * per) + (False,) * per
    n_groups = len(A_GROUPS)
    col = np.arange(w_in.shape[1])
    is_q = (col < n_groups * 3 * A_WIDTH) & ((col // A_WIDTH) % 3 == 0)
    w = (w_in * jnp.asarray(np.where(is_q, Q_PRESCALE, 1.0), F32)).astype(BF16)
    outs, lses = [], []
    gate = None
    for gi, (window, dilation) in enumerate(A_GROUPS):
        assert window // dilation == A_BLK
        plain = (A_WIDTH, 3 * n_groups) if gi == 0 else None
        qkv = in_proj(h, g, scale, shift, w, qkv_flags, cos, sin, tn, dilation,
                      w_block=gi, plain=plain)
        if plain is not None:
            qkv, gate = qkv
        if dilation == 1:
            qkv = qkv[:, None]
        o, lse = banded_group_attention(qkv)
        outs.append(o)
        lses.append(lse)
    return merge_out_proj(outs, lses, gate, w_out.astype(BF16), mod_gate, h)


B_COLS = 3584
B_ROPE_FLAGS = (1, 1, 0, 0, 1, 1, 0)


def _dsa_head_perm():
    group = B_HEADS // B_KV_HEADS
    order = []
    for g2 in range(B_KV_HEADS // 2):
        for r in range(group):
            order += [(2 * g2) * group + r, (2 * g2 + 1) * group + r]
    return np.asarray(order)


def _dsa_weights(w_in, w_out):
    d = w_in.shape[0]
    cuts = np.cumsum((B_WIDTH, B_KV_HEADS * HEAD_DIM, B_KV_HEADS * HEAD_DIM, IDX_HEADS * HEAD_DIM,
                      HEAD_DIM, IDX_HEADS, B_WIDTH))[:-1]
    wq, wk, wv, wqi, wki, wwi, wg = jnp.split(w_in, cuts, axis=1)
    cols = (_dsa_head_perm()[:, None] * HEAD_DIM + np.arange(HEAD_DIM)[None, :]).reshape(-1)
    zeros = lambda n: jnp.zeros((d, n), w_in.dtype)
    wq = wq * Q_PRESCALE
    w = jnp.concatenate([wq[:, cols], wg[:, cols], wqi, wk, wki, wki, zeros(LANES),
                         wwi, zeros(2 * LANES - IDX_HEADS), wv], axis=1)
    assert w.shape[1] == B_COLS
    return w.astype(BF16), w_out[cols, :].astype(BF16)


def _sortable(x):
    bits = pltpu.bitcast(x, I32)
    return bits ^ ((bits >> 31) & jnp.int32(0x7FFFFFFF))


def _bit_transpose32(words):
    a = list(words)
    j, m = 16, 0x0000FFFF
    while j:
        mask = jnp.int32(m - (1 << 32) if m >= (1 << 31) else m)
        shift = jnp.full(a[0].shape, j, I32)
        k = 0
        while k < 32:
            t = (a[k] ^ lax.shift_right_logical(a[k + j], shift)) & mask
            a[k] = a[k] ^ t
            a[k + j] = a[k + j] ^ lax.shift_left(t, shift)
            k = (k + j + 1) & ~j
        j >>= 1
        m = (m ^ (m << j)) & 0xFFFFFFFF
    return a


def _sublane_allsum(x):
    x = x + pltpu.roll(x, 4, 0)
    x = x + pltpu.roll(x, 2, 0)
    return x + pltpu.roll(x, 1, 0)


def _tree_sum(parts):
    parts = list(parts)
    while len(parts) > 1:
        odd = [parts[-1]] if len(parts) % 2 else []
        parts = [parts[j] + parts[j + 1] for j in range(0, len(parts) - 1, 2)] + odd
    return parts[0]


GROUP_KEYS = 256
ATTN_LOOKAHEAD = 1


def _dsa_kernel(q_ref, gate_ref, qi_ref, k_ref, ki_ref, wi_ref, v_ref, y_ref,
                keys_ref, planes_ref, bias_ref, kt_ref, qs_ref, s0_ref, m_ref, l_ref, acc_ref,
                *, k_sel):
    i = pl.program_id(1)
    seq = keys_ref.shape[1]
    n_sub = Q_BLOCK // SEL_LANES
    n_pairs = B_HEADS // 2
    n_groups_max = seq // GROUP_KEYS
    lane_q = lax.broadcasted_iota(I32, (Q_BLOCK, LANES), 1)
    first_head_q = lane_q < HEAD_DIM
    first_head = lax.broadcasted_iota(I32, (SEL_LANES, LANES), 1) < HEAD_DIM
    key_row = lax.broadcasted_iota(I32, (KEY_TILE, SEL_LANES), 0)
    lane_idx = lax.broadcasted_iota(I32, (KEY_TILE, SEL_LANES), 1)

    @pl.when(i == 0)
    def _():
        planes_ref[...] = jnp.zeros(planes_ref.shape, I32)
        keys_ref[...] = jnp.zeros(keys_ref.shape, I32)

    def tiles_of(a):
        return ((i * n_sub + a) * SEL_LANES) // KEY_TILE + 1

    def select(a):
        n_tiles = tiles_of(a)
        rows_a = slice(a * SEL_LANES, (a + 1) * SEL_LANES)
        q_pos = (i * n_sub + a) * SEL_LANES + lane_idx
        keys = keys_ref.at[a]

        w_t = (wi_ref[rows_a, :].astype(F32) * (IDX_HEADS ** -0.5 * HEAD_DIM ** -0.5)).T
        qi_stacked = [_stack_heads(qi_ref[rows_a, p * LANES:(p + 1) * LANES], first_head)
                      for p in range(IDX_HEADS // 2)]

        def score_tile(t):
            base = pl.multiple_of(t * KEY_TILE, KEY_TILE)
            kk = ki_ref[pl.ds(base, KEY_TILE), :]
            score = jnp.zeros((KEY_TILE, SEL_LANES), F32)
            for p in range(IDX_HEADS // 2):
                sc = lax.dot_general(kk, qi_stacked[p], (((1,), (1,)), ((), ())),
                                     preferred_element_type=F32)
                score += jnp.maximum(sc[:, :SEL_LANES], 0.0) * w_t[2 * p:2 * p + 1, :]
                score += jnp.maximum(sc[:, SEL_LANES:], 0.0) * w_t[2 * p + 1:2 * p + 2, :]
            causal = (key_row + base) <= q_pos
            score = jnp.where(causal, score + 0.0, -jnp.inf)
            key = _sortable(score)
            keys[pl.ds(base, KEY_TILE), :] = key
            ukey = key ^ jnp.int32(INT_MIN)
            for g in range(KEY_TILE // GROUP_KEYS):
                words = [ukey[g * GROUP_KEYS + 8 * j:g * GROUP_KEYS + 8 * j + 8, :] for j in range(32)]
                rows = pl.ds(pl.multiple_of(t * (KEY_TILE // 32) + 8 * g, 8), 8)
                for b, plane in enumerate(_bit_transpose32(words)):
                    planes_ref[b, rows, :] = plane

        def score_two(t2, carry):
            score_tile(2 * t2)
            score_tile(2 * t2 + 1)
            return carry

        lax.fori_loop(0, n_tiles // 2, score_two, 0)

        @pl.when(n_tiles % 2 == 1)
        def _():
            score_tile(n_tiles - 1)

        def select_threshold():
            n_groups = n_tiles * (KEY_TILE // GROUP_KEYS)
            alive0 = tuple(jnp.full((8, SEL_LANES), -1, I32) * (g < n_groups).astype(I32)
                           for g in range(n_groups_max))
            need0 = jnp.full((8, SEL_LANES), k_sel, I32)

            def step(st, carry):
                alive, need, prefix = carry
                planes = [planes_ref[st, 8 * g:8 * g + 8, :] for g in range(n_groups_max)]
                ones = [al & pln for al, pln in zip(alive, planes)]
                cnt = _sublane_allsum(_tree_sum([lax.population_count(o) for o in ones]))
                take = cnt >= need
                prefix = jnp.where(take, prefix | jnp.left_shift(jnp.int32(1), 31 - st), prefix)
                need = jnp.where(take, need, need - cnt)
                flip = jnp.where(take, 0, -1)
                alive = tuple(al & (pln ^ flip) for al, pln in zip(alive, planes))
                return alive, need, prefix

            alive, need, prefix = lax.fori_loop(
                0, 32, step, (alive0, need0, jnp.zeros((8, SEL_LANES), I32)))
            thr = (prefix ^ jnp.int32(INT_MIN))[:1, :]
            need_eq = need[:1, :]

            n_eq = _sublane_allsum(_tree_sum([lax.population_count(al) for al in alive]))[:1, :]
            tied = (n_eq > need_eq) & (thr > KEY_NEG_INF)
            idx_bits = (seq - 1).bit_length()

            def tie_break():
                def count_eq_before(cut):
                    def body(t, acc):
                        base = pl.multiple_of(t * KEY_TILE, KEY_TILE)
                        hit = (keys[pl.ds(base, KEY_TILE), :] == thr) & ((key_row + base) < cut)
                        return acc + jnp.sum(hit.astype(I32).reshape(KEY_TILE // 8, 8, SEL_LANES), axis=0)
                    acc = lax.fori_loop(0, n_tiles, body, jnp.zeros((8, SEL_LANES), I32))
                    return jnp.sum(acc, axis=0, keepdims=True)

                def bit_step(it, cut):
                    cand = cut + jnp.left_shift(jnp.int32(1), idx_bits - 1 - it)
                    return jnp.where(count_eq_before(cand) < need_eq, cand, cut)

                return lax.fori_loop(0, idx_bits, bit_step, jnp.zeros((1, SEL_LANES), I32))

            any_tied = jnp.max(tied.astype(I32)) > 0
            idx_cut = lax.cond(any_tied, tie_break, lambda: jnp.full((1, SEL_LANES), seq, I32))
            return thr, idx_cut

        all_selected = (i * n_sub + a + 1) * SEL_LANES <= k_sel
        thr, idx_cut = lax.cond(
            all_selected,
            lambda: (jnp.full((1, SEL_LANES), INT_MIN, I32), jnp.full((1, SEL_LANES), seq, I32)),
            select_threshold)
        return thr, idx_cut, q_pos

    selected = [select(a) for a in range(n_sub)]
    n_tiles = tiles_of(n_sub - 1)

    def write_bias(base):
        k_idx = key_row + base
        for a, (thr, idx_cut, q_pos) in enumerate(selected):
            key = keys_ref[a, pl.ds(base, KEY_TILE), :]
            sel = ((key > thr) | ((key == thr) & (k_idx <= idx_cut))) & (k_idx <= q_pos)
            bias_t = jnp.where(sel, 0.0, -jnp.inf).astype(F32)
            for c in range(KEY_TILE // LANES):
                col = base + c * LANES
                col = col if isinstance(col, int) else pl.multiple_of(col, LANES)
                bias_ref[a * SEL_LANES:(a + 1) * SEL_LANES, pl.ds(col, LANES)] = \
                    bias_t[c * LANES:(c + 1) * LANES, :].T

    write_bias(0)

    for p in range(n_pairs):
        cols = slice(p * LANES, (p + 1) * LANES)
        qs_ref[p] = _stack_heads(q_ref[:, cols], first_head_q)
    heads_per_kv_pair = 2 * (B_HEADS // B_KV_HEADS) // 2

    ones_blk = jnp.ones((KEY_TILE, LANES), BF16)

    @pl.when(i == 0)
    def _():
        def xpose(r, carry):
            rows = pl.ds(pl.multiple_of(r * LANES, LANES), LANES)
            for c in range(kt_ref.shape[0]):
                blk = k_ref[rows, c * LANES:(c + 1) * LANES].astype(F32)
                kt_ref[c, :, rows] = blk.T.astype(BF16)
            return carry
        lax.fori_loop(0, seq // LANES, xpose, 0)

    def logits(base, p):
        kt = kt_ref[p // heads_per_kv_pair, :, pl.ds(base, KEY_TILE)]
        return jnp.dot(qs_ref[p], kt, preferred_element_type=F32)

    lookahead = s0_ref.shape[0]
    for u in range(lookahead):
        s0_ref[u] = logits(0, u)

    def attn_tiles(t0, count, start=False):
        bases = [pl.multiple_of(jnp.minimum(t0 + dt, n_tiles - 1) * KEY_TILE, KEY_TILE)
                 for dt in range(count + 1)]
        units = [(dt, p) for dt in range(count + 1) for p in range(n_pairs)]
        queue = [s0_ref[u] for u in range(lookahead)]
        for u, (dt, p) in enumerate(units[:count * n_pairs]):
            base = bases[dt]
            if p == 0:
                bias = bias_ref[:, pl.ds(base, KEY_TILE)]
                write_bias(bases[dt + 1])
                bias2 = jnp.concatenate([bias, bias], axis=0)
            kv = p // heads_per_kv_pair
            kv_cols = slice(kv * LANES, (kv + 1) * LANES)
            s = queue.pop(0) + bias2
            ahead_dt, ahead_p = units[u + lookahead]
            queue.append(logits(bases[ahead_dt], ahead_p))
            m_blk = s[:, :LANES]
            for c in range(1, KEY_TILE // LANES):
                m_blk = jnp.maximum(m_blk, s[:, c * LANES:(c + 1) * LANES])
            opening = start and dt == 0
            m_cur = jnp.max(m_blk, axis=-1, keepdims=True)
            if opening:
                m_new = jnp.broadcast_to(jnp.maximum(m_cur, F32_MIN), m_blk.shape)
            else:
                m_old = m_ref[p]
                m_new = jnp.maximum(m_old, m_cur)
                alpha = jnp.exp2(m_old - m_new)
            e = jnp.concatenate(
                [jnp.exp2(s[:, c * LANES:(c + 1) * LANES] - m_new).astype(BF16)
                 for c in range(KEY_TILE // LANES)], axis=1)
            v_ext = jnp.concatenate([v_ref[pl.ds(base, KEY_TILE), kv_cols], ones_blk], axis=1)
            pv = jnp.dot(e, v_ext, preferred_element_type=F32)
            if opening:
                acc_ref[p] = pv[:, :LANES]
                l_ref[p] = pv[:, LANES:]
            else:
                acc_ref[p] = acc_ref[p] * alpha + pv[:, :LANES]
                l_ref[p] = l_ref[p] * alpha + pv[:, LANES:]
            m_ref[p] = m_new
        for u in range(lookahead):
            s0_ref[u] = queue[u]

    attn_tiles(0, 1, start=True)

    def attn_two(t2, carry):
        attn_tiles(1 + 2 * t2, 2)
        return carry

    lax.fori_loop(0, (n_tiles - 1) // 2, attn_two, 0)

    @pl.when(n_tiles % 2 == 0)
    def _():
        attn_tiles(n_tiles - 1, 1)

    for p in range(n_pairs):
        cols = slice(p * LANES, (p + 1) * LANES)
        l = l_ref[p]
        acc = acc_ref[p]
        num = jnp.where(first_head_q, acc[:Q_BLOCK], acc[Q_BLOCK:])
        den = jnp.where(first_head_q, l[:Q_BLOCK], l[Q_BLOCK:])
        gate = gate_ref[:, cols].astype(F32)
        y_ref[:, cols] = (num * gate / (den * (1.0 + jnp.exp(-gate)))).astype(y_ref.dtype)


def dsa_attention(proj):
    b, s, _ = proj.shape
    k_sel = min(TOPK_MAX, s // 4)
    assert s % KEY_TILE == 0 and s % Q_BLOCK == 0 and k_sel <= KEY_TILE
    kvw = B_KV_HEADS * HEAD_DIM
    n_pairs = B_HEADS // 2

    def qblock(width, idx):
        return pl.BlockSpec((None, Q_BLOCK, width), lambda bi, i: (bi, i, idx))

    def full(width, idx):
        return pl.BlockSpec((None, s, width), lambda bi, i: (bi, 0, idx))

    return pl.pallas_call(
        functools.partial(_dsa_kernel, k_sel=k_sel),
        grid=(b, s // Q_BLOCK),
        in_specs=[
            qblock(B_WIDTH, 0),
            qblock(B_WIDTH, 1),
            qblock(IDX_HEADS * HEAD_DIM, 4),
            full(kvw, 10),
            full(LANES, 22),
            qblock(LANES, 24),
            full(kvw, 13),
        ],
        out_specs=qblock(B_WIDTH, 0),
        out_shape=jax.ShapeDtypeStruct((b, s, B_WIDTH), BF16),
        scratch_shapes=[
            pltpu.VMEM((Q_BLOCK // SEL_LANES, s, SEL_LANES), I32),
            pltpu.VMEM((32, s // 32, SEL_LANES), I32),
            pltpu.VMEM((Q_BLOCK, s), F32),
            pltpu.VMEM((kvw // LANES, LANES, s), BF16),
            pltpu.VMEM((n_pairs, 2 * Q_BLOCK, LANES), BF16),
            pltpu.VMEM((ATTN_LOOKAHEAD, 2 * Q_BLOCK, KEY_TILE), F32),
            pltpu.VMEM((n_pairs, 2 * Q_BLOCK, LANES), F32),
            pltpu.VMEM((n_pairs, 2 * Q_BLOCK, LANES), F32),
            pltpu.VMEM((n_pairs, 2 * Q_BLOCK, LANES), F32),
        ],
        compiler_params=_cparams(("arbitrary", "arbitrary")),
        name="dsa_attention",
    )(proj, proj, proj, proj, proj, proj, proj)


def dsa_mixer(h, g, scale, shift, w_in_packed, cos, sin):
    proj = in_proj(h, g, scale, shift, w_in_packed, [f == 1 for f in B_ROPE_FLAGS], cos, sin, 512)
    return dsa_attention(proj)


def kernel(x, c, positions, norm_g, ada_w, ada_b, a_w_in, a_w_out, b_w_in, b_w_out, final_g):
    depth = norm_g.shape[0]
    d = x.shape[-1]
    mod = adaln_mod(c, ada_w, ada_b)
    cos, sin = rope_tables(positions)
    h = x
    for i in range(depth):
        shift, scale, gate = mod[i, :, :d], mod[i, :, d:2 * d], mod[i, :, 2 * d:]
        last = i == depth - 1
        if i % 2 == 0:
            h = dilated_layer(h, norm_g[i], scale, shift, gate, a_w_in[i // 2], a_w_out[i // 2], cos, sin)
            if last:
                h = final_norm(h, final_g)
        else:
            w_in, w_out = _dsa_weights(b_w_in[i // 2], b_w_out[i // 2])
            y = dsa_mixer(h, norm_g[i], scale, shift, w_in, cos, sin)
            h = out_proj(y, w_out, gate, h, final_g if last else None)
    return h
```

```python
import functools

import numpy as np
import jax
import jax.numpy as jnp
from jax import lax
from jax.experimental import pallas as pl
from jax.experimental.pallas import tpu as pltpu

F32 = jnp.float32
BF16 = jnp.bfloat16
I32 = jnp.int32

LANES = 128
HEAD_DIM = 64
HALF = HEAD_DIM // 2
ROPE_THETA = 10000.0
NORM_EPS = 1e-6
VMEM_LIMIT = 56 * 1024 * 1024
ROW_CHUNK = 256
ROW_TILE = 1024
PROJ_COL_TILE = 512
MERGE_ROWS = 512

A_HEADS = 16
A_WIDTH = A_HEADS * HEAD_DIM
A_GROUPS = ((128, 1), (512, 4), (2048, 16))
A_BLK = 128
BANDED_LOOKAHEAD = 1

B_HEADS = 16
B_KV_HEADS = 4
B_WIDTH = B_HEADS * HEAD_DIM
IDX_HEADS = 8
TOPK_MAX = 256
Q_BLOCK = 256
SEL_LANES = 128
KEY_TILE = 512
INT_MIN = -(2 ** 31)
KEY_NEG_INF = -2139095041
F32_MIN = float(np.finfo(np.float32).min)
LOG2_E = float(np.log2(np.e))
Q_PRESCALE = LOG2_E * HEAD_DIM ** -0.5


def _cparams(sem):
    return pltpu.CompilerParams(dimension_semantics=sem, vmem_limit_bytes=VMEM_LIMIT)


def _adaln_kernel(c_ref, w_ref, b_ref, o_ref):
    c = c_ref[...]
    ca = (c * (1.0 / (1.0 + jnp.exp(-c)))).astype(BF16)
    acc = jnp.dot(ca, w_ref[...].astype(BF16), preferred_element_type=F32)
    o_ref[...] = acc + b_ref[...]


def adaln_mod(c, ada_w, ada_b):
    depth, d, d3 = ada_w.shape
    b = c.shape[0]
    tn = 1024
    return pl.pallas_call(
        _adaln_kernel,
        grid=(depth, d3 // tn),
        in_specs=[
            pl.BlockSpec((b, d), lambda i, j: (0, 0)),
            pl.BlockSpec((None, d, tn), lambda i, j: (i, 0, j)),
            pl.BlockSpec((None, 1, tn), lambda i, j: (i, 0, j)),
        ],
        out_specs=pl.BlockSpec((None, b, tn), lambda i, j: (i, 0, j)),
        out_shape=jax.ShapeDtypeStruct((depth, b, d3), F32),
        compiler_params=_cparams(("arbitrary", "arbitrary")),
        name="adaln_mod",
    )(c, ada_w, ada_b.reshape(depth, 1, d3))


def _rope_table_kernel(pos_ref, inv_ref, cos_ref, sin_ref):
    ang = pos_ref[...].astype(F32) * inv_ref[...]
    lane = lax.broadcasted_iota(I32, ang.shape, 1)
    first_half = (lane % HEAD_DIM) < HALF
    cos_ref[...] = jnp.cos(ang)
    s = jnp.sin(ang)
    sin_ref[...] = jnp.where(first_half, -s, s)


def rope_tables(positions):
    b, s = positions.shape
    inv_freq = ROPE_THETA ** (-jnp.arange(HALF, dtype=F32) / HALF)
    inv_lane = jnp.tile(inv_freq, LANES // HALF).reshape(1, LANES)
    ts = min(s, ROW_TILE)
    out = jax.ShapeDtypeStruct((b, s, LANES), F32)
    return pl.pallas_call(
        _rope_table_kernel,
        grid=(b, s // ts),
        in_specs=[
            pl.BlockSpec((None, ts, 1), lambda i, j: (i, j, 0)),
            pl.BlockSpec((1, LANES), lambda i, j: (0, 0)),
        ],
        out_specs=[pl.BlockSpec((None, ts, LANES), lambda i, j: (i, j, 0))] * 2,
        out_shape=[out, out],
        compiler_params=_cparams(("arbitrary", "arbitrary")),
        name="rope_tables",
    )(positions.reshape(b, s, 1), inv_lane)


def _rope_block(t, cos, sin_signed, first_half):
    partner = jnp.where(first_half, pltpu.roll(t, LANES - HALF, 1), pltpu.roll(t, HALF, 1))
    return t * cos + partner * sin_signed


def _in_proj_kernel(x_ref, g_ref, sc_ref, sh_ref, w_ref, cos_ref, sin_ref, *rest,
                    dilation, rope_tiles, tn, plain_cols):
    if plain_cols:
        w2_ref, o_ref, o2_ref, u_ref, *ucm = rest
    else:
        o_ref, u_ref, *ucm = rest
    tm = u_ref.shape[0]
    nblk = tn // LANES
    per = ROW_CHUNK // dilation
    x = x_ref[...]
    ms = jnp.mean(x * x, axis=-1, keepdims=True)
    xn = x * lax.rsqrt(ms + NORM_EPS)
    u_ref[...] = (xn * g_ref[...] * (1.0 + sc_ref[...]) + sh_ref[...]).astype(BF16)
    lhs_ref = u_ref
    if dilation > 1:
        lhs_ref = ucm[0]
        dst = lax.broadcasted_iota(I32, (ROW_CHUNK, ROW_CHUNK), 0)
        src = lax.broadcasted_iota(I32, (ROW_CHUNK, ROW_CHUNK), 1)
        perm = (src == (dst % per) * dilation + dst // per).astype(BF16)
        for c in range(tm // ROW_CHUNK):
            rows = slice(c * ROW_CHUNK, (c + 1) * ROW_CHUNK)
            lhs_ref[rows, :] = jnp.dot(perm, u_ref[rows, :], preferred_element_type=F32).astype(BF16)
    lane = lax.broadcasted_iota(I32, (per, LANES), 1)
    first_half = (lane % HEAD_DIM) < HALF
    for j, rope in enumerate(rope_tiles):
        for c in range(tm // ROW_CHUNK):
            acc = jnp.dot(lhs_ref[c * ROW_CHUNK:(c + 1) * ROW_CHUNK, :], w_ref[:, j * tn:(j + 1) * tn],
                          preferred_element_type=F32)
            for r in range(dilation):
                rows = slice(c * per, (c + 1) * per)
                if rope:
                    trows = rows if dilation == 1 else pl.ds(c * ROW_CHUNK + r, per, stride=dilation)
                    cos, sin = cos_ref[trows, :], sin_ref[trows, :]
                for k in range(nblk):
                    cols = slice(j * tn + k * LANES, j * tn + (k + 1) * LANES)
                    blk = acc[r * per:(r + 1) * per, k * LANES:(k + 1) * LANES]
                    if rope:
                        blk = _rope_block(blk, cos, sin, first_half)
                    if dilation > 1:
                        o_ref[r, rows, cols] = blk.astype(o_ref.dtype)
                    else:
                        o_ref[rows, cols] = blk.astype(o_ref.dtype)
    for j in range(plain_cols // tn):
        for c in range(tm // ROW_CHUNK):
            rows = slice(c * ROW_CHUNK, (c + 1) * ROW_CHUNK)
            acc = jnp.dot(u_ref[rows, :], w2_ref[:, j * tn:(j + 1) * tn], preferred_element_type=F32)
            o2_ref[rows, j * tn:(j + 1) * tn] = acc.astype(o2_ref.dtype)


def in_proj(h, g, scale, shift, w, rope_tiles, cos, sin, tn, dilation=1, w_block=0, plain=None):
    b, s, d = h.shape
    n = len(rope_tiles) * tn
    tm = min(s, ROW_TILE)
    assert tm % ROW_CHUNK == 0 and ROW_CHUNK % (16 * dilation) == 0 and (w_block + 1) * n <= w.shape[1]
    scratch = [pltpu.VMEM((tm, d), BF16)]
    tab_spec = pl.BlockSpec((None, tm, LANES), lambda bi, i: (bi, i, 0))
    if dilation == 1:
        out_spec = pl.BlockSpec((None, tm, n), lambda bi, i: (bi, i, 0))
        out_shape = jax.ShapeDtypeStruct((b, s, n), BF16)
    else:
        out_spec = pl.BlockSpec((None, dilation, tm // dilation, n), lambda bi, i: (bi, 0, i, 0))
        out_shape = jax.ShapeDtypeStruct((b, dilation, s // dilation, n), BF16)
        scratch.append(pltpu.VMEM((tm, d), BF16))
    in_specs = [
        pl.BlockSpec((None, tm, d), lambda bi, i: (bi, i, 0)),
        pl.BlockSpec((1, d), lambda bi, i: (0, 0)),
        pl.BlockSpec((None, 1, d), lambda bi, i: (bi, 0, 0)),
        pl.BlockSpec((None, 1, d), lambda bi, i: (bi, 0, 0)),
        pl.BlockSpec((d, n), lambda bi, i: (0, w_block)),
        tab_spec, tab_spec,
    ]
    args = [h, g.reshape(1, d), scale.reshape(b, 1, d), shift.reshape(b, 1, d), w, cos, sin]
    plain_cols = 0
    if plain is not None:
        plain_cols, plain_block = plain
        assert plain_cols % tn == 0 and (plain_block + 1) * plain_cols <= w.shape[1]
        in_specs.append(pl.BlockSpec((d, plain_cols), lambda bi, i: (0, plain_block)))
        args.append(w)
        out_spec = [out_spec, pl.BlockSpec((None, tm, plain_cols), lambda bi, i: (bi, i, 0))]
        out_shape = [out_shape, jax.ShapeDtypeStruct((b, s, plain_cols), BF16)]
    return pl.pallas_call(
        functools.partial(_in_proj_kernel, dilation=dilation, rope_tiles=tuple(rope_tiles), tn=tn,
                          plain_cols=plain_cols),
        grid=(b, s // tm),
        in_specs=in_specs,
        out_specs=out_spec,
        out_shape=out_shape,
        scratch_shapes=scratch,
        compiler_params=_cparams(("arbitrary", "arbitrary")),
        name=f"in_proj_d{dilation}",
    )(*args)


def _final_norm_kernel(x_ref, g_ref, o_ref):
    x = x_ref[...]
    ms = jnp.mean(x * x, axis=-1, keepdims=True)
    o_ref[...] = x * lax.rsqrt(ms + NORM_EPS) * g_ref[...]


def final_norm(h, g):
    b, s, d = h.shape
    tm = min(s, ROW_TILE)
    return pl.pallas_call(
        _final_norm_kernel,
        grid=(b, s // tm),
        in_specs=[
            pl.BlockSpec((None, tm, d), lambda bi, i: (bi, i, 0)),
            pl.BlockSpec((1, d), lambda bi, i: (0, 0)),
        ],
        out_specs=pl.BlockSpec((None, tm, d), lambda bi, i: (bi, i, 0)),
        out_shape=jax.ShapeDtypeStruct((b, s, d), F32),
        compiler_params=_cparams(("arbitrary", "arbitrary")),
        name="final_norm",
    )(h, g.reshape(1, d))


def _stack_heads(pair, first_head):
    zero = jnp.zeros_like(pair)
    return jnp.concatenate([jnp.where(first_head, pair, zero), jnp.where(first_head, zero, pair)], axis=0)


def _banded_kernel(q_ref, kp_ref, kc_ref, vp_ref, vc_ref, o_ref, lse_ref, kwin_ref, vwin_ref, *, tq):
    i = pl.program_id(2)
    n_cls = q_ref.shape[0]
    nsub = tq // A_BLK
    lane = lax.broadcasted_iota(I32, (A_BLK, LANES), 1)
    first_head = lane < HEAD_DIM
    odd_lane = (lane % 2) == 1
    r2 = lax.broadcasted_iota(I32, (2 * A_BLK, 2 * A_BLK), 0) % A_BLK
    c2 = lax.broadcasted_iota(I32, (2 * A_BLK, 2 * A_BLK), 1)
    own_ok = (c2 >= A_BLK) & (c2 - A_BLK <= r2)
    ones_blk = jnp.ones((2 * A_BLK, LANES), BF16)
    n_pairs = A_HEADS // 2

    kwin_ref[:, :A_BLK, :] = kp_ref[...]
    kwin_ref[:, A_BLK:, :] = kc_ref[...]
    vwin_ref[:, :A_BLK, :] = vp_ref[...]
    vwin_ref[:, A_BLK:, :] = vc_ref[...]

    def logits(r, a, p):
        cols = slice(p * LANES, (p + 1) * LANES)
        qs = _stack_heads(q_ref[r, a * A_BLK:(a + 1) * A_BLK, cols], first_head)
        k2 = kwin_ref[r, a * A_BLK:(a + 2) * A_BLK, cols]
        return lax.dot_general(qs, k2, (((1,), (1,)), ((), ())), preferred_element_type=F32)

    def band_bias(prev_shift):
        mask = own_ok | ((c2 < A_BLK) & (c2 >= r2 + prev_shift))
        return jnp.where(mask, 0.0, -jnp.inf).astype(F32)

    bias_inner = band_bias(0)
    biases = [band_bias(jnp.where(i > 0, 0, 2 * A_BLK))] + [bias_inner] * (nsub - 1)

    units = [(r, a, p) for r in range(n_cls) for a in range(nsub) for p in range(n_pairs)]
    queue = [logits(*units[u]) for u in range(BANDED_LOOKAHEAD)]
    lse_tile = None
    for u, (r, a, p) in enumerate(units):
        rows = slice(a * A_BLK, (a + 1) * A_BLK)
        cols = slice(p * LANES, (p + 1) * LANES)
        bias = biases[a]
        if p == 0:
            lse_tile = jnp.zeros((A_BLK, LANES), F32)
        s = queue.pop(0) + bias
        if u + BANDED_LOOKAHEAD < len(units):
            queue.append(logits(*units[u + BANDED_LOOKAHEAD]))
        m = jnp.max(s, axis=-1, keepdims=True)
        e = jnp.exp2(s - m).astype(BF16)
        v_ext = jnp.concatenate([vwin_ref[r, a * A_BLK:(a + 2) * A_BLK, cols], ones_blk], axis=1)
        pv = jnp.dot(e, v_ext, preferred_element_type=F32)
        num = jnp.where(first_head, pv[:A_BLK, :LANES], pv[A_BLK:, :LANES])
        o_ref[r, rows, cols] = num * (1.0 / jnp.where(first_head, pv[:A_BLK, LANES:], pv[A_BLK:, LANES:]))
        den_pair = jnp.where(odd_lane, pv[A_BLK:, LANES:], pv[:A_BLK, LANES:])
        m_pair = jnp.where(odd_lane, m[A_BLK:], m[:A_BLK])
        lse_tile = jnp.where((lane // 2) == p, m_pair + jnp.log(den_pair) * LOG2_E, lse_tile)
        if p == n_pairs - 1:
            lse_ref[r, rows, :] = lse_tile


BANDED_ROWS = 1024


def banded_group_attention(qkv):
    b, d, n, _ = qkv.shape
    tq = min(BANDED_ROWS, n)
    rc = min(BANDED_ROWS // tq, d)
    sub = tq // A_BLK
    assert n % tq == 0 and d % rc == 0

    def cur(blk, width=A_WIDTH):
        return pl.BlockSpec((None, rc, tq, width), lambda bi, r, i: (bi, r, i, blk))

    def prev(blk):
        return pl.BlockSpec((None, rc, A_BLK, A_WIDTH),
                            lambda bi, r, i: (bi, r, jnp.maximum(i * sub - 1, 0), blk))

    return pl.pallas_call(
        functools.partial(_banded_kernel, tq=tq),
        grid=(b, d // rc, n // tq),
        in_specs=[cur(0), prev(1), cur(1), prev(2), cur(2)],
        out_specs=[cur(0), cur(0, LANES)],
        out_shape=[
            jax.ShapeDtypeStruct((b, d, n, A_WIDTH), F32),
            jax.ShapeDtypeStruct((b, d, n, LANES), F32),
        ],
        scratch_shapes=[pltpu.VMEM((rc, tq + A_BLK, A_WIDTH), BF16)] * 2,
        compiler_params=_cparams(("arbitrary", "arbitrary", "arbitrary")),
        name=f"banded_attn_d{d}",
    )(qkv, qkv, qkv, qkv, qkv)


def _expand_heads(x, expand):
    hi = x.astype(BF16)
    lo = (x - hi.astype(F32)).astype(BF16)
    return (jnp.dot(hi, expand, preferred_element_type=F32)
            + jnp.dot(lo, expand, preferred_element_type=F32))


def _merge_out_kernel(o0_ref, o1_ref, o2_ref, l0_ref, l1_ref, l2_ref, gate_ref, ex_ref, w_ref,
                      mod_ref, h_ref, hout_ref, y_ref, ot_ref, lt_ref, *, dilations):
    o_refs, l_refs = (o0_ref, o1_ref, o2_ref), (l0_ref, l1_ref, l2_ref)
    tm = y_ref.shape[0]
    nblk = y_ref.shape[1] // LANES
    for g, d in enumerate(dilations):
        if d == 1:
            continue
        per = tm // d
        for r in range(d):
            lt_ref[g, pl.ds(r, per, stride=d), :] = l_refs[g][r]
            for k in range(nblk):
                ot_ref[g, k, pl.ds(r, per, stride=d), :] = o_refs[g][r, :, k * LANES:(k + 1) * LANES]
    lses = [l_refs[g][0] if d == 1 else lt_ref[g] for g, d in enumerate(dilations)]
    m = jnp.maximum(jnp.maximum(lses[0], lses[1]), lses[2])
    es = [jnp.exp2(l - m) for l in lses]
    inv = 1.0 / (es[0] + es[1] + es[2])
    ex = ex_ref[...]
    alphas = [_expand_heads(e * inv, ex) for e in es[:-1]]
    alphas.append(1.0 - alphas[0] - alphas[1])
    for k in range(nblk):
        cols = slice(k * LANES, (k + 1) * LANES)
        y = jnp.zeros((tm, LANES), F32)
        for g, d in enumerate(dilations):
            og = o_refs[g][0, :, cols] if d == 1 else ot_ref[g, k]
            y += alphas[g][:, cols] * og
        gate = gate_ref[:, cols].astype(F32)
        y_ref[:, cols] = (y * gate * (1.0 / (1.0 + jnp.exp(-gate)))).astype(y_ref.dtype)
    acc = jnp.dot(y_ref[...], w_ref[...], preferred_element_type=F32)
    hout_ref[...] = h_ref[...] + mod_ref[...] * acc


def merge_out_proj(outs, lses, gate, w_out, mod_gate, h):
    b, s, w = gate.shape
    d_model = h.shape[-1]
    tm = min(s, MERGE_ROWS)
    dil = tuple(o.shape[1] for o in outs)
    expand = (jnp.arange(LANES)[:, None] == (jnp.arange(w)[None, :] // HEAD_DIM)).astype(BF16)

    def cm(d, width):
        return pl.BlockSpec((None, d, tm // d, width), lambda bi, i: (bi, 0, i, 0))

    def tspec(width):
        return pl.BlockSpec((None, tm, width), lambda bi, i: (bi, i, 0))

    return pl.pallas_call(
        functools.partial(_merge_out_kernel, dilations=dil),
        grid=(b, s // tm),
        in_specs=[cm(d, w) for d in dil] + [cm(d, LANES) for d in dil]
                 + [tspec(w), pl.BlockSpec((LANES, w), lambda bi, i: (0, 0)),
                    pl.BlockSpec((w, d_model), lambda bi, i: (0, 0)),
                    pl.BlockSpec((None, 1, d_model), lambda bi, i: (bi, 0, 0)),
                    tspec(d_model)],
        out_specs=tspec(d_model),
        out_shape=jax.ShapeDtypeStruct((b, s, d_model), F32),
        scratch_shapes=[pltpu.VMEM((tm, w), BF16),
                        pltpu.VMEM((len(dil), w // LANES, tm, LANES), F32),
                        pltpu.VMEM((len(dil), tm, LANES), F32)],
        compiler_params=_cparams(("arbitrary", "arbitrary")),
        name="merge_out_proj",
    )(*outs, *lses, gate, expand, w_out, mod_gate.reshape(b, 1, d_model), h)


def dilated_layer(h, g, scale, shift, mod_gate, w_in, w_out, cos, sin):
    tn = PROJ_COL_TILE
    per = A_WIDTH // tn
    qkv_flags = (True,) * (2 * per) + (False,) * per
    n_groups = len(A_GROUPS)
    col = np.arange(w_in.shape[1])
    is_q = (col < n_groups * 3 * A_WIDTH) & ((col // A_WIDTH) % 3 == 0)
    w = (w_in * jnp.asarray(np.where(is_q, Q_PRESCALE, 1.0), F32)).astype(BF16)
    outs, lses = [], []
    gate = None
    for gi, (window, dilation) in enumerate(A_GROUPS):
        assert window // dilation == A_BLK
        plain = (A_WIDTH, 3 * n_groups) if gi == 0 else None
        qkv = in_proj(h, g, scale, shift, w, qkv_flags, cos, sin, tn, dilation,
                      w_block=gi, plain=plain)
        if plain is not None:
            qkv, gate = qkv
        if dilation == 1:
            qkv = qkv[:, None]
        o, lse = banded_group_attention(qkv)
        outs.append(o)
        lses.append(lse)
    return merge_out_proj(outs, lses, gate, w_out.astype(BF16), mod_gate, h)


B_COLS = 3584
B_ROPE_FLAGS = (1, 1, 0, 0, 1, 1, 0)


def _dsa_head_perm():
    group = B_HEADS // B_KV_HEADS
    order = []
    for g2 in range(B_KV_HEADS // 2):
        for r in range(group):
            order += [(2 * g2) * group + r, (2 * g2 + 1) * group + r]
    return np.asarray(order)


def _dsa_weights(w_in, w_out):
    d = w_in.shape[0]
    cuts = np.cumsum((B_WIDTH, B_KV_HEADS * HEAD_DIM, B_KV_HEADS * HEAD_DIM, IDX_HEADS * HEAD_DIM,
                      HEAD_DIM, IDX_HEADS, B_WIDTH))[:-1]
    wq, wk, wv, wqi, wki, wwi, wg = jnp.split(w_in, cuts, axis=1)
    cols = (_dsa_head_perm()[:, None] * HEAD_DIM + np.arange(HEAD_DIM)[None, :]).reshape(-1)
    zeros = lambda n: jnp.zeros((d, n), w_in.dtype)
    wq = wq * Q_PRESCALE
    w = jnp.concatenate([wq[:, cols], wg[:, cols], wqi, wk, wki, wki, zeros(LANES),
                         wwi, zeros(2 * LANES - IDX_HEADS), wv], axis=1)
    assert w.shape[1] == B_COLS
    return w.astype(BF16), w_out[cols, :].astype(BF16)


def _sortable(x):
    bits = pltpu.bitcast(x, I32)
    return bits ^ ((bits >> 31) & jnp.int32(0x7FFFFFFF))


def _bit_transpose32(words):
    a = list(words)
    j, m = 16, 0x0000FFFF
    while j:
        mask = jnp.int32(m - (1 << 32) if m >= (1 << 31) else m)
        shift = jnp.full(a[0].shape, j, I32)
        k = 0
        while k < 32:
            t = (a[k] ^ lax.shift_right_logical(a[k + j], shift)) & mask
            a[k] = a[k] ^ t
            a[k + j] = a[k + j] ^ lax.shift_left(t, shift)
            k = (k + j + 1) & ~j
        j >>= 1
        m = (m ^ (m << j)) & 0xFFFFFFFF
    return a


def _sublane_allsum(x):
    x = x + pltpu.roll(x, 4, 0)
    x = x + pltpu.roll(x, 2, 0)
    return x + pltpu.roll(x, 1, 0)


def _tree_sum(parts):
    parts = list(parts)
    while len(parts) > 1:
        odd = [parts[-1]] if len(parts) % 2 else []
        parts = [parts[j] + parts[j + 1] for j in range(0, len(parts) - 1, 2)] + odd
    return parts[0]


GROUP_KEYS = 256
ATTN_LOOKAHEAD = 1


def _dsa_kernel(q_ref, gate_ref, qi_ref, k_ref, ki_ref, wi_ref, v_ref, wout_ref, mod_ref, h_ref, *rest,
                k_sel, final):
    if final:
        fg_ref, hout_ref, *scratch = rest
    else:
        hout_ref, *scratch = rest
    keys_ref, planes_ref, bias_ref, kt_ref, qs_ref, s0_ref, m_ref, l_ref, acc_ref, y_ref = scratch
    i = pl.program_id(1)
    seq = keys_ref.shape[1]
    n_sub = Q_BLOCK // SEL_LANES
    n_pairs = B_HEADS // 2
    n_groups_max = seq // GROUP_KEYS
    lane_q = lax.broadcasted_iota(I32, (Q_BLOCK, LANES), 1)
    first_head_q = lane_q < HEAD_DIM
    first_head = lax.broadcasted_iota(I32, (SEL_LANES, LANES), 1) < HEAD_DIM
    key_row = lax.broadcasted_iota(I32, (KEY_TILE, SEL_LANES), 0)
    lane_idx = lax.broadcasted_iota(I32, (KEY_TILE, SEL_LANES), 1)

    @pl.when(i == 0)
    def _():
        planes_ref[...] = jnp.zeros(planes_ref.shape, I32)
        keys_ref[...] = jnp.zeros(keys_ref.shape, I32)

    def tiles_of(a):
        return ((i * n_sub + a) * SEL_LANES) // KEY_TILE + 1

    def select(a):
        n_tiles = tiles_of(a)
        rows_a = slice(a * SEL_LANES, (a + 1) * SEL_LANES)
        q_pos = (i * n_sub + a) * SEL_LANES + lane_idx
        keys = keys_ref.at[a]

        w_t = (wi_ref[rows_a, :].astype(F32) * (IDX_HEADS ** -0.5 * HEAD_DIM ** -0.5)).T
        qi_stacked = [_stack_heads(qi_ref[rows_a, p * LANES:(p + 1) * LANES], first_head)
                      for p in range(IDX_HEADS // 2)]

        def score_tile(t):
            base = pl.multiple_of(t * KEY_TILE, KEY_TILE)
            kk = ki_ref[pl.ds(base, KEY_TILE), :]
            score = jnp.zeros((KEY_TILE, SEL_LANES), F32)
            for p in range(IDX_HEADS // 2):
                sc = lax.dot_general(kk, qi_stacked[p], (((1,), (1,)), ((), ())),
                                     preferred_element_type=F32)
                score += jnp.maximum(sc[:, :SEL_LANES], 0.0) * w_t[2 * p:2 * p + 1, :]
                score += jnp.maximum(sc[:, SEL_LANES:], 0.0) * w_t[2 * p + 1:2 * p + 2, :]
            causal = (key_row + base) <= q_pos
            score = jnp.where(causal, score + 0.0, -jnp.inf)
            key = _sortable(score)
            keys[pl.ds(base, KEY_TILE), :] = key
            ukey = key ^ jnp.int32(INT_MIN)
            for g in range(KEY_TILE // GROUP_KEYS):
                words = [ukey[g * GROUP_KEYS + 8 * j:g * GROUP_KEYS + 8 * j + 8, :] for j in range(32)]
                rows = pl.ds(pl.multiple_of(t * (KEY_TILE // 32) + 8 * g, 8), 8)
                for b, plane in enumerate(_bit_transpose32(words)):
                    planes_ref[b, rows, :] = plane

        def score_two(t2, carry):
            score_tile(2 * t2)
            score_tile(2 * t2 + 1)
            return carry

        lax.fori_loop(0, n_tiles // 2, score_two, 0)

        @pl.when(n_tiles % 2 == 1)
        def _():
            score_tile(n_tiles - 1)

        def select_threshold():
            n_groups = n_tiles * (KEY_TILE // GROUP_KEYS)
            alive0 = tuple(jnp.full((8, SEL_LANES), -1, I32) * (g < n_groups).astype(I32)
                           for g in range(n_groups_max))
            need0 = jnp.full((8, SEL_LANES), k_sel, I32)

            def step(st, carry):
                alive, need, prefix = carry
                planes = [planes_ref[st, 8 * g:8 * g + 8, :] for g in range(n_groups_max)]
                ones = [al & pln for al, pln in zip(alive, planes)]
                cnt = _sublane_allsum(_tree_sum([lax.population_count(o) for o in ones]))
                take = cnt >= need
                prefix = jnp.where(take, prefix | jnp.left_shift(jnp.int32(1), 31 - st), prefix)
                need = jnp.where(take, need, need - cnt)
                flip = jnp.where(take, 0, -1)
                alive = tuple(al & (pln ^ flip) for al, pln in zip(alive, planes))
                return alive, need, prefix

            alive, need, prefix = lax.fori_loop(
                0, 32, step, (alive0, need0, jnp.zeros((8, SEL_LANES), I32)))
            thr = (prefix ^ jnp.int32(INT_MIN))[:1, :]
            need_eq = need[:1, :]

            n_eq = _sublane_allsum(_tree_sum([lax.population_count(al) for al in alive]))[:1, :]
            tied = (n_eq > need_eq) & (thr > KEY_NEG_INF)
            idx_bits = (seq - 1).bit_length()

            def tie_break():
                def count_eq_before(cut):
                    def body(t, acc):
                        base = pl.multiple_of(t * KEY_TILE, KEY_TILE)
                        hit = (keys[pl.ds(base, KEY_TILE), :] == thr) & ((key_row + base) < cut)
                        return acc + jnp.sum(hit.astype(I32).reshape(KEY_TILE // 8, 8, SEL_LANES), axis=0)
                    acc = lax.fori_loop(0, n_tiles, body, jnp.zeros((8, SEL_LANES), I32))
                    return jnp.sum(acc, axis=0, keepdims=True)

                def bit_step(it, cut):
                    cand = cut + jnp.left_shift(jnp.int32(1), idx_bits - 1 - it)
                    return jnp.where(count_eq_before(cand) < need_eq, cand, cut)

                return lax.fori_loop(0, idx_bits, bit_step, jnp.zeros((1, SEL_LANES), I32))

            any_tied = jnp.max(tied.astype(I32)) > 0
            idx_cut = lax.cond(any_tied, tie_break, lambda: jnp.full((1, SEL_LANES), seq, I32))
            return thr, idx_cut

        all_selected = (i * n_sub + a + 1) * SEL_LANES <= k_sel
        thr, idx_cut = lax.cond(
            all_selected,
            lambda: (jnp.full((1, SEL_LANES), INT_MIN, I32), jnp.full((1, SEL_LANES), seq, I32)),
            select_threshold)
        return thr, idx_cut, q_pos

    selected = [select(a) for a in range(n_sub)]
    n_tiles = tiles_of(n_sub - 1)

    def write_bias(base):
        k_idx = key_row + base
        for a, (thr, idx_cut, q_pos) in enumerate(selected):
            key = keys_ref[a, pl.ds(base, KEY_TILE), :]
            sel = ((key > thr) | ((key == thr) & (k_idx <= idx_cut))) & (k_idx <= q_pos)
            bias_t = jnp.where(sel, 0.0, -jnp.inf).astype(F32)
            for c in range(KEY_TILE // LANES):
                col = base + c * LANES
                col = col if isinstance(col, int) else pl.multiple_of(col, LANES)
                bias_ref[a * SEL_LANES:(a + 1) * SEL_LANES, pl.ds(col, LANES)] = \
                    bias_t[c * LANES:(c + 1) * LANES, :].T

    write_bias(0)

    for p in range(n_pairs):
        cols = slice(p * LANES, (p + 1) * LANES)
        qs_ref[p] = _stack_heads(q_ref[:, cols], first_head_q)
    heads_per_kv_pair = 2 * (B_HEADS // B_KV_HEADS) // 2

    ones_blk = jnp.ones((KEY_TILE, LANES), BF16)

    @pl.when(i == 0)
    def _():
        def xpose(r, carry):
            rows = pl.ds(pl.multiple_of(r * LANES, LANES), LANES)
            for c in range(kt_ref.shape[0]):
                blk = k_ref[rows, c * LANES:(c + 1) * LANES].astype(F32)
                kt_ref[c, :, rows] = blk.T.astype(BF16)
            return carry
        lax.fori_loop(0, seq // LANES, xpose, 0)

    def logits(base, p):
        kt = kt_ref[p // heads_per_kv_pair, :, pl.ds(base, KEY_TILE)]
        return jnp.dot(qs_ref[p], kt, preferred_element_type=F32)

    lookahead = s0_ref.shape[0]
    for u in range(lookahead):
        s0_ref[u] = logits(0, u)

    def attn_tiles(t0, count, start=False):
        bases = [pl.multiple_of(jnp.minimum(t0 + dt, n_tiles - 1) * KEY_TILE, KEY_TILE)
                 for dt in range(count + 1)]
        units = [(dt, p) for dt in range(count + 1) for p in range(n_pairs)]
        queue = [s0_ref[u] for u in range(lookahead)]
        for u, (dt, p) in enumerate(units[:count * n_pairs]):
            base = bases[dt]
            if p == 0:
                bias = bias_ref[:, pl.ds(base, KEY_TILE)]
                write_bias(bases[dt + 1])
                bias2 = jnp.concatenate([bias, bias], axis=0)
            kv = p // heads_per_kv_pair
            kv_cols = slice(kv * LANES, (kv + 1) * LANES)
            s = queue.pop(0) + bias2
            ahead_dt, ahead_p = units[u + lookahead]
            queue.append(logits(bases[ahead_dt], ahead_p))
            m_blk = s[:, :LANES]
            for c in range(1, KEY_TILE // LANES):
                m_blk = jnp.maximum(m_blk, s[:, c * LANES:(c + 1) * LANES])
            opening = start and dt == 0
            m_cur = jnp.max(m_blk, axis=-1, keepdims=True)
            if opening:
                m_new = jnp.broadcast_to(jnp.maximum(m_cur, F32_MIN), m_blk.shape)
            else:
                m_old = m_ref[p]
                m_new = jnp.maximum(m_old, m_cur)
                alpha = jnp.exp2(m_old - m_new)
            e = jnp.concatenate(
                [jnp.exp2(s[:, c * LANES:(c + 1) * LANES] - m_new).astype(BF16)
                 for c in range(KEY_TILE // LANES)], axis=1)
            v_ext = jnp.concatenate([v_ref[pl.ds(base, KEY_TILE), kv_cols], ones_blk], axis=1)
            pv = jnp.dot(e, v_ext, preferred_element_type=F32)
            if opening:
                acc_ref[p] = pv[:, :LANES]
                l_ref[p] = pv[:, LANES:]
            else:
                acc_ref[p] = acc_ref[p] * alpha + pv[:, :LANES]
                l_ref[p] = l_ref[p] * alpha + pv[:, LANES:]
            m_ref[p] = m_new
        for u in range(lookahead):
            s0_ref[u] = queue[u]

    attn_tiles(0, 1, start=True)

    def attn_two(t2, carry):
        attn_tiles(1 + 2 * t2, 2)
        return carry

    lax.fori_loop(0, (n_tiles - 1) // 2, attn_two, 0)

    @pl.when(n_tiles % 2 == 0)
    def _():
        attn_tiles(n_tiles - 1, 1)

    for p in range(n_pairs):
        cols = slice(p * LANES, (p + 1) * LANES)
        l = l_ref[p]
        acc = acc_ref[p]
        num = jnp.where(first_head_q, acc[:Q_BLOCK], acc[Q_BLOCK:])
        den = jnp.where(first_head_q, l[:Q_BLOCK], l[Q_BLOCK:])
        gate = gate_ref[:, cols].astype(F32)
        y_ref[:, cols] = (num * gate / (den * (1.0 + jnp.exp(-gate)))).astype(y_ref.dtype)

    h_new = h_ref[...] + mod_ref[...] * jnp.dot(y_ref[...], wout_ref[...], preferred_element_type=F32)
    if final:
        ms = jnp.mean(h_new * h_new, axis=-1, keepdims=True)
        h_new = h_new * lax.rsqrt(ms + NORM_EPS) * fg_ref[...]
    hout_ref[...] = h_new


def dsa_attention(proj, w_out, mod_gate, h, final_g=None):
    b, s, _ = proj.shape
    d_model = h.shape[-1]
    k_sel = min(TOPK_MAX, s // 4)
    assert s % KEY_TILE == 0 and s % Q_BLOCK == 0 and k_sel <= KEY_TILE
    kvw = B_KV_HEADS * HEAD_DIM
    n_pairs = B_HEADS // 2

    def qblock(width, idx):
        return pl.BlockSpec((None, Q_BLOCK, width), lambda bi, i: (bi, i, idx))

    def full(width, idx):
        return pl.BlockSpec((None, s, width), lambda bi, i: (bi, 0, idx))

    in_specs = [
        qblock(B_WIDTH, 0),
        qblock(B_WIDTH, 1),
        qblock(IDX_HEADS * HEAD_DIM, 4),
        full(kvw, 10),
        full(LANES, 22),
        qblock(LANES, 24),
        full(kvw, 13),
        pl.BlockSpec((B_WIDTH, d_model), lambda bi, i: (0, 0)),
        pl.BlockSpec((None, 1, d_model), lambda bi, i: (bi, 0, 0)),
        qblock(d_model, 0),
    ]
    args = [proj] * 7 + [w_out, mod_gate.reshape(b, 1, d_model), h]
    if final_g is not None:
        in_specs.append(pl.BlockSpec((1, d_model), lambda bi, i: (0, 0)))
        args.append(final_g.reshape(1, d_model))
    return pl.pallas_call(
        functools.partial(_dsa_kernel, k_sel=k_sel, final=final_g is not None),
        grid=(b, s // Q_BLOCK),
        in_specs=in_specs,
        out_specs=qblock(d_model, 0),
        out_shape=jax.ShapeDtypeStruct((b, s, d_model), F32),
        scratch_shapes=[
            pltpu.VMEM((Q_BLOCK // SEL_LANES, s, SEL_LANES), I32),
            pltpu.VMEM((32, s // 32, SEL_LANES), I32),
            pltpu.VMEM((Q_BLOCK, s), F32),
            pltpu.VMEM((kvw // LANES, LANES, s), BF16),
            pltpu.VMEM((n_pairs, 2 * Q_BLOCK, LANES), BF16),
            pltpu.VMEM((ATTN_LOOKAHEAD, 2 * Q_BLOCK, KEY_TILE), F32),
            pltpu.VMEM((n_pairs, 2 * Q_BLOCK, LANES), F32),
            pltpu.VMEM((n_pairs, 2 * Q_BLOCK, LANES), F32),
            pltpu.VMEM((n_pairs, 2 * Q_BLOCK, LANES), F32),
            pltpu.VMEM((Q_BLOCK, B_WIDTH), BF16),
        ],
        compiler_params=_cparams(("arbitrary", "arbitrary")),
        name="dsa_attention",
    )(*args)


def dsa_layer(h, g, scale, shift, mod_gate, w_in_packed, w_out, cos, sin, final_g=None):
    proj = in_proj(h, g, scale, shift, w_in_packed, [f == 1 for f in B_ROPE_FLAGS], cos, sin,
                   PROJ_COL_TILE)
    return dsa_attention(proj, w_out, mod_gate, h, final_g)


def kernel(x, c, positions, norm_g, ada_w, ada_b, a_w_in, a_w_out, b_w_in, b_w_out, final_g):
    depth = norm_g.shape[0]
    d = x.shape[-1]
    mod = adaln_mod(c, ada_w, ada_b)
    cos, sin = rope_tables(positions)
    h = x
    for i in range(depth):
        shift, scale, gate = mod[i, :, :d], mod[i, :, d:2 * d], mod[i, :, 2 * d:]
        last = i == depth - 1
        if i % 2 == 0:
            h = dilated_layer(h, norm_g[i], scale, shift, gate, a_w_in[i // 2], a_w_out[i // 2], cos, sin)
            if last:
                h = final_norm(h, final_g)
        else:
            w_in, w_out = _dsa_weights(b_w_in[i // 2], b_w_out[i // 2])
            h = dsa_layer(h, norm_g[i], scale, shift, gate, w_in, w_out, cos, sin,
                          final_g if last else None)
    return h
```

```python
import functools

import numpy as np
import jax
import jax.numpy as jnp
from jax import lax
from jax.experimental import pallas as pl
from jax.experimental.pallas import tpu as pltpu

F32 = jnp.float32
BF16 = jnp.bfloat16
I32 = jnp.int32

LANES = 128
HEAD_DIM = 64
HALF = HEAD_DIM // 2
ROPE_THETA = 10000.0
NORM_EPS = 1e-6
VMEM_LIMIT = 56 * 1024 * 1024
ROW_CHUNK = 256
ROW_TILE = 1024
PROJ_COL_TILE = 512
MERGE_ROWS = 512

A_HEADS = 16
A_WIDTH = A_HEADS * HEAD_DIM
A_GROUPS = ((128, 1), (512, 4), (2048, 16))
A_BLK = 128
BANDED_LOOKAHEAD = 1

B_HEADS = 16
B_KV_HEADS = 4
B_WIDTH = B_HEADS * HEAD_DIM
IDX_HEADS = 8
TOPK_MAX = 256
Q_BLOCK = 256
SEL_LANES = 128
KEY_TILE = 512
INT_MIN = -(2 ** 31)
KEY_NEG_INF = -2139095041
F32_MIN = float(np.finfo(np.float32).min)
LOG2_E = float(np.log2(np.e))
Q_PRESCALE = LOG2_E * HEAD_DIM ** -0.5


def _cparams(sem):
    return pltpu.CompilerParams(dimension_semantics=sem, vmem_limit_bytes=VMEM_LIMIT)


def _adaln_kernel(c_ref, w_ref, b_ref, o_ref):
    c = c_ref[...]
    ca = (c * (1.0 / (1.0 + jnp.exp(-c)))).astype(BF16)
    acc = jnp.dot(ca, w_ref[...].astype(BF16), preferred_element_type=F32)
    o_ref[...] = acc + b_ref[...]


def adaln_mod(c, ada_w, ada_b):
    depth, d, d3 = ada_w.shape
    b = c.shape[0]
    tn = 1024
    return pl.pallas_call(
        _adaln_kernel,
        grid=(depth, d3 // tn),
        in_specs=[
            pl.BlockSpec((b, d), lambda i, j: (0, 0)),
            pl.BlockSpec((None, d, tn), lambda i, j: (i, 0, j)),
            pl.BlockSpec((None, 1, tn), lambda i, j: (i, 0, j)),
        ],
        out_specs=pl.BlockSpec((None, b, tn), lambda i, j: (i, 0, j)),
        out_shape=jax.ShapeDtypeStruct((depth, b, d3), F32),
        compiler_params=_cparams(("arbitrary", "arbitrary")),
        name="adaln_mod",
    )(c, ada_w, ada_b.reshape(depth, 1, d3))


def _rope_table_kernel(pos_ref, inv_ref, cos_ref, sin_ref):
    ang = pos_ref[...].astype(F32) * inv_ref[...]
    lane = lax.broadcasted_iota(I32, ang.shape, 1)
    first_half = (lane % HEAD_DIM) < HALF
    cos_ref[...] = jnp.cos(ang)
    s = jnp.sin(ang)
    sin_ref[...] = jnp.where(first_half, -s, s)


def rope_tables(positions):
    b, s = positions.shape
    inv_freq = ROPE_THETA ** (-jnp.arange(HALF, dtype=F32) / HALF)
    inv_lane = jnp.tile(inv_freq, LANES // HALF).reshape(1, LANES)
    ts = min(s, ROW_TILE)
    out = jax.ShapeDtypeStruct((b, s, LANES), F32)
    return pl.pallas_call(
        _rope_table_kernel,
        grid=(b, s // ts),
        in_specs=[
            pl.BlockSpec((None, ts, 1), lambda i, j: (i, j, 0)),
            pl.BlockSpec((1, LANES), lambda i, j: (0, 0)),
        ],
        out_specs=[pl.BlockSpec((None, ts, LANES), lambda i, j: (i, j, 0))] * 2,
        out_shape=[out, out],
        compiler_params=_cparams(("arbitrary", "arbitrary")),
        name="rope_tables",
    )(positions.reshape(b, s, 1), inv_lane)


def _rope_block(t, cos, sin_signed, first_half):
    partner = jnp.where(first_half, pltpu.roll(t, LANES - HALF, 1), pltpu.roll(t, HALF, 1))
    return t * cos + partner * sin_signed


def _in_proj_kernel(x_ref, g_ref, sc_ref, sh_ref, w_ref, cos_ref, sin_ref, *rest,
                    dilation, rope_tiles, tn, plain_cols):
    if plain_cols:
        w2_ref, o_ref, o2_ref, u_ref, *ucm = rest
    else:
        o_ref, u_ref, *ucm = rest
    tm = u_ref.shape[0]
    nblk = tn // LANES
    per = ROW_CHUNK // dilation
    x = x_ref[...]
    ms = jnp.mean(x * x, axis=-1, keepdims=True)
    xn = x * lax.rsqrt(ms + NORM_EPS)
    u_ref[...] = (xn * g_ref[...] * (1.0 + sc_ref[...]) + sh_ref[...]).astype(BF16)
    lhs_ref = u_ref
    if dilation > 1:
        lhs_ref = ucm[0]
        dst = lax.broadcasted_iota(I32, (ROW_CHUNK, ROW_CHUNK), 0)
        src = lax.broadcasted_iota(I32, (ROW_CHUNK, ROW_CHUNK), 1)
        perm = (src == (dst % per) * dilation + dst // per).astype(BF16)
        for c in range(tm // ROW_CHUNK):
            rows = slice(c * ROW_CHUNK, (c + 1) * ROW_CHUNK)
            lhs_ref[rows, :] = jnp.dot(perm, u_ref[rows, :], preferred_element_type=F32).astype(BF16)
    lane = lax.broadcasted_iota(I32, (per, LANES), 1)
    first_half = (lane % HEAD_DIM) < HALF
    for j, rope in enumerate(rope_tiles):
        for c in range(tm // ROW_CHUNK):
            acc = jnp.dot(lhs_ref[c * ROW_CHUNK:(c + 1) * ROW_CHUNK, :], w_ref[:, j * tn:(j + 1) * tn],
                          preferred_element_type=F32)
            for r in range(dilation):
                rows = slice(c * per, (c + 1) * per)
                if rope:
                    trows = rows if dilation == 1 else pl.ds(c * ROW_CHUNK + r, per, stride=dilation)
                    cos, sin = cos_ref[trows, :], sin_ref[trows, :]
                for k in range(nblk):
                    cols = slice(j * tn + k * LANES, j * tn + (k + 1) * LANES)
                    blk = acc[r * per:(r + 1) * per, k * LANES:(k + 1) * LANES]
                    if rope:
                        blk = _rope_block(blk, cos, sin, first_half)
                    if dilation > 1:
                        o_ref[r, rows, cols] = blk.astype(o_ref.dtype)
                    else:
                        o_ref[rows, cols] = blk.astype(o_ref.dtype)
    for j in range(plain_cols // tn):
        for c in range(tm // ROW_CHUNK):
            rows = slice(c * ROW_CHUNK, (c + 1) * ROW_CHUNK)
            acc = jnp.dot(u_ref[rows, :], w2_ref[:, j * tn:(j + 1) * tn], preferred_element_type=F32)
            o2_ref[rows, j * tn:(j + 1) * tn] = acc.astype(o2_ref.dtype)


def in_proj(h, g, scale, shift, w, rope_tiles, cos, sin, tn, dilation=1, w_block=0, plain=None):
    b, s, d = h.shape
    n = len(rope_tiles) * tn
    tm = min(s, ROW_TILE)
    assert tm % ROW_CHUNK == 0 and ROW_CHUNK % (16 * dilation) == 0 and (w_block + 1) * n <= w.shape[1]
    scratch = [pltpu.VMEM((tm, d), BF16)]
    tab_spec = pl.BlockSpec((None, tm, LANES), lambda bi, i: (bi, i, 0))
    if dilation == 1:
        out_spec = pl.BlockSpec((None, tm, n), lambda bi, i: (bi, i, 0))
        out_shape = jax.ShapeDtypeStruct((b, s, n), BF16)
    else:
        out_spec = pl.BlockSpec((None, dilation, tm // dilation, n), lambda bi, i: (bi, 0, i, 0))
        out_shape = jax.ShapeDtypeStruct((b, dilation, s // dilation, n), BF16)
        scratch.append(pltpu.VMEM((tm, d), BF16))
    in_specs = [
        pl.BlockSpec((None, tm, d), lambda bi, i: (bi, i, 0)),
        pl.BlockSpec((1, d), lambda bi, i: (0, 0)),
        pl.BlockSpec((None, 1, d), lambda bi, i: (bi, 0, 0)),
        pl.BlockSpec((None, 1, d), lambda bi, i: (bi, 0, 0)),
        pl.BlockSpec((d, n), lambda bi, i: (0, w_block)),
        tab_spec, tab_spec,
    ]
    args = [h, g.reshape(1, d), scale.reshape(b, 1, d), shift.reshape(b, 1, d), w, cos, sin]
    plain_cols = 0
    if plain is not None:
        plain_cols, plain_block = plain
        assert plain_cols % tn == 0 and (plain_block + 1) * plain_cols <= w.shape[1]
        in_specs.append(pl.BlockSpec((d, plain_cols), lambda bi, i: (0, plain_block)))
        args.append(w)
        out_spec = [out_spec, pl.BlockSpec((None, tm, plain_cols), lambda bi, i: (bi, i, 0))]
        out_shape = [out_shape, jax.ShapeDtypeStruct((b, s, plain_cols), BF16)]
    return pl.pallas_call(
        functools.partial(_in_proj_kernel, dilation=dilation, rope_tiles=tuple(rope_tiles), tn=tn,
                          plain_cols=plain_cols),
        grid=(b, s // tm),
        in_specs=in_specs,
        out_specs=out_spec,
        out_shape=out_shape,
        scratch_shapes=scratch,
        compiler_params=_cparams(("arbitrary", "arbitrary")),
        name=f"in_proj_d{dilation}",
    )(*args)


def _final_norm_kernel(x_ref, g_ref, o_ref):
    x = x_ref[...]
    ms = jnp.mean(x * x, axis=-1, keepdims=True)
    o_ref[...] = x * lax.rsqrt(ms + NORM_EPS) * g_ref[...]


def final_norm(h, g):
    b, s, d = h.shape
    tm = min(s, ROW_TILE)
    return pl.pallas_call(
        _final_norm_kernel,
        grid=(b, s // tm),
        in_specs=[
            pl.BlockSpec((None, tm, d), lambda bi, i: (bi, i, 0)),
            pl.BlockSpec((1, d), lambda bi, i: (0, 0)),
        ],
        out_specs=pl.BlockSpec((None, tm, d), lambda bi, i: (bi, i, 0)),
        out_shape=jax.ShapeDtypeStruct((b, s, d), F32),
        compiler_params=_cparams(("arbitrary", "arbitrary")),
        name="final_norm",
    )(h, g.reshape(1, d))


def _stack_heads(pair, first_head):
    zero = jnp.zeros_like(pair)
    return jnp.concatenate([jnp.where(first_head, pair, zero), jnp.where(first_head, zero, pair)], axis=0)


def _banded_kernel(q_ref, kp_ref, kc_ref, vp_ref, vc_ref, o_ref, lse_ref, kwin_ref, vwin_ref, *, tq):
    i = pl.program_id(2)
    n_cls = q_ref.shape[0]
    nsub = tq // A_BLK
    lane = lax.broadcasted_iota(I32, (A_BLK, LANES), 1)
    first_head = lane < HEAD_DIM
    odd_lane = (lane % 2) == 1
    r2 = lax.broadcasted_iota(I32, (2 * A_BLK, 2 * A_BLK), 0) % A_BLK
    c2 = lax.broadcasted_iota(I32, (2 * A_BLK, 2 * A_BLK), 1)
    own_ok = (c2 >= A_BLK) & (c2 - A_BLK <= r2)
    ones_blk = jnp.ones((2 * A_BLK, LANES), BF16)
    n_pairs = A_HEADS // 2

    kwin_ref[:, :A_BLK, :] = kp_ref[...]
    kwin_ref[:, A_BLK:, :] = kc_ref[...]
    vwin_ref[:, :A_BLK, :] = vp_ref[...]
    vwin_ref[:, A_BLK:, :] = vc_ref[...]

    def logits(r, a, p):
        cols = slice(p * LANES, (p + 1) * LANES)
        qs = _stack_heads(q_ref[r, a * A_BLK:(a + 1) * A_BLK, cols], first_head)
        k2 = kwin_ref[r, a * A_BLK:(a + 2) * A_BLK, cols]
        return lax.dot_general(qs, k2, (((1,), (1,)), ((), ())), preferred_element_type=F32)

    def band_bias(prev_shift):
        mask = own_ok | ((c2 < A_BLK) & (c2 >= r2 + prev_shift))
        return jnp.where(mask, 0.0, -jnp.inf).astype(F32)

    bias_inner = band_bias(0)
    biases = [band_bias(jnp.where(i > 0, 0, 2 * A_BLK))] + [bias_inner] * (nsub - 1)

    units = [(r, a, p) for r in range(n_cls) for a in range(nsub) for p in range(n_pairs)]
    queue = [logits(*units[u]) for u in range(BANDED_LOOKAHEAD)]
    lse_tile = None
    for u, (r, a, p) in enumerate(units):
        rows = slice(a * A_BLK, (a + 1) * A_BLK)
        cols = slice(p * LANES, (p + 1) * LANES)
        bias = biases[a]
        if p == 0:
            lse_tile = jnp.zeros((A_BLK, LANES), F32)
        s = queue.pop(0) + bias
        if u + BANDED_LOOKAHEAD < len(units):
            queue.append(logits(*units[u + BANDED_LOOKAHEAD]))
        m = jnp.max(s, axis=-1, keepdims=True)
        e = jnp.exp2(s - m).astype(BF16)
        v_ext = jnp.concatenate([vwin_ref[r, a * A_BLK:(a + 2) * A_BLK, cols], ones_blk], axis=1)
        pv = jnp.dot(e, v_ext, preferred_element_type=F32)
        num = jnp.where(first_head, pv[:A_BLK, :LANES], pv[A_BLK:, :LANES])
        o_ref[r, rows, cols] = num * (1.0 / jnp.where(first_head, pv[:A_BLK, LANES:], pv[A_BLK:, LANES:]))
        den_pair = jnp.where(odd_lane, pv[A_BLK:, LANES:], pv[:A_BLK, LANES:])
        m_pair = jnp.where(odd_lane, m[A_BLK:], m[:A_BLK])
        lse_tile = jnp.where((lane // 2) == p, m_pair + jnp.log(den_pair) * LOG2_E, lse_tile)
        if p == n_pairs - 1:
            lse_ref[r, rows, :] = lse_tile


BANDED_ROWS = 1024


def banded_group_attention(qkv):
    b, d, n, _ = qkv.shape
    tq = min(BANDED_ROWS, n)
    rc = min(BANDED_ROWS // tq, d)
    sub = tq // A_BLK
    assert n % tq == 0 and d % rc == 0

    def cur(blk, width=A_WIDTH):
        return pl.BlockSpec((None, rc, tq, width), lambda bi, r, i: (bi, r, i, blk))

    def prev(blk):
        return pl.BlockSpec((None, rc, A_BLK, A_WIDTH),
                            lambda bi, r, i: (bi, r, jnp.maximum(i * sub - 1, 0), blk))

    return pl.pallas_call(
        functools.partial(_banded_kernel, tq=tq),
        grid=(b, d // rc, n // tq),
        in_specs=[cur(0), prev(1), cur(1), prev(2), cur(2)],
        out_specs=[cur(0), cur(0, LANES)],
        out_shape=[
            jax.ShapeDtypeStruct((b, d, n, A_WIDTH), F32),
            jax.ShapeDtypeStruct((b, d, n, LANES), F32),
        ],
        scratch_shapes=[pltpu.VMEM((rc, tq + A_BLK, A_WIDTH), BF16)] * 2,
        compiler_params=_cparams(("arbitrary", "arbitrary", "arbitrary")),
        name=f"banded_attn_d{d}",
    )(qkv, qkv, qkv, qkv, qkv)


def _expand_heads(x, expand):
    hi = x.astype(BF16)
    lo = (x - hi.astype(F32)).astype(BF16)
    return (jnp.dot(hi, expand, preferred_element_type=F32)
            + jnp.dot(lo, expand, preferred_element_type=F32))


def _merge_out_kernel(o0_ref, o1_ref, o2_ref, l0_ref, l1_ref, l2_ref, gate_ref, ex_ref, w_ref,
                      mod_ref, h_ref, hout_ref, y_ref, ot_ref, lt_ref, *, dilations):
    o_refs, l_refs = (o0_ref, o1_ref, o2_ref), (l0_ref, l1_ref, l2_ref)
    tm = y_ref.shape[0]
    nblk = y_ref.shape[1] // LANES
    for g, d in enumerate(dilations):
        if d == 1:
            continue
        per = tm // d
        for r in range(d):
            lt_ref[g, pl.ds(r, per, stride=d), :] = l_refs[g][r]
            for k in range(nblk):
                ot_ref[g, k, pl.ds(r, per, stride=d), :] = o_refs[g][r, :, k * LANES:(k + 1) * LANES]
    lses = [l_refs[g][0] if d == 1 else lt_ref[g] for g, d in enumerate(dilations)]
    m = jnp.maximum(jnp.maximum(lses[0], lses[1]), lses[2])
    es = [jnp.exp2(l - m) for l in lses]
    inv = 1.0 / (es[0] + es[1] + es[2])
    ex = ex_ref[...]
    alphas = [_expand_heads(e * inv, ex) for e in es[:-1]]
    alphas.append(1.0 - alphas[0] - alphas[1])
    for k in range(nblk):
        cols = slice(k * LANES, (k + 1) * LANES)
        y = jnp.zeros((tm, LANES), F32)
        for g, d in enumerate(dilations):
            og = o_refs[g][0, :, cols] if d == 1 else ot_ref[g, k]
            y += alphas[g][:, cols] * og
        gate = gate_ref[:, cols].astype(F32)
        y_ref[:, cols] = (y * gate * (1.0 / (1.0 + jnp.exp(-gate)))).astype(y_ref.dtype)
    acc = jnp.dot(y_ref[...], w_ref[...], preferred_element_type=F32)
    hout_ref[...] = h_ref[...] + mod_ref[...] * acc


def merge_out_proj(outs, lses, gate, w_out, mod_gate, h):
    b, s, w = gate.shape
    d_model = h.shape[-1]
    tm = min(s, MERGE_ROWS)
    dil = tuple(o.shape[1] for o in outs)
    expand = (jnp.arange(LANES)[:, None] == (jnp.arange(w)[None, :] // HEAD_DIM)).astype(BF16)

    def cm(d, width):
        return pl.BlockSpec((None, d, tm // d, width), lambda bi, i: (bi, 0, i, 0))

    def tspec(width):
        return pl.BlockSpec((None, tm, width), lambda bi, i: (bi, i, 0))

    return pl.pallas_call(
        functools.partial(_merge_out_kernel, dilations=dil),
        grid=(b, s // tm),
        in_specs=[cm(d, w) for d in dil] + [cm(d, LANES) for d in dil]
                 + [tspec(w), pl.BlockSpec((LANES, w), lambda bi, i: (0, 0)),
                    pl.BlockSpec((w, d_model), lambda bi, i: (0, 0)),
                    pl.BlockSpec((None, 1, d_model), lambda bi, i: (bi, 0, 0)),
                    tspec(d_model)],
        out_specs=tspec(d_model),
        out_shape=jax.ShapeDtypeStruct((b, s, d_model), F32),
        scratch_shapes=[pltpu.VMEM((tm, w), BF16),
                        pltpu.VMEM((len(dil), w // LANES, tm, LANES), F32),
                        pltpu.VMEM((len(dil), tm, LANES), F32)],
        compiler_params=_cparams(("arbitrary", "arbitrary")),
        name="merge_out_proj",
    )(*outs, *lses, gate, expand, w_out, mod_gate.reshape(b, 1, d_model), h)


def dilated_layer(h, g, scale, shift, mod_gate, w_in, w_out, cos, sin):
    tn = PROJ_COL_TILE
    per = A_WIDTH // tn
    qkv_flags = (True,) * (2 * per) + (False,) * per
    n_groups = len(A_GROUPS)
    col = np.arange(w_in.shape[1])
    is_q = (col < n_groups * 3 * A_WIDTH) & ((col // A_WIDTH) % 3 == 0)
    w = (w_in * jnp.asarray(np.where(is_q, Q_PRESCALE, 1.0), F32)).astype(BF16)
    outs, lses = [], []
    gate = None
    for gi, (window, dilation) in enumerate(A_GROUPS):
        assert window // dilation == A_BLK
        plain = (A_WIDTH, 3 * n_groups) if gi == 0 else None
        qkv = in_proj(h, g, scale, shift, w, qkv_flags, cos, sin, tn, dilation,
                      w_block=gi, plain=plain)
        if plain is not None:
            qkv, gate = qkv
        if dilation == 1:
            qkv = qkv[:, None]
        o, lse = banded_group_attention(qkv)
        outs.append(o)
        lses.append(lse)
    return merge_out_proj(outs, lses, gate, w_out.astype(BF16), mod_gate, h)


B_COLS = 3584
B_ROPE_FLAGS = (1, 1, 0, 0, 1, 1, 0)


def _dsa_head_perm():
    group = B_HEADS // B_KV_HEADS
    order = []
    for g2 in range(B_KV_HEADS // 2):
        for r in range(group):
            order += [(2 * g2) * group + r, (2 * g2 + 1) * group + r]
    return np.asarray(order)


def _dsa_weights(w_in, w_out):
    d = w_in.shape[0]
    cuts = np.cumsum((B_WIDTH, B_KV_HEADS * HEAD_DIM, B_KV_HEADS * HEAD_DIM, IDX_HEADS * HEAD_DIM,
                      HEAD_DIM, IDX_HEADS, B_WIDTH))[:-1]
    wq, wk, wv, wqi, wki, wwi, wg = jnp.split(w_in, cuts, axis=1)
    cols = (_dsa_head_perm()[:, None] * HEAD_DIM + np.arange(HEAD_DIM)[None, :]).reshape(-1)
    zeros = lambda n: jnp.zeros((d, n), w_in.dtype)
    wq = wq * Q_PRESCALE
    w = jnp.concatenate([wq[:, cols], wg[:, cols], wqi, wk, wki, wki, zeros(LANES),
                         wwi, zeros(2 * LANES - IDX_HEADS), wv], axis=1)
    assert w.shape[1] == B_COLS
    return w.astype(BF16), w_out[cols, :].astype(BF16)


def _sortable(x):
    bits = pltpu.bitcast(x, I32)
    return bits ^ ((bits >> 31) & jnp.int32(0x7FFFFFFF))


def _bit_transpose32(words):
    a = list(words)
    j, m = 16, 0x0000FFFF
    while j:
        mask = jnp.int32(m - (1 << 32) if m >= (1 << 31) else m)
        shift = jnp.full(a[0].shape, j, I32)
        k = 0
        while k < 32:
            t = (a[k] ^ lax.shift_right_logical(a[k + j], shift)) & mask
            a[k] = a[k] ^ t
            a[k + j] = a[k + j] ^ lax.shift_left(t, shift)
            k = (k + j + 1) & ~j
        j >>= 1
        m = (m ^ (m << j)) & 0xFFFFFFFF
    return a


def _sublane_allsum(x):
    x = x + pltpu.roll(x, 4, 0)
    x = x + pltpu.roll(x, 2, 0)
    return x + pltpu.roll(x, 1, 0)


def _tree_sum(parts):
    parts = list(parts)
    while len(parts) > 1:
        odd = [parts[-1]] if len(parts) % 2 else []
        parts = [parts[j] + parts[j + 1] for j in range(0, len(parts) - 1, 2)] + odd
    return parts[0]


GROUP_KEYS = 256
ATTN_LOOKAHEAD = 1


def _dsa_kernel(q_ref, gate_ref, qi_ref, k_ref, ki_ref, wi_ref, v_ref, wout_ref, mod_ref, h_ref, *rest,
                k_sel, final):
    if final:
        fg_ref, hout_ref, *scratch = rest
    else:
        hout_ref, *scratch = rest
    keys_ref, planes_ref, bias_ref, kt_ref, qs_ref, s0_ref, m_ref, l_ref, acc_ref, y_ref = scratch
    i = pl.program_id(1)
    seq = keys_ref.shape[1]
    n_sub = Q_BLOCK // SEL_LANES
    n_pairs = B_HEADS // 2
    n_groups_max = seq // GROUP_KEYS
    lane_q = lax.broadcasted_iota(I32, (Q_BLOCK, LANES), 1)
    first_head_q = lane_q < HEAD_DIM
    first_head = lax.broadcasted_iota(I32, (SEL_LANES, LANES), 1) < HEAD_DIM
    key_row = lax.broadcasted_iota(I32, (KEY_TILE, SEL_LANES), 0)
    lane_idx = lax.broadcasted_iota(I32, (KEY_TILE, SEL_LANES), 1)

    @pl.when(i == 0)
    def _():
        planes_ref[...] = jnp.zeros(planes_ref.shape, I32)
        keys_ref[...] = jnp.zeros(keys_ref.shape, I32)

    def tiles_of(a):
        return ((i * n_sub + a) * SEL_LANES) // KEY_TILE + 1

    def select(a):
        n_tiles = tiles_of(a)
        rows_a = slice(a * SEL_LANES, (a + 1) * SEL_LANES)
        q_pos = (i * n_sub + a) * SEL_LANES + lane_idx
        keys = keys_ref.at[a]

        w_t = (wi_ref[rows_a, :].astype(F32) * (IDX_HEADS ** -0.5 * HEAD_DIM ** -0.5)).T
        qi_stacked = [_stack_heads(qi_ref[rows_a, p * LANES:(p + 1) * LANES], first_head)
                      for p in range(IDX_HEADS // 2)]

        def score_tile(t):
            base = pl.multiple_of(t * KEY_TILE, KEY_TILE)
            kk = ki_ref[pl.ds(base, KEY_TILE), :]
            score = jnp.zeros((KEY_TILE, SEL_LANES), F32)
            for p in range(IDX_HEADS // 2):
                sc = lax.dot_general(kk, qi_stacked[p], (((1,), (1,)), ((), ())),
                                     preferred_element_type=F32)
                score += jnp.maximum(sc[:, :SEL_LANES], 0.0) * w_t[2 * p:2 * p + 1, :]
                score += jnp.maximum(sc[:, SEL_LANES:], 0.0) * w_t[2 * p + 1:2 * p + 2, :]
            causal = (key_row + base) <= q_pos
            score = jnp.where(causal, score + 0.0, -jnp.inf)
            key = _sortable(score)
            keys[pl.ds(base, KEY_TILE), :] = key
            ukey = key ^ jnp.int32(INT_MIN)
            for g in range(KEY_TILE // GROUP_KEYS):
                words = [ukey[g * GROUP_KEYS + 8 * j:g * GROUP_KEYS + 8 * j + 8, :] for j in range(32)]
                rows = pl.ds(pl.multiple_of(t * (KEY_TILE // 32) + 8 * g, 8), 8)
                for b, plane in enumerate(_bit_transpose32(words)):
                    planes_ref[b, rows, :] = plane

        def score_two(t2, carry):
            score_tile(2 * t2)
            score_tile(2 * t2 + 1)
            return carry

        lax.fori_loop(0, n_tiles // 2, score_two, 0)

        @pl.when(n_tiles % 2 == 1)
        def _():
            score_tile(n_tiles - 1)

        def select_threshold():
            n_groups = n_tiles * (KEY_TILE // GROUP_KEYS)
            alive0 = tuple(jnp.full((8, SEL_LANES), -1, I32) * (g < n_groups).astype(I32)
                           for g in range(n_groups_max))
            need0 = jnp.full((8, SEL_LANES), k_sel, I32)

            def step(st, carry):
                alive, need, prefix = carry
                planes = [planes_ref[st, 8 * g:8 * g + 8, :] for g in range(n_groups_max)]
                ones = [al & pln for al, pln in zip(alive, planes)]
                cnt = _sublane_allsum(_tree_sum([lax.population_count(o) for o in ones]))
                take = cnt >= need
                prefix = jnp.where(take, prefix | jnp.left_shift(jnp.int32(1), 31 - st), prefix)
                need = jnp.where(take, need, need - cnt)
                flip = jnp.where(take, 0, -1)
                alive = tuple(al & (pln ^ flip) for al, pln in zip(alive, planes))
                return alive, need, prefix

            alive, need, prefix = lax.fori_loop(
                0, 32, step, (alive0, need0, jnp.zeros((8, SEL_LANES), I32)))
            thr = (prefix ^ jnp.int32(INT_MIN))[:1, :]
            need_eq = need[:1, :]

            n_eq = _sublane_allsum(_tree_sum([lax.population_count(al) for al in alive]))[:1, :]
            tied = (n_eq > need_eq) & (thr > KEY_NEG_INF)
            idx_bits = (seq - 1).bit_length()

            def tie_break():
                def count_eq_before(cut):
                    def body(t, acc):
                        base = pl.multiple_of(t * KEY_TILE, KEY_TILE)
                        hit = (keys[pl.ds(base, KEY_TILE), :] == thr) & ((key_row + base) < cut)
                        return acc + jnp.sum(hit.astype(I32).reshape(KEY_TILE // 8, 8, SEL_LANES), axis=0)
                    acc = lax.fori_loop(0, n_tiles, body, jnp.zeros((8, SEL_LANES), I32))
                    return jnp.sum(acc, axis=0, keepdims=True)

                def bit_step(it, cut):
                    cand = cut + jnp.left_shift(jnp.int32(1), idx_bits - 1 - it)
                    return jnp.where(count_eq_before(cand) < need_eq, cand, cut)

                return lax.fori_loop(0, idx_bits, bit_step, jnp.zeros((1, SEL_LANES), I32))

            any_tied = jnp.max(tied.astype(I32)) > 0
            idx_cut = lax.cond(any_tied, tie_break, lambda: jnp.full((1, SEL_LANES), seq, I32))
            return thr, idx_cut

        all_selected = (i * n_sub + a + 1) * SEL_LANES <= k_sel
        thr, idx_cut = lax.cond(
            all_selected,
            lambda: (jnp.full((1, SEL_LANES), INT_MIN, I32), jnp.full((1, SEL_LANES), seq, I32)),
            select_threshold)
        return thr, idx_cut, q_pos

    selected = [select(a) for a in range(n_sub)]
    n_tiles = tiles_of(n_sub - 1)

    def write_bias(base):
        k_idx = key_row + base
        for a, (thr, idx_cut, q_pos) in enumerate(selected):
            key = keys_ref[a, pl.ds(base, KEY_TILE), :]
            sel = ((key > thr) | ((key == thr) & (k_idx <= idx_cut))) & (k_idx <= q_pos)
            bias_t = jnp.where(sel, 0.0, -jnp.inf).astype(F32)
            for c in range(KEY_TILE // LANES):
                col = base + c * LANES
                col = col if isinstance(col, int) else pl.multiple_of(col, LANES)
                bias_ref[a * SEL_LANES:(a + 1) * SEL_LANES, pl.ds(col, LANES)] = \
                    bias_t[c * LANES:(c + 1) * LANES, :].T

    write_bias(0)

    for p in range(n_pairs):
        cols = slice(p * LANES, (p + 1) * LANES)
        qs_ref[p] = _stack_heads(q_ref[:, cols], first_head_q)
    heads_per_kv_pair = 2 * (B_HEADS // B_KV_HEADS) // 2

    ones_blk = jnp.ones((KEY_TILE, LANES), BF16)

    @pl.when(i == 0)
    def _():
        def xpose(r, carry):
            rows = pl.ds(pl.multiple_of(r * LANES, LANES), LANES)
            for c in range(kt_ref.shape[0]):
                blk = k_ref[rows, c * LANES:(c + 1) * LANES].astype(F32)
                kt_ref[c, :, rows] = blk.T.astype(BF16)
            return carry
        lax.fori_loop(0, seq // LANES, xpose, 0)

    def logits(base, p):
        kt = kt_ref[p // heads_per_kv_pair, :, pl.ds(base, KEY_TILE)]
        return jnp.dot(qs_ref[p], kt, preferred_element_type=F32)

    lookahead = s0_ref.shape[0]
    for u in range(lookahead):
        s0_ref[u] = logits(0, u)

    def attn_tiles(t0, count, start=False):
        bases = [pl.multiple_of(jnp.minimum(t0 + dt, n_tiles - 1) * KEY_TILE, KEY_TILE)
                 for dt in range(count + 1)]
        units = [(dt, p) for dt in range(count + 1) for p in range(n_pairs)]
        queue = [s0_ref[u] for u in range(lookahead)]
        for u, (dt, p) in enumerate(units[:count * n_pairs]):
            base = bases[dt]
            if p == 0:
                bias = bias_ref[:, pl.ds(base, KEY_TILE)]
                write_bias(bases[dt + 1])
                bias2 = jnp.concatenate([bias, bias], axis=0)
            kv = p // heads_per_kv_pair
            kv_cols = slice(kv * LANES, (kv + 1) * LANES)
            s = queue.pop(0) + bias2
            ahead_dt, ahead_p = units[u + lookahead]
            queue.append(logits(bases[ahead_dt], ahead_p))
            m_blk = s[:, :LANES]
            for c in range(1, KEY_TILE // LANES):
                m_blk = jnp.maximum(m_blk, s[:, c * LANES:(c + 1) * LANES])
            opening = start and dt == 0
            m_cur = jnp.max(m_blk, axis=-1, keepdims=True)
            if opening:
                m_new = jnp.broadcast_to(jnp.maximum(m_cur, F32_MIN), m_blk.shape)
            else:
                m_old = m_ref[p]
                m_new = jnp.maximum(m_old, m_cur)
                alpha = jnp.exp2(m_old - m_new)
            e = jnp.concatenate(
                [jnp.exp2(s[:, c * LANES:(c + 1) * LANES] - m_new).astype(BF16)
                 for c in range(KEY_TILE // LANES)], axis=1)
            v_ext = jnp.concatenate([v_ref[pl.ds(base, KEY_TILE), kv_cols], ones_blk], axis=1)
            pv = jnp.dot(e, v_ext, preferred_element_type=F32)
            if opening:
                acc_ref[p] = pv[:, :LANES]
                l_ref[p] = pv[:, LANES:]
            else:
                acc_ref[p] = acc_ref[p] * alpha + pv[:, :LANES]
                l_ref[p] = l_ref[p] * alpha + pv[:, LANES:]
            m_ref[p] = m_new
        for u in range(lookahead):
            s0_ref[u] = queue[u]

    @pl.when(n_tiles >= 2)
    def _():
        attn_tiles(0, 2, start=True)

    @pl.when(n_tiles < 2)
    def _():
        attn_tiles(0, 1, start=True)

    def attn_two(t2, carry):
        attn_tiles(2 * t2, 2)
        return carry

    lax.fori_loop(1, n_tiles // 2, attn_two, 0)

    @pl.when((n_tiles % 2 == 1) & (n_tiles > 1))
    def _():
        attn_tiles(n_tiles - 1, 1)

    for p in range(n_pairs):
        cols = slice(p * LANES, (p + 1) * LANES)
        l = l_ref[p]
        acc = acc_ref[p]
        num = jnp.where(first_head_q, acc[:Q_BLOCK], acc[Q_BLOCK:])
        den = jnp.where(first_head_q, l[:Q_BLOCK], l[Q_BLOCK:])
        gate = gate_ref[:, cols].astype(F32)
        y_ref[:, cols] = (num * gate / (den * (1.0 + jnp.exp(-gate)))).astype(y_ref.dtype)

    h_new = h_ref[...] + mod_ref[...] * jnp.dot(y_ref[...], wout_ref[...], preferred_element_type=F32)
    if final:
        ms = jnp.mean(h_new * h_new, axis=-1, keepdims=True)
        h_new = h_new * lax.rsqrt(ms + NORM_EPS) * fg_ref[...]
    hout_ref[...] = h_new


def dsa_attention(proj, w_out, mod_gate, h, final_g=None):
    b, s, _ = proj.shape
    d_model = h.shape[-1]
    k_sel = min(TOPK_MAX, s // 4)
    assert s % KEY_TILE == 0 and s % Q_BLOCK == 0 and k_sel <= KEY_TILE
    kvw = B_KV_HEADS * HEAD_DIM
    n_pairs = B_HEADS // 2

    def qblock(width, idx):
        return pl.BlockSpec((None, Q_BLOCK, width), lambda bi, i: (bi, i, idx))

    def full(width, idx):
        return pl.BlockSpec((None, s, width), lambda bi, i: (bi, 0, idx))

    in_specs = [
        qblock(B_WIDTH, 0),
        qblock(B_WIDTH, 1),
        qblock(IDX_HEADS * HEAD_DIM, 4),
        full(kvw, 10),
        full(LANES, 22),
        qblock(LANES, 24),
        full(kvw, 13),
        pl.BlockSpec((B_WIDTH, d_model), lambda bi, i: (0, 0)),
        pl.BlockSpec((None, 1, d_model), lambda bi, i: (bi, 0, 0)),
        qblock(d_model, 0),
    ]
    args = [proj] * 7 + [w_out, mod_gate.reshape(b, 1, d_model), h]
    if final_g is not None:
        in_specs.append(pl.BlockSpec((1, d_model), lambda bi, i: (0, 0)))
        args.append(final_g.reshape(1, d_model))
    return pl.pallas_call(
        functools.partial(_dsa_kernel, k_sel=k_sel, final=final_g is not None),
        grid=(b, s // Q_BLOCK),
        in_specs=in_specs,
        out_specs=qblock(d_model, 0),
        out_shape=jax.ShapeDtypeStruct((b, s, d_model), F32),
        scratch_shapes=[
            pltpu.VMEM((Q_BLOCK // SEL_LANES, s, SEL_LANES), I32),
            pltpu.VMEM((32, s // 32, SEL_LANES), I32),
            pltpu.VMEM((Q_BLOCK, s), F32),
            pltpu.VMEM((kvw // LANES, LANES, s), BF16),
            pltpu.VMEM((n_pairs, 2 * Q_BLOCK, LANES), BF16),
            pltpu.VMEM((ATTN_LOOKAHEAD, 2 * Q_BLOCK, KEY_TILE), F32),
            pltpu.VMEM((n_pairs, 2 * Q_BLOCK, LANES), F32),
            pltpu.VMEM((n_pairs, 2 * Q_BLOCK, LANES), F32),
            pltpu.VMEM((n_pairs, 2 * Q_BLOCK, LANES), F32),
            pltpu.VMEM((Q_BLOCK, B_WIDTH), BF16),
        ],
        compiler_params=_cparams(("arbitrary", "arbitrary")),
        name="dsa_attention",
    )(*args)


def dsa_layer(h, g, scale, shift, mod_gate, w_in_packed, w_out, cos, sin, final_g=None):
    proj = in_proj(h, g, scale, shift, w_in_packed, [f == 1 for f in B_ROPE_FLAGS], cos, sin,
                   PROJ_COL_TILE)
    return dsa_attention(proj, w_out, mod_gate, h, final_g)


def kernel(x, c, positions, norm_g, ada_w, ada_b, a_w_in, a_w_out, b_w_in, b_w_out, final_g):
    depth = norm_g.shape[0]
    d = x.shape[-1]
    mod = adaln_mod(c, ada_w, ada_b)
    cos, sin = rope_tables(positions)
    h = x
    for i in range(depth):
        shift, scale, gate = mod[i, :, :d], mod[i, :, d:2 * d], mod[i, :, 2 * d:]
        last = i == depth - 1
        if i % 2 == 0:
            h = dilated_layer(h, norm_g[i], scale, shift, gate, a_w_in[i // 2], a_w_out[i // 2], cos, sin)
            if last:
                h = final_norm(h, final_g)
        else:
            w_in, w_out = _dsa_weights(b_w_in[i // 2], b_w_out[i // 2])
            h = dsa_layer(h, norm_g[i], scale, shift, gate, w_in, w_out, cos, sin,
                          final_g if last else None)
    return h
```

```python
import functools

import numpy as np
import jax
import jax.numpy as jnp
from jax import lax
from jax.experimental import pallas as pl
from jax.experimental.pallas import tpu as pltpu

F32 = jnp.float32
BF16 = jnp.bfloat16
I32 = jnp.int32

LANES = 128
HEAD_DIM = 64
HALF = HEAD_DIM // 2
ROPE_THETA = 10000.0
NORM_EPS = 1e-6
VMEM_LIMIT = 56 * 1024 * 1024
ROW_CHUNK = 256
ROW_TILE = 1024
PROJ_COL_TILE = 512
MERGE_ROWS = 512

A_HEADS = 16
A_WIDTH = A_HEADS * HEAD_DIM
A_GROUPS = ((128, 1), (512, 4), (2048, 16))
A_BLK = 128
BANDED_LOOKAHEAD = 1

B_HEADS = 16
B_KV_HEADS = 4
B_WIDTH = B_HEADS * HEAD_DIM
IDX_HEADS = 8
TOPK_MAX = 256
Q_BLOCK = 256
SEL_LANES = 128
KEY_TILE = 512
INT_MIN = -(2 ** 31)
KEY_NEG_INF = -2139095041
F32_MIN = float(np.finfo(np.float32).min)
LOG2_E = float(np.log2(np.e))
Q_PRESCALE = LOG2_E * HEAD_DIM ** -0.5


def _cparams(sem):
    return pltpu.CompilerParams(dimension_semantics=sem, vmem_limit_bytes=VMEM_LIMIT)


def _adaln_kernel(c_ref, w_ref, b_ref, o_ref):
    c = c_ref[...]
    ca = (c * (1.0 / (1.0 + jnp.exp(-c)))).astype(BF16)
    acc = jnp.dot(ca, w_ref[...].astype(BF16), preferred_element_type=F32)
    o_ref[...] = acc + b_ref[...]


def adaln_mod(c, ada_w, ada_b):
    depth, d, d3 = ada_w.shape
    b = c.shape[0]
    tn = 1024
    return pl.pallas_call(
        _adaln_kernel,
        grid=(depth, d3 // tn),
        in_specs=[
            pl.BlockSpec((b, d), lambda i, j: (0, 0)),
            pl.BlockSpec((None, d, tn), lambda i, j: (i, 0, j)),
            pl.BlockSpec((None, 1, tn), lambda i, j: (i, 0, j)),
        ],
        out_specs=pl.BlockSpec((None, b, tn), lambda i, j: (i, 0, j)),
        out_shape=jax.ShapeDtypeStruct((depth, b, d3), F32),
        compiler_params=_cparams(("arbitrary", "arbitrary")),
        name="adaln_mod",
    )(c, ada_w, ada_b.reshape(depth, 1, d3))


def _rope_table_kernel(pos_ref, inv_ref, cos_ref, sin_ref, nsin_ref):
    ang = pos_ref[...].astype(F32) * inv_ref[...]
    cos_ref[...] = jnp.cos(ang)
    s = jnp.sin(ang)
    sin_ref[...] = s
    nsin_ref[...] = -s


def rope_tables(positions):
    b, s = positions.shape
    per_row = LANES // HALF
    assert s % per_row == 0
    inv_freq = ROPE_THETA ** (-jnp.arange(HALF, dtype=F32) / HALF)
    inv_lane = jnp.tile(inv_freq, per_row).reshape(1, LANES)
    pos_rows = jnp.repeat(positions.reshape(b, s // per_row, per_row), HALF, axis=2)
    rows = s // per_row
    ts = min(rows, ROW_TILE)
    spec = pl.BlockSpec((None, ts, LANES), lambda i, j: (i, j, 0))
    out = jax.ShapeDtypeStruct((b, rows, LANES), F32)
    cos, sin, nsin = pl.pallas_call(
        _rope_table_kernel,
        grid=(b, rows // ts),
        in_specs=[spec, pl.BlockSpec((1, LANES), lambda i, j: (0, 0))],
        out_specs=[spec] * 3,
        out_shape=[out] * 3,
        compiler_params=_cparams(("arbitrary", "arbitrary")),
        name="rope_tables",
    )(pos_rows, inv_lane)
    cos, sin, nsin = (t.reshape(b, s, HALF) for t in (cos, sin, nsin))
    return (jnp.tile(cos, (1, 1, per_row)),
            jnp.concatenate([nsin, sin] * (per_row // 2), axis=2))


def _rope_block(t, cos, sin_signed, first_half):
    partner = jnp.where(first_half, pltpu.roll(t, LANES - HALF, 1), pltpu.roll(t, HALF, 1))
    return t * cos + partner * sin_signed


def _in_proj_kernel(x_ref, g_ref, sc_ref, sh_ref, w_ref, cos_ref, sin_ref, *rest,
                    dilation, rope_tiles, tn, plain_cols):
    if plain_cols:
        w2_ref, o_ref, o2_ref, u_ref, *ucm = rest
    else:
        o_ref, u_ref, *ucm = rest
    tm = u_ref.shape[0]
    nblk = tn // LANES
    per = ROW_CHUNK // dilation
    x = x_ref[...]
    ms = jnp.mean(x * x, axis=-1, keepdims=True)
    xn = x * lax.rsqrt(ms + NORM_EPS)
    u_ref[...] = (xn * g_ref[...] * (1.0 + sc_ref[...]) + sh_ref[...]).astype(BF16)
    lhs_ref = u_ref
    if dilation > 1:
        lhs_ref = ucm[0]
        dst = lax.broadcasted_iota(I32, (ROW_CHUNK, ROW_CHUNK), 0)
        src = lax.broadcasted_iota(I32, (ROW_CHUNK, ROW_CHUNK), 1)
        perm = (src == (dst % per) * dilation + dst // per).astype(BF16)
        for c in range(tm // ROW_CHUNK):
            rows = slice(c * ROW_CHUNK, (c + 1) * ROW_CHUNK)
            lhs_ref[rows, :] = jnp.dot(perm, u_ref[rows, :], preferred_element_type=F32).astype(BF16)
    lane = lax.broadcasted_iota(I32, (per, LANES), 1)
    first_half = (lane % HEAD_DIM) < HALF
    for j, rope in enumerate(rope_tiles):
        for c in range(tm // ROW_CHUNK):
            acc = jnp.dot(lhs_ref[c * ROW_CHUNK:(c + 1) * ROW_CHUNK, :], w_ref[:, j * tn:(j + 1) * tn],
                          preferred_element_type=F32)
            for r in range(dilation):
                rows = slice(c * per, (c + 1) * per)
                if rope:
                    trows = rows if dilation == 1 else pl.ds(c * ROW_CHUNK + r, per, stride=dilation)
                    cos, sin = cos_ref[trows, :], sin_ref[trows, :]
                for k in range(nblk):
                    cols = slice(j * tn + k * LANES, j * tn + (k + 1) * LANES)
                    blk = acc[r * per:(r + 1) * per, k * LANES:(k + 1) * LANES]
                    if rope:
                        blk = _rope_block(blk, cos, sin, first_half)
                    if dilation > 1:
                        o_ref[r, rows, cols] = blk.astype(o_ref.dtype)
                    else:
                        o_ref[rows, cols] = blk.astype(o_ref.dtype)
    for j in range(plain_cols // tn):
        for c in range(tm // ROW_CHUNK):
            rows = slice(c * ROW_CHUNK, (c + 1) * ROW_CHUNK)
            acc = jnp.dot(u_ref[rows, :], w2_ref[:, j * tn:(j + 1) * tn], preferred_element_type=F32)
            o2_ref[rows, j * tn:(j + 1) * tn] = acc.astype(o2_ref.dtype)


def in_proj(h, g, scale, shift, w, rope_tiles, cos, sin, tn, dilation=1, w_block=0, plain=None):
    b, s, d = h.shape
    n = len(rope_tiles) * tn
    tm = min(s, ROW_TILE)
    assert tm % ROW_CHUNK == 0 and ROW_CHUNK % (16 * dilation) == 0 and (w_block + 1) * n <= w.shape[1]
    scratch = [pltpu.VMEM((tm, d), BF16)]
    tab_spec = pl.BlockSpec((None, tm, LANES), lambda bi, i: (bi, i, 0))
    if dilation == 1:
        out_spec = pl.BlockSpec((None, tm, n), lambda bi, i: (bi, i, 0))
        out_shape = jax.ShapeDtypeStruct((b, s, n), BF16)
    else:
        out_spec = pl.BlockSpec((None, dilation, tm // dilation, n), lambda bi, i: (bi, 0, i, 0))
        out_shape = jax.ShapeDtypeStruct((b, dilation, s // dilation, n), BF16)
        scratch.append(pltpu.VMEM((tm, d), BF16))
    in_specs = [
        pl.BlockSpec((None, tm, d), lambda bi, i: (bi, i, 0)),
        pl.BlockSpec((1, d), lambda bi, i: (0, 0)),
        pl.BlockSpec((None, 1, d), lambda bi, i: (bi, 0, 0)),
        pl.BlockSpec((None, 1, d), lambda bi, i: (bi, 0, 0)),
        pl.BlockSpec((d, n), lambda bi, i: (0, w_block)),
        tab_spec, tab_spec,
    ]
    args = [h, g.reshape(1, d), scale.reshape(b, 1, d), shift.reshape(b, 1, d), w, cos, sin]
    plain_cols = 0
    if plain is not None:
        plain_cols, plain_block = plain
        assert plain_cols % tn == 0 and (plain_block + 1) * plain_cols <= w.shape[1]
        in_specs.append(pl.BlockSpec((d, plain_cols), lambda bi, i: (0, plain_block)))
        args.append(w)
        out_spec = [out_spec, pl.BlockSpec((None, tm, plain_cols), lambda bi, i: (bi, i, 0))]
        out_shape = [out_shape, jax.ShapeDtypeStruct((b, s, plain_cols), BF16)]
    return pl.pallas_call(
        functools.partial(_in_proj_kernel, dilation=dilation, rope_tiles=tuple(rope_tiles), tn=tn,
                          plain_cols=plain_cols),
        grid=(b, s // tm),
        in_specs=in_specs,
        out_specs=out_spec,
        out_shape=out_shape,
        scratch_shapes=scratch,
        compiler_params=_cparams(("arbitrary", "arbitrary")),
        name=f"in_proj_d{dilation}",
    )(*args)


def _final_norm_kernel(x_ref, g_ref, o_ref):
    x = x_ref[...]
    ms = jnp.mean(x * x, axis=-1, keepdims=True)
    o_ref[...] = x * lax.rsqrt(ms + NORM_EPS) * g_ref[...]


def final_norm(h, g):
    b, s, d = h.shape
    tm = min(s, ROW_TILE)
    return pl.pallas_call(
        _final_norm_kernel,
        grid=(b, s // tm),
        in_specs=[
            pl.BlockSpec((None, tm, d), lambda bi, i: (bi, i, 0)),
            pl.BlockSpec((1, d), lambda bi, i: (0, 0)),
        ],
        out_specs=pl.BlockSpec((None, tm, d), lambda bi, i: (bi, i, 0)),
        out_shape=jax.ShapeDtypeStruct((b, s, d), F32),
        compiler_params=_cparams(("arbitrary", "arbitrary")),
        name="final_norm",
    )(h, g.reshape(1, d))


def _stack_heads(pair, first_head):
    zero = jnp.zeros_like(pair)
    return jnp.concatenate([jnp.where(first_head, pair, zero), jnp.where(first_head, zero, pair)], axis=0)


def _banded_kernel(q_ref, kp_ref, kc_ref, vp_ref, vc_ref, o_ref, lse_ref, kwin_ref, vwin_ref, *, tq):
    i = pl.program_id(2)
    n_cls = q_ref.shape[0]
    nsub = tq // A_BLK
    lane = lax.broadcasted_iota(I32, (A_BLK, LANES), 1)
    first_head = lane < HEAD_DIM
    odd_lane = (lane % 2) == 1
    r2 = lax.broadcasted_iota(I32, (2 * A_BLK, 2 * A_BLK), 0) % A_BLK
    c2 = lax.broadcasted_iota(I32, (2 * A_BLK, 2 * A_BLK), 1)
    own_ok = (c2 >= A_BLK) & (c2 - A_BLK <= r2)
    ones_blk = jnp.ones((2 * A_BLK, LANES), BF16)
    n_pairs = A_HEADS // 2

    kwin_ref[:, :A_BLK, :] = kp_ref[...]
    kwin_ref[:, A_BLK:, :] = kc_ref[...]
    vwin_ref[:, :A_BLK, :] = vp_ref[...]
    vwin_ref[:, A_BLK:, :] = vc_ref[...]

    def logits(r, a, p):
        cols = slice(p * LANES, (p + 1) * LANES)
        qs = _stack_heads(q_ref[r, a * A_BLK:(a + 1) * A_BLK, cols], first_head)
        k2 = kwin_ref[r, a * A_BLK:(a + 2) * A_BLK, cols]
        return lax.dot_general(qs, k2, (((1,), (1,)), ((), ())), preferred_element_type=F32)

    def band_bias(prev_shift):
        mask = own_ok | ((c2 < A_BLK) & (c2 >= r2 + prev_shift))
        return jnp.where(mask, 0.0, -jnp.inf).astype(F32)

    bias_inner = band_bias(0)
    biases = [band_bias(jnp.where(i > 0, 0, 2 * A_BLK))] + [bias_inner] * (nsub - 1)

    units = [(r, a, p) for r in range(n_cls) for a in range(nsub) for p in range(n_pairs)]
    queue = [logits(*units[u]) for u in range(BANDED_LOOKAHEAD)]
    lse_tile = None
    for u, (r, a, p) in enumerate(units):
        rows = slice(a * A_BLK, (a + 1) * A_BLK)
        cols = slice(p * LANES, (p + 1) * LANES)
        bias = biases[a]
        if p == 0:
            lse_tile = jnp.zeros((A_BLK, LANES), F32)
        s = queue.pop(0) + bias
        if u + BANDED_LOOKAHEAD < len(units):
            queue.append(logits(*units[u + BANDED_LOOKAHEAD]))
        m = jnp.max(s, axis=-1, keepdims=True)
        e = jnp.exp2(s - m).astype(BF16)
        v_ext = jnp.concatenate([vwin_ref[r, a * A_BLK:(a + 2) * A_BLK, cols], ones_blk], axis=1)
        pv = jnp.dot(e, v_ext, preferred_element_type=F32)
        num = jnp.where(first_head, pv[:A_BLK, :LANES], pv[A_BLK:, :LANES])
        o_ref[r, rows, cols] = num * (1.0 / jnp.where(first_head, pv[:A_BLK, LANES:], pv[A_BLK:, LANES:]))
        den_pair = jnp.where(odd_lane, pv[A_BLK:, LANES:], pv[:A_BLK, LANES:])
        m_pair = jnp.where(odd_lane, m[A_BLK:], m[:A_BLK])
        lse_tile = jnp.where((lane // 2) == p, m_pair + jnp.log(den_pair) * LOG2_E, lse_tile)
        if p == n_pairs - 1:
            lse_ref[r, rows, :] = lse_tile


BANDED_ROWS = 1024


def banded_group_attention(qkv):
    b, d, n, _ = qkv.shape
    tq = min(BANDED_ROWS, n)
    rc = min(BANDED_ROWS // tq, d)
    sub = tq // A_BLK
    assert n % tq == 0 and d % rc == 0

    def cur(blk, width=A_WIDTH):
        return pl.BlockSpec((None, rc, tq, width), lambda bi, r, i: (bi, r, i, blk))

    def prev(blk):
        return pl.BlockSpec((None, rc, A_BLK, A_WIDTH),
                            lambda bi, r, i: (bi, r, jnp.maximum(i * sub - 1, 0), blk))

    return pl.pallas_call(
        functools.partial(_banded_kernel, tq=tq),
        grid=(b, d // rc, n // tq),
        in_specs=[cur(0), prev(1), cur(1), prev(2), cur(2)],
        out_specs=[cur(0), cur(0, LANES)],
        out_shape=[
            jax.ShapeDtypeStruct((b, d, n, A_WIDTH), F32),
            jax.ShapeDtypeStruct((b, d, n, LANES), F32),
        ],
        scratch_shapes=[pltpu.VMEM((rc, tq + A_BLK, A_WIDTH), BF16)] * 2,
        compiler_params=_cparams(("arbitrary", "arbitrary", "arbitrary")),
        name=f"banded_attn_d{d}",
    )(qkv, qkv, qkv, qkv, qkv)


def _expand_heads(x, expand):
    hi = x.astype(BF16)
    lo = (x - hi.astype(F32)).astype(BF16)
    return (jnp.dot(hi, expand, preferred_element_type=F32)
            + jnp.dot(lo, expand, preferred_element_type=F32))


def _merge_out_kernel(o0_ref, o1_ref, o2_ref, l0_ref, l1_ref, l2_ref, gate_ref, ex_ref, w_ref,
                      mod_ref, h_ref, hout_ref, y_ref, ot_ref, lt_ref, *, dilations):
    o_refs, l_refs = (o0_ref, o1_ref, o2_ref), (l0_ref, l1_ref, l2_ref)
    tm = y_ref.shape[0]
    nblk = y_ref.shape[1] // LANES
    for g, d in enumerate(dilations):
        if d == 1:
            continue
        per = tm // d
        for r in range(d):
            lt_ref[g, pl.ds(r, per, stride=d), :] = l_refs[g][r]
            for k in range(nblk):
                ot_ref[g, k, pl.ds(r, per, stride=d), :] = o_refs[g][r, :, k * LANES:(k + 1) * LANES]
    lses = [l_refs[g][0] if d == 1 else lt_ref[g] for g, d in enumerate(dilations)]
    m = jnp.maximum(jnp.maximum(lses[0], lses[1]), lses[2])
    es = [jnp.exp2(l - m) for l in lses]
    inv = 1.0 / (es[0] + es[1] + es[2])
    ex = ex_ref[...]
    alphas = [_expand_heads(e * inv, ex) for e in es[:-1]]
    alphas.append(1.0 - alphas[0] - alphas[1])
    for k in range(nblk):
        cols = slice(k * LANES, (k + 1) * LANES)
        y = jnp.zeros((tm, LANES), F32)
        for g, d in enumerate(dilations):
            og = o_refs[g][0, :, cols] if d == 1 else ot_ref[g, k]
            y += alphas[g][:, cols] * og
        gate = gate_ref[:, cols].astype(F32)
        y_ref[:, cols] = (y * gate * (1.0 / (1.0 + jnp.exp(-gate)))).astype(y_ref.dtype)
    acc = jnp.dot(y_ref[...], w_ref[...], preferred_element_type=F32)
    hout_ref[...] = h_ref[...] + mod_ref[...] * acc


def merge_out_proj(outs, lses, gate, w_out, mod_gate, h):
    b, s, w = gate.shape
    d_model = h.shape[-1]
    tm = min(s, MERGE_ROWS)
    dil = tuple(o.shape[1] for o in outs)
    expand = (jnp.arange(LANES)[:, None] == (jnp.arange(w)[None, :] // HEAD_DIM)).astype(BF16)

    def cm(d, width):
        return pl.BlockSpec((None, d, tm // d, width), lambda bi, i: (bi, 0, i, 0))

    def tspec(width):
        return pl.BlockSpec((None, tm, width), lambda bi, i: (bi, i, 0))

    return pl.pallas_call(
        functools.partial(_merge_out_kernel, dilations=dil),
        grid=(b, s // tm),
        in_specs=[cm(d, w) for d in dil] + [cm(d, LANES) for d in dil]
                 + [tspec(w), pl.BlockSpec((LANES, w), lambda bi, i: (0, 0)),
                    pl.BlockSpec((w, d_model), lambda bi, i: (0, 0)),
                    pl.BlockSpec((None, 1, d_model), lambda bi, i: (bi, 0, 0)),
                    tspec(d_model)],
        out_specs=tspec(d_model),
        out_shape=jax.ShapeDtypeStruct((b, s, d_model), F32),
        scratch_shapes=[pltpu.VMEM((tm, w), BF16),
                        pltpu.VMEM((len(dil), w // LANES, tm, LANES), F32),
                        pltpu.VMEM((len(dil), tm, LANES), F32)],
        compiler_params=_cparams(("arbitrary", "arbitrary")),
        name="merge_out_proj",
    )(*outs, *lses, gate, expand, w_out, mod_gate.reshape(b, 1, d_model), h)


def dilated_layer(h, g, scale, shift, mod_gate, w_in, w_out, cos, sin):
    tn = PROJ_COL_TILE
    per = A_WIDTH // tn
    qkv_flags = (True,) * (2 * per) + (False,) * per
    n_groups = len(A_GROUPS)
    col = np.arange(w_in.shape[1])
    is_q = (col < n_groups * 3 * A_WIDTH) & ((col // A_WIDTH) % 3 == 0)
    w = (w_in * jnp.asarray(np.where(is_q, Q_PRESCALE, 1.0), F32)).astype(BF16)
    outs, lses = [], []
    gate = None
    for gi, (window, dilation) in enumerate(A_GROUPS):
        assert window // dilation == A_BLK
        plain = (A_WIDTH, 3 * n_groups) if gi == 0 else None
        qkv = in_proj(h, g, scale, shift, w, qkv_flags, cos, sin, tn, dilation,
                      w_block=gi, plain=plain)
        if plain is not None:
            qkv, gate = qkv
        if dilation == 1:
            qkv = qkv[:, None]
        o, lse = banded_group_attention(qkv)
        outs.append(o)
        lses.append(lse)
    return merge_out_proj(outs, lses, gate, w_out.astype(BF16), mod_gate, h)


B_COLS = 3584
B_ROPE_FLAGS = (1, 1, 0, 0, 1, 1, 0)


def _dsa_head_perm():
    group = B_HEADS // B_KV_HEADS
    order = []
    for g2 in range(B_KV_HEADS // 2):
        for r in range(group):
            order += [(2 * g2) * group + r, (2 * g2 + 1) * group + r]
    return np.asarray(order)


def _dsa_weights(w_in, w_out):
    d = w_in.shape[0]
    cuts = np.cumsum((B_WIDTH, B_KV_HEADS * HEAD_DIM, B_KV_HEADS * HEAD_DIM, IDX_HEADS * HEAD_DIM,
                      HEAD_DIM, IDX_HEADS, B_WIDTH))[:-1]
    wq, wk, wv, wqi, wki, wwi, wg = jnp.split(w_in, cuts, axis=1)
    cols = (_dsa_head_perm()[:, None] * HEAD_DIM + np.arange(HEAD_DIM)[None, :]).reshape(-1)
    zeros = lambda n: jnp.zeros((d, n), w_in.dtype)
    wq = wq * Q_PRESCALE
    w = jnp.concatenate([wq[:, cols], wg[:, cols], wqi, wk, wki, wki, zeros(LANES),
                         wwi, zeros(2 * LANES - IDX_HEADS), wv], axis=1)
    assert w.shape[1] == B_COLS
    return w.astype(BF16), w_out[cols, :].astype(BF16)


def _sortable(x):
    bits = pltpu.bitcast(x, I32)
    return bits ^ ((bits >> 31) & jnp.int32(0x7FFFFFFF))


def _bit_transpose32(words):
    a = list(words)
    j, m = 16, 0x0000FFFF
    while j:
        mask = jnp.int32(m - (1 << 32) if m >= (1 << 31) else m)
        shift = jnp.full(a[0].shape, j, I32)
        k = 0
        while k < 32:
            t = (a[k] ^ lax.shift_right_logical(a[k + j], shift)) & mask
            a[k] = a[k] ^ t
            a[k + j] = a[k + j] ^ lax.shift_left(t, shift)
            k = (k + j + 1) & ~j
        j >>= 1
        m = (m ^ (m << j)) & 0xFFFFFFFF
    return a


def _sublane_allsum(x):
    x = x + pltpu.roll(x, 4, 0)
    x = x + pltpu.roll(x, 2, 0)
    return x + pltpu.roll(x, 1, 0)


def _tree_sum(parts):
    parts = list(parts)
    while len(parts) > 1:
        odd = [parts[-1]] if len(parts) % 2 else []
        parts = [parts[j] + parts[j + 1] for j in range(0, len(parts) - 1, 2)] + odd
    return parts[0]


GROUP_KEYS = 256
ATTN_LOOKAHEAD = 1


def _dsa_kernel(q_ref, gate_ref, qi_ref, k_ref, ki_ref, wi_ref, v_ref, wout_ref, mod_ref, h_ref, *rest,
                k_sel, final):
    if final:
        fg_ref, hout_ref, *scratch = rest
    else:
        hout_ref, *scratch = rest
    keys_ref, planes_ref, bias_ref, kt_ref, qs_ref, s0_ref, m_ref, l_ref, acc_ref, y_ref = scratch
    i = pl.program_id(1)
    seq = keys_ref.shape[1]
    n_sub = Q_BLOCK // SEL_LANES
    n_pairs = B_HEADS // 2
    n_groups_max = seq // GROUP_KEYS
    lane_q = lax.broadcasted_iota(I32, (Q_BLOCK, LANES), 1)
    first_head_q = lane_q < HEAD_DIM
    first_head = lax.broadcasted_iota(I32, (SEL_LANES, LANES), 1) < HEAD_DIM
    key_row = lax.broadcasted_iota(I32, (KEY_TILE, SEL_LANES), 0)
    lane_idx = lax.broadcasted_iota(I32, (KEY_TILE, SEL_LANES), 1)

    @pl.when(i == 0)
    def _():
        planes_ref[...] = jnp.zeros(planes_ref.shape, I32)
        keys_ref[...] = jnp.zeros(keys_ref.shape, I32)

    def tiles_of(a):
        return ((i * n_sub + a) * SEL_LANES) // KEY_TILE + 1

    def select(a):
        n_tiles = tiles_of(a)
        rows_a = slice(a * SEL_LANES, (a + 1) * SEL_LANES)
        q_pos = (i * n_sub + a) * SEL_LANES + lane_idx
        keys = keys_ref.at[a]

        w_t = (wi_ref[rows_a, :].astype(F32) * (IDX_HEADS ** -0.5 * HEAD_DIM ** -0.5)).T
        qi_stacked = [_stack_heads(qi_ref[rows_a, p * LANES:(p + 1) * LANES], first_head)
                      for p in range(IDX_HEADS // 2)]

        def score_tile(t):
            base = pl.multiple_of(t * KEY_TILE, KEY_TILE)
            kk = ki_ref[pl.ds(base, KEY_TILE), :]
            score = jnp.zeros((KEY_TILE, SEL_LANES), F32)
            for p in range(IDX_HEADS // 2):
                sc = lax.dot_general(kk, qi_stacked[p], (((1,), (1,)), ((), ())),
                                     preferred_element_type=F32)
                score += jnp.maximum(sc[:, :SEL_LANES], 0.0) * w_t[2 * p:2 * p + 1, :]
                score += jnp.maximum(sc[:, SEL_LANES:], 0.0) * w_t[2 * p + 1:2 * p + 2, :]
            causal = (key_row + base) <= q_pos
            score = jnp.where(causal, score + 0.0, -jnp.inf)
            key = _sortable(score)
            keys[pl.ds(base, KEY_TILE), :] = key
            ukey = key ^ jnp.int32(INT_MIN)
            for g in range(KEY_TILE // GROUP_KEYS):
                words = [ukey[g * GROUP_KEYS + 8 * j:g * GROUP_KEYS + 8 * j + 8, :] for j in range(32)]
                rows = pl.ds(pl.multiple_of(t * (KEY_TILE // 32) + 8 * g, 8), 8)
                for b, plane in enumerate(_bit_transpose32(words)):
                    planes_ref[b, rows, :] = plane

        def score_two(t2, carry):
            score_tile(2 * t2)
            score_tile(2 * t2 + 1)
            return carry

        lax.fori_loop(0, n_tiles // 2, score_two, 0)

        @pl.when(n_tiles % 2 == 1)
        def _():
            score_tile(n_tiles - 1)

        def select_threshold():
            n_groups = n_tiles * (KEY_TILE // GROUP_KEYS)
            alive0 = tuple(jnp.full((8, SEL_LANES), -1, I32) * (g < n_groups).astype(I32)
                           for g in range(n_groups_max))
            need0 = jnp.full((8, SEL_LANES), k_sel, I32)

            def step(st, carry):
                alive, need, prefix = carry
                planes = [planes_ref[st, 8 * g:8 * g + 8, :] for g in range(n_groups_max)]
                ones = [al & pln for al, pln in zip(alive, planes)]
                cnt = _sublane_allsum(_tree_sum([lax.population_count(o) for o in ones]))
                take = cnt >= need
                prefix = jnp.where(take, prefix | jnp.left_shift(jnp.int32(1), 31 - st), prefix)
                need = jnp.where(take, need, need - cnt)
                flip = jnp.where(take, 0, -1)
                alive = tuple(al & (pln ^ flip) for al, pln in zip(alive, planes))
                return alive, need, prefix

            alive, need, prefix = lax.fori_loop(
                0, 32, step, (alive0, need0, jnp.zeros((8, SEL_LANES), I32)))
            thr = (prefix ^ jnp.int32(INT_MIN))[:1, :]
            need_eq = need[:1, :]

            n_eq = _sublane_allsum(_tree_sum([lax.population_count(al) for al in alive]))[:1, :]
            tied = (n_eq > need_eq) & (thr > KEY_NEG_INF)
            idx_bits = (seq - 1).bit_length()

            def tie_break():
                def count_eq_before(cut):
                    def body(t, acc):
                        base = pl.multiple_of(t * KEY_TILE, KEY_TILE)
                        hit = (keys[pl.ds(base, KEY_TILE), :] == thr) & ((key_row + base) < cut)
                        return acc + jnp.sum(hit.astype(I32).reshape(KEY_TILE // 8, 8, SEL_LANES), axis=0)
                    acc = lax.fori_loop(0, n_tiles, body, jnp.zeros((8, SEL_LANES), I32))
                    return jnp.sum(acc, axis=0, keepdims=True)

                def bit_step(it, cut):
                    cand = cut + jnp.left_shift(jnp.int32(1), idx_bits - 1 - it)
                    return jnp.where(count_eq_before(cand) < need_eq, cand, cut)

                return lax.fori_loop(0, idx_bits, bit_step, jnp.zeros((1, SEL_LANES), I32))

            any_tied = jnp.max(tied.astype(I32)) > 0
            idx_cut = lax.cond(any_tied, tie_break, lambda: jnp.full((1, SEL_LANES), seq, I32))
            return thr, idx_cut

        all_selected = (i * n_sub + a + 1) * SEL_LANES <= k_sel
        thr, idx_cut = lax.cond(
            all_selected,
            lambda: (jnp.full((1, SEL_LANES), INT_MIN, I32), jnp.full((1, SEL_LANES), seq, I32)),
            select_threshold)
        return thr, idx_cut, q_pos

    selected = [select(a) for a in range(n_sub)]
    n_tiles = tiles_of(n_sub - 1)

    def write_bias(base):
        k_idx = key_row + base
        for a, (thr, idx_cut, q_pos) in enumerate(selected):
            key = keys_ref[a, pl.ds(base, KEY_TILE), :]
            sel = ((key > thr) | ((key == thr) & (k_idx <= idx_cut))) & (k_idx <= q_pos)
            bias_t = jnp.where(sel, 0.0, -jnp.inf).astype(F32)
            for c in range(KEY_TILE // LANES):
                col = base + c * LANES
                col = col if isinstance(col, int) else pl.multiple_of(col, LANES)
                bias_ref[a * SEL_LANES:(a + 1) * SEL_LANES, pl.ds(col, LANES)] = \
                    bias_t[c * LANES:(c + 1) * LANES, :].T

    write_bias(0)

    for p in range(n_pairs):
        cols = slice(p * LANES, (p + 1) * LANES)
        qs_ref[p] = _stack_heads(q_ref[:, cols], first_head_q)
    heads_per_kv_pair = 2 * (B_HEADS // B_KV_HEADS) // 2

    ones_blk = jnp.ones((KEY_TILE, LANES), BF16)

    @pl.when(i == 0)
    def _():
        def xpose(r, carry):
            rows = pl.ds(pl.multiple_of(r * LANES, LANES), LANES)
            for c in range(kt_ref.shape[0]):
                blk = k_ref[rows, c * LANES:(c + 1) * LANES].astype(F32)
                kt_ref[c, :, rows] = blk.T.astype(BF16)
            return carry
        lax.fori_loop(0, seq // LANES, xpose, 0)

    def logits(base, p):
        kt = kt_ref[p // heads_per_kv_pair, :, pl.ds(base, KEY_TILE)]
        return jnp.dot(qs_ref[p], kt, preferred_element_type=F32)

    lookahead = s0_ref.shape[0]
    for u in range(lookahead):
        s0_ref[u] = logits(0, u)

    def attn_tiles(t0, count, start=False):
        bases = [pl.multiple_of(jnp.minimum(t0 + dt, n_tiles - 1) * KEY_TILE, KEY_TILE)
                 for dt in range(count + 1)]
        units = [(dt, p) for dt in range(count + 1) for p in range(n_pairs)]
        queue = [s0_ref[u] for u in range(lookahead)]
        for u, (dt, p) in enumerate(units[:count * n_pairs]):
            base = bases[dt]
            if p == 0:
                bias = bias_ref[:, pl.ds(base, KEY_TILE)]
                write_bias(bases[dt + 1])
                bias2 = jnp.concatenate([bias, bias], axis=0)
            kv = p // heads_per_kv_pair
            kv_cols = slice(kv * LANES, (kv + 1) * LANES)
            s = queue.pop(0) + bias2
            ahead_dt, ahead_p = units[u + lookahead]
            queue.append(logits(bases[ahead_dt], ahead_p))
            m_blk = s[:, :LANES]
            for c in range(1, KEY_TILE // LANES):
                m_blk = jnp.maximum(m_blk, s[:, c * LANES:(c + 1) * LANES])
            opening = start and dt == 0
            m_cur = jnp.max(m_blk, axis=-1, keepdims=True)
            if opening:
                m_new = jnp.broadcast_to(jnp.maximum(m_cur, F32_MIN), m_blk.shape)
            else:
                m_old = m_ref[p]
                m_new = jnp.maximum(m_old, m_cur)
                alpha = jnp.exp2(m_old - m_new)
            e = jnp.concatenate(
                [jnp.exp2(s[:, c * LANES:(c + 1) * LANES] - m_new).astype(BF16)
                 for c in range(KEY_TILE // LANES)], axis=1)
            v_ext = jnp.concatenate([v_ref[pl.ds(base, KEY_TILE), kv_cols], ones_blk], axis=1)
            pv = jnp.dot(e, v_ext, preferred_element_type=F32)
            if opening:
                acc_ref[p] = pv[:, :LANES]
                l_ref[p] = pv[:, LANES:]
            else:
                acc_ref[p] = acc_ref[p] * alpha + pv[:, :LANES]
                l_ref[p] = l_ref[p] * alpha + pv[:, LANES:]
            m_ref[p] = m_new
        for u in range(lookahead):
            s0_ref[u] = queue[u]

    @pl.when(n_tiles >= 2)
    def _():
        attn_tiles(0, 2, start=True)

    @pl.when(n_tiles < 2)
    def _():
        attn_tiles(0, 1, start=True)

    def attn_two(t2, carry):
        attn_tiles(2 * t2, 2)
        return carry

    lax.fori_loop(1, n_tiles // 2, attn_two, 0)

    @pl.when((n_tiles % 2 == 1) & (n_tiles > 1))
    def _():
        attn_tiles(n_tiles - 1, 1)

    for p in range(n_pairs):
        cols = slice(p * LANES, (p + 1) * LANES)
        l = l_ref[p]
        acc = acc_ref[p]
        num = jnp.where(first_head_q, acc[:Q_BLOCK], acc[Q_BLOCK:])
        den = jnp.where(first_head_q, l[:Q_BLOCK], l[Q_BLOCK:])
        gate = gate_ref[:, cols].astype(F32)
        y_ref[:, cols] = (num * gate / (den * (1.0 + jnp.exp(-gate)))).astype(y_ref.dtype)

    h_new = h_ref[...] + mod_ref[...] * jnp.dot(y_ref[...], wout_ref[...], preferred_element_type=F32)
    if final:
        ms = jnp.mean(h_new * h_new, axis=-1, keepdims=True)
        h_new = h_new * lax.rsqrt(ms + NORM_EPS) * fg_ref[...]
    hout_ref[...] = h_new


def dsa_attention(proj, w_out, mod_gate, h, final_g=None):
    b, s, _ = proj.shape
    d_model = h.shape[-1]
    k_sel = min(TOPK_MAX, s // 4)
    assert s % KEY_TILE == 0 and s % Q_BLOCK == 0 and k_sel <= KEY_TILE
    kvw = B_KV_HEADS * HEAD_DIM
    n_pairs = B_HEADS // 2

    def qblock(width, idx):
        return pl.BlockSpec((None, Q_BLOCK, width), lambda bi, i: (bi, i, idx))

    def full(width, idx):
        return pl.BlockSpec((None, s, width), lambda bi, i: (bi, 0, idx))

    in_specs = [
        qblock(B_WIDTH, 0),
        qblock(B_WIDTH, 1),
        qblock(IDX_HEADS * HEAD_DIM, 4),
        full(kvw, 10),
        full(LANES, 22),
        qblock(LANES, 24),
        full(kvw, 13),
        pl.BlockSpec((B_WIDTH, d_model), lambda bi, i: (0, 0)),
        pl.BlockSpec((None, 1, d_model), lambda bi, i: (bi, 0, 0)),
        qblock(d_model, 0),
    ]
    args = [proj] * 7 + [w_out, mod_gate.reshape(b, 1, d_model), h]
    if final_g is not None:
        in_specs.append(pl.BlockSpec((1, d_model), lambda bi, i: (0, 0)))
        args.append(final_g.reshape(1, d_model))
    return pl.pallas_call(
        functools.partial(_dsa_kernel, k_sel=k_sel, final=final_g is not None),
        grid=(b, s // Q_BLOCK),
        in_specs=in_specs,
        out_specs=qblock(d_model, 0),
        out_shape=jax.ShapeDtypeStruct((b, s, d_model), F32),
        scratch_shapes=[
            pltpu.VMEM((Q_BLOCK // SEL_LANES, s, SEL_LANES), I32),
            pltpu.VMEM((32, s // 32, SEL_LANES), I32),
            pltpu.VMEM((Q_BLOCK, s), F32),
            pltpu.VMEM((kvw // LANES, LANES, s), BF16),
            pltpu.VMEM((n_pairs, 2 * Q_BLOCK, LANES), BF16),
            pltpu.VMEM((ATTN_LOOKAHEAD, 2 * Q_BLOCK, KEY_TILE), F32),
            pltpu.VMEM((n_pairs, 2 * Q_BLOCK, LANES), F32),
            pltpu.VMEM((n_pairs, 2 * Q_BLOCK, LANES), F32),
            pltpu.VMEM((n_pairs, 2 * Q_BLOCK, LANES), F32),
            pltpu.VMEM((Q_BLOCK, B_WIDTH), BF16),
        ],
        compiler_params=_cparams(("arbitrary", "arbitrary")),
        name="dsa_attention",
    )(*args)


def dsa_layer(h, g, scale, shift, mod_gate, w_in_packed, w_out, cos, sin, final_g=None):
    proj = in_proj(h, g, scale, shift, w_in_packed, [f == 1 for f in B_ROPE_FLAGS], cos, sin,
                   PROJ_COL_TILE)
    return dsa_attention(proj, w_out, mod_gate, h, final_g)


def kernel(x, c, positions, norm_g, ada_w, ada_b, a_w_in, a_w_out, b_w_in, b_w_out, final_g):
    depth = norm_g.shape[0]
    d = x.shape[-1]
    mod = adaln_mod(c, ada_w, ada_b)
    cos, sin = rope_tables(positions)
    h = x
    for i in range(depth):
        shift, scale, gate = mod[i, :, :d], mod[i, :, d:2 * d], mod[i, :, 2 * d:]
        last = i == depth - 1
        if i % 2 == 0:
            h = dilated_layer(h, norm_g[i], scale, shift, gate, a_w_in[i // 2], a_w_out[i // 2], cos, sin)
            if last:
                h = final_norm(h, final_g)
        else:
            w_in, w_out = _dsa_weights(b_w_in[i // 2], b_w_out[i // 2])
            h = dsa_layer(h, norm_g[i], scale, shift, gate, w_in, w_out, cos, sin,
                          final_g if last else None)
    return h
```
